```python
import math
import jax
import jax.numpy as jnp
from jax import lax
import numpy as np

D_MODEL = 1024
BATCH = 2
SEQ = 8192
DEPTH = 2

GRID_W = 64
CTX_LEN = 256
HEAD_DIM = 64
ROPE_THETA = 10000.0
NORM_EPS = 1e-6
NEG_INF = -1e30

SWA_HEADS = 8
SWA_KV_HEADS = 2
WINDOW = 128
BLOCK = 128

RWKV_HEADS = 8
RWKV_WIDTH = RWKV_HEADS * HEAD_DIM
DECAY_LORA = 64
AAA_LORA = 64
GATE_LORA = 128
RWKV_GN_EPS = 64e-5

DIFF_HEADS = 4
DIFF_V_DIM = 2 * HEAD_DIM
DIFF_SUBLN_EPS = 1e-5

N_BRANCHES = 3

D_FF = 2816
N_EXPERTS = 8
TOP_K = 2
D_FF_EXPERT = 3584
MOE_BLOCK = 128
N_DENSE = (DEPTH + 1) // 2
N_MOE = DEPTH // 2

SWA_Q = SWA_HEADS * HEAD_DIM
SWA_KV = SWA_KV_HEADS * HEAD_DIM
RWKV_SPLITS = (RWKV_WIDTH, RWKV_WIDTH, RWKV_WIDTH, DECAY_LORA, DECAY_LORA, AAA_LORA, AAA_LORA, GATE_LORA)
RWKV_COLS = 3 * RWKV_WIDTH + 2 * DECAY_LORA + 2 * AAA_LORA + GATE_LORA
DIFF_QK = DIFF_HEADS * 2 * HEAD_DIM
DIFF_V = DIFF_HEADS * DIFF_V_DIM
IN_SPLITS = (SWA_Q, SWA_KV, SWA_KV, RWKV_COLS, DIFF_QK, DIFF_QK, DIFF_V, N_BRANCHES * D_MODEL)
D_IN = SWA_Q + 2 * SWA_KV + RWKV_COLS + 2 * DIFF_QK + DIFF_V + N_BRANCHES * D_MODEL

kernel_name = 'hybrid_diffusion_trunk_swa_rwkv7_diffattn_moe'


def rmsnorm(x, g, eps=NORM_EPS):
    xf = x.astype(jnp.float32)
    y = xf * lax.rsqrt(jnp.mean(xf * xf, axis=-1, keepdims=True) + eps)
    return (y * g.astype(jnp.float32)).astype(x.dtype)


def modulate(h, shift, scale):
    return h * (1 + scale) + shift


def _split(t, sizes):
    out, start = [], 0
    for n in sizes:
        out.append(t[..., start:start + n])
        start += n
    return out


def axial_rope_tables(n_tokens):
    rows = n_tokens // GRID_W
    row = jnp.broadcast_to(jnp.arange(rows, dtype=jnp.float32)[:, None], (rows, GRID_W)).reshape(-1)
    col = jnp.broadcast_to(jnp.arange(GRID_W, dtype=jnp.float32)[None, :], (rows, GRID_W)).reshape(-1)
    axis_dim = HEAD_DIM // 2
    inv_freq = ROPE_THETA ** (-jnp.arange(0, axis_dim, 2, dtype=jnp.float32) / axis_dim)
    ang = jnp.stack([row[:, None] * inv_freq, col[:, None] * inv_freq], axis=1)
    return jnp.cos(ang), jnp.sin(ang)


def apply_rope(x, cos, sin):
    shp = x.shape
    xr = x.reshape(shp[:-1] + (2, 2, HEAD_DIM // 4))
    bshape = (shp[1],) + (1,) * (x.ndim - 3) + (2, HEAD_DIM // 4)
    cs = cos.reshape(bshape).astype(x.dtype)
    sn = sin.reshape(bshape).astype(x.dtype)
    x1, x2 = xr[..., 0, :], xr[..., 1, :]
    out = jnp.stack([x1 * cs - x2 * sn, x1 * sn + x2 * cs], axis=-2)
    return out.reshape(shp)


def swa_attention(q_l, k_l, v_l, q_c, k_c, v_c, sink, need_ctx):
    B, S = q_l.shape[:2]
    L = k_c.shape[1]
    nb = S // BLOCK
    G = SWA_HEADS // SWA_KV_HEADS
    scale = HEAD_DIM ** -0.5
    sink_f = sink.astype(jnp.float32).reshape(SWA_KV_HEADS, G)
    qb = q_l.reshape(B, nb, BLOCK, SWA_KV_HEADS, G, HEAD_DIM)

    def windows(t):
        tp = jnp.pad(t, ((0, 0), (BLOCK, BLOCK), (0, 0), (0, 0)))
        tb = tp.reshape(B, nb + 2, BLOCK, SWA_KV_HEADS, HEAD_DIM)
        return jnp.concatenate([tb[:, :-2], tb[:, 1:-1], tb[:, 2:]], axis=2)

    kw, vw = windows(k_l), windows(v_l)
    qpos = (jnp.arange(nb)[:, None] * BLOCK + jnp.arange(BLOCK)[None, :])[:, :, None]
    kpos = ((jnp.arange(nb)[:, None] - 1) * BLOCK + jnp.arange(3 * BLOCK)[None, :])[:, None, :]
    valid = (jnp.abs(kpos - qpos) <= WINDOW) & (kpos >= 0) & (kpos < S)
    s_loc = jnp.einsum('bnqhgd,bnkhd->bnhgqk', qb, kw, preferred_element_type=jnp.float32) * scale
    s_loc = jnp.where(valid[None, :, None, None], s_loc, NEG_INF)
    s_ctx = jnp.einsum('bnqhgd,blhd->bnhgql', qb, k_c, preferred_element_type=jnp.float32) * scale
    s_sink = jnp.broadcast_to(sink_f[None, None, :, :, None, None], s_loc.shape[:-1] + (1,))
    p = jax.nn.softmax(jnp.concatenate([s_loc, s_ctx, s_sink], axis=-1), axis=-1)
    nw = 3 * BLOCK
    o = (jnp.einsum('bnhgqk,bnkhd->bnqhgd', p[..., :nw].astype(v_l.dtype), vw)
         + jnp.einsum('bnhgql,blhd->bnqhgd', p[..., nw:nw + L].astype(v_l.dtype), v_c))
    o_lat = o.reshape(B, S, SWA_Q)
    if not need_ctx:
        return o_lat, None
    qcg = q_c.reshape(B, L, SWA_KV_HEADS, G, HEAD_DIM)
    sc = jnp.einsum('blhgd,bmhd->bhglm', qcg, k_c, preferred_element_type=jnp.float32) * scale
    sc_sink = jnp.broadcast_to(sink_f[None, :, :, None, None], sc.shape[:-1] + (1,))
    pc = jax.nn.softmax(jnp.concatenate([sc, sc_sink], axis=-1), axis=-1)
    oc = jnp.einsum('bhglm,bmhd->blhgd', pc[..., :L].astype(v_c.dtype), v_c)
    return o_lat, oc.reshape(B, L, SWA_Q)


def token_shift(p, mu_prev, mu_next):
    prev = jnp.pad(p[:, :-1], ((0, 0), (1, 0), (0, 0)))
    nxt = jnp.pad(p[:, 1:], ((0, 0), (0, 1), (0, 0)))
    return p + mu_prev * (prev - p) + mu_next * (nxt - p)


def rwkv_scan(s0, r, w, k, v, a_vec, b_vec, reverse, emit):
    xs = tuple(jnp.moveaxis(t, 1, 0) for t in (r, w, k, v, a_vec, b_vec))

    def step(s, inp):
        r_t, w_t, k_t, v_t, a_t, b_t = inp
        sa = jnp.einsum('bhvk,bhk->bhv', s, a_t)
        s = s * w_t[:, :, None, :] + sa[..., None] * b_t[:, :, None, :] + v_t[..., None] * k_t[:, :, None, :]
        y = jnp.einsum('bhvk,bhk->bhv', s, r_t) if emit else None
        return s, y

    s_fin, ys = lax.scan(step, s0, xs, reverse=reverse)
    return s_fin, (jnp.moveaxis(ys, 0, 1) if emit else None)


def _rwkv_prepare(p, lp):
    B, T = p.shape[:2]
    f32 = jnp.float32
    p = token_shift(p, lp['rwkv_mu_prev'], lp['rwkv_mu_next'])
    r, k, v, wd_f, wd_b, ad_f, ad_b, gd = _split(p, RWKV_SPLITS)
    hd = lambda t: t.astype(f32).reshape(B, T, RWKV_HEADS, HEAD_DIM)
    g = jax.nn.sigmoid(gd) @ lp['rwkv_g_up']
    k_h = hd(k)
    kk = k_h * lp['rwkv_k_k'].astype(f32).reshape(RWKV_HEADS, HEAD_DIM)
    kk = kk / jnp.maximum(jnp.sqrt(jnp.sum(kk * kk, axis=-1, keepdims=True)), 1e-12)
    k_a = lp['rwkv_k_a'].astype(f32).reshape(RWKV_HEADS, HEAD_DIM)
    dirs = []
    for d, (wd, ad) in enumerate(((wd_f, ad_f), (wd_b, ad_b))):
        w_raw = (lp['rwkv_w0'][d] + jnp.tanh(wd) @ lp['rwkv_w_up'][d]).astype(f32)
        w_log = -jax.nn.softplus(-w_raw) - 0.5
        decay = hd(jnp.exp(-jnp.exp(w_log)))
        a = hd(jax.nn.sigmoid((lp['rwkv_a0'][d] + ad @ lp['rwkv_a_up'][d]).astype(f32)))
        k_eff = k_h * (1 + (a - 1) * k_a)
        dirs.append((decay, k_eff, a))
    return hd(r), hd(v), kk, g, dirs


def _rwkv_output(ys, r, v, k_effs, g, lp, dtype):
    B, T = r.shape[:2]
    f32 = jnp.float32
    y = ys[0] + ys[1]
    mu = jnp.mean(y, axis=-1, keepdims=True)
    var = jnp.mean(jnp.square(y - mu), axis=-1, keepdims=True)
    y = ((y - mu) * lax.rsqrt(var + RWKV_GN_EPS)).reshape(B, T, RWKV_WIDTH)
    y = y * lp['rwkv_lnx_g'].astype(f32) + lp['rwkv_lnx_b'].astype(f32)
    r_k = lp['rwkv_r_k'].astype(f32)
    bonus = jnp.sum(r * (k_effs[0] + k_effs[1]) * r_k, axis=-1, keepdims=True) * v
    y = y + bonus.reshape(B, T, RWKV_WIDTH)
    return (y * g.astype(f32)).astype(dtype)


def rwkv7_bidirectional(p_lat, p_ctx, lp, need_ctx):
    B = p_lat.shape[0]
    r_l, v_l, kk_l, g_l, dirs_l = _rwkv_prepare(p_lat, lp)
    r_c, v_c, kk_c, g_c, dirs_c = _rwkv_prepare(p_ctx, lp)
    s0 = jnp.zeros((B, RWKV_HEADS, HEAD_DIM, HEAD_DIM), jnp.float32)
    y_lat, y_ctx = [], []
    for d, reverse in enumerate((False, True)):
        w_c, k_c, a_c = dirs_c[d]
        s_ctx, y_c = rwkv_scan(s0, r_c, w_c, k_c, v_c, -kk_c, kk_c * a_c, reverse, need_ctx)
        w_l, k_l, a_l = dirs_l[d]
        _, y_l = rwkv_scan(s_ctx, r_l, w_l, k_l, v_l, -kk_l, kk_l * a_l, reverse, True)
        y_lat.append(y_l)
        y_ctx.append(y_c)
    out_l = _rwkv_output(y_lat, r_l, v_l, [dd[1] for dd in dirs_l], g_l, lp, p_lat.dtype)
    if not need_ctx:
        return out_l, None
    out_c = _rwkv_output(y_ctx, r_c, v_c, [dd[1] for dd in dirs_c], g_c, lp, p_ctx.dtype)
    return out_l, out_c


def diff_attention(q_l, k_l, v_l, q_c, k_c, v_c, lam, subln_g, lam_init, need_ctx):
    B, S = q_l.shape[:2]
    nb = S // BLOCK
    scale = HEAD_DIM ** -0.5
    k_all = jnp.concatenate([k_l, k_c], axis=1)
    v_all = jnp.concatenate([v_l, v_c], axis=1)

    def attend(q, k, v):
        s = jnp.einsum('bqhcd,bkhcd->bhcqk', q, k, preferred_element_type=jnp.float32) * scale
        p = jax.nn.softmax(s, axis=-1)
        a = p[:, :, 0] - lam * p[:, :, 1]
        return jnp.einsum('bhqk,bkhe->bqhe', a.astype(v.dtype), v)

    def finish(o):
        return (rmsnorm(o, subln_g, DIFF_SUBLN_EPS) * (1.0 - lam_init)).reshape(o.shape[0], o.shape[1], DIFF_V)

    q_blocks = jnp.moveaxis(q_l.reshape(B, nb, BLOCK, DIFF_HEADS, 2, HEAD_DIM), 1, 0)
    o = lax.map(lambda qb: attend(qb, k_all, v_all), q_blocks)
    o_lat = finish(jnp.moveaxis(o, 0, 1).reshape(B, S, DIFF_HEADS, DIFF_V_DIM))
    if not need_ctx:
        return o_lat, None
    return o_lat, finish(attend(q_c, k_c, v_c))


def token_mixer(h_l, h_c, lp, layer, need_ctx, cos, sin):
    pl = _split(h_l @ lp['w_in'], IN_SPLITS)
    pc = _split(h_c @ lp['w_in'], IN_SPLITS)
    heads = lambda t, n: t.reshape(t.shape[0], t.shape[1], n, HEAD_DIM)
    pair = lambda t: t.reshape(t.shape[0], t.shape[1], DIFF_HEADS, 2, HEAD_DIM)
    vh = lambda t: t.reshape(t.shape[0], t.shape[1], DIFF_HEADS, DIFF_V_DIM)
    ya_l, ya_c = swa_attention(
        apply_rope(heads(pl[0], SWA_HEADS), cos, sin), apply_rope(heads(pl[1], SWA_KV_HEADS), cos, sin),
        heads(pl[2], SWA_KV_HEADS), heads(pc[0], SWA_HEADS), heads(pc[1], SWA_KV_HEADS),
        heads(pc[2], SWA_KV_HEADS), lp['swa_sink'], need_ctx)
    yb_l, yb_c = rwkv7_bidirectional(pl[3], pc[3], lp, need_ctx)
    lam_vec = lp['diff_lambda'].astype(jnp.float32)
    lam_init = 0.8 - 0.6 * math.exp(-0.3 * layer)
    lam = jnp.exp(jnp.sum(lam_vec[0] * lam_vec[1])) - jnp.exp(jnp.sum(lam_vec[2] * lam_vec[3])) + lam_init
    yc_l, yc_c = diff_attention(
        apply_rope(pair(pl[4]), cos, sin), apply_rope(pair(pl[5]), cos, sin), vh(pl[6]),
        pair(pc[4]), pair(pc[5]), vh(pc[6]), lam, lp['diff_subln_g'], lam_init, need_ctx)

    def merge(ya, yb, yc, gate_cols):
        gates = jax.nn.sigmoid(gate_cols.astype(jnp.float32)).astype(ya.dtype)
        g_a, g_b, g_c = _split(gates, (D_MODEL, D_MODEL, D_MODEL))
        merged = g_a * (ya @ lp['proj_swa']) + g_b * (yb @ lp['proj_rwkv']) + g_c * (yc @ lp['proj_diff'])
        return merged @ lp['w_out']

    out_l = merge(ya_l, yb_l, yc_l, pl[7])
    out_c = merge(ya_c, yb_c, yc_c, pc[7]) if need_ctx else None
    return out_l, out_c


def swiglu(h, w_gate, w_up, w_down):
    return (jax.nn.silu(h @ w_gate) * (h @ w_up)) @ w_down


def moe_swiglu(h, router_w, w_gate, w_up, w_down):
    N, D = h.shape
    logits = (h @ router_w).astype(jnp.float32)
    top_v, top_i = lax.top_k(logits, TOP_K)
    top_w = jax.nn.softmax(top_v, axis=-1)
    A = N * TOP_K
    e_flat = top_i.reshape(A)
    t_flat = jnp.repeat(jnp.arange(N, dtype=jnp.int32), TOP_K)
    w_flat = top_w.reshape(A)
    order = jnp.argsort(e_flat)
    e_sorted, t_sorted, w_sorted = e_flat[order], t_flat[order], w_flat[order]
    counts = jnp.zeros((N_EXPERTS,), jnp.int32).at[e_flat].add(1)
    starts = jnp.cumsum(counts) - counts
    padded = (counts + MOE_BLOCK - 1) // MOE_BLOCK * MOE_BLOCK
    pends = jnp.cumsum(padded)
    pstarts = pends - padded
    dest = pstarts[e_sorted] + jnp.arange(A, dtype=jnp.int32) - starts[e_sorted]
    n_blocks = -(-A // MOE_BLOCK) + N_EXPERTS
    P = n_blocks * MOE_BLOCK
    tok = jnp.zeros((P,), jnp.int32).at[dest].set(t_sorted)
    wts = jnp.zeros((P,), jnp.float32).at[dest].set(w_sorted)
    block_start = jnp.arange(n_blocks, dtype=jnp.int32) * MOE_BLOCK
    blk_e = jnp.minimum(jnp.searchsorted(pends, block_start, side='right'), N_EXPERTS - 1)

    def run(args):
        tb, wb, e = args
        xb = h[tb]
        hid = jax.nn.silu(xb @ w_gate[e]) * (xb @ w_up[e])
        return (hid @ w_down[e]) * wb[:, None].astype(h.dtype)

    yb = lax.map(run, (tok.reshape(n_blocks, MOE_BLOCK), wts.reshape(n_blocks, MOE_BLOCK), blk_e))
    return jnp.zeros_like(h).at[tok].add(yb.reshape(P, D))


def setup_inputs(seed: int = 0) -> dict:
    key = jax.random.key(seed)
    ks = iter(jax.random.split(key, 40))
    f32 = jnp.float32
    Dm = D_MODEL

    def nrm(shape, scale):
        return jax.random.normal(next(ks), shape, f32) * scale

    def gain(shape):
        return 1.0 + 0.02 * jax.random.normal(next(ks), shape, f32)

    def unif(shape, lo, hi):
        return jax.random.uniform(next(ks), shape, f32, lo, hi)

    return {
        'x': nrm((BATCH, SEQ, Dm), 1.0),
        'c': nrm((BATCH, Dm), 1.0),
        'ctx': nrm((BATCH, CTX_LEN, Dm), 1.0),
        'c_ctx': nrm((Dm,), 1.0),
        'ada_w': nrm((DEPTH, Dm, 6 * Dm), 0.5 * Dm ** -0.5),
        'ada_b': nrm((DEPTH, 6 * Dm), 0.02),
        'pre_mix_g': gain((DEPTH, Dm)),
        'post_mix_g': gain((DEPTH, Dm)),
        'pre_ffn_g': gain((DEPTH, Dm)),
        'post_ffn_g': gain((DEPTH, Dm)),
        'w_in': nrm((DEPTH, Dm, D_IN), Dm ** -0.5),
        'swa_sink': nrm((DEPTH, SWA_HEADS), 0.5),
        'rwkv_mu_prev': unif((DEPTH, RWKV_COLS), 0.0, 0.5),
        'rwkv_mu_next': unif((DEPTH, RWKV_COLS), 0.0, 0.5),
        'rwkv_w0': unif((DEPTH, 2, RWKV_WIDTH), -6.0, 1.0),
        'rwkv_w_up': nrm((DEPTH, 2, DECAY_LORA, RWKV_WIDTH), 0.1 * DECAY_LORA ** -0.5),
        'rwkv_a0': nrm((DEPTH, 2, RWKV_WIDTH), 0.1),
        'rwkv_a_up': nrm((DEPTH, 2, AAA_LORA, RWKV_WIDTH), 0.1 * AAA_LORA ** -0.5),
        'rwkv_g_up': nrm((DEPTH, GATE_LORA, RWKV_WIDTH), GATE_LORA ** -0.5),
        'rwkv_k_k': 0.85 + nrm((DEPTH, RWKV_WIDTH), 0.05),
        'rwkv_k_a': 1.0 + nrm((DEPTH, RWKV_WIDTH), 0.05),
        'rwkv_r_k': nrm((DEPTH, RWKV_HEADS, HEAD_DIM), 0.1),
        'rwkv_lnx_g': gain((DEPTH, RWKV_WIDTH)),
        'rwkv_lnx_b': nrm((DEPTH, RWKV_WIDTH), 0.02),
        'diff_lambda': nrm((DEPTH, 4, HEAD_DIM), 0.1),
        'diff_subln_g': gain((DEPTH, DIFF_V_DIM)),
        'proj_swa': nrm((DEPTH, SWA_Q, Dm), SWA_Q ** -0.5),
        'proj_rwkv': nrm((DEPTH, RWKV_WIDTH, Dm), RWKV_WIDTH ** -0.5),
        'proj_diff': nrm((DEPTH, DIFF_V, Dm), DIFF_V ** -0.5),
        'w_out': nrm((DEPTH, Dm, Dm), Dm ** -0.5),
        'ffn_w_gate': nrm((N_DENSE, Dm, D_FF), Dm ** -0.5),
        'ffn_w_up': nrm((N_DENSE, Dm, D_FF), Dm ** -0.5),
        'ffn_w_down': nrm((N_DENSE, D_FF, Dm), D_FF ** -0.5),
        'router_w': nrm((N_MOE, Dm, N_EXPERTS), Dm ** -0.5),
        'moe_w_gate': nrm((N_MOE, N_EXPERTS, Dm, D_FF_EXPERT), Dm ** -0.5),
        'moe_w_up': nrm((N_MOE, N_EXPERTS, Dm, D_FF_EXPERT), Dm ** -0.5),
        'moe_w_down': nrm((N_MOE, N_EXPERTS, D_FF_EXPERT, Dm), D_FF_EXPERT ** -0.5),
    }


def reference(x, c, ctx, c_ctx, ada_w, ada_b, pre_mix_g, post_mix_g, pre_ffn_g, post_ffn_g,
              w_in, swa_sink, rwkv_mu_prev, rwkv_mu_next, rwkv_w0, rwkv_w_up, rwkv_a0, rwkv_a_up,
              rwkv_g_up, rwkv_k_k, rwkv_k_a, rwkv_r_k, rwkv_lnx_g, rwkv_lnx_b, diff_lambda,
              diff_subln_g, proj_swa, proj_rwkv, proj_diff, w_out, ffn_w_gate, ffn_w_up,
              ffn_w_down, router_w, moe_w_gate, moe_w_up, moe_w_down):
    B, S, D = x.shape
    L = ctx.shape[1]
    cos, sin = axial_rope_tables(S)
    h_lat, h_ctx = x, ctx
    for layer in range(DEPTH):
        need_ctx = layer < DEPTH - 1
        lp = {
            'w_in': w_in[layer], 'swa_sink': swa_sink[layer],
            'rwkv_mu_prev': rwkv_mu_prev[layer], 'rwkv_mu_next': rwkv_mu_next[layer],
            'rwkv_w0': rwkv_w0[layer], 'rwkv_w_up': rwkv_w_up[layer],
            'rwkv_a0': rwkv_a0[layer], 'rwkv_a_up': rwkv_a_up[layer], 'rwkv_g_up': rwkv_g_up[layer],
            'rwkv_k_k': rwkv_k_k[layer], 'rwkv_k_a': rwkv_k_a[layer], 'rwkv_r_k': rwkv_r_k[layer],
            'rwkv_lnx_g': rwkv_lnx_g[layer], 'rwkv_lnx_b': rwkv_lnx_b[layer],
            'diff_lambda': diff_lambda[layer], 'diff_subln_g': diff_subln_g[layer],
            'proj_swa': proj_swa[layer], 'proj_rwkv': proj_rwkv[layer], 'proj_diff': proj_diff[layer],
            'w_out': w_out[layer],
        }
        mod_l = jnp.split((jax.nn.silu(c) @ ada_w[layer] + ada_b[layer])[:, None, :], 6, axis=-1)
        mod_c = jnp.split((jax.nn.silu(c_ctx) @ ada_w[layer] + ada_b[layer])[None, None, :], 6, axis=-1)

        a_l = modulate(rmsnorm(h_lat, pre_mix_g[layer]), mod_l[0], mod_l[1])
        a_c = modulate(rmsnorm(h_ctx, pre_mix_g[layer]), mod_c[0], mod_c[1])
        m_l, m_c = token_mixer(a_l, a_c, lp, layer, need_ctx, cos, sin)
        h_lat = h_lat + mod_l[2] * rmsnorm(m_l, post_mix_g[layer])
        if need_ctx:
            h_ctx = h_ctx + mod_c[2] * rmsnorm(m_c, post_mix_g[layer])

        f_l = modulate(rmsnorm(h_lat, pre_ffn_g[layer]), mod_l[3], mod_l[4]).reshape(B * S, D)
        if need_ctx:
            f_c = modulate(rmsnorm(h_ctx, pre_ffn_g[layer]), mod_c[3], mod_c[4]).reshape(B * L, D)
            toks = jnp.concatenate([f_l, f_c], axis=0)
        else:
            toks = f_l
        j = layer // 2
        if layer % 2 == 0:
            f = swiglu(toks, ffn_w_gate[j], ffn_w_up[j], ffn_w_down[j])
        else:
            f = moe_swiglu(toks, router_w[j], moe_w_gate[j], moe_w_up[j], moe_w_down[j])
        h_lat = h_lat + mod_l[5] * rmsnorm(f[:B * S].reshape(B, S, D), post_ffn_g[layer])
        if need_ctx:
            h_ctx = h_ctx + mod_c[5] * rmsnorm(f[B * S:].reshape(B, L, D), post_ffn_g[layer])
    return h_lat
```

```python
import functools
import math

import jax
import jax.numpy as jnp
from jax import lax
from jax.experimental import pallas as pl
from jax.experimental.pallas import tpu as pltpu

F32 = jnp.float32
BF16 = jnp.bfloat16
HI = lax.Precision.HIGHEST

HEAD_DIM = 64
GRID_W = 64
ROPE_THETA = 10000.0
NORM_EPS = 1e-6
NEG_INF = -1e30
SWA_HEADS = 8
SWA_KV_HEADS = 2
SWA_BLOCK = 128
RWKV_HEADS = 8
RWKV_WIDTH = RWKV_HEADS * HEAD_DIM
DECAY_LORA = 64
AAA_LORA = 64
GATE_LORA = 128
RWKV_GN_EPS = 64e-5
DIFF_HEADS = 4
DIFF_V_DIM = 2 * HEAD_DIM
DIFF_SUBLN_EPS = 1e-5
N_EXPERTS = 8
SWA_Q = SWA_HEADS * HEAD_DIM
SWA_KV = SWA_KV_HEADS * HEAD_DIM
RWKV_COLS = 3 * RWKV_WIDTH + 2 * DECAY_LORA + 2 * AAA_LORA + GATE_LORA
DIFF_QK = DIFF_HEADS * 2 * HEAD_DIM
DIFF_V = DIFF_HEADS * DIFF_V_DIM

LANES = 128
VMEM_LIMIT = 48 * 1024 * 1024
ROW_TILE = 512
RWKV_CHUNK = 64
RWKV_STEP_ROWS = 256
DIFF_TQ = 256
DIFF_TK = 256
MOE_TILE = 512
MOE_TF = 512
FFN_TF = 256
COMBINE_TILE = 256


def _cp(sem, **kw):
    return pltpu.CompilerParams(dimension_semantics=sem, vmem_limit_bytes=VMEM_LIMIT, **kw)


def _seg_of_block(i, lat_blocks_per_batch, n_batch):
    return jnp.minimum(i // lat_blocks_per_batch, n_batch)


def _rms(x, eps):
    return x * lax.rsqrt(jnp.mean(x * x, axis=-1, keepdims=True) + eps)


def _sigmoid(x):
    return 1.0 / (1.0 + jnp.exp(-x))


def _ada_kernel(x_ref, w_ref, b_ref, o_ref):
    x = x_ref[...]
    s = x * _sigmoid(x)
    o_ref[...] = jnp.dot(s, w_ref[...], precision=HI, preferred_element_type=F32) + b_ref[...]


def _ada(cond, w, b):
    rows, d = cond.shape
    n = w.shape[1]
    return pl.pallas_call(
        _ada_kernel,
        out_shape=jax.ShapeDtypeStruct((rows, n), F32),
        grid=(n // d,),
        in_specs=[pl.BlockSpec((rows, d), lambda j: (0, 0)),
                  pl.BlockSpec((d, d), lambda j: (0, j)),
                  pl.BlockSpec((1, d), lambda j: (0, j))],
        out_specs=pl.BlockSpec((rows, d), lambda j: (0, j)),
        compiler_params=_cp(("parallel",)),
        name="ada_mod",
    )(cond, w, b)


def _prenorm_kernel(h_ref, g_ref, mod_ref, o_ref, *, d, shift_idx, scale_idx):
    y = _rms(h_ref[...], NORM_EPS) * g_ref[...]
    shift = mod_ref[0, :, shift_idx * d:(shift_idx + 1) * d]
    scale = mod_ref[0, :, scale_idx * d:(scale_idx + 1) * d]
    o_ref[...] = (y * (1.0 + scale) + shift).astype(o_ref.dtype)


def _prenorm(h, g, mods, shift_idx, scale_idx, lat_bpb, n_batch):
    m, d = h.shape
    tm = ROW_TILE
    seg = functools.partial(_seg_of_block, lat_blocks_per_batch=lat_bpb, n_batch=n_batch)
    return pl.pallas_call(
        functools.partial(_prenorm_kernel, d=d, shift_idx=shift_idx, scale_idx=scale_idx),
        out_shape=jax.ShapeDtypeStruct((m, d), BF16),
        grid=(m // tm,),
        in_specs=[pl.BlockSpec((tm, d), lambda i: (i, 0)),
                  pl.BlockSpec((1, d), lambda i: (0, 0)),
                  pl.BlockSpec((1, 1, mods.shape[2]), lambda i: (seg(i), 0, 0))],
        out_specs=pl.BlockSpec((tm, d), lambda i: (i, 0)),
        compiler_params=_cp(("parallel",)),
        name="prenorm",
    )(h, g, mods)


def _proj_kernel(*refs, splits, rope_cols, scale_cols, q_scale):
    if rope_cols:
        a_ref, w_ref, cos_ref, sin_ref = refs[:4]
        outs = refs[4:]
    else:
        a_ref, w_ref = refs[:2]
        outs = refs[2:]
    y = jnp.dot(a_ref[...], w_ref[...], preferred_element_type=F32)
    tm, tn = y.shape
    if rope_cols:
        cos = cos_ref[...]
        sin = sin_ref[...]
        lane = lax.broadcasted_iota(jnp.int32, (tm, LANES), 1)
        first_half = (lane % 32) < 16
        pieces = []
        for c in range(tn // LANES):
            yc = y[:, c * LANES:(c + 1) * LANES]
            if c * LANES < rope_cols:
                partner = jnp.where(first_half, pltpu.roll(yc, LANES - 16, 1), pltpu.roll(yc, 16, 1))
                yc = yc * cos + partner * sin
            if c * LANES < scale_cols:
                yc = yc * q_scale
            pieces.append(yc)
        y = jnp.concatenate(pieces, axis=1)
    start = 0
    for o_ref, width in zip(outs, splits):
        o_ref[...] = y[:, start:start + width].astype(o_ref.dtype)
        start += width


def _proj(a, w, splits, dtypes, rope=None, rope_cols=0, scale_cols=0, tn=None):
    m, k = a.shape
    n = w.shape[1]
    tm = ROW_TILE
    tn = n if tn is None else tn
    assert sum(splits) == tn and (len(splits) == 1 or tn == n)
    in_specs = [pl.BlockSpec((tm, k), lambda i, j: (i, 0)),
                pl.BlockSpec((k, tn), lambda i, j: (0, j))]
    args = [a, w]
    if rope_cols:
        in_specs += [pl.BlockSpec((tm, LANES), lambda i, j: (i, 0))] * 2
        args += list(rope)
    out_specs = []
    out_shape = []
    if len(splits) == 1:
        out_specs.append(pl.BlockSpec((tm, tn), lambda i, j: (i, j)))
        out_shape.append(jax.ShapeDtypeStruct((m, n), dtypes[0]))
    else:
        for width, dt in zip(splits, dtypes):
            out_specs.append(pl.BlockSpec((tm, width), lambda i, j: (i, 0)))
            out_shape.append(jax.ShapeDtypeStruct((m, width), dt))
    res = pl.pallas_call(
        functools.partial(_proj_kernel, splits=tuple(splits), rope_cols=rope_cols, scale_cols=scale_cols,
                          q_scale=HEAD_DIM ** -0.5),
        out_shape=out_shape,
        grid=(m // tm, n // tn),
        in_specs=in_specs,
        out_specs=out_specs,
        compiler_params=_cp(("parallel", "parallel")),
        name="proj",
    )(*args)
    return res


def _swa_kernel(sink_ref, q_ref, kp_ref, kc_ref, kn_ref, kx_ref, o_ref, *, nb, n_lat_blocks):
    i = pl.program_id(0)
    is_lat = i < n_lat_blocks
    n = i % nb
    blk = SWA_BLOCK
    q = q_ref[...]
    kv = jnp.concatenate([kp_ref[...], kc_ref[...], kn_ref[...], kx_ref[...]], axis=0)
    nkeys = kv.shape[0]
    r = lax.broadcasted_iota(jnp.int32, (blk, nkeys), 0)
    j = lax.broadcasted_iota(jnp.int32, (blk, nkeys), 1)
    lo = jnp.where(n > 0, 0, blk)
    hi = jnp.where(n < nb - 1, 3 * blk, 2 * blk)
    valid_loc = (j >= r) & (j <= r + 2 * blk) & (j >= lo) & (j < hi) & is_lat
    valid = valid_loc | (j >= 3 * blk)
    group = SWA_HEADS // SWA_KV_HEADS
    outs = []
    for hk in range(SWA_KV_HEADS):
        k_h = kv[:, hk * HEAD_DIM:(hk + 1) * HEAD_DIM]
        v_h = kv[:, SWA_KV + hk * HEAD_DIM:SWA_KV + (hk + 1) * HEAD_DIM]
        for g in range(group):
            h = hk * group + g
            qh = q[:, h * HEAD_DIM:(h + 1) * HEAD_DIM]
            s = lax.dot_general(qh, k_h, (((1,), (1,)), ((), ())), preferred_element_type=F32)
            s = jnp.where(valid, s, NEG_INF)
            sk = sink_ref[h]
            mx = jnp.maximum(jnp.max(s, axis=-1, keepdims=True), sk)
            p = jnp.exp(s - mx)
            den = jnp.sum(p, axis=-1, keepdims=True) + jnp.exp(sk - mx)
            o = jnp.dot(p.astype(BF16), v_h, preferred_element_type=F32)
            outs.append(o * (1.0 / den))
    o_ref[...] = jnp.concatenate(outs, axis=1).astype(o_ref.dtype)


def _swa(q, kv, sink, n_batch, seq, ctx_len):
    m = q.shape[0]
    blk = SWA_BLOCK
    nb = seq // blk
    n_lat = n_batch * nb
    cpb = ctx_len // blk

    def batch_of(i):
        return jnp.where(i < n_lat, i // nb, (i - n_lat) // cpb)

    def prev_idx(i, s):
        return (jnp.where(i < n_lat, batch_of(i) * nb + jnp.maximum(i % nb - 1, 0), i), 0)

    def next_idx(i, s):
        return (jnp.where(i < n_lat, batch_of(i) * nb + jnp.minimum(i % nb + 1, nb - 1), i), 0)

    def ctx_idx(i, s):
        return (n_batch * seq // ctx_len + batch_of(i), 0)

    grid_spec = pltpu.PrefetchScalarGridSpec(
        num_scalar_prefetch=1,
        grid=(m // blk,),
        in_specs=[pl.BlockSpec((blk, SWA_Q), lambda i, s: (i, 0)),
                  pl.BlockSpec((blk, 2 * SWA_KV), prev_idx),
                  pl.BlockSpec((blk, 2 * SWA_KV), lambda i, s: (i, 0)),
                  pl.BlockSpec((blk, 2 * SWA_KV), next_idx),
                  pl.BlockSpec((ctx_len, 2 * SWA_KV), ctx_idx)],
        out_specs=pl.BlockSpec((blk, SWA_Q), lambda i, s: (i, 0)),
    )
    return pl.pallas_call(
        functools.partial(_swa_kernel, nb=nb, n_lat_blocks=n_lat),
        out_shape=jax.ShapeDtypeStruct((m, SWA_Q), BF16),
        grid_spec=grid_spec,
        compiler_params=_cp(("parallel",)),
        name="swa_attn",
    )(sink, q, kv, kv, kv, kv)


def _diff_kernel(lam_ref, g_ref, q_ref, kl_ref, vl_ref, kc_ref, vc_ref, o_ref, m_sc, l_sc, acc_sc,
                 *, n_lat_qblocks, n_lat_chunks, coef):
    i = pl.program_id(1)
    is_lat = i < n_lat_qblocks
    q = q_ref[...]
    tq = q.shape[0]
    lane = lax.broadcasted_iota(jnp.int32, q.shape, 1)
    zero = jnp.zeros_like(q)
    qq = jnp.concatenate([jnp.where(lane < HEAD_DIM, q, zero), jnp.where(lane >= HEAD_DIM, q, zero)], axis=0)

    def chunk(k, v):
        s = lax.dot_general(qq, k, (((1,), (1,)), ((), ())), preferred_element_type=F32)
        m_old = m_sc[...]
        m_new = jnp.maximum(m_old, jnp.max(s, axis=-1, keepdims=True))
        alpha = jnp.exp(m_old - m_new)
        p = jnp.exp(s - m_new)
        l_sc[...] = alpha * l_sc[...] + jnp.sum(p, axis=-1, keepdims=True)
        acc_sc[...] = alpha * acc_sc[...] + jnp.dot(p.astype(BF16), v, preferred_element_type=F32)
        m_sc[...] = m_new

    m_sc[...] = jnp.full(m_sc.shape, NEG_INF, F32)
    l_sc[...] = jnp.zeros(l_sc.shape, F32)
    acc_sc[...] = jnp.zeros(acc_sc.shape, F32)
    chunk(kc_ref[...], vc_ref[...])

    def body(c, carry):
        off = pl.multiple_of(c * DIFF_TK, DIFF_TK)
        chunk(kl_ref[pl.ds(off, DIFF_TK), :], vl_ref[pl.ds(off, DIFF_TK), :])
        return carry

    lax.fori_loop(0, jnp.where(is_lat, n_lat_chunks, 0), body, 0)
    o = acc_sc[...] * (1.0 / l_sc[...])
    od = o[:tq] - lam_ref[0] * o[tq:]
    y = _rms(od, DIFF_SUBLN_EPS) * g_ref[...] * coef
    o_ref[...] = y.astype(o_ref.dtype)


def _diff(q, k, v, lam, subln_g, lam_init, n_batch, seq, ctx_len):
    m = q.shape[0]
    tq = DIFF_TQ
    n_lat_q = n_batch * seq // tq
    qpb = seq // tq
    assert ctx_len == tq == DIFF_TK

    def batch_of(i):
        return jnp.where(i < n_lat_q, i // qpb, i - n_lat_q)

    grid_spec = pltpu.PrefetchScalarGridSpec(
        num_scalar_prefetch=1,
        grid=(DIFF_HEADS, m // tq),
        in_specs=[pl.BlockSpec((1, DIFF_V_DIM), lambda h, i, s: (0, 0)),
                  pl.BlockSpec((tq, LANES), lambda h, i, s: (i, h)),
                  pl.BlockSpec((seq, LANES), lambda h, i, s: (batch_of(i), h)),
                  pl.BlockSpec((seq, LANES), lambda h, i, s: (batch_of(i), h)),
                  pl.BlockSpec((ctx_len, LANES), lambda h, i, s: (n_batch * seq // ctx_len + batch_of(i), h)),
                  pl.BlockSpec((ctx_len, LANES), lambda h, i, s: (n_batch * seq // ctx_len + batch_of(i), h))],
        out_specs=pl.BlockSpec((tq, LANES), lambda h, i, s: (i, h)),
        scratch_shapes=[pltpu.VMEM((2 * tq, 1), F32), pltpu.VMEM((2 * tq, 1), F32),
                        pltpu.VMEM((2 * tq, DIFF_V_DIM), F32)],
    )
    return pl.pallas_call(
        functools.partial(_diff_kernel, n_lat_qblocks=n_lat_q, n_lat_chunks=seq // DIFF_TK,
                          coef=1.0 - lam_init),
        out_shape=jax.ShapeDtypeStruct((m, DIFF_V), BF16),
        grid_spec=grid_spec,
        compiler_params=_cp(("parallel", "arbitrary")),
        name="diff_attn",
    )(lam, subln_g, q, k, v, k, v)


def _rwkv_prep_kernel(p_ref, hp_ref, hn_ref, mup_ref, mun_ref, kk_w_ref, ka_ref, rk_ref, w0_ref, a0_ref,
                      wup_ref, aup_ref, gup_ref, ones_ref,
                      r_ref, v_ref, kk_ref, ld_ref, ke_ref, bb_ref, g_ref, bonus_ref, sc,
                      *, tm, lat_rows, seq, ctx_len):
    i = pl.program_id(0)
    w = RWKV_WIDTH
    sc[0:8, :] = hp_ref[...]
    sc[8:8 + tm, :] = p_ref[...]
    sc[8 + tm:16 + tm, :] = hn_ref[...]
    p = p_ref[...]
    prev = sc[7:7 + tm, :]
    nxt = sc[9:9 + tm, :]
    row = i * tm + lax.broadcasted_iota(jnp.int32, (tm, 1), 0)
    pos = jnp.where(row < lat_rows, row % seq, (row - lat_rows) % ctx_len)
    seg_len = jnp.where(row < lat_rows, seq, ctx_len)
    prev = jnp.where(pos == 0, 0.0, prev)
    nxt = jnp.where(pos == seg_len - 1, 0.0, nxt)
    ps = p + mup_ref[...] * (prev - p) + mun_ref[...] * (nxt - p)

    r = ps[:, 0:w]
    k = ps[:, w:2 * w]
    v = ps[:, 2 * w:3 * w]
    wd = ps[:, 3 * w:3 * w + 2 * DECAY_LORA]
    ad = ps[:, 3 * w + 2 * DECAY_LORA:3 * w + 2 * DECAY_LORA + 2 * AAA_LORA]
    gd = ps[:, 3 * w + 2 * DECAY_LORA + 2 * AAA_LORA:]

    ones_bd = ones_ref[...]
    g = jnp.dot(_sigmoid(gd).astype(BF16), gup_ref[...], preferred_element_type=F32)
    kk = k * kk_w_ref[...]
    ss = jnp.dot(kk * kk, ones_bd, precision=HI, preferred_element_type=F32)
    kk = kk / jnp.maximum(jnp.sqrt(ss), 1e-12)
    w_raw = w0_ref[...] + jnp.dot(jnp.tanh(wd).astype(BF16), wup_ref[...], preferred_element_type=F32)
    a_raw = a0_ref[...] + jnp.dot(ad.astype(BF16), aup_ref[...], preferred_element_type=F32)
    ld = -math.exp(-0.5) * _sigmoid(w_raw)
    a = _sigmoid(a_raw)
    ka = ka_ref[...]
    ke_sum = jnp.zeros_like(k)
    for d in range(2):
        a_d = a[:, d * w:(d + 1) * w]
        ke = k * (1.0 + (a_d - 1.0) * ka)
        ld_ref[d] = ld[:, d * w:(d + 1) * w]
        ke_ref[d] = ke
        bb_ref[d] = kk * a_d
        ke_sum = ke_sum + ke
    rk = jnp.dot(r * ke_sum * rk_ref[...], ones_bd, precision=HI, preferred_element_type=F32)
    r_ref[...] = r
    v_ref[...] = v
    kk_ref[...] = kk
    g_ref[...] = g
    bonus_ref[...] = rk * v


def _rwkv_prep(p, lp, n_batch, seq, ctx_len):
    m, cols = p.shape
    tm = 256
    w = RWKV_WIDTH
    lat_rows = n_batch * seq
    row = lambda a: a.reshape(1, -1).astype(F32)

    def blockdiag(u):
        z = jnp.zeros_like(u[0])
        return jnp.concatenate([jnp.concatenate([u[0], z], axis=1), jnp.concatenate([z, u[1]], axis=1)], axis=0)

    head = jnp.arange(w) // HEAD_DIM
    ones_bd = (head[:, None] == head[None, :]).astype(F32)
    full = lambda shape: pl.BlockSpec(shape, lambda i: (0,) * len(shape))
    nb8 = m // 8
    outs = pl.pallas_call(
        functools.partial(_rwkv_prep_kernel, tm=tm, lat_rows=lat_rows, seq=seq, ctx_len=ctx_len),
        out_shape=[jax.ShapeDtypeStruct((m, w), F32)] * 3
        + [jax.ShapeDtypeStruct((2, m, w), F32)] * 3
        + [jax.ShapeDtypeStruct((m, w), F32)] * 2,
        grid=(m // tm,),
        in_specs=[pl.BlockSpec((tm, cols), lambda i: (i, 0)),
                  pl.BlockSpec((8, cols), lambda i: (jnp.maximum(i * (tm // 8) - 1, 0), 0)),
                  pl.BlockSpec((8, cols), lambda i: (jnp.minimum((i + 1) * (tm // 8), nb8 - 1), 0)),
                  full((1, cols)), full((1, cols)), full((1, w)), full((1, w)), full((1, w)),
                  full((1, 2 * w)), full((1, 2 * w)),
                  full((2 * DECAY_LORA, 2 * w)), full((2 * AAA_LORA, 2 * w)), full((GATE_LORA, w)),
                  full((w, w))],
        out_specs=[pl.BlockSpec((tm, w), lambda i: (i, 0))] * 3
        + [pl.BlockSpec((2, tm, w), lambda i: (0, i, 0))] * 3
        + [pl.BlockSpec((tm, w), lambda i: (i, 0))] * 2,
        scratch_shapes=[pltpu.VMEM((tm + 16, cols), F32)],
        compiler_params=_cp(("parallel",)),
        name="rwkv_prep",
    )(p, p, p, row(lp['rwkv_mu_prev']), row(lp['rwkv_mu_next']), row(lp['rwkv_k_k']), row(lp['rwkv_k_a']),
      row(lp['rwkv_r_k']), row(lp['rwkv_w0']), row(lp['rwkv_a0']),
      blockdiag(lp['rwkv_w_up']).astype(BF16), blockdiag(lp['rwkv_a_up']).astype(BF16),
      lp['rwkv_g_up'].astype(BF16), ones_bd)
    return outs


def _rwkv_chunk(d, off, r_ref, v_ref, kk_ref, ld_ref, ke_ref, bb_ref, y_ref, s_sc):
    c = RWKV_CHUNK
    hd = HEAD_DIM
    ti = lax.broadcasted_iota(jnp.int32, (c, c), 0)
    tj = lax.broadcasted_iota(jnp.int32, (c, c), 1)
    incl_b = (tj <= ti) if d == 0 else (tj >= ti)
    incl = incl_b.astype(F32)
    strict = incl - (ti == tj).astype(F32)
    eye = (ti == tj).astype(F32)

    ld = ld_ref[0, pl.ds(off, c), :]
    r = r_ref[pl.ds(off, c), :]
    v = v_ref[pl.ds(off, c), :]
    kk = kk_ref[pl.ds(off, c), :]
    ke = ke_ref[0, pl.ds(off, c), :]
    bb = bb_ref[0, pl.ds(off, c), :]
    cum = jnp.dot(incl, ld, precision=HI, preferred_element_type=F32)
    tot = jnp.sum(ld, axis=0, keepdims=True)
    at_all = -kk * jnp.exp(cum - ld)
    rt_all = r * jnp.exp(cum)
    einv = jnp.exp(-cum)
    bt_all = bb * einv
    kt_all = ke * einv
    etail = jnp.exp(tot - cum)
    bh_all = bb * etail
    kh_all = ke * etail
    wtot = jnp.exp(tot)

    nt = (((1,), (1,)), ((), ()))
    tn = (((0,), (0,)), ((), ()))
    ys = []
    for h in range(RWKV_HEADS):
        sl = slice(h * hd, (h + 1) * hd)
        at, rt, bt, kt, bh, kh, vv = (x[:, sl] for x in (at_all, rt_all, bt_all, kt_all, bh_all, kh_all, v))
        gm = lax.dot_general(jnp.concatenate([at, rt], axis=0), jnp.concatenate([bt, kt], axis=0), nt,
                             precision=HI, preferred_element_type=F32)
        aab = gm[:c, :c] * strict
        aak = gm[:c, c:] * strict
        arb = gm[c:, :c] * incl
        ark = gm[c:, c:] * incl
        pw = aab
        tm_ = eye + aab
        for _ in range(int(math.log2(c)) - 1):
            pw = jnp.dot(pw, pw, precision=HI, preferred_element_type=F32)
            tm_ = tm_ + jnp.dot(tm_, pw, precision=HI, preferred_element_type=F32)
        av = jnp.dot(jnp.concatenate([aak, ark], axis=0), vv, precision=HI, preferred_element_type=F32)
        au = jnp.dot(tm_, jnp.concatenate([at, av[:c]], axis=1), precision=HI, preferred_element_type=F32)
        ry = jnp.dot(arb, au, precision=HI, preferred_element_type=F32)
        rbar = rt + ry[:, :hd]
        ybar = ry[:, hd:] + av[c:]
        mz = lax.dot_general(au, bh, tn, precision=HI, preferred_element_type=F32)
        mm = eye * wtot[:, sl] + mz[:hd]
        zz = mz[hd:] + lax.dot_general(vv, kh, tn, precision=HI, preferred_element_type=F32)
        s = s_sc[d, h]
        ys.append(lax.dot_general(rbar, s, nt, precision=HI, preferred_element_type=F32) + ybar)
        s_sc[d, h] = jnp.dot(s, mm, precision=HI, preferred_element_type=F32) + zz
    y_ref[0, pl.ds(off, c), :] = jnp.concatenate(ys, axis=1)


def _rwkv_scan_kernel(rf, vf, kf, ldf, kef, bbf, rb, vb, kb, ldb, keb, bbb, yf, yb, s_sc, *, n_chunks):
    @pl.when(pl.program_id(1) == 0)
    def _():
        s_sc[...] = jnp.zeros(s_sc.shape, F32)

    def body(cc, carry):
        off_f = pl.multiple_of(cc * RWKV_CHUNK, RWKV_CHUNK)
        _rwkv_chunk(0, off_f, rf, vf, kf, ldf, kef, bbf, yf, s_sc)
        off_b = pl.multiple_of((n_chunks - 1 - cc) * RWKV_CHUNK, RWKV_CHUNK)
        _rwkv_chunk(1, off_b, rb, vb, kb, ldb, keb, bbb, yb, s_sc)
        return carry

    lax.fori_loop(0, n_chunks, body, 0)


def _rwkv_scan(r, v, kk, ld, ke, bb, n_batch, seq, ctx_len):
    m, w = r.shape
    ts = RWKV_STEP_ROWS
    assert ctx_len == ts
    lpb = seq // ts
    ctx0 = n_batch * seq // ts
    nj = 1 + lpb

    def fwd(b, j):
        return jnp.where(j == 0, ctx0 + b, b * lpb + j - 1)

    def bwd(b, j):
        return jnp.where(j == 0, ctx0 + b, b * lpb + lpb - j)

    shared = lambda f: pl.BlockSpec((ts, w), lambda b, j: (f(b, j), 0))
    per_dir = lambda f, d: pl.BlockSpec((1, ts, w), lambda b, j: (d, f(b, j), 0))
    y = pl.pallas_call(
        functools.partial(_rwkv_scan_kernel, n_chunks=ts // RWKV_CHUNK),
        out_shape=[jax.ShapeDtypeStruct((1, m, w), F32)] * 2,
        grid=(n_batch, nj),
        in_specs=[shared(fwd), shared(fwd), shared(fwd), per_dir(fwd, 0), per_dir(fwd, 0), per_dir(fwd, 0),
                  shared(bwd), shared(bwd), shared(bwd), per_dir(bwd, 1), per_dir(bwd, 1), per_dir(bwd, 1)],
        out_specs=[pl.BlockSpec((1, ts, w), lambda b, j: (0, fwd(b, j), 0)),
                   pl.BlockSpec((1, ts, w), lambda b, j: (0, bwd(b, j), 0))],
        scratch_shapes=[pltpu.VMEM((2, RWKV_HEADS, HEAD_DIM, HEAD_DIM), F32)],
        compiler_params=_cp(("parallel", "arbitrary")),
        name="rwkv_scan",
    )(r, v, kk, ld, ke, bb, r, v, kk, ld, ke, bb)
    return y


def _rwkv_out_kernel(yf_ref, yb_ref, bonus_ref, g_ref, lg_ref, lb_ref, ones_ref, o_ref):
    y = yf_ref[0] + yb_ref[0]
    ones_bd = ones_ref[...]
    inv = 1.0 / HEAD_DIM
    mu = jnp.dot(y, ones_bd, precision=HI, preferred_element_type=F32) * inv
    yc = y - mu
    var = jnp.dot(yc * yc, ones_bd, precision=HI, preferred_element_type=F32) * inv
    yn = yc * lax.rsqrt(var + RWKV_GN_EPS) * lg_ref[...] + lb_ref[...]
    o_ref[...] = ((yn + bonus_ref[...]) * g_ref[...]).astype(o_ref.dtype)


def _rwkv_out(yf, yb, bonus, g, lnx_g, lnx_b):
    m, w = bonus.shape
    tm = ROW_TILE
    head = jnp.arange(w) // HEAD_DIM
    ones_bd = (head[:, None] == head[None, :]).astype(F32)
    blk = pl.BlockSpec((tm, w), lambda i: (i, 0))
    blk3 = pl.BlockSpec((1, tm, w), lambda i: (0, i, 0))
    one = pl.BlockSpec((1, w), lambda i: (0, 0))
    return pl.pallas_call(
        _rwkv_out_kernel,
        out_shape=jax.ShapeDtypeStruct((m, w), BF16),
        grid=(m // tm,),
        in_specs=[blk3, blk3, blk, blk, one, one, pl.BlockSpec((w, w), lambda i: (0, 0))],
        out_specs=blk,
        compiler_params=_cp(("parallel",)),
        name="rwkv_out",
    )(yf, yb, bonus, g, lnx_g.reshape(1, w), lnx_b.reshape(1, w), ones_bd)


def _merge_kernel(ya_ref, yb_ref, yc_ref, gt_ref, h_ref, pa_ref, pb_ref, pc_ref, wo_ref, gpost_ref, gpre_ref,
                  mod_ref, h_out, f_out, *, d):
    gates = gt_ref[...].astype(F32)
    merged = (_sigmoid(gates[:, 0:d]) * jnp.dot(ya_ref[...], pa_ref[...], preferred_element_type=F32)
              + _sigmoid(gates[:, d:2 * d]) * jnp.dot(yb_ref[...], pb_ref[...], preferred_element_type=F32)
              + _sigmoid(gates[:, 2 * d:3 * d]) * jnp.dot(yc_ref[...], pc_ref[...], preferred_element_type=F32))
    out = jnp.dot(merged.astype(BF16), wo_ref[...], preferred_element_type=F32)
    mod = lambda idx: mod_ref[0, :, idx * d:(idx + 1) * d]
    hn = h_ref[...] + mod(2) * (_rms(out, NORM_EPS) * gpost_ref[...])
    h_out[...] = hn
    f = (_rms(hn, NORM_EPS) * gpre_ref[...]) * (1.0 + mod(4)) + mod(3)
    f_out[...] = f.astype(f_out.dtype)


def _merge(ya, yb, yc, gates, h, pa, pb, pc, wo, g_post, g_pre, mods, f_dtype, lat_bpb, n_batch):
    m, d = h.shape
    tm = 256
    bpb = lat_bpb * (ROW_TILE // tm)
    seg = functools.partial(_seg_of_block, lat_blocks_per_batch=bpb, n_batch=n_batch)
    rows = lambda width: pl.BlockSpec((tm, width), lambda i: (i, 0))
    full = lambda a: pl.BlockSpec(a.shape, lambda i: (0, 0))
    return pl.pallas_call(
        functools.partial(_merge_kernel, d=d),
        out_shape=[jax.ShapeDtypeStruct((m, d), F32), jax.ShapeDtypeStruct((m, d), f_dtype)],
        grid=(m // tm,),
        in_specs=[rows(ya.shape[1]), rows(yb.shape[1]), rows(yc.shape[1]), rows(gates.shape[1]), rows(d),
                  full(pa), full(pb), full(pc), full(wo),
                  pl.BlockSpec((1, d), lambda i: (0, 0)), pl.BlockSpec((1, d), lambda i: (0, 0)),
                  pl.BlockSpec((1, 1, mods.shape[2]), lambda i: (seg(i), 0, 0))],
        out_specs=[rows(d), rows(d)],
        compiler_params=_cp(("parallel",)),
        name="merge",
    )(ya, yb, yc, gates, h, pa, pb, pc, wo, g_post, g_pre, mods)


def _swiglu_hidden(x, wg, wu):
    hg = jnp.dot(x, wg, preferred_element_type=F32)
    hu = jnp.dot(x, wu, preferred_element_type=F32)
    return (hg * _sigmoid(hg) * hu).astype(BF16)


def _ffn_kernel(f_ref, wg_ref, wu_ref, wd_ref, h_ref, gpost_ref, mod_ref, o_ref, acc, *, d):
    j = pl.program_id(1)

    @pl.when(j == 0)
    def _():
        acc[...] = jnp.zeros(acc.shape, F32)

    hid = _swiglu_hidden(f_ref[...], wg_ref[...], wu_ref[...])
    acc[...] += jnp.dot(hid, wd_ref[...], preferred_element_type=F32)

    @pl.when(j == pl.num_programs(1) - 1)
    def _():
        gate = mod_ref[0, :, 5 * d:6 * d]
        o_ref[...] = h_ref[...] + gate * (_rms(acc[...], NORM_EPS) * gpost_ref[...])


def _ffn(f, wg, wu, wd, h, g_post, mods, lat_bpb, n_batch):
    m, d = h.shape
    ff = wg.shape[1]
    tm, tf = ROW_TILE, FFN_TF
    seg = functools.partial(_seg_of_block, lat_blocks_per_batch=lat_bpb, n_batch=n_batch)
    return pl.pallas_call(
        functools.partial(_ffn_kernel, d=d),
        out_shape=jax.ShapeDtypeStruct((m, d), F32),
        grid=(m // tm, ff // tf),
        in_specs=[pl.BlockSpec((tm, d), lambda i, j: (i, 0)),
                  pl.BlockSpec((d, tf), lambda i, j: (0, j)),
                  pl.BlockSpec((d, tf), lambda i, j: (0, j)),
                  pl.BlockSpec((tf, d), lambda i, j: (j, 0)),
                  pl.BlockSpec((tm, d), lambda i, j: (i, 0)),
                  pl.BlockSpec((1, d), lambda i, j: (0, 0)),
                  pl.BlockSpec((1, 1, mods.shape[2]), lambda i, j: (seg(i), 0, 0))],
        out_specs=pl.BlockSpec((tm, d), lambda i, j: (i, 0)),
        scratch_shapes=[pltpu.VMEM((tm, d), F32)],
        compiler_params=_cp(("parallel", "arbitrary")),
        name="ffn_dense",
    )(f, wg, wu, wd, h, g_post, mods)


def _router_kernel(f_ref, w_ref, idx_ref, wt_ref):
    logits = jnp.dot(f_ref[...], w_ref[...], precision=HI, preferred_element_type=F32)
    lane = lax.broadcasted_iota(jnp.int32, logits.shape, 1)
    logits = jnp.where(lane < N_EXPERTS, logits, -jnp.inf)
    m1 = jnp.max(logits, axis=-1, keepdims=True)
    i1 = jnp.min(jnp.where(logits == m1, lane, LANES), axis=-1, keepdims=True)
    rest = jnp.where(lane == i1, -jnp.inf, logits)
    m2 = jnp.max(rest, axis=-1, keepdims=True)
    i2 = jnp.min(jnp.where(rest == m2, lane, LANES), axis=-1, keepdims=True)
    e = jnp.exp(m2 - m1)
    w1 = 1.0 / (1.0 + e)
    w2 = e / (1.0 + e)
    idx_ref[...] = jnp.where(lane == 0, i1, jnp.where(lane == 1, i2, 0))
    wt_ref[...] = jnp.where(lane == 0, w1, jnp.where(lane == 1, w2, 0.0))


def _router(f, router_w, n_rows):
    d = f.shape[1]
    tm = ROW_TILE
    w_pad = jnp.zeros((d, LANES), F32).at[:, :N_EXPERTS].set(router_w)
    return pl.pallas_call(
        _router_kernel,
        out_shape=[jax.ShapeDtypeStruct((n_rows, LANES), jnp.int32), jax.ShapeDtypeStruct((n_rows, LANES), F32)],
        grid=(n_rows // tm,),
        in_specs=[pl.BlockSpec((tm, d), lambda i: (i, 0)), pl.BlockSpec((d, LANES), lambda i: (0, 0))],
        out_specs=[pl.BlockSpec((tm, LANES), lambda i: (i, 0))] * 2,
        compiler_params=_cp(("parallel",)),
        name="router",
    )(f, w_pad)


def _moe_gather_copy(f_hbm, xbuf, sem, src_row, dst_row):
    return pltpu.make_async_copy(f_hbm.at[pl.ds(src_row, 1), :], xbuf.at[pl.ds(dst_row, 1), :], sem)


def _moe_ffn_kernel(blk_e_ref, nused_ref, tok_ref, f_hbm, wg_ref, wu_ref, wd_ref, y_ref, xbuf, xb, acc, sem,
                    *, tm):
    i = pl.program_id(0)
    j = pl.program_id(1)
    active = i < nused_ref[0]

    @pl.when(active & (j == 0))
    def _():
        def issue(r, carry):
            _moe_gather_copy(f_hbm, xbuf, sem, tok_ref[i * tm + r], r).start()
            return carry

        lax.fori_loop(0, tm, issue, 0)
        pltpu.make_async_copy(f_hbm.at[pl.ds(0, tm), :], xbuf, sem).wait()
        xb[...] = xbuf[...].astype(BF16)
        acc[...] = jnp.zeros(acc.shape, F32)

    @pl.when(active)
    def _():
        hid = _swiglu_hidden(xb[...], wg_ref[0], wu_ref[0])
        acc[...] += jnp.dot(hid, wd_ref[0], preferred_element_type=F32)

    @pl.when(j == pl.num_programs(1) - 1)
    def _():
        y_ref[...] = acc[...]


def _moe_ffn(f, blk_e, nused, tok, wg, wu, wd, n_blocks):
    d = f.shape[1]
    ff = wg.shape[2]
    tm, tf = MOE_TILE, MOE_TF

    def e_of(i, be, nu):
        return be[jnp.minimum(i, nu[0] - 1)]

    grid_spec = pltpu.PrefetchScalarGridSpec(
        num_scalar_prefetch=3,
        grid=(n_blocks, ff // tf),
        in_specs=[pl.BlockSpec(memory_space=pl.ANY),
                  pl.BlockSpec((1, d, tf), lambda i, j, be, nu, tk: (e_of(i, be, nu), 0, j)),
                  pl.BlockSpec((1, d, tf), lambda i, j, be, nu, tk: (e_of(i, be, nu), 0, j)),
                  pl.BlockSpec((1, tf, d), lambda i, j, be, nu, tk: (e_of(i, be, nu), j, 0))],
        out_specs=pl.BlockSpec((tm, d), lambda i, j, be, nu, tk: (i, 0)),
        scratch_shapes=[pltpu.VMEM((tm, d), F32), pltpu.VMEM((tm, d), BF16), pltpu.VMEM((tm, d), F32),
                        pltpu.SemaphoreType.DMA(())],
    )
    return pl.pallas_call(
        functools.partial(_moe_ffn_kernel, tm=tm),
        out_shape=jax.ShapeDtypeStruct((n_blocks * tm, d), F32),
        grid_spec=grid_spec,
        compiler_params=_cp(("arbitrary", "arbitrary")),
        name="moe_ffn",
    )(blk_e, nused, tok, f, wg, wu, wd)


def _moe_combine_kernel(p0_ref, p1_ref, y_hbm, wt_ref, h_ref, gpost_ref, mod_ref, o_ref, b0, b1, sem, *, tm, d):
    i = pl.program_id(0)

    def issue(r, carry):
        pltpu.make_async_copy(y_hbm.at[pl.ds(p0_ref[i * tm + r], 1), :], b0.at[pl.ds(r, 1), :], sem.at[0]).start()
        pltpu.make_async_copy(y_hbm.at[pl.ds(p1_ref[i * tm + r], 1), :], b1.at[pl.ds(r, 1), :], sem.at[1]).start()
        return carry

    lax.fori_loop(0, tm, issue, 0)
    pltpu.make_async_copy(y_hbm.at[pl.ds(0, tm), :], b0, sem.at[0]).wait()
    pltpu.make_async_copy(y_hbm.at[pl.ds(0, tm), :], b1, sem.at[1]).wait()
    wt = wt_ref[...]
    y = b0[...] * wt[:, 0:1] + b1[...] * wt[:, 1:2]
    gate = mod_ref[0, :, 5 * d:6 * d]
    o_ref[...] = h_ref[...] + gate * (_rms(y, NORM_EPS) * gpost_ref[...])


def _moe_combine(pos0, pos1, y, wt, h, g_post, mods, n_rows, rows_per_batch, n_batch):
    d = h.shape[1]
    tm = COMBINE_TILE
    seg = functools.partial(_seg_of_block, lat_blocks_per_batch=rows_per_batch // tm, n_batch=n_batch)
    grid_spec = pltpu.PrefetchScalarGridSpec(
        num_scalar_prefetch=2,
        grid=(n_rows // tm,),
        in_specs=[pl.BlockSpec(memory_space=pl.ANY),
                  pl.BlockSpec((tm, LANES), lambda i, a, b: (i, 0)),
                  pl.BlockSpec((tm, d), lambda i, a, b: (i, 0)),
                  pl.BlockSpec((1, d), lambda i, a, b: (0, 0)),
                  pl.BlockSpec((1, 1, mods.shape[2]), lambda i, a, b: (seg(i), 0, 0))],
        out_specs=pl.BlockSpec((tm, d), lambda i, a, b: (i, 0)),
        scratch_shapes=[pltpu.VMEM((tm, d), F32), pltpu.VMEM((tm, d), F32), pltpu.SemaphoreType.DMA((2,))],
    )
    return pl.pallas_call(
        functools.partial(_moe_combine_kernel, tm=tm, d=d),
        out_shape=jax.ShapeDtypeStruct((n_rows, d), F32),
        grid_spec=grid_spec,
        compiler_params=_cp(("arbitrary",)),
        name="moe_combine",
    )(pos0, pos1, y, wt, h, g_post, mods)


def _moe_slots(top_i, tile):
    n = top_i.shape[0]
    a = n * 2
    e_flat = top_i.reshape(a)
    onehot = (e_flat[:, None] == jnp.arange(N_EXPERTS, dtype=jnp.int32)[None, :]).astype(jnp.int32)
    csum = jnp.cumsum(onehot, axis=0)
    rank = jnp.sum(csum * onehot, axis=1) - 1
    counts = csum[-1]
    padded = (counts + tile - 1) // tile * tile
    pends = jnp.cumsum(padded)
    pstarts = pends - padded
    dest = (jnp.sum(onehot * pstarts[None, :], axis=1) + rank).astype(jnp.int32)
    n_blocks = a // tile + N_EXPERTS
    tok = jnp.zeros((n_blocks * tile,), jnp.int32).at[dest].set(jnp.arange(a, dtype=jnp.int32) // 2)
    block_start = jnp.arange(n_blocks, dtype=jnp.int32) * tile
    blk_e = jnp.minimum(jnp.searchsorted(pends, block_start, side='right'), N_EXPERTS - 1).astype(jnp.int32)
    nused = (pends[-1:] // tile).astype(jnp.int32)
    return tok, blk_e, nused, dest.reshape(n, 2), n_blocks


def _rope_tables(n_batch, seq, ctx_len):
    t = jnp.arange(seq, dtype=jnp.int32)
    row = (t // GRID_W).astype(F32)
    col = (t % GRID_W).astype(F32)
    axis_dim = HEAD_DIM // 2
    inv_freq = ROPE_THETA ** (-jnp.arange(0, axis_dim, 2, dtype=F32) / axis_dim)
    dd = jnp.arange(LANES) % HEAD_DIM
    pos = jnp.where((dd // axis_dim)[None, :] == 0, row[:, None], col[:, None])
    ang = pos * inv_freq[dd % (axis_dim // 2)][None, :]
    cos = jnp.cos(ang)
    sin = jnp.where(((dd % axis_dim) < axis_dim // 2)[None, :], -jnp.sin(ang), jnp.sin(ang))
    n_ctx = n_batch * ctx_len
    cos = jnp.concatenate([jnp.tile(cos, (n_batch, 1)), jnp.ones((n_ctx, LANES), F32)], axis=0)
    sin = jnp.concatenate([jnp.tile(sin, (n_batch, 1)), jnp.zeros((n_ctx, LANES), F32)], axis=0)
    return cos, sin


def kernel(x, c, ctx, c_ctx, ada_w, ada_b, pre_mix_g, post_mix_g, pre_ffn_g, post_ffn_g, w_in, swa_sink,
           rwkv_mu_prev, rwkv_mu_next, rwkv_w0, rwkv_w_up, rwkv_a0, rwkv_a_up, rwkv_g_up, rwkv_k_k, rwkv_k_a,
           rwkv_r_k, rwkv_lnx_g, rwkv_lnx_b, diff_lambda, diff_subln_g, proj_swa, proj_rwkv, proj_diff, w_out,
           ffn_w_gate, ffn_w_up, ffn_w_down, router_w, moe_w_gate, moe_w_up, moe_w_down):
    n_batch, seq, d = x.shape
    ctx_len = ctx.shape[1]
    depth = w_in.shape[0]
    lat_rows = n_batch * seq
    lat_bpb = seq // ROW_TILE
    assert seq % ROW_TILE == 0 and (n_batch * ctx_len) % ROW_TILE == 0

    h = jnp.concatenate([x.reshape(lat_rows, d), ctx.reshape(n_batch * ctx_len, d)], axis=0)
    m = h.shape[0]
    cond = jnp.zeros((8, d), F32).at[:n_batch].set(c).at[n_batch].set(c_ctx)
    rope = _rope_tables(n_batch, seq, ctx_len)
    row = lambda a: a.reshape(1, -1)

    o_swa = 0
    o_rwkv = o_swa + SWA_Q + 2 * SWA_KV
    o_diff = o_rwkv + RWKV_COLS
    o_gate = o_diff + 2 * DIFF_QK + DIFF_V
    o_end = o_gate + 3 * d

    for layer in range(depth):
        mods = _ada(cond, ada_w[layer], ada_b[layer].reshape(1, -1))[:, None, :]
        wl = w_in[layer].astype(BF16)
        a = _prenorm(h, row(pre_mix_g[layer]), mods, 0, 1, lat_bpb, n_batch)

        q_swa, kv_swa = _proj(a, wl[:, o_swa:o_rwkv], (SWA_Q, 2 * SWA_KV), (BF16, BF16), rope=rope,
                              rope_cols=SWA_Q + SWA_KV, scale_cols=SWA_Q)
        (p_rwkv,) = _proj(a, wl[:, o_rwkv:o_diff], (RWKV_COLS,), (F32,))
        q_diff, k_diff, v_diff = _proj(a, wl[:, o_diff:o_gate], (DIFF_QK, DIFF_QK, DIFF_V), (BF16,) * 3,
                                       rope=rope, rope_cols=2 * DIFF_QK, scale_cols=DIFF_QK)
        (gates,) = _proj(a, wl[:, o_gate:o_end], (d,), (BF16,), tn=d)

        ya = _swa(q_swa, kv_swa, swa_sink[layer].astype(F32), n_batch, seq, ctx_len)

        lp = {'rwkv_mu_prev': rwkv_mu_prev[layer], 'rwkv_mu_next': rwkv_mu_next[layer],
              'rwkv_w0': rwkv_w0[layer], 'rwkv_w_up': rwkv_w_up[layer], 'rwkv_a0': rwkv_a0[layer],
              'rwkv_a_up': rwkv_a_up[layer], 'rwkv_g_up': rwkv_g_up[layer], 'rwkv_k_k': rwkv_k_k[layer],
              'rwkv_k_a': rwkv_k_a[layer], 'rwkv_r_k': rwkv_r_k[layer]}
        r_, v_, kk_, ld_, ke_, bb_, g_, bonus_ = _rwkv_prep(p_rwkv, lp, n_batch, seq, ctx_len)
        y_f, y_b = _rwkv_scan(r_, v_, kk_, ld_, ke_, bb_, n_batch, seq, ctx_len)
        yb = _rwkv_out(y_f, y_b, bonus_, g_, rwkv_lnx_g[layer], rwkv_lnx_b[layer])

        lam_vec = diff_lambda[layer].astype(F32)
        lam_init = 0.8 - 0.6 * math.exp(-0.3 * layer)
        lam = (jnp.exp(jnp.sum(lam_vec[0] * lam_vec[1])) - jnp.exp(jnp.sum(lam_vec[2] * lam_vec[3]))
               + lam_init).reshape(1)
        yc = _diff(q_diff, k_diff, v_diff, lam, row(diff_subln_g[layer]), lam_init, n_batch, seq, ctx_len)

        moe_layer = layer % 2 == 1
        jj = layer // 2
        h, f = _merge(ya, yb, yc, gates, h, proj_swa[layer].astype(BF16), proj_rwkv[layer].astype(BF16),
                      proj_diff[layer].astype(BF16), w_out[layer].astype(BF16), row(post_mix_g[layer]),
                      row(pre_ffn_g[layer]), mods, F32 if moe_layer else BF16, lat_bpb, n_batch)
        need_ctx = layer < depth - 1
        if not moe_layer:
            h = _ffn(f, ffn_w_gate[jj].astype(BF16), ffn_w_up[jj].astype(BF16), ffn_w_down[jj].astype(BF16),
                     h, row(post_ffn_g[layer]), mods, lat_bpb, n_batch)
        else:
            n_tok = m if need_ctx else lat_rows
            top_i, top_w = _router(f, router_w[jj], n_tok)
            tok, blk_e, nused, dest, n_blocks = _moe_slots(top_i[:, :2], MOE_TILE)
            y = _moe_ffn(f, blk_e, nused, tok, moe_w_gate[jj].astype(BF16), moe_w_up[jj].astype(BF16),
                         moe_w_down[jj].astype(BF16), n_blocks)
            h = _moe_combine(dest[:, 0], dest[:, 1], y, top_w, h, row(post_ffn_g[layer]), mods, n_tok, seq,
                             n_batch)
    return h[:lat_rows].reshape(n_batch, seq, d)
```

```python
import functools
import math

import jax
import jax.numpy as jnp
from jax import lax
from jax.experimental import pallas as pl
from jax.experimental.pallas import tpu as pltpu

F32 = jnp.float32
BF16 = jnp.bfloat16
HI = lax.Precision.HIGHEST

HEAD_DIM = 64
GRID_W = 64
ROPE_THETA = 10000.0
NORM_EPS = 1e-6
NEG_INF = -1e30
SWA_HEADS = 8
SWA_KV_HEADS = 2
SWA_BLOCK = 128
RWKV_HEADS = 8
RWKV_WIDTH = RWKV_HEADS * HEAD_DIM
DECAY_LORA = 64
AAA_LORA = 64
GATE_LORA = 128
RWKV_GN_EPS = 64e-5
DIFF_HEADS = 4
DIFF_V_DIM = 2 * HEAD_DIM
DIFF_SUBLN_EPS = 1e-5
N_EXPERTS = 8
SWA_Q = SWA_HEADS * HEAD_DIM
SWA_KV = SWA_KV_HEADS * HEAD_DIM
RWKV_COLS = 3 * RWKV_WIDTH + 2 * DECAY_LORA + 2 * AAA_LORA + GATE_LORA
DIFF_QK = DIFF_HEADS * 2 * HEAD_DIM
DIFF_V = DIFF_HEADS * DIFF_V_DIM

LANES = 128
VMEM_LIMIT = 48 * 1024 * 1024
ROW_TILE = 512
RWKV_CHUNK = 64
RWKV_INV_BASE = 8
RWKV_STEP_ROWS = 256
DIFF_TQ = 256
DIFF_TK = 256
MOE_TILE = 512
MOE_TF = 512
FFN_TF = 256
COMBINE_TILE = 256


def _cp(sem, **kw):
    return pltpu.CompilerParams(dimension_semantics=sem, vmem_limit_bytes=VMEM_LIMIT, **kw)


def _seg_of_block(i, lat_blocks_per_batch, n_batch):
    return jnp.minimum(i // lat_blocks_per_batch, n_batch)


def _rms(x, eps):
    return x * lax.rsqrt(jnp.mean(x * x, axis=-1, keepdims=True) + eps)


def _sigmoid(x):
    return 1.0 / (1.0 + jnp.exp(-x))


def _ada_kernel(x_ref, w_ref, b_ref, o_ref):
    x = x_ref[...]
    s = x * _sigmoid(x)
    o_ref[...] = jnp.dot(s, w_ref[...], precision=HI, preferred_element_type=F32) + b_ref[...]


def _ada(cond, w, b):
    rows, d = cond.shape
    n = w.shape[1]
    return pl.pallas_call(
        _ada_kernel,
        out_shape=jax.ShapeDtypeStruct((rows, n), F32),
        grid=(n // d,),
        in_specs=[pl.BlockSpec((rows, d), lambda j: (0, 0)),
                  pl.BlockSpec((d, d), lambda j: (0, j)),
                  pl.BlockSpec((1, d), lambda j: (0, j))],
        out_specs=pl.BlockSpec((rows, d), lambda j: (0, j)),
        compiler_params=_cp(("parallel",)),
        name="ada_mod",
    )(cond, w, b)


def _prenorm_kernel(h_ref, g_ref, mod_ref, o_ref, *, d, shift_idx, scale_idx):
    y = _rms(h_ref[...], NORM_EPS) * g_ref[...]
    shift = mod_ref[0, :, shift_idx * d:(shift_idx + 1) * d]
    scale = mod_ref[0, :, scale_idx * d:(scale_idx + 1) * d]
    o_ref[...] = (y * (1.0 + scale) + shift).astype(o_ref.dtype)


def _prenorm(h, g, mods, shift_idx, scale_idx, lat_bpb, n_batch):
    m, d = h.shape
    tm = ROW_TILE
    seg = functools.partial(_seg_of_block, lat_blocks_per_batch=lat_bpb, n_batch=n_batch)
    return pl.pallas_call(
        functools.partial(_prenorm_kernel, d=d, shift_idx=shift_idx, scale_idx=scale_idx),
        out_shape=jax.ShapeDtypeStruct((m, d), BF16),
        grid=(m // tm,),
        in_specs=[pl.BlockSpec((tm, d), lambda i: (i, 0)),
                  pl.BlockSpec((1, d), lambda i: (0, 0)),
                  pl.BlockSpec((1, 1, mods.shape[2]), lambda i: (seg(i), 0, 0))],
        out_specs=pl.BlockSpec((tm, d), lambda i: (i, 0)),
        compiler_params=_cp(("parallel",)),
        name="prenorm",
    )(h, g, mods)


def _proj_kernel(*refs, splits, rope_cols, scale_cols, q_scale):
    if rope_cols:
        a_ref, w_ref, cos_ref, sin_ref = refs[:4]
        outs = refs[4:]
    else:
        a_ref, w_ref = refs[:2]
        outs = refs[2:]
    y = jnp.dot(a_ref[...], w_ref[...], preferred_element_type=F32)
    tm, tn = y.shape
    if rope_cols:
        cos = cos_ref[...]
        sin = sin_ref[...]
        lane = lax.broadcasted_iota(jnp.int32, (tm, LANES), 1)
        first_half = (lane % 32) < 16
        pieces = []
        for c in range(tn // LANES):
            yc = y[:, c * LANES:(c + 1) * LANES]
            if c * LANES < rope_cols:
                partner = jnp.where(first_half, pltpu.roll(yc, LANES - 16, 1), pltpu.roll(yc, 16, 1))
                yc = yc * cos + partner * sin
            if c * LANES < scale_cols:
                yc = yc * q_scale
            pieces.append(yc)
        y = jnp.concatenate(pieces, axis=1)
    start = 0
    for o_ref, width in zip(outs, splits):
        o_ref[...] = y[:, start:start + width].astype(o_ref.dtype)
        start += width


def _proj(a, w, splits, dtypes, rope=None, rope_cols=0, scale_cols=0, tn=None):
    m, k = a.shape
    n = w.shape[1]
    tm = ROW_TILE
    tn = n if tn is None else tn
    assert sum(splits) == tn and (len(splits) == 1 or tn == n)
    in_specs = [pl.BlockSpec((tm, k), lambda i, j: (i, 0)),
                pl.BlockSpec((k, tn), lambda i, j: (0, j))]
    args = [a, w]
    if rope_cols:
        in_specs += [pl.BlockSpec((tm, LANES), lambda i, j: (i, 0))] * 2
        args += list(rope)
    out_specs = []
    out_shape = []
    if len(splits) == 1:
        out_specs.append(pl.BlockSpec((tm, tn), lambda i, j: (i, j)))
        out_shape.append(jax.ShapeDtypeStruct((m, n), dtypes[0]))
    else:
        for width, dt in zip(splits, dtypes):
            out_specs.append(pl.BlockSpec((tm, width), lambda i, j: (i, 0)))
            out_shape.append(jax.ShapeDtypeStruct((m, width), dt))
    res = pl.pallas_call(
        functools.partial(_proj_kernel, splits=tuple(splits), rope_cols=rope_cols, scale_cols=scale_cols,
                          q_scale=HEAD_DIM ** -0.5),
        out_shape=out_shape,
        grid=(m // tm, n // tn),
        in_specs=in_specs,
        out_specs=out_specs,
        compiler_params=_cp(("parallel", "parallel")),
        name="proj",
    )(*args)
    return res


def _swa_kernel(sink_ref, q_ref, kp_ref, kc_ref, kn_ref, kx_ref, o_ref, *, nb, n_lat_blocks):
    i = pl.program_id(0)
    is_lat = i < n_lat_blocks
    n = i % nb
    blk = SWA_BLOCK
    q = q_ref[...]
    kv = jnp.concatenate([kp_ref[...], kc_ref[...], kn_ref[...], kx_ref[...]], axis=0)
    nkeys = kv.shape[0]
    r = lax.broadcasted_iota(jnp.int32, (blk, nkeys), 0)
    j = lax.broadcasted_iota(jnp.int32, (blk, nkeys), 1)
    lo = jnp.where(n > 0, 0, blk)
    hi = jnp.where(n < nb - 1, 3 * blk, 2 * blk)
    valid_loc = (j >= r) & (j <= r + 2 * blk) & (j >= lo) & (j < hi) & is_lat
    valid = valid_loc | (j >= 3 * blk)
    group = SWA_HEADS // SWA_KV_HEADS
    outs = []
    for hk in range(SWA_KV_HEADS):
        k_h = kv[:, hk * HEAD_DIM:(hk + 1) * HEAD_DIM]
        v_h = kv[:, SWA_KV + hk * HEAD_DIM:SWA_KV + (hk + 1) * HEAD_DIM]
        for g in range(group):
            h = hk * group + g
            qh = q[:, h * HEAD_DIM:(h + 1) * HEAD_DIM]
            s = lax.dot_general(qh, k_h, (((1,), (1,)), ((), ())), preferred_element_type=F32)
            s = jnp.where(valid, s, NEG_INF)
            sk = sink_ref[h]
            mx = jnp.maximum(jnp.max(s, axis=-1, keepdims=True), sk)
            p = jnp.exp(s - mx)
            den = jnp.sum(p, axis=-1, keepdims=True) + jnp.exp(sk - mx)
            o = jnp.dot(p.astype(BF16), v_h, preferred_element_type=F32)
            outs.append(o * (1.0 / den))
    o_ref[...] = jnp.concatenate(outs, axis=1).astype(o_ref.dtype)


def _swa(q, kv, sink, n_batch, seq, ctx_len):
    m = q.shape[0]
    blk = SWA_BLOCK
    nb = seq // blk
    n_lat = n_batch * nb
    cpb = ctx_len // blk

    def batch_of(i):
        return jnp.where(i < n_lat, i // nb, (i - n_lat) // cpb)

    def prev_idx(i, s):
        return (jnp.where(i < n_lat, batch_of(i) * nb + jnp.maximum(i % nb - 1, 0), i), 0)

    def next_idx(i, s):
        return (jnp.where(i < n_lat, batch_of(i) * nb + jnp.minimum(i % nb + 1, nb - 1), i), 0)

    def ctx_idx(i, s):
        return (n_batch * seq // ctx_len + batch_of(i), 0)

    grid_spec = pltpu.PrefetchScalarGridSpec(
        num_scalar_prefetch=1,
        grid=(m // blk,),
        in_specs=[pl.BlockSpec((blk, SWA_Q), lambda i, s: (i, 0)),
                  pl.BlockSpec((blk, 2 * SWA_KV), prev_idx),
                  pl.BlockSpec((blk, 2 * SWA_KV), lambda i, s: (i, 0)),
                  pl.BlockSpec((blk, 2 * SWA_KV), next_idx),
                  pl.BlockSpec((ctx_len, 2 * SWA_KV), ctx_idx)],
        out_specs=pl.BlockSpec((blk, SWA_Q), lambda i, s: (i, 0)),
    )
    return pl.pallas_call(
        functools.partial(_swa_kernel, nb=nb, n_lat_blocks=n_lat),
        out_shape=jax.ShapeDtypeStruct((m, SWA_Q), BF16),
        grid_spec=grid_spec,
        compiler_params=_cp(("parallel",)),
        name="swa_attn",
    )(sink, q, kv, kv, kv, kv)


def _diff_kernel(*refs, n_lat_chunks, coef):
    if n_lat_chunks:
        lam_ref, gcol_ref, q_ref, kc_ref, vtc_ref, kl_ref, vtl_ref, o_ref, m_sc, l_sc, acc_sc = refs
    else:
        lam_ref, gcol_ref, q_ref, kc_ref, vtc_ref, o_ref, m_sc, l_sc, acc_sc = refs
    q = q_ref[...]
    tq = q.shape[0]
    lane = lax.broadcasted_iota(jnp.int32, q.shape, 1)
    zero = jnp.zeros_like(q)
    qq = jnp.concatenate([jnp.where(lane < HEAD_DIM, q, zero), jnp.where(lane >= HEAD_DIM, q, zero)], axis=0)

    def chunk(k, vt):
        st = lax.dot_general(k, qq, (((1,), (1,)), ((), ())), preferred_element_type=F32)
        m_old = m_sc[...]
        m_new = jnp.maximum(m_old, jnp.max(st, axis=0, keepdims=True))
        alpha = jnp.exp(m_old - m_new)
        pt = jnp.exp(st - m_new)
        l_sc[...] = alpha * l_sc[...] + jnp.sum(pt, axis=0, keepdims=True)
        acc_sc[...] = alpha * acc_sc[...] + jnp.dot(vt, pt.astype(BF16), preferred_element_type=F32)
        m_sc[...] = m_new

    m_sc[...] = jnp.full(m_sc.shape, NEG_INF, F32)
    l_sc[...] = jnp.zeros(l_sc.shape, F32)
    acc_sc[...] = jnp.zeros(acc_sc.shape, F32)
    chunk(kc_ref[...], vtc_ref[0, 0])

    if n_lat_chunks:
        def body(c, carry):
            off = pl.multiple_of(c * DIFF_TK, DIFF_TK)
            chunk(kl_ref[pl.ds(off, DIFF_TK), :], vtl_ref[0, c])
            return carry

        lax.fori_loop(0, n_lat_chunks, body, 0, unroll=2)
    ot = acc_sc[...] * (1.0 / l_sc[...])
    odt = ot[:, :tq] - lam_ref[0] * ot[:, tq:]
    ms = jnp.mean(odt * odt, axis=0, keepdims=True)
    yt = odt * lax.rsqrt(ms + DIFF_SUBLN_EPS) * (gcol_ref[...] * coef)
    o_ref[...] = yt.T.astype(o_ref.dtype)


def _diff(q, k, v, lam, subln_g, lam_init, n_batch, seq, ctx_len):
    m = q.shape[0]
    tk = DIFF_TK
    assert ctx_len == tk and seq % tk == 0
    dv = DIFF_V_DIM
    vt = v.reshape(m // tk, tk, DIFF_HEADS, dv).transpose(2, 0, 3, 1)
    gcol = subln_g.reshape(dv, 1)
    lat_chunks = seq // tk
    ctx0 = n_batch * seq // ctx_len

    def call(tq, n_q, q_block0, batch_of, with_lat):
        in_specs = [pl.BlockSpec((dv, 1), lambda h, i, s: (0, 0)),
                    pl.BlockSpec((tq, LANES), lambda h, i, s: (q_block0 + i, h)),
                    pl.BlockSpec((ctx_len, LANES), lambda h, i, s: (ctx0 + batch_of(i), h)),
                    pl.BlockSpec((1, 1, dv, tk), lambda h, i, s: (h, ctx0 + batch_of(i), 0, 0))]
        args = [lam, gcol, q, k, vt]
        if with_lat:
            in_specs += [pl.BlockSpec((seq, LANES), lambda h, i, s: (batch_of(i), h)),
                         pl.BlockSpec((1, lat_chunks, dv, tk), lambda h, i, s: (h, batch_of(i), 0, 0))]
            args += [k, vt]
        grid_spec = pltpu.PrefetchScalarGridSpec(
            num_scalar_prefetch=1,
            grid=(DIFF_HEADS, n_q),
            in_specs=in_specs,
            out_specs=pl.BlockSpec((tq, LANES), lambda h, i, s: (i, h)),
            scratch_shapes=[pltpu.VMEM((1, 2 * tq), F32), pltpu.VMEM((1, 2 * tq), F32),
                            pltpu.VMEM((dv, 2 * tq), F32)],
        )
        return pl.pallas_call(
            functools.partial(_diff_kernel, n_lat_chunks=lat_chunks if with_lat else 0, coef=1.0 - lam_init),
            out_shape=jax.ShapeDtypeStruct((n_q * tq, DIFF_V), BF16),
            grid_spec=grid_spec,
            compiler_params=_cp(("parallel", "arbitrary")),
            name="diff_attn" if with_lat else "diff_attn_ctx",
        )(*args)

    tq = DIFF_TQ
    y_lat = call(tq, n_batch * seq // tq, 0, lambda i: i // (seq // tq), True)
    y_ctx = call(ctx_len, n_batch, ctx0, lambda i: i, False)
    return jnp.concatenate([y_lat, y_ctx], axis=0)


def _rwkv_prep_kernel(p_ref, hp_ref, hn_ref, mup_ref, mun_ref, kk_w_ref, ka_ref, rk_ref, w0_ref, a0_ref,
                      wup_ref, aup_ref, gup_ref, ones_ref,
                      r_ref, v_ref, kk_ref, ld_ref, ke_ref, bb_ref, g_ref, bonus_ref, sc,
                      *, tm, lat_rows, seq, ctx_len):
    i = pl.program_id(0)
    w = RWKV_WIDTH
    sc[0:8, :] = hp_ref[...]
    sc[8:8 + tm, :] = p_ref[...]
    sc[8 + tm:16 + tm, :] = hn_ref[...]
    p = p_ref[...]
    prev = sc[7:7 + tm, :]
    nxt = sc[9:9 + tm, :]
    row = i * tm + lax.broadcasted_iota(jnp.int32, (tm, 1), 0)
    pos = jnp.where(row < lat_rows, row % seq, (row - lat_rows) % ctx_len)
    seg_len = jnp.where(row < lat_rows, seq, ctx_len)
    prev = jnp.where(pos == 0, 0.0, prev)
    nxt = jnp.where(pos == seg_len - 1, 0.0, nxt)
    ps = p + mup_ref[...] * (prev - p) + mun_ref[...] * (nxt - p)

    r = ps[:, 0:w]
    k = ps[:, w:2 * w]
    v = ps[:, 2 * w:3 * w]
    wd = ps[:, 3 * w:3 * w + 2 * DECAY_LORA]
    ad = ps[:, 3 * w + 2 * DECAY_LORA:3 * w + 2 * DECAY_LORA + 2 * AAA_LORA]
    gd = ps[:, 3 * w + 2 * DECAY_LORA + 2 * AAA_LORA:]

    ones_bd = ones_ref[...]
    g = jnp.dot(_sigmoid(gd).astype(BF16), gup_ref[...], preferred_element_type=F32)
    kk = k * kk_w_ref[...]
    ss = jnp.dot(kk * kk, ones_bd, precision=HI, preferred_element_type=F32)
    kk = kk / jnp.maximum(jnp.sqrt(ss), 1e-12)
    w_raw = w0_ref[...] + jnp.dot(jnp.tanh(wd).astype(BF16), wup_ref[...], preferred_element_type=F32)
    a_raw = a0_ref[...] + jnp.dot(ad.astype(BF16), aup_ref[...], preferred_element_type=F32)
    ld = -math.exp(-0.5) * _sigmoid(w_raw)
    a = _sigmoid(a_raw)
    ka = ka_ref[...]
    ke_sum = jnp.zeros_like(k)
    for d in range(2):
        a_d = a[:, d * w:(d + 1) * w]
        ke = k * (1.0 + (a_d - 1.0) * ka)
        ld_ref[d] = ld[:, d * w:(d + 1) * w]
        ke_ref[d] = ke
        bb_ref[d] = kk * a_d
        ke_sum = ke_sum + ke
    rk = jnp.dot(r * ke_sum * rk_ref[...], ones_bd, precision=HI, preferred_element_type=F32)
    r_ref[...] = r
    v_ref[...] = v
    kk_ref[...] = kk
    g_ref[...] = g
    bonus_ref[...] = rk * v


def _rwkv_prep(p, lp, n_batch, seq, ctx_len):
    m, cols = p.shape
    tm = 256
    w = RWKV_WIDTH
    lat_rows = n_batch * seq
    row = lambda a: a.reshape(1, -1).astype(F32)

    def blockdiag(u):
        z = jnp.zeros_like(u[0])
        return jnp.concatenate([jnp.concatenate([u[0], z], axis=1), jnp.concatenate([z, u[1]], axis=1)], axis=0)

    head = jnp.arange(w) // HEAD_DIM
    ones_bd = (head[:, None] == head[None, :]).astype(F32)
    full = lambda shape: pl.BlockSpec(shape, lambda i: (0,) * len(shape))
    nb8 = m // 8
    outs = pl.pallas_call(
        functools.partial(_rwkv_prep_kernel, tm=tm, lat_rows=lat_rows, seq=seq, ctx_len=ctx_len),
        out_shape=[jax.ShapeDtypeStruct((m, w), F32)] * 3
        + [jax.ShapeDtypeStruct((2, m, w), F32)] * 3
        + [jax.ShapeDtypeStruct((m, w), F32)] * 2,
        grid=(m // tm,),
        in_specs=[pl.BlockSpec((tm, cols), lambda i: (i, 0)),
                  pl.BlockSpec((8, cols), lambda i: (jnp.maximum(i * (tm // 8) - 1, 0), 0)),
                  pl.BlockSpec((8, cols), lambda i: (jnp.minimum((i + 1) * (tm // 8), nb8 - 1), 0)),
                  full((1, cols)), full((1, cols)), full((1, w)), full((1, w)), full((1, w)),
                  full((1, 2 * w)), full((1, 2 * w)),
                  full((2 * DECAY_LORA, 2 * w)), full((2 * AAA_LORA, 2 * w)), full((GATE_LORA, w)),
                  full((w, w))],
        out_specs=[pl.BlockSpec((tm, w), lambda i: (i, 0))] * 3
        + [pl.BlockSpec((2, tm, w), lambda i: (0, i, 0))] * 3
        + [pl.BlockSpec((tm, w), lambda i: (i, 0))] * 2,
        scratch_shapes=[pltpu.VMEM((tm + 16, cols), F32)],
        compiler_params=_cp(("parallel",)),
        name="rwkv_prep",
    )(p, p, p, row(lp['rwkv_mu_prev']), row(lp['rwkv_mu_next']), row(lp['rwkv_k_k']), row(lp['rwkv_k_a']),
      row(lp['rwkv_r_k']), row(lp['rwkv_w0']), row(lp['rwkv_a0']),
      blockdiag(lp['rwkv_w_up']).astype(BF16), blockdiag(lp['rwkv_a_up']).astype(BF16),
      lp['rwkv_g_up'].astype(BF16), ones_bd)
    return outs


def _rwkv_chunk(d, off, r_ref, v_ref, kk_ref, ld_ref, ke_ref, bb_ref, y_ref, s_sc):
    c = RWKV_CHUNK
    hd = HEAD_DIM
    ti = lax.broadcasted_iota(jnp.int32, (c, c), 0)
    tj = lax.broadcasted_iota(jnp.int32, (c, c), 1)
    incl_b = (tj <= ti) if d == 0 else (tj >= ti)
    incl = incl_b.astype(F32)
    strict = incl - (ti == tj).astype(F32)
    eye = (ti == tj).astype(F32)
    diag_mask = ((ti // RWKV_INV_BASE) == (tj // RWKV_INV_BASE)).astype(F32)
    off_masks = []
    sz = RWKV_INV_BASE
    while sz < c:
        off_masks.append((((ti // (2 * sz)) == (tj // (2 * sz))) & ((ti // sz) != (tj // sz))).astype(F32))
        sz *= 2

    ld = ld_ref[0, pl.ds(off, c), :]
    r = r_ref[pl.ds(off, c), :]
    v = v_ref[pl.ds(off, c), :]
    kk = kk_ref[pl.ds(off, c), :]
    ke = ke_ref[0, pl.ds(off, c), :]
    bb = bb_ref[0, pl.ds(off, c), :]
    cum = jnp.dot(incl, ld, precision=HI, preferred_element_type=F32)
    tot = jnp.sum(ld, axis=0, keepdims=True)
    at_all = -kk * jnp.exp(cum - ld)
    rt_all = r * jnp.exp(cum)
    einv = jnp.exp(-cum)
    bt_all = bb * einv
    kt_all = ke * einv
    etail = jnp.exp(tot - cum)
    bh_all = bb * etail
    kh_all = ke * etail
    wtot = jnp.exp(tot)

    nt = (((1,), (1,)), ((), ()))
    bdot = lambda x, y: jnp.dot(x.astype(BF16), y.astype(BF16), preferred_element_type=F32)
    at_b, rt_b, bt_b, kt_b, bh_b, kh_b, v_b = (x.astype(BF16) for x in
                                               (at_all, rt_all, bt_all, kt_all, bh_all, kh_all, v))
    ys = []
    for h in range(RWKV_HEADS):
        sl = slice(h * hd, (h + 1) * hd)
        at, rt, bt, kt, bh, kh, vv = (x[:, sl] for x in (at_b, rt_b, bt_b, kt_b, bh_b, kh_b, v_b))
        gm = lax.dot_general(jnp.concatenate([at, rt], axis=0), jnp.concatenate([bt, kt], axis=0), nt,
                             preferred_element_type=F32)
        aab = gm[:c, :c] * strict
        aak = gm[:c, c:] * strict
        arb = gm[c:, :c] * incl
        ark = gm[c:, c:] * incl
        nd = aab * diag_mask
        pw = nd
        tm_ = eye + nd
        for _ in range(int(math.log2(RWKV_INV_BASE)) - 1):
            pw = bdot(pw, pw)
            tm_ = tm_ + bdot(tm_, pw)
        for off_mask in off_masks:
            tm_ = tm_ + bdot(bdot(tm_, aab * off_mask), tm_)
        av = bdot(jnp.concatenate([aak, ark], axis=0), vv)
        au = bdot(tm_, jnp.concatenate([at, av[:c].astype(BF16)], axis=1))
        ry = bdot(arb, au)
        rbar = rt_all[:, sl] + ry[:, :hd]
        ybar = ry[:, hd:] + av[c:]
        mz = bdot(au.T, bh)
        mm = eye * wtot[:, sl] + mz[:hd]
        zz = mz[hd:] + bdot(v[:, sl].T, kh)
        s = s_sc[d, h]
        s_b = s.astype(BF16)
        ys.append(lax.dot_general(rbar.astype(BF16), s_b, nt, preferred_element_type=F32) + ybar)
        s_sc[d, h] = jnp.dot(s_b, mm.astype(BF16), preferred_element_type=F32) + zz
    y_ref[0, pl.ds(off, c), :] = jnp.concatenate(ys, axis=1)


def _rwkv_scan_kernel(rf, vf, kf, ldf, kef, bbf, rb, vb, kb, ldb, keb, bbb, yf, yb, s_sc, *, n_chunks):
    @pl.when(pl.program_id(1) == 0)
    def _():
        s_sc[...] = jnp.zeros(s_sc.shape, F32)

    def body(cc, carry):
        off_f = pl.multiple_of(cc * RWKV_CHUNK, RWKV_CHUNK)
        _rwkv_chunk(0, off_f, rf, vf, kf, ldf, kef, bbf, yf, s_sc)
        off_b = pl.multiple_of((n_chunks - 1 - cc) * RWKV_CHUNK, RWKV_CHUNK)
        _rwkv_chunk(1, off_b, rb, vb, kb, ldb, keb, bbb, yb, s_sc)
        return carry

    lax.fori_loop(0, n_chunks, body, 0)


def _rwkv_scan(r, v, kk, ld, ke, bb, n_batch, seq, ctx_len):
    m, w = r.shape
    ts = RWKV_STEP_ROWS
    assert ctx_len == ts
    lpb = seq // ts
    ctx0 = n_batch * seq // ts
    nj = 1 + lpb

    def fwd(b, j):
        return jnp.where(j == 0, ctx0 + b, b * lpb + j - 1)

    def bwd(b, j):
        return jnp.where(j == 0, ctx0 + b, b * lpb + lpb - j)

    shared = lambda f: pl.BlockSpec((ts, w), lambda b, j: (f(b, j), 0))
    per_dir = lambda f, d: pl.BlockSpec((1, ts, w), lambda b, j: (d, f(b, j), 0))
    y = pl.pallas_call(
        functools.partial(_rwkv_scan_kernel, n_chunks=ts // RWKV_CHUNK),
        out_shape=[jax.ShapeDtypeStruct((1, m, w), F32)] * 2,
        grid=(n_batch, nj),
        in_specs=[shared(fwd), shared(fwd), shared(fwd), per_dir(fwd, 0), per_dir(fwd, 0), per_dir(fwd, 0),
                  shared(bwd), shared(bwd), shared(bwd), per_dir(bwd, 1), per_dir(bwd, 1), per_dir(bwd, 1)],
        out_specs=[pl.BlockSpec((1, ts, w), lambda b, j: (0, fwd(b, j), 0)),
                   pl.BlockSpec((1, ts, w), lambda b, j: (0, bwd(b, j), 0))],
        scratch_shapes=[pltpu.VMEM((2, RWKV_HEADS, HEAD_DIM, HEAD_DIM), F32)],
        compiler_params=_cp(("parallel", "arbitrary")),
        name="rwkv_scan",
    )(r, v, kk, ld, ke, bb, r, v, kk, ld, ke, bb)
    return y


def _rwkv_out_kernel(yf_ref, yb_ref, bonus_ref, g_ref, lg_ref, lb_ref, ones_ref, o_ref):
    y = yf_ref[0] + yb_ref[0]
    ones_bd = ones_ref[...]
    inv = 1.0 / HEAD_DIM
    mu = jnp.dot(y, ones_bd, precision=HI, preferred_element_type=F32) * inv
    yc = y - mu
    var = jnp.dot(yc * yc, ones_bd, precision=HI, preferred_element_type=F32) * inv
    yn = yc * lax.rsqrt(var + RWKV_GN_EPS) * lg_ref[...] + lb_ref[...]
    o_ref[...] = ((yn + bonus_ref[...]) * g_ref[...]).astype(o_ref.dtype)


def _rwkv_out(yf, yb, bonus, g, lnx_g, lnx_b):
    m, w = bonus.shape
    tm = ROW_TILE
    head = jnp.arange(w) // HEAD_DIM
    ones_bd = (head[:, None] == head[None, :]).astype(F32)
    blk = pl.BlockSpec((tm, w), lambda i: (i, 0))
    blk3 = pl.BlockSpec((1, tm, w), lambda i: (0, i, 0))
    one = pl.BlockSpec((1, w), lambda i: (0, 0))
    return pl.pallas_call(
        _rwkv_out_kernel,
        out_shape=jax.ShapeDtypeStruct((m, w), BF16),
        grid=(m // tm,),
        in_specs=[blk3, blk3, blk, blk, one, one, pl.BlockSpec((w, w), lambda i: (0, 0))],
        out_specs=blk,
        compiler_params=_cp(("parallel",)),
        name="rwkv_out",
    )(yf, yb, bonus, g, lnx_g.reshape(1, w), lnx_b.reshape(1, w), ones_bd)


def _merge_kernel(ya_ref, yb_ref, yc_ref, gt_ref, h_ref, pa_ref, pb_ref, pc_ref, wo_ref, gpost_ref, gpre_ref,
                  mod_ref, h_out, f_out, *, d):
    gates = gt_ref[...].astype(F32)
    merged = (_sigmoid(gates[:, 0:d]) * jnp.dot(ya_ref[...], pa_ref[...], preferred_element_type=F32)
              + _sigmoid(gates[:, d:2 * d]) * jnp.dot(yb_ref[...], pb_ref[...], preferred_element_type=F32)
              + _sigmoid(gates[:, 2 * d:3 * d]) * jnp.dot(yc_ref[...], pc_ref[...], preferred_element_type=F32))
    out = jnp.dot(merged.astype(BF16), wo_ref[...], preferred_element_type=F32)
    mod = lambda idx: mod_ref[0, :, idx * d:(idx + 1) * d]
    hn = h_ref[...] + mod(2) * (_rms(out, NORM_EPS) * gpost_ref[...])
    h_out[...] = hn
    f = (_rms(hn, NORM_EPS) * gpre_ref[...]) * (1.0 + mod(4)) + mod(3)
    f_out[...] = f.astype(f_out.dtype)


def _merge(ya, yb, yc, gates, h, pa, pb, pc, wo, g_post, g_pre, mods, f_dtype, lat_bpb, n_batch):
    m, d = h.shape
    tm = 256
    bpb = lat_bpb * (ROW_TILE // tm)
    seg = functools.partial(_seg_of_block, lat_blocks_per_batch=bpb, n_batch=n_batch)
    rows = lambda width: pl.BlockSpec((tm, width), lambda i: (i, 0))
    full = lambda a: pl.BlockSpec(a.shape, lambda i: (0, 0))
    return pl.pallas_call(
        functools.partial(_merge_kernel, d=d),
        out_shape=[jax.ShapeDtypeStruct((m, d), F32), jax.ShapeDtypeStruct((m, d), f_dtype)],
        grid=(m // tm,),
        in_specs=[rows(ya.shape[1]), rows(yb.shape[1]), rows(yc.shape[1]), rows(gates.shape[1]), rows(d),
                  full(pa), full(pb), full(pc), full(wo),
                  pl.BlockSpec((1, d), lambda i: (0, 0)), pl.BlockSpec((1, d), lambda i: (0, 0)),
                  pl.BlockSpec((1, 1, mods.shape[2]), lambda i: (seg(i), 0, 0))],
        out_specs=[rows(d), rows(d)],
        compiler_params=_cp(("parallel",)),
        name="merge",
    )(ya, yb, yc, gates, h, pa, pb, pc, wo, g_post, g_pre, mods)


def _swiglu_hidden(x, wg, wu):
    hg = jnp.dot(x, wg, preferred_element_type=F32)
    hu = jnp.dot(x, wu, preferred_element_type=F32)
    return (hg * _sigmoid(hg) * hu).astype(BF16)


def _ffn_kernel(f_ref, wg_ref, wu_ref, wd_ref, h_ref, gpost_ref, mod_ref, o_ref, acc, *, d):
    j = pl.program_id(1)

    @pl.when(j == 0)
    def _():
        acc[...] = jnp.zeros(acc.shape, F32)

    hid = _swiglu_hidden(f_ref[...], wg_ref[...], wu_ref[...])
    acc[...] += jnp.dot(hid, wd_ref[...], preferred_element_type=F32)

    @pl.when(j == pl.num_programs(1) - 1)
    def _():
        gate = mod_ref[0, :, 5 * d:6 * d]
        o_ref[...] = h_ref[...] + gate * (_rms(acc[...], NORM_EPS) * gpost_ref[...])


def _ffn(f, wg, wu, wd, h, g_post, mods, lat_bpb, n_batch):
    m, d = h.shape
    ff = wg.shape[1]
    tm, tf = ROW_TILE, FFN_TF
    seg = functools.partial(_seg_of_block, lat_blocks_per_batch=lat_bpb, n_batch=n_batch)
    return pl.pallas_call(
        functools.partial(_ffn_kernel, d=d),
        out_shape=jax.ShapeDtypeStruct((m, d), F32),
        grid=(m // tm, ff // tf),
        in_specs=[pl.BlockSpec((tm, d), lambda i, j: (i, 0)),
                  pl.BlockSpec((d, tf), lambda i, j: (0, j)),
                  pl.BlockSpec((d, tf), lambda i, j: (0, j)),
                  pl.BlockSpec((tf, d), lambda i, j: (j, 0)),
                  pl.BlockSpec((tm, d), lambda i, j: (i, 0)),
                  pl.BlockSpec((1, d), lambda i, j: (0, 0)),
                  pl.BlockSpec((1, 1, mods.shape[2]), lambda i, j: (seg(i), 0, 0))],
        out_specs=pl.BlockSpec((tm, d), lambda i, j: (i, 0)),
        scratch_shapes=[pltpu.VMEM((tm, d), F32)],
        compiler_params=_cp(("parallel", "arbitrary")),
        name="ffn_dense",
    )(f, wg, wu, wd, h, g_post, mods)


def _router_kernel(f_ref, w_ref, idx_ref, wt_ref):
    logits = jnp.dot(f_ref[...], w_ref[...], precision=HI, preferred_element_type=F32)
    lane = lax.broadcasted_iota(jnp.int32, logits.shape, 1)
    logits = jnp.where(lane < N_EXPERTS, logits, -jnp.inf)
    m1 = jnp.max(logits, axis=-1, keepdims=True)
    i1 = jnp.min(jnp.where(logits == m1, lane, LANES), axis=-1, keepdims=True)
    rest = jnp.where(lane == i1, -jnp.inf, logits)
    m2 = jnp.max(rest, axis=-1, keepdims=True)
    i2 = jnp.min(jnp.where(rest == m2, lane, LANES), axis=-1, keepdims=True)
    e = jnp.exp(m2 - m1)
    w1 = 1.0 / (1.0 + e)
    w2 = e / (1.0 + e)
    idx_ref[...] = jnp.where(lane == 0, i1, jnp.where(lane == 1, i2, 0))
    wt_ref[...] = jnp.where(lane == 0, w1, jnp.where(lane == 1, w2, 0.0))


def _router(f, router_w, n_rows):
    d = f.shape[1]
    tm = ROW_TILE
    w_pad = jnp.zeros((d, LANES), F32).at[:, :N_EXPERTS].set(router_w)
    return pl.pallas_call(
        _router_kernel,
        out_shape=[jax.ShapeDtypeStruct((n_rows, LANES), jnp.int32), jax.ShapeDtypeStruct((n_rows, LANES), F32)],
        grid=(n_rows // tm,),
        in_specs=[pl.BlockSpec((tm, d), lambda i: (i, 0)), pl.BlockSpec((d, LANES), lambda i: (0, 0))],
        out_specs=[pl.BlockSpec((tm, LANES), lambda i: (i, 0))] * 2,
        compiler_params=_cp(("parallel",)),
        name="router",
    )(f, w_pad)


def _moe_gather_copy(f_hbm, xbuf, sem, src_row, dst_row):
    return pltpu.make_async_copy(f_hbm.at[pl.ds(src_row, 1), :], xbuf.at[pl.ds(dst_row, 1), :], sem)


def _moe_ffn_kernel(blk_e_ref, nused_ref, tok_ref, f_hbm, wg_ref, wu_ref, wd_ref, y_ref, xbuf, xb, acc, sem,
                    *, tm):
    i = pl.program_id(0)
    j = pl.program_id(1)
    active = i < nused_ref[0]

    @pl.when(active & (j == 0))
    def _():
        def issue(r, carry):
            _moe_gather_copy(f_hbm, xbuf, sem, tok_ref[i * tm + r], r).start()
            return carry

        lax.fori_loop(0, tm, issue, 0)
        pltpu.make_async_copy(f_hbm.at[pl.ds(0, tm), :], xbuf, sem).wait()
        xb[...] = xbuf[...].astype(BF16)
        acc[...] = jnp.zeros(acc.shape, F32)

    @pl.when(active)
    def _():
        hid = _swiglu_hidden(xb[...], wg_ref[0], wu_ref[0])
        acc[...] += jnp.dot(hid, wd_ref[0], preferred_element_type=F32)

    @pl.when(j == pl.num_programs(1) - 1)
    def _():
        y_ref[...] = acc[...]


def _moe_ffn(f, blk_e, nused, tok, wg, wu, wd, n_blocks):
    d = f.shape[1]
    ff = wg.shape[2]
    tm, tf = MOE_TILE, MOE_TF

    def e_of(i, be, nu):
        return be[jnp.minimum(i, nu[0] - 1)]

    grid_spec = pltpu.PrefetchScalarGridSpec(
        num_scalar_prefetch=3,
        grid=(n_blocks, ff // tf),
        in_specs=[pl.BlockSpec(memory_space=pl.ANY),
                  pl.BlockSpec((1, d, tf), lambda i, j, be, nu, tk: (e_of(i, be, nu), 0, j)),
                  pl.BlockSpec((1, d, tf), lambda i, j, be, nu, tk: (e_of(i, be, nu), 0, j)),
                  pl.BlockSpec((1, tf, d), lambda i, j, be, nu, tk: (e_of(i, be, nu), j, 0))],
        out_specs=pl.BlockSpec((tm, d), lambda i, j, be, nu, tk: (i, 0)),
        scratch_shapes=[pltpu.VMEM((tm, d), F32), pltpu.VMEM((tm, d), BF16), pltpu.VMEM((tm, d), F32),
                        pltpu.SemaphoreType.DMA(())],
    )
    return pl.pallas_call(
        functools.partial(_moe_ffn_kernel, tm=tm),
        out_shape=jax.ShapeDtypeStruct((n_blocks * tm, d), F32),
        grid_spec=grid_spec,
        compiler_params=_cp(("arbitrary", "arbitrary")),
        name="moe_ffn",
    )(blk_e, nused, tok, f, wg, wu, wd)


def _moe_combine_kernel(p0_ref, p1_ref, y_hbm, wt_ref, h_ref, gpost_ref, mod_ref, o_ref, b0, b1, sem, *, tm, d):
    i = pl.program_id(0)

    def issue(r, carry):
        pltpu.make_async_copy(y_hbm.at[pl.ds(p0_ref[i * tm + r], 1), :], b0.at[pl.ds(r, 1), :], sem.at[0]).start()
        pltpu.make_async_copy(y_hbm.at[pl.ds(p1_ref[i * tm + r], 1), :], b1.at[pl.ds(r, 1), :], sem.at[1]).start()
        return carry

    lax.fori_loop(0, tm, issue, 0)
    pltpu.make_async_copy(y_hbm.at[pl.ds(0, tm), :], b0, sem.at[0]).wait()
    pltpu.make_async_copy(y_hbm.at[pl.ds(0, tm), :], b1, sem.at[1]).wait()
    wt = wt_ref[...]
    y = b0[...] * wt[:, 0:1] + b1[...] * wt[:, 1:2]
    gate = mod_ref[0, :, 5 * d:6 * d]
    o_ref[...] = h_ref[...] + gate * (_rms(y, NORM_EPS) * gpost_ref[...])


def _moe_combine(pos0, pos1, y, wt, h, g_post, mods, n_rows, rows_per_batch, n_batch):
    d = h.shape[1]
    tm = COMBINE_TILE
    seg = functools.partial(_seg_of_block, lat_blocks_per_batch=rows_per_batch // tm, n_batch=n_batch)
    grid_spec = pltpu.PrefetchScalarGridSpec(
        num_scalar_prefetch=2,
        grid=(n_rows // tm,),
        in_specs=[pl.BlockSpec(memory_space=pl.ANY),
                  pl.BlockSpec((tm, LANES), lambda i, a, b: (i, 0)),
                  pl.BlockSpec((tm, d), lambda i, a, b: (i, 0)),
                  pl.BlockSpec((1, d), lambda i, a, b: (0, 0)),
                  pl.BlockSpec((1, 1, mods.shape[2]), lambda i, a, b: (seg(i), 0, 0))],
        out_specs=pl.BlockSpec((tm, d), lambda i, a, b: (i, 0)),
        scratch_shapes=[pltpu.VMEM((tm, d), F32), pltpu.VMEM((tm, d), F32), pltpu.SemaphoreType.DMA((2,))],
    )
    return pl.pallas_call(
        functools.partial(_moe_combine_kernel, tm=tm, d=d),
        out_shape=jax.ShapeDtypeStruct((n_rows, d), F32),
        grid_spec=grid_spec,
        compiler_params=_cp(("arbitrary",)),
        name="moe_combine",
    )(pos0, pos1, y, wt, h, g_post, mods)


def _moe_slots(top_i, tile):
    n = top_i.shape[0]
    a = n * 2
    e_flat = top_i.reshape(a)
    onehot = (e_flat[:, None] == jnp.arange(N_EXPERTS, dtype=jnp.int32)[None, :]).astype(jnp.int32)
    csum = jnp.cumsum(onehot, axis=0)
    rank = jnp.sum(csum * onehot, axis=1) - 1
    counts = csum[-1]
    padded = (counts + tile - 1) // tile * tile
    pends = jnp.cumsum(padded)
    pstarts = pends - padded
    dest = (jnp.sum(onehot * pstarts[None, :], axis=1) + rank).astype(jnp.int32)
    n_blocks = a // tile + N_EXPERTS
    tok = jnp.zeros((n_blocks * tile,), jnp.int32).at[dest].set(jnp.arange(a, dtype=jnp.int32) // 2)
    block_start = jnp.arange(n_blocks, dtype=jnp.int32) * tile
    blk_e = jnp.minimum(jnp.searchsorted(pends, block_start, side='right'), N_EXPERTS - 1).astype(jnp.int32)
    nused = (pends[-1:] // tile).astype(jnp.int32)
    return tok, blk_e, nused, dest.reshape(n, 2), n_blocks


def _rope_tables(n_batch, seq, ctx_len):
    t = jnp.arange(seq, dtype=jnp.int32)
    row = (t // GRID_W).astype(F32)
    col = (t % GRID_W).astype(F32)
    axis_dim = HEAD_DIM // 2
    inv_freq = ROPE_THETA ** (-jnp.arange(0, axis_dim, 2, dtype=F32) / axis_dim)
    dd = jnp.arange(LANES) % HEAD_DIM
    pos = jnp.where((dd // axis_dim)[None, :] == 0, row[:, None], col[:, None])
    ang = pos * inv_freq[dd % (axis_dim // 2)][None, :]
    cos = jnp.cos(ang)
    sin = jnp.where(((dd % axis_dim) < axis_dim // 2)[None, :], -jnp.sin(ang), jnp.sin(ang))
    n_ctx = n_batch * ctx_len
    cos = jnp.concatenate([jnp.tile(cos, (n_batch, 1)), jnp.ones((n_ctx, LANES), F32)], axis=0)
    sin = jnp.concatenate([jnp.tile(sin, (n_batch, 1)), jnp.zeros((n_ctx, LANES), F32)], axis=0)
    return cos, sin


def kernel(x, c, ctx, c_ctx, ada_w, ada_b, pre_mix_g, post_mix_g, pre_ffn_g, post_ffn_g, w_in, swa_sink,
           rwkv_mu_prev, rwkv_mu_next, rwkv_w0, rwkv_w_up, rwkv_a0, rwkv_a_up, rwkv_g_up, rwkv_k_k, rwkv_k_a,
           rwkv_r_k, rwkv_lnx_g, rwkv_lnx_b, diff_lambda, diff_subln_g, proj_swa, proj_rwkv, proj_diff, w_out,
           ffn_w_gate, ffn_w_up, ffn_w_down, router_w, moe_w_gate, moe_w_up, moe_w_down):
    n_batch, seq, d = x.shape
    ctx_len = ctx.shape[1]
    depth = w_in.shape[0]
    lat_rows = n_batch * seq
    lat_bpb = seq // ROW_TILE
    assert seq % ROW_TILE == 0 and (n_batch * ctx_len) % ROW_TILE == 0

    h = jnp.concatenate([x.reshape(lat_rows, d), ctx.reshape(n_batch * ctx_len, d)], axis=0)
    m = h.shape[0]
    cond = jnp.zeros((8, d), F32).at[:n_batch].set(c).at[n_batch].set(c_ctx)
    rope = _rope_tables(n_batch, seq, ctx_len)
    row = lambda a: a.reshape(1, -1)

    o_swa = 0
    o_rwkv = o_swa + SWA_Q + 2 * SWA_KV
    o_diff = o_rwkv + RWKV_COLS
    o_gate = o_diff + 2 * DIFF_QK + DIFF_V
    o_end = o_gate + 3 * d

    for layer in range(depth):
        mods = _ada(cond, ada_w[layer], ada_b[layer].reshape(1, -1))[:, None, :]
        wl = w_in[layer].astype(BF16)
        a = _prenorm(h, row(pre_mix_g[layer]), mods, 0, 1, lat_bpb, n_batch)

        q_swa, kv_swa = _proj(a, wl[:, o_swa:o_rwkv], (SWA_Q, 2 * SWA_KV), (BF16, BF16), rope=rope,
                              rope_cols=SWA_Q + SWA_KV, scale_cols=SWA_Q)
        (p_rwkv,) = _proj(a, wl[:, o_rwkv:o_diff], (RWKV_COLS,), (F32,))
        q_diff, k_diff, v_diff = _proj(a, wl[:, o_diff:o_gate], (DIFF_QK, DIFF_QK, DIFF_V), (BF16,) * 3,
                                       rope=rope, rope_cols=2 * DIFF_QK, scale_cols=DIFF_QK)
        (gates,) = _proj(a, wl[:, o_gate:o_end], (d,), (BF16,), tn=d)

        ya = _swa(q_swa, kv_swa, swa_sink[layer].astype(F32), n_batch, seq, ctx_len)

        lp = {'rwkv_mu_prev': rwkv_mu_prev[layer], 'rwkv_mu_next': rwkv_mu_next[layer],
              'rwkv_w0': rwkv_w0[layer], 'rwkv_w_up': rwkv_w_up[layer], 'rwkv_a0': rwkv_a0[layer],
              'rwkv_a_up': rwkv_a_up[layer], 'rwkv_g_up': rwkv_g_up[layer], 'rwkv_k_k': rwkv_k_k[layer],
              'rwkv_k_a': rwkv_k_a[layer], 'rwkv_r_k': rwkv_r_k[layer]}
        r_, v_, kk_, ld_, ke_, bb_, g_, bonus_ = _rwkv_prep(p_rwkv, lp, n_batch, seq, ctx_len)
        y_f, y_b = _rwkv_scan(r_, v_, kk_, ld_, ke_, bb_, n_batch, seq, ctx_len)
        yb = _rwkv_out(y_f, y_b, bonus_, g_, rwkv_lnx_g[layer], rwkv_lnx_b[layer])

        lam_vec = diff_lambda[layer].astype(F32)
        lam_init = 0.8 - 0.6 * math.exp(-0.3 * layer)
        lam = (jnp.exp(jnp.sum(lam_vec[0] * lam_vec[1])) - jnp.exp(jnp.sum(lam_vec[2] * lam_vec[3]))
               + lam_init).reshape(1)
        yc = _diff(q_diff, k_diff, v_diff, lam, row(diff_subln_g[layer]), lam_init, n_batch, seq, ctx_len)

        moe_layer = layer % 2 == 1
        jj = layer // 2
        h, f = _merge(ya, yb, yc, gates, h, proj_swa[layer].astype(BF16), proj_rwkv[layer].astype(BF16),
                      proj_diff[layer].astype(BF16), w_out[layer].astype(BF16), row(post_mix_g[layer]),
                      row(pre_ffn_g[layer]), mods, F32 if moe_layer else BF16, lat_bpb, n_batch)
        need_ctx = layer < depth - 1
        if not moe_layer:
            h = _ffn(f, ffn_w_gate[jj].astype(BF16), ffn_w_up[jj].astype(BF16), ffn_w_down[jj].astype(BF16),
                     h, row(post_ffn_g[layer]), mods, lat_bpb, n_batch)
        else:
            n_tok = m if need_ctx else lat_rows
            top_i, top_w = _router(f, router_w[jj], n_tok)
            tok, blk_e, nused, dest, n_blocks = _moe_slots(top_i[:, :2], MOE_TILE)
            y = _moe_ffn(f, blk_e, nused, tok, moe_w_gate[jj].astype(BF16), moe_w_up[jj].astype(BF16),
                         moe_w_down[jj].astype(BF16), n_blocks)
            h = _moe_combine(dest[:, 0], dest[:, 1], y, top_w, h, row(post_ffn_g[layer]), mods, n_tok, seq,
                             n_batch)
    return h[:lat_rows].reshape(n_batch, seq, d)
```

```python
import functools
import math

import jax
import jax.numpy as jnp
from jax import lax
from jax.experimental import pallas as pl
from jax.experimental.pallas import tpu as pltpu

F32 = jnp.float32
BF16 = jnp.bfloat16
HI = lax.Precision.HIGHEST

HEAD_DIM = 64
GRID_W = 64
ROPE_THETA = 10000.0
NORM_EPS = 1e-6
NEG_INF = -1e30
SWA_HEADS = 8
SWA_KV_HEADS = 2
SWA_BLOCK = 128
RWKV_HEADS = 8
RWKV_WIDTH = RWKV_HEADS * HEAD_DIM
DECAY_LORA = 64
AAA_LORA = 64
GATE_LORA = 128
RWKV_GN_EPS = 64e-5
DIFF_HEADS = 4
DIFF_V_DIM = 2 * HEAD_DIM
DIFF_SUBLN_EPS = 1e-5
N_EXPERTS = 8
SWA_Q = SWA_HEADS * HEAD_DIM
SWA_KV = SWA_KV_HEADS * HEAD_DIM
RWKV_COLS = 3 * RWKV_WIDTH + 2 * DECAY_LORA + 2 * AAA_LORA + GATE_LORA
DIFF_QK = DIFF_HEADS * 2 * HEAD_DIM
DIFF_V = DIFF_HEADS * DIFF_V_DIM

LANES = 128
VMEM_LIMIT = 48 * 1024 * 1024
ROW_TILE = 512
RWKV_CHUNK = 64
RWKV_INV_BASE = 8
RWKV_STEP_ROWS = 256
DIFF_TQ = 256
DIFF_TK = 256
DIFF_UNROLL = 4
DIFF_ONES_ROWS = 16
MOE_TILE = 512
MOE_TF = 512
FFN_TF = 256
COMBINE_TILE = 256


def _cp(sem, **kw):
    return pltpu.CompilerParams(dimension_semantics=sem, vmem_limit_bytes=VMEM_LIMIT, **kw)


def _seg_of_block(i, lat_blocks_per_batch, n_batch):
    return jnp.minimum(i // lat_blocks_per_batch, n_batch)


def _rms(x, eps):
    return x * lax.rsqrt(jnp.mean(x * x, axis=-1, keepdims=True) + eps)


def _sigmoid(x):
    return 1.0 / (1.0 + jnp.exp(-x))


def _ada_kernel(x_ref, w_ref, b_ref, o_ref):
    x = x_ref[...]
    s = x * _sigmoid(x)
    o_ref[...] = jnp.dot(s, w_ref[...], precision=HI, preferred_element_type=F32) + b_ref[...]


def _ada(cond, w, b):
    rows, d = cond.shape
    n = w.shape[1]
    return pl.pallas_call(
        _ada_kernel,
        out_shape=jax.ShapeDtypeStruct((rows, n), F32),
        grid=(n // d,),
        in_specs=[pl.BlockSpec((rows, d), lambda j: (0, 0)),
                  pl.BlockSpec((d, d), lambda j: (0, j)),
                  pl.BlockSpec((1, d), lambda j: (0, j))],
        out_specs=pl.BlockSpec((rows, d), lambda j: (0, j)),
        compiler_params=_cp(("parallel",)),
        name="ada_mod",
    )(cond, w, b)


def _prenorm_kernel(h_ref, g_ref, mod_ref, o_ref, *, d, shift_idx, scale_idx):
    y = _rms(h_ref[...], NORM_EPS) * g_ref[...]
    shift = mod_ref[0, :, shift_idx * d:(shift_idx + 1) * d]
    scale = mod_ref[0, :, scale_idx * d:(scale_idx + 1) * d]
    o_ref[...] = (y * (1.0 + scale) + shift).astype(o_ref.dtype)


def _prenorm(h, g, mods, shift_idx, scale_idx, lat_bpb, n_batch):
    m, d = h.shape
    tm = ROW_TILE
    seg = functools.partial(_seg_of_block, lat_blocks_per_batch=lat_bpb, n_batch=n_batch)
    return pl.pallas_call(
        functools.partial(_prenorm_kernel, d=d, shift_idx=shift_idx, scale_idx=scale_idx),
        out_shape=jax.ShapeDtypeStruct((m, d), BF16),
        grid=(m // tm,),
        in_specs=[pl.BlockSpec((tm, d), lambda i: (i, 0)),
                  pl.BlockSpec((1, d), lambda i: (0, 0)),
                  pl.BlockSpec((1, 1, mods.shape[2]), lambda i: (seg(i), 0, 0))],
        out_specs=pl.BlockSpec((tm, d), lambda i: (i, 0)),
        compiler_params=_cp(("parallel",)),
        name="prenorm",
    )(h, g, mods)


def _proj_kernel(*refs, splits, rope_cols, scale_cols, q_scale):
    if rope_cols:
        a_ref, w_ref, cos_ref, sin_ref = refs[:4]
        outs = refs[4:]
    else:
        a_ref, w_ref = refs[:2]
        outs = refs[2:]
    y = jnp.dot(a_ref[...], w_ref[...], preferred_element_type=F32)
    tm, tn = y.shape
    if rope_cols:
        cos = cos_ref[...]
        sin = sin_ref[...]
        lane = lax.broadcasted_iota(jnp.int32, (tm, LANES), 1)
        first_half = (lane % 32) < 16
        pieces = []
        for c in range(tn // LANES):
            yc = y[:, c * LANES:(c + 1) * LANES]
            if c * LANES < rope_cols:
                partner = jnp.where(first_half, pltpu.roll(yc, LANES - 16, 1), pltpu.roll(yc, 16, 1))
                yc = yc * cos + partner * sin
            if c * LANES < scale_cols:
                yc = yc * q_scale
            pieces.append(yc)
        y = jnp.concatenate(pieces, axis=1)
    start = 0
    for o_ref, width in zip(outs, splits):
        o_ref[...] = y[:, start:start + width].astype(o_ref.dtype)
        start += width


def _proj(a, w, splits, dtypes, rope=None, rope_cols=0, scale_cols=0, q_scale=1.0, tn=None):
    m, k = a.shape
    n = w.shape[1]
    tm = ROW_TILE
    tn = n if tn is None else tn
    assert sum(splits) == tn and (len(splits) == 1 or tn == n)
    in_specs = [pl.BlockSpec((tm, k), lambda i, j: (i, 0)),
                pl.BlockSpec((k, tn), lambda i, j: (0, j))]
    args = [a, w]
    if rope_cols:
        in_specs += [pl.BlockSpec((tm, LANES), lambda i, j: (i, 0))] * 2
        args += list(rope)
    out_specs = []
    out_shape = []
    if len(splits) == 1:
        out_specs.append(pl.BlockSpec((tm, tn), lambda i, j: (i, j)))
        out_shape.append(jax.ShapeDtypeStruct((m, n), dtypes[0]))
    else:
        for width, dt in zip(splits, dtypes):
            out_specs.append(pl.BlockSpec((tm, width), lambda i, j: (i, 0)))
            out_shape.append(jax.ShapeDtypeStruct((m, width), dt))
    res = pl.pallas_call(
        functools.partial(_proj_kernel, splits=tuple(splits), rope_cols=rope_cols, scale_cols=scale_cols,
                          q_scale=q_scale),
        out_shape=out_shape,
        grid=(m // tm, n // tn),
        in_specs=in_specs,
        out_specs=out_specs,
        compiler_params=_cp(("parallel", "parallel")),
        name="proj",
    )(*args)
    return res


def _swa_kernel(sink_ref, q_ref, kp_ref, kc_ref, kn_ref, kx_ref, o_ref, *, nb, n_lat_blocks):
    i = pl.program_id(0)
    is_lat = i < n_lat_blocks
    n = i % nb
    blk = SWA_BLOCK
    q = q_ref[...]
    kv = jnp.concatenate([kp_ref[...], kc_ref[...], kn_ref[...], kx_ref[...]], axis=0)
    nkeys = kv.shape[0]
    r = lax.broadcasted_iota(jnp.int32, (blk, nkeys), 0)
    j = lax.broadcasted_iota(jnp.int32, (blk, nkeys), 1)
    lo = jnp.where(n > 0, 0, blk)
    hi = jnp.where(n < nb - 1, 3 * blk, 2 * blk)
    valid_loc = (j >= r) & (j <= r + 2 * blk) & (j >= lo) & (j < hi) & is_lat
    valid = valid_loc | (j >= 3 * blk)
    group = SWA_HEADS // SWA_KV_HEADS
    outs = []
    for hk in range(SWA_KV_HEADS):
        k_h = kv[:, hk * HEAD_DIM:(hk + 1) * HEAD_DIM]
        v_h = kv[:, SWA_KV + hk * HEAD_DIM:SWA_KV + (hk + 1) * HEAD_DIM]
        for g in range(group):
            h = hk * group + g
            qh = q[:, h * HEAD_DIM:(h + 1) * HEAD_DIM]
            s = lax.dot_general(qh, k_h, (((1,), (1,)), ((), ())), preferred_element_type=F32)
            s = jnp.where(valid, s, NEG_INF)
            sk = sink_ref[h]
            mx = jnp.maximum(jnp.max(s, axis=-1, keepdims=True), sk)
            p = jnp.exp(s - mx)
            den = jnp.sum(p, axis=-1, keepdims=True) + jnp.exp(sk - mx)
            o = jnp.dot(p.astype(BF16), v_h, preferred_element_type=F32)
            outs.append(o * (1.0 / den))
    o_ref[...] = jnp.concatenate(outs, axis=1).astype(o_ref.dtype)


def _swa(q, kv, sink, n_batch, seq, ctx_len):
    m = q.shape[0]
    blk = SWA_BLOCK
    nb = seq // blk
    n_lat = n_batch * nb
    cpb = ctx_len // blk

    def batch_of(i):
        return jnp.where(i < n_lat, i // nb, (i - n_lat) // cpb)

    def prev_idx(i, s):
        return (jnp.where(i < n_lat, batch_of(i) * nb + jnp.maximum(i % nb - 1, 0), i), 0)

    def next_idx(i, s):
        return (jnp.where(i < n_lat, batch_of(i) * nb + jnp.minimum(i % nb + 1, nb - 1), i), 0)

    def ctx_idx(i, s):
        return (n_batch * seq // ctx_len + batch_of(i), 0)

    grid_spec = pltpu.PrefetchScalarGridSpec(
        num_scalar_prefetch=1,
        grid=(m // blk,),
        in_specs=[pl.BlockSpec((blk, SWA_Q), lambda i, s: (i, 0)),
                  pl.BlockSpec((blk, 2 * SWA_KV), prev_idx),
                  pl.BlockSpec((blk, 2 * SWA_KV), lambda i, s: (i, 0)),
                  pl.BlockSpec((blk, 2 * SWA_KV), next_idx),
                  pl.BlockSpec((ctx_len, 2 * SWA_KV), ctx_idx)],
        out_specs=pl.BlockSpec((blk, SWA_Q), lambda i, s: (i, 0)),
    )
    return pl.pallas_call(
        functools.partial(_swa_kernel, nb=nb, n_lat_blocks=n_lat),
        out_shape=jax.ShapeDtypeStruct((m, SWA_Q), BF16),
        grid_spec=grid_spec,
        compiler_params=_cp(("parallel",)),
        name="swa_attn",
    )(sink, q, kv, kv, kv, kv)


def _diff_kernel(*refs, n_lat_chunks, coef):
    if n_lat_chunks:
        lam_ref, gcol_ref, q_ref, kc_ref, vtc_ref, kl_ref, vtl_ref, o_ref, m_sc, acc_sc, st_a, st_b = refs
    else:
        lam_ref, gcol_ref, q_ref, kc_ref, vtc_ref, o_ref, m_sc, acc_sc, st_a, st_b = refs
    q = q_ref[...]
    tq = q.shape[0]
    dv = DIFF_V_DIM
    lane = lax.broadcasted_iota(jnp.int32, q.shape, 1)
    zero = jnp.zeros_like(q)
    qq = jnp.concatenate([jnp.where(lane < HEAD_DIM, q, zero), jnp.where(lane >= HEAD_DIM, q, zero)], axis=0)

    def scores(k):
        return lax.dot_general(k, qq, (((1,), (1,)), ((), ())), preferred_element_type=F32)

    def accumulate(st_ref, vt):
        st = st_ref[...]
        m_old = m_sc[...]
        m_new = jnp.maximum(m_old, jnp.max(st, axis=0, keepdims=True))
        alpha = jnp.exp2(m_old - m_new)
        pt = jnp.exp2((st - m_new).astype(BF16))
        acc_sc[...] = alpha * acc_sc[...] + jnp.dot(vt, pt, preferred_element_type=F32)
        m_sc[...] = m_new

    k_lat = lambda c: kl_ref[pl.ds(pl.multiple_of(c * DIFF_TK, DIFF_TK), DIFF_TK), :]
    m_sc[...] = jnp.full(m_sc.shape, NEG_INF, F32)
    acc_sc[...] = jnp.zeros(acc_sc.shape, F32)
    bufs = (st_a, st_b)
    n = n_lat_chunks
    if n:
        unroll = DIFF_UNROLL
        st_a[...] = scores(k_lat(0))
        n_trips = (n - 1) // unroll

        def body(j, carry):
            for u in range(unroll):
                c = j * unroll + u
                bufs[(u + 1) % 2][...] = scores(k_lat(c + 1))
                accumulate(bufs[u % 2], vtl_ref[0, c])
            return carry

        lax.fori_loop(0, n_trips, body, 0)
        for c in range(n_trips * unroll, n):
            bufs[(c + 1) % 2][...] = scores(k_lat(c + 1) if c + 1 < n else kc_ref[...])
            accumulate(bufs[c % 2], vtl_ref[0, c])
    else:
        st_a[...] = scores(kc_ref[...])
    accumulate(bufs[n % 2], vtc_ref[0, 0])
    acc = acc_sc[...]
    ot = acc[:dv] * (1.0 / acc[dv:dv + 1])
    odt = ot[:, :tq] - lam_ref[0] * ot[:, tq:]
    ms = jnp.mean(odt * odt, axis=0, keepdims=True)
    yt = odt * lax.rsqrt(ms + DIFF_SUBLN_EPS) * (gcol_ref[...] * coef)
    o_ref[...] = yt.T.astype(o_ref.dtype)


def _diff(q, k, v, lam, subln_g, lam_init, n_batch, seq, ctx_len):
    m = q.shape[0]
    tk = DIFF_TK
    assert ctx_len == tk and seq % tk == 0
    dv = DIFF_V_DIM
    vt = v.reshape(m // tk, tk, DIFF_HEADS, dv).transpose(2, 0, 3, 1)
    vt = jnp.concatenate([vt, jnp.ones(vt.shape[:2] + (DIFF_ONES_ROWS, tk), vt.dtype)], axis=2)
    dva = dv + DIFF_ONES_ROWS
    gcol = subln_g.reshape(dv, 1)
    lat_chunks = seq // tk
    ctx0 = n_batch * seq // ctx_len

    def call(tq, n_q, q_block0, batch_of, with_lat):
        in_specs = [pl.BlockSpec((dv, 1), lambda h, i, s: (0, 0)),
                    pl.BlockSpec((tq, LANES), lambda h, i, s: (q_block0 + i, h)),
                    pl.BlockSpec((ctx_len, LANES), lambda h, i, s: (ctx0 + batch_of(i), h)),
                    pl.BlockSpec((1, 1, dva, tk), lambda h, i, s: (h, ctx0 + batch_of(i), 0, 0))]
        args = [lam, gcol, q, k, vt]
        if with_lat:
            in_specs += [pl.BlockSpec((seq, LANES), lambda h, i, s: (batch_of(i), h)),
                         pl.BlockSpec((1, lat_chunks, dva, tk), lambda h, i, s: (h, batch_of(i), 0, 0))]
            args += [k, vt]
        grid_spec = pltpu.PrefetchScalarGridSpec(
            num_scalar_prefetch=1,
            grid=(DIFF_HEADS, n_q),
            in_specs=in_specs,
            out_specs=pl.BlockSpec((tq, LANES), lambda h, i, s: (i, h)),
            scratch_shapes=[pltpu.VMEM((1, 2 * tq), F32), pltpu.VMEM((dva, 2 * tq), F32),
                            pltpu.VMEM((tk, 2 * tq), F32), pltpu.VMEM((tk, 2 * tq), F32)],
        )
        return pl.pallas_call(
            functools.partial(_diff_kernel, n_lat_chunks=lat_chunks if with_lat else 0, coef=1.0 - lam_init),
            out_shape=jax.ShapeDtypeStruct((n_q * tq, DIFF_V), BF16),
            grid_spec=grid_spec,
            compiler_params=_cp(("parallel", "arbitrary")),
            name="diff_attn" if with_lat else "diff_attn_ctx",
        )(*args)

    tq = DIFF_TQ
    y_lat = call(tq, n_batch * seq // tq, 0, lambda i: i // (seq // tq), True)
    y_ctx = call(ctx_len, n_batch, ctx0, lambda i: i, False)
    return jnp.concatenate([y_lat, y_ctx], axis=0)


def _rwkv_prep_kernel(p_ref, hp_ref, hn_ref, mup_ref, mun_ref, kk_w_ref, ka_ref, rk_ref, w0_ref, a0_ref,
                      wup_ref, aup_ref, gup_ref, ones_ref,
                      r_ref, v_ref, kk_ref, ld_ref, ke_ref, bb_ref, g_ref, bonus_ref, sc,
                      *, tm, lat_rows, seq, ctx_len):
    i = pl.program_id(0)
    w = RWKV_WIDTH
    sc[0:8, :] = hp_ref[...]
    sc[8:8 + tm, :] = p_ref[...]
    sc[8 + tm:16 + tm, :] = hn_ref[...]
    p = p_ref[...]
    prev = sc[7:7 + tm, :]
    nxt = sc[9:9 + tm, :]
    row = i * tm + lax.broadcasted_iota(jnp.int32, (tm, 1), 0)
    pos = jnp.where(row < lat_rows, row % seq, (row - lat_rows) % ctx_len)
    seg_len = jnp.where(row < lat_rows, seq, ctx_len)
    prev = jnp.where(pos == 0, 0.0, prev)
    nxt = jnp.where(pos == seg_len - 1, 0.0, nxt)
    ps = p + mup_ref[...] * (prev - p) + mun_ref[...] * (nxt - p)

    r = ps[:, 0:w]
    k = ps[:, w:2 * w]
    v = ps[:, 2 * w:3 * w]
    wd = ps[:, 3 * w:3 * w + 2 * DECAY_LORA]
    ad = ps[:, 3 * w + 2 * DECAY_LORA:3 * w + 2 * DECAY_LORA + 2 * AAA_LORA]
    gd = ps[:, 3 * w + 2 * DECAY_LORA + 2 * AAA_LORA:]

    ones_bd = ones_ref[...]
    g = jnp.dot(_sigmoid(gd).astype(BF16), gup_ref[...], preferred_element_type=F32)
    kk = k * kk_w_ref[...]
    ss = jnp.dot(kk * kk, ones_bd, precision=HI, preferred_element_type=F32)
    kk = kk / jnp.maximum(jnp.sqrt(ss), 1e-12)
    w_raw = w0_ref[...] + jnp.dot(jnp.tanh(wd).astype(BF16), wup_ref[...], preferred_element_type=F32)
    a_raw = a0_ref[...] + jnp.dot(ad.astype(BF16), aup_ref[...], preferred_element_type=F32)
    ld = -math.exp(-0.5) * _sigmoid(w_raw)
    a = _sigmoid(a_raw)
    ka = ka_ref[...]
    ke_sum = jnp.zeros_like(k)
    for d in range(2):
        a_d = a[:, d * w:(d + 1) * w]
        ke = k * (1.0 + (a_d - 1.0) * ka)
        ld_ref[d] = ld[:, d * w:(d + 1) * w]
        ke_ref[d] = ke
        bb_ref[d] = kk * a_d
        ke_sum = ke_sum + ke
    rk = jnp.dot(r * ke_sum * rk_ref[...], ones_bd, precision=HI, preferred_element_type=F32)
    r_ref[...] = r
    v_ref[...] = v
    kk_ref[...] = kk
    g_ref[...] = g
    bonus_ref[...] = rk * v


def _rwkv_prep(p, lp, n_batch, seq, ctx_len):
    m, cols = p.shape
    tm = 256
    w = RWKV_WIDTH
    lat_rows = n_batch * seq
    row = lambda a: a.reshape(1, -1).astype(F32)

    def blockdiag(u):
        z = jnp.zeros_like(u[0])
        return jnp.concatenate([jnp.concatenate([u[0], z], axis=1), jnp.concatenate([z, u[1]], axis=1)], axis=0)

    head = jnp.arange(w) // HEAD_DIM
    ones_bd = (head[:, None] == head[None, :]).astype(F32)
    full = lambda shape: pl.BlockSpec(shape, lambda i: (0,) * len(shape))
    nb8 = m // 8
    outs = pl.pallas_call(
        functools.partial(_rwkv_prep_kernel, tm=tm, lat_rows=lat_rows, seq=seq, ctx_len=ctx_len),
        out_shape=[jax.ShapeDtypeStruct((m, w), F32)] * 3
        + [jax.ShapeDtypeStruct((2, m, w), F32)] * 3
        + [jax.ShapeDtypeStruct((m, w), F32)] * 2,
        grid=(m // tm,),
        in_specs=[pl.BlockSpec((tm, cols), lambda i: (i, 0)),
                  pl.BlockSpec((8, cols), lambda i: (jnp.maximum(i * (tm // 8) - 1, 0), 0)),
                  pl.BlockSpec((8, cols), lambda i: (jnp.minimum((i + 1) * (tm // 8), nb8 - 1), 0)),
                  full((1, cols)), full((1, cols)), full((1, w)), full((1, w)), full((1, w)),
                  full((1, 2 * w)), full((1, 2 * w)),
                  full((2 * DECAY_LORA, 2 * w)), full((2 * AAA_LORA, 2 * w)), full((GATE_LORA, w)),
                  full((w, w))],
        out_specs=[pl.BlockSpec((tm, w), lambda i: (i, 0))] * 3
        + [pl.BlockSpec((2, tm, w), lambda i: (0, i, 0))] * 3
        + [pl.BlockSpec((tm, w), lambda i: (i, 0))] * 2,
        scratch_shapes=[pltpu.VMEM((tm + 16, cols), F32)],
        compiler_params=_cp(("parallel",)),
        name="rwkv_prep",
    )(p, p, p, row(lp['rwkv_mu_prev']), row(lp['rwkv_mu_next']), row(lp['rwkv_k_k']), row(lp['rwkv_k_a']),
      row(lp['rwkv_r_k']), row(lp['rwkv_w0']), row(lp['rwkv_a0']),
      blockdiag(lp['rwkv_w_up']).astype(BF16), blockdiag(lp['rwkv_a_up']).astype(BF16),
      lp['rwkv_g_up'].astype(BF16), ones_bd)
    return outs


def _rwkv_chunk_prep(d, off, r_ref, v_ref, kk_ref, ld_ref, ke_ref, bb_ref, incl):
    c = RWKV_CHUNK
    ld = ld_ref[0, pl.ds(off, c), :]
    r = r_ref[pl.ds(off, c), :]
    v = v_ref[pl.ds(off, c), :]
    kk = kk_ref[pl.ds(off, c), :]
    ke = ke_ref[0, pl.ds(off, c), :]
    bb = bb_ref[0, pl.ds(off, c), :]
    cum = jnp.dot(incl, ld, precision=HI, preferred_element_type=F32)
    tot = jnp.sum(ld, axis=0, keepdims=True)
    rt = r * jnp.exp(cum)
    einv = jnp.exp(-cum)
    etail = jnp.exp(tot - cum)
    return dict(at=(-kk * jnp.exp(cum - ld)).astype(BF16), rt=rt, rt_b=rt.astype(BF16),
                bt=(bb * einv).astype(BF16), kt=(ke * einv).astype(BF16),
                bh=(bb * etail).astype(BF16), kh=(ke * etail).astype(BF16),
                v=v, v_b=v.astype(BF16), wtot=jnp.exp(tot))


def _rwkv_chunk_pair(offs, in_refs, y_refs, s_sc):
    c = RWKV_CHUNK
    hd = HEAD_DIM
    ti = lax.broadcasted_iota(jnp.int32, (c, c), 0)
    tj = lax.broadcasted_iota(jnp.int32, (c, c), 1)
    eye = (ti == tj).astype(F32)
    incl = [(tj <= ti).astype(F32), (tj >= ti).astype(F32)]
    strict = [m - eye for m in incl]
    diag_mask = ((ti // RWKV_INV_BASE) == (tj // RWKV_INV_BASE)).astype(F32)
    off_masks = []
    sz = RWKV_INV_BASE
    while sz < c:
        off_masks.append((((ti // (2 * sz)) == (tj // (2 * sz))) & ((ti // sz) != (tj // sz))).astype(F32))
        sz *= 2

    pre = [_rwkv_chunk_prep(d, offs[d], *in_refs[d], incl[d]) for d in range(2)]
    lanes = [(d, h) for d in range(2) for h in range(RWKV_HEADS)]
    sl = lambda h: slice(h * hd, (h + 1) * hd)
    get = lambda name: [pre[d][name][:, sl(h)] for d, h in lanes]
    at, rt, rt_b, bt, kt, bh, kh, v, v_b = (get(n) for n in ('at', 'rt', 'rt_b', 'bt', 'kt', 'bh', 'kh', 'v', 'v_b'))
    nl = range(len(lanes))
    nt = (((1,), (1,)), ((), ()))
    bdot = lambda x, y: jnp.dot(x.astype(BF16), y.astype(BF16), preferred_element_type=F32)

    gm = [lax.dot_general(jnp.concatenate([at[i], rt_b[i]], axis=0), jnp.concatenate([bt[i], kt[i]], axis=0), nt,
                          preferred_element_type=F32) for i in nl]
    zz0 = [bdot(v[i].T, kh[i]) for i in nl]
    aab = [gm[i][:c, :c] * strict[lanes[i][0]] for i in nl]
    aak = [gm[i][:c, c:] * strict[lanes[i][0]] for i in nl]
    arb = [gm[i][c:, :c] * incl[lanes[i][0]] for i in nl]
    ark = [gm[i][c:, c:] * incl[lanes[i][0]] for i in nl]
    av = [bdot(jnp.concatenate([aak[i], ark[i]], axis=0), v_b[i]) for i in nl]
    pw = [aab[i] * diag_mask for i in nl]
    tm_ = [eye + pw[i] for i in nl]
    for _ in range(int(math.log2(RWKV_INV_BASE)) - 1):
        pw = [bdot(pw[i], pw[i]) for i in nl]
        tm_ = [tm_[i] + bdot(tm_[i], pw[i]) for i in nl]
    for off_mask in off_masks:
        tn = [bdot(tm_[i], aab[i] * off_mask) for i in nl]
        tm_ = [tm_[i] + bdot(tn[i], tm_[i]) for i in nl]
    au = [bdot(tm_[i], jnp.concatenate([at[i], av[i][:c].astype(BF16)], axis=1)) for i in nl]
    ry = [bdot(arb[i], au[i]) for i in nl]
    mz = [bdot(au[i].T, bh[i]) for i in nl]
    s_old = [s_sc[d, h].astype(BF16) for d, h in lanes]
    ys = [lax.dot_general((rt[i] + ry[i][:, :hd]).astype(BF16), s_old[i], nt, preferred_element_type=F32)
          + ry[i][:, hd:] + av[i][c:] for i in nl]
    s_new = [jnp.dot(s_old[i], (eye * pre[lanes[i][0]]['wtot'][:, sl(lanes[i][1])] + mz[i][:hd]).astype(BF16),
                     preferred_element_type=F32) + mz[i][hd:] + zz0[i] for i in nl]
    for i, (d, h) in enumerate(lanes):
        s_sc[d, h] = s_new[i]
    for d in range(2):
        y_refs[d][0, pl.ds(offs[d], c), :] = jnp.concatenate(
            [ys[i] for i in nl if lanes[i][0] == d], axis=1)


def _rwkv_scan_kernel(rf, vf, kf, ldf, kef, bbf, rb, vb, kb, ldb, keb, bbb, yf, yb, s_sc, *, n_chunks):
    @pl.when(pl.program_id(1) == 0)
    def _():
        s_sc[...] = jnp.zeros(s_sc.shape, F32)

    def body(cc, carry):
        off_f = pl.multiple_of(cc * RWKV_CHUNK, RWKV_CHUNK)
        off_b = pl.multiple_of((n_chunks - 1 - cc) * RWKV_CHUNK, RWKV_CHUNK)
        _rwkv_chunk_pair((off_f, off_b), ((rf, vf, kf, ldf, kef, bbf), (rb, vb, kb, ldb, keb, bbb)),
                         (yf, yb), s_sc)
        return carry

    lax.fori_loop(0, n_chunks, body, 0)


def _rwkv_scan(r, v, kk, ld, ke, bb, n_batch, seq, ctx_len):
    m, w = r.shape
    ts = RWKV_STEP_ROWS
    assert ctx_len == ts
    lpb = seq // ts
    ctx0 = n_batch * seq // ts
    nj = 1 + lpb

    def fwd(b, j):
        return jnp.where(j == 0, ctx0 + b, b * lpb + j - 1)

    def bwd(b, j):
        return jnp.where(j == 0, ctx0 + b, b * lpb + lpb - j)

    shared = lambda f: pl.BlockSpec((ts, w), lambda b, j: (f(b, j), 0))
    per_dir = lambda f, d: pl.BlockSpec((1, ts, w), lambda b, j: (d, f(b, j), 0))
    y = pl.pallas_call(
        functools.partial(_rwkv_scan_kernel, n_chunks=ts // RWKV_CHUNK),
        out_shape=[jax.ShapeDtypeStruct((1, m, w), F32)] * 2,
        grid=(n_batch, nj),
        in_specs=[shared(fwd), shared(fwd), shared(fwd), per_dir(fwd, 0), per_dir(fwd, 0), per_dir(fwd, 0),
                  shared(bwd), shared(bwd), shared(bwd), per_dir(bwd, 1), per_dir(bwd, 1), per_dir(bwd, 1)],
        out_specs=[pl.BlockSpec((1, ts, w), lambda b, j: (0, fwd(b, j), 0)),
                   pl.BlockSpec((1, ts, w), lambda b, j: (0, bwd(b, j), 0))],
        scratch_shapes=[pltpu.VMEM((2, RWKV_HEADS, HEAD_DIM, HEAD_DIM), F32)],
        compiler_params=_cp(("parallel", "arbitrary")),
        name="rwkv_scan",
    )(r, v, kk, ld, ke, bb, r, v, kk, ld, ke, bb)
    return y


def _rwkv_out_kernel(yf_ref, yb_ref, bonus_ref, g_ref, lg_ref, lb_ref, ones_ref, o_ref):
    y = yf_ref[0] + yb_ref[0]
    ones_bd = ones_ref[...]
    inv = 1.0 / HEAD_DIM
    mu = jnp.dot(y, ones_bd, precision=HI, preferred_element_type=F32) * inv
    yc = y - mu
    var = jnp.dot(yc * yc, ones_bd, precision=HI, preferred_element_type=F32) * inv
    yn = yc * lax.rsqrt(var + RWKV_GN_EPS) * lg_ref[...] + lb_ref[...]
    o_ref[...] = ((yn + bonus_ref[...]) * g_ref[...]).astype(o_ref.dtype)


def _rwkv_out(yf, yb, bonus, g, lnx_g, lnx_b):
    m, w = bonus.shape
    tm = ROW_TILE
    head = jnp.arange(w) // HEAD_DIM
    ones_bd = (head[:, None] == head[None, :]).astype(F32)
    blk = pl.BlockSpec((tm, w), lambda i: (i, 0))
    blk3 = pl.BlockSpec((1, tm, w), lambda i: (0, i, 0))
    one = pl.BlockSpec((1, w), lambda i: (0, 0))
    return pl.pallas_call(
        _rwkv_out_kernel,
        out_shape=jax.ShapeDtypeStruct((m, w), BF16),
        grid=(m // tm,),
        in_specs=[blk3, blk3, blk, blk, one, one, pl.BlockSpec((w, w), lambda i: (0, 0))],
        out_specs=blk,
        compiler_params=_cp(("parallel",)),
        name="rwkv_out",
    )(yf, yb, bonus, g, lnx_g.reshape(1, w), lnx_b.reshape(1, w), ones_bd)


def _merge_kernel(ya_ref, yb_ref, yc_ref, gt_ref, h_ref, pa_ref, pb_ref, pc_ref, wo_ref, gpost_ref, gpre_ref,
                  mod_ref, h_out, f_out, *, d):
    gates = gt_ref[...].astype(F32)
    merged = (_sigmoid(gates[:, 0:d]) * jnp.dot(ya_ref[...], pa_ref[...], preferred_element_type=F32)
              + _sigmoid(gates[:, d:2 * d]) * jnp.dot(yb_ref[...], pb_ref[...], preferred_element_type=F32)
              + _sigmoid(gates[:, 2 * d:3 * d]) * jnp.dot(yc_ref[...], pc_ref[...], preferred_element_type=F32))
    out = jnp.dot(merged.astype(BF16), wo_ref[...], preferred_element_type=F32)
    mod = lambda idx: mod_ref[0, :, idx * d:(idx + 1) * d]
    hn = h_ref[...] + mod(2) * (_rms(out, NORM_EPS) * gpost_ref[...])
    h_out[...] = hn
    f = (_rms(hn, NORM_EPS) * gpre_ref[...]) * (1.0 + mod(4)) + mod(3)
    f_out[...] = f.astype(f_out.dtype)


def _merge(ya, yb, yc, gates, h, pa, pb, pc, wo, g_post, g_pre, mods, f_dtype, lat_bpb, n_batch):
    m, d = h.shape
    tm = 256
    bpb = lat_bpb * (ROW_TILE // tm)
    seg = functools.partial(_seg_of_block, lat_blocks_per_batch=bpb, n_batch=n_batch)
    rows = lambda width: pl.BlockSpec((tm, width), lambda i: (i, 0))
    full = lambda a: pl.BlockSpec(a.shape, lambda i: (0, 0))
    return pl.pallas_call(
        functools.partial(_merge_kernel, d=d),
        out_shape=[jax.ShapeDtypeStruct((m, d), F32), jax.ShapeDtypeStruct((m, d), f_dtype)],
        grid=(m // tm,),
        in_specs=[rows(ya.shape[1]), rows(yb.shape[1]), rows(yc.shape[1]), rows(gates.shape[1]), rows(d),
                  full(pa), full(pb), full(pc), full(wo),
                  pl.BlockSpec((1, d), lambda i: (0, 0)), pl.BlockSpec((1, d), lambda i: (0, 0)),
                  pl.BlockSpec((1, 1, mods.shape[2]), lambda i: (seg(i), 0, 0))],
        out_specs=[rows(d), rows(d)],
        compiler_params=_cp(("parallel",)),
        name="merge",
    )(ya, yb, yc, gates, h, pa, pb, pc, wo, g_post, g_pre, mods)


def _swiglu_hidden(x, wg, wu):
    hg = jnp.dot(x, wg, preferred_element_type=F32)
    hu = jnp.dot(x, wu, preferred_element_type=F32)
    return (hg * _sigmoid(hg) * hu).astype(BF16)


def _ffn_kernel(f_ref, wg_ref, wu_ref, wd_ref, h_ref, gpost_ref, mod_ref, o_ref, acc, *, d):
    j = pl.program_id(1)

    @pl.when(j == 0)
    def _():
        acc[...] = jnp.zeros(acc.shape, F32)

    hid = _swiglu_hidden(f_ref[...], wg_ref[...], wu_ref[...])
    acc[...] += jnp.dot(hid, wd_ref[...], preferred_element_type=F32)

    @pl.when(j == pl.num_programs(1) - 1)
    def _():
        gate = mod_ref[0, :, 5 * d:6 * d]
        o_ref[...] = h_ref[...] + gate * (_rms(acc[...], NORM_EPS) * gpost_ref[...])


def _ffn(f, wg, wu, wd, h, g_post, mods, lat_bpb, n_batch):
    m, d = h.shape
    ff = wg.shape[1]
    tm, tf = ROW_TILE, FFN_TF
    seg = functools.partial(_seg_of_block, lat_blocks_per_batch=lat_bpb, n_batch=n_batch)
    return pl.pallas_call(
        functools.partial(_ffn_kernel, d=d),
        out_shape=jax.ShapeDtypeStruct((m, d), F32),
        grid=(m // tm, ff // tf),
        in_specs=[pl.BlockSpec((tm, d), lambda i, j: (i, 0)),
                  pl.BlockSpec((d, tf), lambda i, j: (0, j)),
                  pl.BlockSpec((d, tf), lambda i, j: (0, j)),
                  pl.BlockSpec((tf, d), lambda i, j: (j, 0)),
                  pl.BlockSpec((tm, d), lambda i, j: (i, 0)),
                  pl.BlockSpec((1, d), lambda i, j: (0, 0)),
                  pl.BlockSpec((1, 1, mods.shape[2]), lambda i, j: (seg(i), 0, 0))],
        out_specs=pl.BlockSpec((tm, d), lambda i, j: (i, 0)),
        scratch_shapes=[pltpu.VMEM((tm, d), F32)],
        compiler_params=_cp(("parallel", "arbitrary")),
        name="ffn_dense",
    )(f, wg, wu, wd, h, g_post, mods)


def _router_kernel(f_ref, w_ref, idx_ref, wt_ref):
    logits = jnp.dot(f_ref[...], w_ref[...], precision=HI, preferred_element_type=F32)
    lane = lax.broadcasted_iota(jnp.int32, logits.shape, 1)
    logits = jnp.where(lane < N_EXPERTS, logits, -jnp.inf)
    m1 = jnp.max(logits, axis=-1, keepdims=True)
    i1 = jnp.min(jnp.where(logits == m1, lane, LANES), axis=-1, keepdims=True)
    rest = jnp.where(lane == i1, -jnp.inf, logits)
    m2 = jnp.max(rest, axis=-1, keepdims=True)
    i2 = jnp.min(jnp.where(rest == m2, lane, LANES), axis=-1, keepdims=True)
    e = jnp.exp(m2 - m1)
    w1 = 1.0 / (1.0 + e)
    w2 = e / (1.0 + e)
    idx_ref[...] = jnp.where(lane == 0, i1, jnp.where(lane == 1, i2, 0))
    wt_ref[...] = jnp.where(lane == 0, w1, jnp.where(lane == 1, w2, 0.0))


def _router(f, router_w, n_rows):
    d = f.shape[1]
    tm = ROW_TILE
    w_pad = jnp.zeros((d, LANES), F32).at[:, :N_EXPERTS].set(router_w)
    return pl.pallas_call(
        _router_kernel,
        out_shape=[jax.ShapeDtypeStruct((n_rows, LANES), jnp.int32), jax.ShapeDtypeStruct((n_rows, LANES), F32)],
        grid=(n_rows // tm,),
        in_specs=[pl.BlockSpec((tm, d), lambda i: (i, 0)), pl.BlockSpec((d, LANES), lambda i: (0, 0))],
        out_specs=[pl.BlockSpec((tm, LANES), lambda i: (i, 0))] * 2,
        compiler_params=_cp(("parallel",)),
        name="router",
    )(f, w_pad)


def _moe_gather_copy(f_hbm, xbuf, sem, src_row, dst_row):
    return pltpu.make_async_copy(f_hbm.at[pl.ds(src_row, 1), :], xbuf.at[pl.ds(dst_row, 1), :], sem)


def _moe_ffn_kernel(blk_e_ref, nused_ref, tok_ref, f_hbm, wg_ref, wu_ref, wd_ref, y_ref, xbuf, xb, acc, sem,
                    *, tm):
    i = pl.program_id(0)
    j = pl.program_id(1)
    active = i < nused_ref[0]

    @pl.when(active & (j == 0))
    def _():
        def issue(r, carry):
            _moe_gather_copy(f_hbm, xbuf, sem, tok_ref[i * tm + r], r).start()
            return carry

        lax.fori_loop(0, tm, issue, 0)
        pltpu.make_async_copy(f_hbm.at[pl.ds(0, tm), :], xbuf, sem).wait()
        xb[...] = xbuf[...].astype(BF16)
        acc[...] = jnp.zeros(acc.shape, F32)

    @pl.when(active)
    def _():
        hid = _swiglu_hidden(xb[...], wg_ref[0], wu_ref[0])
        acc[...] += jnp.dot(hid, wd_ref[0], preferred_element_type=F32)

    @pl.when(j == pl.num_programs(1) - 1)
    def _():
        y_ref[...] = acc[...]


def _moe_ffn(f, blk_e, nused, tok, wg, wu, wd, n_blocks):
    d = f.shape[1]
    ff = wg.shape[2]
    tm, tf = MOE_TILE, MOE_TF

    def e_of(i, be, nu):
        return be[jnp.minimum(i, nu[0] - 1)]

    grid_spec = pltpu.PrefetchScalarGridSpec(
        num_scalar_prefetch=3,
        grid=(n_blocks, ff // tf),
        in_specs=[pl.BlockSpec(memory_space=pl.ANY),
                  pl.BlockSpec((1, d, tf), lambda i, j, be, nu, tk: (e_of(i, be, nu), 0, j)),
                  pl.BlockSpec((1, d, tf), lambda i, j, be, nu, tk: (e_of(i, be, nu), 0, j)),
                  pl.BlockSpec((1, tf, d), lambda i, j, be, nu, tk: (e_of(i, be, nu), j, 0))],
        out_specs=pl.BlockSpec((tm, d), lambda i, j, be, nu, tk: (i, 0)),
        scratch_shapes=[pltpu.VMEM((tm, d), F32), pltpu.VMEM((tm, d), BF16), pltpu.VMEM((tm, d), F32),
                        pltpu.SemaphoreType.DMA(())],
    )
    return pl.pallas_call(
        functools.partial(_moe_ffn_kernel, tm=tm),
        out_shape=jax.ShapeDtypeStruct((n_blocks * tm, d), F32),
        grid_spec=grid_spec,
        compiler_params=_cp(("arbitrary", "arbitrary")),
        name="moe_ffn",
    )(blk_e, nused, tok, f, wg, wu, wd)


def _moe_combine_kernel(p0_ref, p1_ref, y_hbm, wt_ref, h_ref, gpost_ref, mod_ref, o_ref, b0, b1, sem, *, tm, d):
    i = pl.program_id(0)

    def issue(r, carry):
        pltpu.make_async_copy(y_hbm.at[pl.ds(p0_ref[i * tm + r], 1), :], b0.at[pl.ds(r, 1), :], sem.at[0]).start()
        pltpu.make_async_copy(y_hbm.at[pl.ds(p1_ref[i * tm + r], 1), :], b1.at[pl.ds(r, 1), :], sem.at[1]).start()
        return carry

    lax.fori_loop(0, tm, issue, 0)
    pltpu.make_async_copy(y_hbm.at[pl.ds(0, tm), :], b0, sem.at[0]).wait()
    pltpu.make_async_copy(y_hbm.at[pl.ds(0, tm), :], b1, sem.at[1]).wait()
    wt = wt_ref[...]
    y = b0[...] * wt[:, 0:1] + b1[...] * wt[:, 1:2]
    gate = mod_ref[0, :, 5 * d:6 * d]
    o_ref[...] = h_ref[...] + gate * (_rms(y, NORM_EPS) * gpost_ref[...])


def _moe_combine(pos0, pos1, y, wt, h, g_post, mods, n_rows, rows_per_batch, n_batch):
    d = h.shape[1]
    tm = COMBINE_TILE
    seg = functools.partial(_seg_of_block, lat_blocks_per_batch=rows_per_batch // tm, n_batch=n_batch)
    grid_spec = pltpu.PrefetchScalarGridSpec(
        num_scalar_prefetch=2,
        grid=(n_rows // tm,),
        in_specs=[pl.BlockSpec(memory_space=pl.ANY),
                  pl.BlockSpec((tm, LANES), lambda i, a, b: (i, 0)),
                  pl.BlockSpec((tm, d), lambda i, a, b: (i, 0)),
                  pl.BlockSpec((1, d), lambda i, a, b: (0, 0)),
                  pl.BlockSpec((1, 1, mods.shape[2]), lambda i, a, b: (seg(i), 0, 0))],
        out_specs=pl.BlockSpec((tm, d), lambda i, a, b: (i, 0)),
        scratch_shapes=[pltpu.VMEM((tm, d), F32), pltpu.VMEM((tm, d), F32), pltpu.SemaphoreType.DMA((2,))],
    )
    return pl.pallas_call(
        functools.partial(_moe_combine_kernel, tm=tm, d=d),
        out_shape=jax.ShapeDtypeStruct((n_rows, d), F32),
        grid_spec=grid_spec,
        compiler_params=_cp(("arbitrary",)),
        name="moe_combine",
    )(pos0, pos1, y, wt, h, g_post, mods)


def _moe_slots(top_i, tile):
    n = top_i.shape[0]
    a = n * 2
    e_flat = top_i.reshape(a)
    onehot = (e_flat[:, None] == jnp.arange(N_EXPERTS, dtype=jnp.int32)[None, :]).astype(jnp.int32)
    csum = jnp.cumsum(onehot, axis=0)
    rank = jnp.sum(csum * onehot, axis=1) - 1
    counts = csum[-1]
    padded = (counts + tile - 1) // tile * tile
    pends = jnp.cumsum(padded)
    pstarts = pends - padded
    dest = (jnp.sum(onehot * pstarts[None, :], axis=1) + rank).astype(jnp.int32)
    n_blocks = a // tile + N_EXPERTS
    tok = jnp.zeros((n_blocks * tile,), jnp.int32).at[dest].set(jnp.arange(a, dtype=jnp.int32) // 2)
    block_start = jnp.arange(n_blocks, dtype=jnp.int32) * tile
    blk_e = jnp.minimum(jnp.searchsorted(pends, block_start, side='right'), N_EXPERTS - 1).astype(jnp.int32)
    nused = (pends[-1:] // tile).astype(jnp.int32)
    return tok, blk_e, nused, dest.reshape(n, 2), n_blocks


def _rope_tables(n_batch, seq, ctx_len):
    t = jnp.arange(seq, dtype=jnp.int32)
    row = (t // GRID_W).astype(F32)
    col = (t % GRID_W).astype(F32)
    axis_dim = HEAD_DIM // 2
    inv_freq = ROPE_THETA ** (-jnp.arange(0, axis_dim, 2, dtype=F32) / axis_dim)
    dd = jnp.arange(LANES) % HEAD_DIM
    pos = jnp.where((dd // axis_dim)[None, :] == 0, row[:, None], col[:, None])
    ang = pos * inv_freq[dd % (axis_dim // 2)][None, :]
    cos = jnp.cos(ang)
    sin = jnp.where(((dd % axis_dim) < axis_dim // 2)[None, :], -jnp.sin(ang), jnp.sin(ang))
    n_ctx = n_batch * ctx_len
    cos = jnp.concatenate([jnp.tile(cos, (n_batch, 1)), jnp.ones((n_ctx, LANES), F32)], axis=0)
    sin = jnp.concatenate([jnp.tile(sin, (n_batch, 1)), jnp.zeros((n_ctx, LANES), F32)], axis=0)
    return cos, sin


def kernel(x, c, ctx, c_ctx, ada_w, ada_b, pre_mix_g, post_mix_g, pre_ffn_g, post_ffn_g, w_in, swa_sink,
           rwkv_mu_prev, rwkv_mu_next, rwkv_w0, rwkv_w_up, rwkv_a0, rwkv_a_up, rwkv_g_up, rwkv_k_k, rwkv_k_a,
           rwkv_r_k, rwkv_lnx_g, rwkv_lnx_b, diff_lambda, diff_subln_g, proj_swa, proj_rwkv, proj_diff, w_out,
           ffn_w_gate, ffn_w_up, ffn_w_down, router_w, moe_w_gate, moe_w_up, moe_w_down):
    n_batch, seq, d = x.shape
    ctx_len = ctx.shape[1]
    depth = w_in.shape[0]
    lat_rows = n_batch * seq
    lat_bpb = seq // ROW_TILE
    assert seq % ROW_TILE == 0 and (n_batch * ctx_len) % ROW_TILE == 0

    h = jnp.concatenate([x.reshape(lat_rows, d), ctx.reshape(n_batch * ctx_len, d)], axis=0)
    m = h.shape[0]
    cond = jnp.zeros((8, d), F32).at[:n_batch].set(c).at[n_batch].set(c_ctx)
    rope = _rope_tables(n_batch, seq, ctx_len)
    row = lambda a: a.reshape(1, -1)

    o_swa = 0
    o_rwkv = o_swa + SWA_Q + 2 * SWA_KV
    o_diff = o_rwkv + RWKV_COLS
    o_gate = o_diff + 2 * DIFF_QK + DIFF_V
    o_end = o_gate + 3 * d

    for layer in range(depth):
        mods = _ada(cond, ada_w[layer], ada_b[layer].reshape(1, -1))[:, None, :]
        wl = w_in[layer].astype(BF16)
        a = _prenorm(h, row(pre_mix_g[layer]), mods, 0, 1, lat_bpb, n_batch)

        q_swa, kv_swa = _proj(a, wl[:, o_swa:o_rwkv], (SWA_Q, 2 * SWA_KV), (BF16, BF16), rope=rope,
                              rope_cols=SWA_Q + SWA_KV, scale_cols=SWA_Q, q_scale=HEAD_DIM ** -0.5)
        (p_rwkv,) = _proj(a, wl[:, o_rwkv:o_diff], (RWKV_COLS,), (F32,))
        q_diff, k_diff, v_diff = _proj(a, wl[:, o_diff:o_gate], (DIFF_QK, DIFF_QK, DIFF_V), (BF16,) * 3,
                                       rope=rope, rope_cols=2 * DIFF_QK, scale_cols=DIFF_QK,
                                       q_scale=HEAD_DIM ** -0.5 * math.log2(math.e))
        (gates,) = _proj(a, wl[:, o_gate:o_end], (d,), (BF16,), tn=d)

        ya = _swa(q_swa, kv_swa, swa_sink[layer].astype(F32), n_batch, seq, ctx_len)

        lp = {'rwkv_mu_prev': rwkv_mu_prev[layer], 'rwkv_mu_next': rwkv_mu_next[layer],
              'rwkv_w0': rwkv_w0[layer], 'rwkv_w_up': rwkv_w_up[layer], 'rwkv_a0': rwkv_a0[layer],
              'rwkv_a_up': rwkv_a_up[layer], 'rwkv_g_up': rwkv_g_up[layer], 'rwkv_k_k': rwkv_k_k[layer],
              'rwkv_k_a': rwkv_k_a[layer], 'rwkv_r_k': rwkv_r_k[layer]}
        r_, v_, kk_, ld_, ke_, bb_, g_, bonus_ = _rwkv_prep(p_rwkv, lp, n_batch, seq, ctx_len)
        y_f, y_b = _rwkv_scan(r_, v_, kk_, ld_, ke_, bb_, n_batch, seq, ctx_len)
        yb = _rwkv_out(y_f, y_b, bonus_, g_, rwkv_lnx_g[layer], rwkv_lnx_b[layer])

        lam_vec = diff_lambda[layer].astype(F32)
        lam_init = 0.8 - 0.6 * math.exp(-0.3 * layer)
        lam = (jnp.exp(jnp.sum(lam_vec[0] * lam_vec[1])) - jnp.exp(jnp.sum(lam_vec[2] * lam_vec[3]))
               + lam_init).reshape(1)
        yc = _diff(q_diff, k_diff, v_diff, lam, row(diff_subln_g[layer]), lam_init, n_batch, seq, ctx_len)

        moe_layer = layer % 2 == 1
        jj = layer // 2
        h, f = _merge(ya, yb, yc, gates, h, proj_swa[layer].astype(BF16), proj_rwkv[layer].astype(BF16),
                      proj_diff[layer].astype(BF16), w_out[layer].astype(BF16), row(post_mix_g[layer]),
                      row(pre_ffn_g[layer]), mods, F32 if moe_layer else BF16, lat_bpb, n_batch)
        need_ctx = layer < depth - 1
        if not moe_layer:
            h = _ffn(f, ffn_w_gate[jj].astype(BF16), ffn_w_up[jj].astype(BF16), ffn_w_down[jj].astype(BF16),
                     h, row(post_ffn_g[layer]), mods, lat_bpb, n_batch)
        else:
            n_tok = m if need_ctx else lat_rows
            top_i, top_w = _router(f, router_w[jj], n_tok)
            tok, blk_e, nused, dest, n_blocks = _moe_slots(top_i[:, :2], MOE_TILE)
            y = _moe_ffn(f, blk_e, nused, tok, moe_w_gate[jj].astype(BF16), moe_w_up[jj].astype(BF16),
                         moe_w_down[jj].astype(BF16), n_blocks)
            h = _moe_combine(dest[:, 0], dest[:, 1], y, top_w, h, row(post_ffn_g[layer]), mods, n_tok, seq,
                             n_batch)
    return h[:lat_rows].reshape(n_batch, seq, d)
```

```python
import functools
import math

import jax
import jax.numpy as jnp
from jax import lax
from jax.experimental import pallas as pl
from jax.experimental.pallas import tpu as pltpu

F32 = jnp.float32
BF16 = jnp.bfloat16
HI = lax.Precision.HIGHEST

HEAD_DIM = 64
GRID_W = 64
ROPE_THETA = 10000.0
NORM_EPS = 1e-6
NEG_INF = -1e30
SWA_HEADS = 8
SWA_KV_HEADS = 2
SWA_BLOCK = 128
RWKV_HEADS = 8
RWKV_WIDTH = RWKV_HEADS * HEAD_DIM
DECAY_LORA = 64
AAA_LORA = 64
GATE_LORA = 128
RWKV_GN_EPS = 64e-5
DIFF_HEADS = 4
DIFF_V_DIM = 2 * HEAD_DIM
DIFF_SUBLN_EPS = 1e-5
N_EXPERTS = 8
SWA_Q = SWA_HEADS * HEAD_DIM
SWA_KV = SWA_KV_HEADS * HEAD_DIM
assert SWA_KV_HEADS == 2 and SWA_KV == 128
RWKV_COLS = 3 * RWKV_WIDTH + 2 * DECAY_LORA + 2 * AAA_LORA + GATE_LORA
DIFF_QK = DIFF_HEADS * 2 * HEAD_DIM
DIFF_V = DIFF_HEADS * DIFF_V_DIM

LANES = 128
VMEM_LIMIT = 48 * 1024 * 1024
ROW_TILE = 512
RWKV_CHUNK = 64
RWKV_INV_BASE = 8
RWKV_STEP_ROWS = 256
DIFF_TQ = 256
DIFF_TK = 1024
DIFF_UNROLL = 2
DIFF_ONES_ROWS = 16
MOE_TILE = 512
MOE_TF = 1792
FFN_TF = 1408
COMBINE_TILE = 256


def _cp(sem, **kw):
    return pltpu.CompilerParams(dimension_semantics=sem, vmem_limit_bytes=VMEM_LIMIT, **kw)


def _seg_of_block(i, lat_blocks_per_batch, n_batch):
    return jnp.minimum(i // lat_blocks_per_batch, n_batch)


def _rms(x, eps):
    return x * lax.rsqrt(jnp.mean(x * x, axis=-1, keepdims=True) + eps)


def _sigmoid(x):
    return 1.0 / (1.0 + jnp.exp(-x))


def _group_sum(x, ones_bd):
    hi = x.astype(BF16)
    r1 = x - hi.astype(F32)
    mid = r1.astype(BF16)
    lo = (r1 - mid.astype(F32)).astype(BF16)
    rows = x.shape[0]
    parts = jnp.dot(jnp.concatenate([hi, mid, lo], axis=0), ones_bd, preferred_element_type=F32)
    return parts[:rows] + parts[rows:2 * rows] + parts[2 * rows:]


def _ada_kernel(x_ref, w_ref, b_ref, o_ref):
    x = x_ref[...]
    s = x * _sigmoid(x)
    o_ref[...] = jnp.dot(s, w_ref[...], precision=HI, preferred_element_type=F32) + b_ref[...]


def _ada(cond, w, b):
    rows, d = cond.shape
    n = w.shape[1]
    return pl.pallas_call(
        _ada_kernel,
        out_shape=jax.ShapeDtypeStruct((rows, n), F32),
        grid=(n // d,),
        in_specs=[pl.BlockSpec((rows, d), lambda j: (0, 0)),
                  pl.BlockSpec((d, d), lambda j: (0, j)),
                  pl.BlockSpec((1, d), lambda j: (0, j))],
        out_specs=pl.BlockSpec((rows, d), lambda j: (0, j)),
        compiler_params=_cp(("parallel",)),
        name="ada_mod",
    )(cond, w, b)


def _prenorm_kernel(h_ref, g_ref, mod_ref, o_ref, *, d, shift_idx, scale_idx):
    y = _rms(h_ref[...], NORM_EPS) * g_ref[...]
    shift = mod_ref[0, :, shift_idx * d:(shift_idx + 1) * d]
    scale = mod_ref[0, :, scale_idx * d:(scale_idx + 1) * d]
    o_ref[...] = (y * (1.0 + scale) + shift).astype(o_ref.dtype)


def _prenorm(h, g, mods, shift_idx, scale_idx, lat_bpb, n_batch):
    m, d = h.shape
    tm = ROW_TILE
    seg = functools.partial(_seg_of_block, lat_blocks_per_batch=lat_bpb, n_batch=n_batch)
    return pl.pallas_call(
        functools.partial(_prenorm_kernel, d=d, shift_idx=shift_idx, scale_idx=scale_idx),
        out_shape=jax.ShapeDtypeStruct((m, d), BF16),
        grid=(m // tm,),
        in_specs=[pl.BlockSpec((tm, d), lambda i: (i, 0)),
                  pl.BlockSpec((1, d), lambda i: (0, 0)),
                  pl.BlockSpec((1, 1, mods.shape[2]), lambda i: (seg(i), 0, 0))],
        out_specs=pl.BlockSpec((tm, d), lambda i: (i, 0)),
        compiler_params=_cp(("parallel",)),
        name="prenorm",
    )(h, g, mods)


def _proj_kernel(*refs, splits, rope_cols, scale_cols, q_scale):
    if rope_cols:
        a_ref, w_ref, cos_ref, sin_ref = refs[:4]
        outs = refs[4:]
    else:
        a_ref, w_ref = refs[:2]
        outs = refs[2:]
    y = jnp.dot(a_ref[...], w_ref[...], preferred_element_type=F32)
    tm, tn = y.shape
    if rope_cols:
        cos = cos_ref[...]
        sin = sin_ref[...]
        lane = lax.broadcasted_iota(jnp.int32, (tm, LANES), 1)
        first_half = (lane % 32) < 16
        pieces = []
        for c in range(tn // LANES):
            yc = y[:, c * LANES:(c + 1) * LANES]
            if c * LANES < rope_cols:
                partner = jnp.where(first_half, pltpu.roll(yc, LANES - 16, 1), pltpu.roll(yc, 16, 1))
                yc = yc * cos + partner * sin
            if c * LANES < scale_cols:
                yc = yc * q_scale
            pieces.append(yc)
        y = jnp.concatenate(pieces, axis=1)
    start = 0
    for o_ref, width in zip(outs, splits):
        o_ref[...] = y[:, start:start + width].astype(o_ref.dtype)
        start += width


def _proj(a, w, splits, dtypes, rope=None, rope_cols=0, scale_cols=0, q_scale=1.0, tn=None):
    m, k = a.shape
    n = w.shape[1]
    tm = ROW_TILE
    tn = n if tn is None else tn
    assert sum(splits) == tn and (len(splits) == 1 or tn == n)
    in_specs = [pl.BlockSpec((tm, k), lambda i, j: (i, 0)),
                pl.BlockSpec((k, tn), lambda i, j: (0, j))]
    args = [a, w]
    if rope_cols:
        in_specs += [pl.BlockSpec((tm, LANES), lambda i, j: (i, 0))] * 2
        args += list(rope)
    out_specs = []
    out_shape = []
    if len(splits) == 1:
        out_specs.append(pl.BlockSpec((tm, tn), lambda i, j: (i, j)))
        out_shape.append(jax.ShapeDtypeStruct((m, n), dtypes[0]))
    else:
        for width, dt in zip(splits, dtypes):
            out_specs.append(pl.BlockSpec((tm, width), lambda i, j: (i, 0)))
            out_shape.append(jax.ShapeDtypeStruct((m, width), dt))
    res = pl.pallas_call(
        functools.partial(_proj_kernel, splits=tuple(splits), rope_cols=rope_cols, scale_cols=scale_cols,
                          q_scale=q_scale),
        out_shape=out_shape,
        grid=(m // tm, n // tn),
        in_specs=in_specs,
        out_specs=out_specs,
        compiler_params=_cp(("parallel", "parallel")),
        name="proj",
    )(*args)
    return res


def _swa_kernel(sink_ref, q_ref, kp_ref, kc_ref, kn_ref, kx_ref, o_ref, *, nb, n_lat_blocks):
    i = pl.program_id(0)
    is_lat = i < n_lat_blocks
    n = i % nb
    blk = SWA_BLOCK
    q = q_ref[...]
    kv = jnp.concatenate([kp_ref[...], kc_ref[...], kn_ref[...], kx_ref[...]], axis=0)
    nkeys = kv.shape[0]
    k_t = kv[:, :SWA_KV]
    v_aug = jnp.concatenate([kv[:, SWA_KV:], jnp.ones((nkeys, SWA_KV), kv.dtype)], axis=1)
    r = lax.broadcasted_iota(jnp.int32, (blk, nkeys), 0)
    j = lax.broadcasted_iota(jnp.int32, (blk, nkeys), 1)
    lo = jnp.where(n > 0, 0, blk)
    hi = jnp.where(n < nb - 1, 3 * blk, 2 * blk)
    valid_loc = (j >= r) & (j <= r + 2 * blk) & (j >= lo) & (j < hi) & is_lat
    bias = jnp.where(valid_loc | (j >= 3 * blk), 0.0, NEG_INF)
    low_half = lax.broadcasted_iota(jnp.int32, (blk, SWA_KV), 1) < HEAD_DIM
    zero = jnp.zeros((blk, SWA_KV), q.dtype)
    group = SWA_HEADS // SWA_KV_HEADS
    q_rows, sinks = [], []
    for t in range(group):
        qt = q[:, t * SWA_KV:(t + 1) * SWA_KV]
        q_rows += [jnp.where(low_half, qt, zero), jnp.where(low_half, zero, qt)]
        sinks += [jnp.full((blk, 1), sink_ref[t], F32), jnp.full((blk, 1), sink_ref[group + t], F32)]
    scores = [lax.dot_general(jnp.concatenate(q_rows[2 * t:2 * t + 2], axis=0), k_t, (((1,), (1,)), ((), ())),
                              preferred_element_type=F32) for t in range(group)]
    tiles = []
    for t in range(group):
        sk = jnp.concatenate(sinks[2 * t:2 * t + 2], axis=0)
        s = (scores[t].reshape(2, blk, nkeys) + bias[None]).reshape(2 * blk, nkeys)
        mx = jnp.maximum(jnp.max(s, axis=-1, keepdims=True), sk)
        p = jnp.exp((s - mx).astype(BF16))
        oa = jnp.dot(p, v_aug, preferred_element_type=F32)
        den = oa[:, SWA_KV:] + jnp.exp(sk - mx)
        on = oa[:, :SWA_KV] * (1.0 / den)
        tiles.append(jnp.where(low_half, on[:blk], on[blk:]))
    o_ref[...] = jnp.concatenate(tiles, axis=1).astype(o_ref.dtype)


def _swa(q, kv, sink, n_batch, seq, ctx_len):
    m = q.shape[0]
    blk = SWA_BLOCK
    nb = seq // blk
    n_lat = n_batch * nb
    cpb = ctx_len // blk

    def batch_of(i):
        return jnp.where(i < n_lat, i // nb, (i - n_lat) // cpb)

    def prev_idx(i, s):
        return (jnp.where(i < n_lat, batch_of(i) * nb + jnp.maximum(i % nb - 1, 0), i), 0)

    def next_idx(i, s):
        return (jnp.where(i < n_lat, batch_of(i) * nb + jnp.minimum(i % nb + 1, nb - 1), i), 0)

    def ctx_idx(i, s):
        return (n_batch * seq // ctx_len + batch_of(i), 0)

    grid_spec = pltpu.PrefetchScalarGridSpec(
        num_scalar_prefetch=1,
        grid=(m // blk,),
        in_specs=[pl.BlockSpec((blk, SWA_Q), lambda i, s: (i, 0)),
                  pl.BlockSpec((blk, 2 * SWA_KV), prev_idx),
                  pl.BlockSpec((blk, 2 * SWA_KV), lambda i, s: (i, 0)),
                  pl.BlockSpec((blk, 2 * SWA_KV), next_idx),
                  pl.BlockSpec((ctx_len, 2 * SWA_KV), ctx_idx)],
        out_specs=pl.BlockSpec((blk, SWA_Q), lambda i, s: (i, 0)),
    )
    return pl.pallas_call(
        functools.partial(_swa_kernel, nb=nb, n_lat_blocks=n_lat),
        out_shape=jax.ShapeDtypeStruct((m, SWA_Q), BF16),
        grid_spec=grid_spec,
        compiler_params=_cp(("parallel",)),
        name="swa_attn",
    )(sink, q, kv, kv, kv, kv)


def _diff_kernel(*refs, n_lat_chunks, coef):
    if n_lat_chunks:
        lam_ref, gcol_ref, q_ref, kc_ref, vtc_ref, kl_ref, vtl_ref, o_ref, m_sc, acc_sc, st_a, st_b = refs
    else:
        lam_ref, gcol_ref, q_ref, kc_ref, vtc_ref, o_ref, m_sc, acc_sc, st_a, st_b = refs
    q = q_ref[...]
    tq = q.shape[0]
    dv = DIFF_V_DIM
    lane = lax.broadcasted_iota(jnp.int32, q.shape, 1)
    zero = jnp.zeros_like(q)
    qq = jnp.concatenate([jnp.where(lane < HEAD_DIM, q, zero), jnp.where(lane >= HEAD_DIM, q, zero)], axis=0)

    def scores(k):
        return lax.dot_general(k, qq, (((1,), (1,)), ((), ())), preferred_element_type=F32)

    def accumulate(st_ref, vt):
        st = st_ref[0:vt.shape[1], :]
        m_old = m_sc[...]
        m_new = jnp.maximum(m_old, jnp.max(st, axis=0, keepdims=True))
        alpha = jnp.exp2(m_old - m_new)
        pt = jnp.exp2((st - m_new).astype(BF16))
        acc_sc[...] = alpha * acc_sc[...] + jnp.dot(vt, pt, preferred_element_type=F32)
        m_sc[...] = m_new

    tk = st_a.shape[0]
    k_lat = lambda c: kl_ref[pl.ds(pl.multiple_of(c * tk, tk), tk), :]
    m_sc[...] = jnp.full(m_sc.shape, NEG_INF, F32)
    acc_sc[...] = jnp.zeros(acc_sc.shape, F32)
    bufs = (st_a, st_b)
    n = n_lat_chunks
    if n:
        unroll = DIFF_UNROLL
        st_a[...] = scores(k_lat(0))
        n_trips = (n - 1) // unroll

        def body(j, carry):
            for u in range(unroll):
                c = j * unroll + u
                bufs[(u + 1) % 2][...] = scores(k_lat(c + 1))
                accumulate(bufs[u % 2], vtl_ref[0, c])
            return carry

        lax.fori_loop(0, n_trips, body, 0)
        for c in range(n_trips * unroll, n):
            if c + 1 < n:
                bufs[(c + 1) % 2][...] = scores(k_lat(c + 1))
            else:
                bufs[(c + 1) % 2][0:kc_ref.shape[0], :] = scores(kc_ref[...])
            accumulate(bufs[c % 2], vtl_ref[0, c])
    else:
        st_a[0:kc_ref.shape[0], :] = scores(kc_ref[...])
    accumulate(bufs[n % 2], vtc_ref[0, 0])
    acc = acc_sc[...]
    ot = acc[:dv] * (1.0 / acc[dv:dv + 1])
    odt = ot[:, :tq] - lam_ref[0] * ot[:, tq:]
    ms = jnp.mean(odt * odt, axis=0, keepdims=True)
    yt = odt * lax.rsqrt(ms + DIFF_SUBLN_EPS) * (gcol_ref[...] * coef)
    o_ref[...] = yt.T.astype(o_ref.dtype)


def _diff(q, k, v, lam, subln_g, lam_init, n_batch, seq, ctx_len):
    m = q.shape[0]
    tk = min(DIFF_TK, seq)
    assert seq % tk == 0 and tk % ctx_len == 0
    dv = DIFF_V_DIM
    dva = dv + DIFF_ONES_ROWS
    lat_rows = n_batch * seq

    def transposed_chunks(rows, size):
        t = rows.reshape(rows.shape[0] // size, size, DIFF_HEADS, dv).transpose(2, 0, 3, 1)
        return jnp.concatenate([t, jnp.ones(t.shape[:2] + (DIFF_ONES_ROWS, size), t.dtype)], axis=2)

    vt_lat = transposed_chunks(v[:lat_rows], tk)
    vt_ctx = transposed_chunks(v[lat_rows:], ctx_len)
    gcol = subln_g.reshape(dv, 1)
    lat_chunks = seq // tk
    ctx0 = n_batch * seq // ctx_len

    def call(tq, n_q, q_block0, batch_of, with_lat):
        in_specs = [pl.BlockSpec((dv, 1), lambda h, i, s: (0, 0)),
                    pl.BlockSpec((tq, LANES), lambda h, i, s: (q_block0 + i, h)),
                    pl.BlockSpec((ctx_len, LANES), lambda h, i, s: (ctx0 + batch_of(i), h)),
                    pl.BlockSpec((1, 1, dva, ctx_len), lambda h, i, s: (h, batch_of(i), 0, 0))]
        args = [lam, gcol, q, k, vt_ctx]
        if with_lat:
            in_specs += [pl.BlockSpec((seq, LANES), lambda h, i, s: (batch_of(i), h)),
                         pl.BlockSpec((1, lat_chunks, dva, tk), lambda h, i, s: (h, batch_of(i), 0, 0))]
            args += [k, vt_lat]
        grid_spec = pltpu.PrefetchScalarGridSpec(
            num_scalar_prefetch=1,
            grid=(DIFF_HEADS, n_q),
            in_specs=in_specs,
            out_specs=pl.BlockSpec((tq, LANES), lambda h, i, s: (i, h)),
            scratch_shapes=[pltpu.VMEM((1, 2 * tq), F32), pltpu.VMEM((dva, 2 * tq), F32),
                            pltpu.VMEM((tk, 2 * tq), F32), pltpu.VMEM((tk, 2 * tq), F32)],
        )
        return pl.pallas_call(
            functools.partial(_diff_kernel, n_lat_chunks=lat_chunks if with_lat else 0, coef=1.0 - lam_init),
            out_shape=jax.ShapeDtypeStruct((n_q * tq, DIFF_V), BF16),
            grid_spec=grid_spec,
            compiler_params=_cp(("parallel", "arbitrary")),
            name="diff_attn" if with_lat else "diff_attn_ctx",
        )(*args)

    tq = DIFF_TQ
    y_lat = call(tq, n_batch * seq // tq, 0, lambda i: i // (seq // tq), True)
    y_ctx = call(ctx_len, n_batch, ctx0, lambda i: i, False)
    return jnp.concatenate([y_lat, y_ctx], axis=0)


def _rwkv_prep_kernel(p_ref, hp_ref, hn_ref, mup_ref, mun_ref, kk_w_ref, ka_ref, rk_ref, w0_ref, a0_ref,
                      wup_ref, aup_ref, gup_ref, ones_ref,
                      r_ref, v_ref, kk_ref, ld_ref, ke_ref, bb_ref, g_ref, bonus_ref, sc,
                      *, tm, lat_rows, seq, ctx_len):
    i = pl.program_id(0)
    w = RWKV_WIDTH
    sc[0:8, :] = hp_ref[...]
    sc[8:8 + tm, :] = p_ref[...]
    sc[8 + tm:16 + tm, :] = hn_ref[...]
    p = p_ref[...]
    prev = sc[7:7 + tm, :]
    nxt = sc[9:9 + tm, :]
    row = i * tm + lax.broadcasted_iota(jnp.int32, (tm, 1), 0)
    pos = jnp.where(row < lat_rows, row % seq, (row - lat_rows) % ctx_len)
    seg_len = jnp.where(row < lat_rows, seq, ctx_len)
    prev = jnp.where(pos == 0, 0.0, prev)
    nxt = jnp.where(pos == seg_len - 1, 0.0, nxt)
    ps = p + mup_ref[...] * (prev - p) + mun_ref[...] * (nxt - p)

    r = ps[:, 0:w]
    k = ps[:, w:2 * w]
    v = ps[:, 2 * w:3 * w]
    wd = ps[:, 3 * w:3 * w + 2 * DECAY_LORA]
    ad = ps[:, 3 * w + 2 * DECAY_LORA:3 * w + 2 * DECAY_LORA + 2 * AAA_LORA]
    gd = ps[:, 3 * w + 2 * DECAY_LORA + 2 * AAA_LORA:]

    ones_bd = ones_ref[...]
    g = jnp.dot(_sigmoid(gd).astype(BF16), gup_ref[...], preferred_element_type=F32)
    kk = k * kk_w_ref[...]
    ss = _group_sum(kk * kk, ones_bd)
    kk = kk / jnp.maximum(jnp.sqrt(ss), 1e-12)
    w_raw = w0_ref[...] + jnp.dot(jnp.tanh(wd).astype(BF16), wup_ref[...], preferred_element_type=F32)
    a_raw = a0_ref[...] + jnp.dot(ad.astype(BF16), aup_ref[...], preferred_element_type=F32)
    ld = -math.exp(-0.5) * _sigmoid(w_raw)
    a = _sigmoid(a_raw)
    ka = ka_ref[...]
    ke_sum = jnp.zeros_like(k)
    for d in range(2):
        a_d = a[:, d * w:(d + 1) * w]
        ke = k * (1.0 + (a_d - 1.0) * ka)
        ld_ref[d] = ld[:, d * w:(d + 1) * w]
        ke_ref[d] = ke
        bb_ref[d] = kk * a_d
        ke_sum = ke_sum + ke
    rk = _group_sum(r * ke_sum * rk_ref[...], ones_bd)
    r_ref[...] = r
    v_ref[...] = v
    kk_ref[...] = kk
    g_ref[...] = g
    bonus_ref[...] = rk * v


def _rwkv_prep(p, lp, n_batch, seq, ctx_len):
    m, cols = p.shape
    tm = 256
    w = RWKV_WIDTH
    lat_rows = n_batch * seq
    row = lambda a: a.reshape(1, -1).astype(F32)

    def blockdiag(u):
        z = jnp.zeros_like(u[0])
        return jnp.concatenate([jnp.concatenate([u[0], z], axis=1), jnp.concatenate([z, u[1]], axis=1)], axis=0)

    head = jnp.arange(w) // HEAD_DIM
    ones_bd = (head[:, None] == head[None, :]).astype(BF16)
    full = lambda shape: pl.BlockSpec(shape, lambda i: (0,) * len(shape))
    nb8 = m // 8
    outs = pl.pallas_call(
        functools.partial(_rwkv_prep_kernel, tm=tm, lat_rows=lat_rows, seq=seq, ctx_len=ctx_len),
        out_shape=[jax.ShapeDtypeStruct((m, w), F32)] * 3
        + [jax.ShapeDtypeStruct((2, m, w), F32)] * 3
        + [jax.ShapeDtypeStruct((m, w), F32)] * 2,
        grid=(m // tm,),
        in_specs=[pl.BlockSpec((tm, cols), lambda i: (i, 0)),
                  pl.BlockSpec((8, cols), lambda i: (jnp.maximum(i * (tm // 8) - 1, 0), 0)),
                  pl.BlockSpec((8, cols), lambda i: (jnp.minimum((i + 1) * (tm // 8), nb8 - 1), 0)),
                  full((1, cols)), full((1, cols)), full((1, w)), full((1, w)), full((1, w)),
                  full((1, 2 * w)), full((1, 2 * w)),
                  full((2 * DECAY_LORA, 2 * w)), full((2 * AAA_LORA, 2 * w)), full((GATE_LORA, w)),
                  full((w, w))],
        out_specs=[pl.BlockSpec((tm, w), lambda i: (i, 0))] * 3
        + [pl.BlockSpec((2, tm, w), lambda i: (0, i, 0))] * 3
        + [pl.BlockSpec((tm, w), lambda i: (i, 0))] * 2,
        scratch_shapes=[pltpu.VMEM((tm + 16, cols), F32)],
        compiler_params=_cp(("parallel",)),
        name="rwkv_prep",
    )(p, p, p, row(lp['rwkv_mu_prev']), row(lp['rwkv_mu_next']), row(lp['rwkv_k_k']), row(lp['rwkv_k_a']),
      row(lp['rwkv_r_k']), row(lp['rwkv_w0']), row(lp['rwkv_a0']),
      blockdiag(lp['rwkv_w_up']).astype(BF16), blockdiag(lp['rwkv_a_up']).astype(BF16),
      lp['rwkv_g_up'].astype(BF16), ones_bd)
    return outs


def _rwkv_chunk_prep(d, off, r_ref, v_ref, kk_ref, ld_ref, ke_ref, bb_ref, incl):
    c = RWKV_CHUNK
    ld = ld_ref[0, pl.ds(off, c), :]
    r = r_ref[pl.ds(off, c), :]
    v = v_ref[pl.ds(off, c), :]
    kk = kk_ref[pl.ds(off, c), :]
    ke = ke_ref[0, pl.ds(off, c), :]
    bb = bb_ref[0, pl.ds(off, c), :]
    cum = jnp.dot(incl, ld, precision=HI, preferred_element_type=F32)
    tot = jnp.sum(ld, axis=0, keepdims=True)
    rt = r * jnp.exp(cum)
    einv = jnp.exp(-cum)
    etail = jnp.exp(tot - cum)
    return dict(at=(-kk * jnp.exp(cum - ld)).astype(BF16), rt=rt, rt_b=rt.astype(BF16),
                bt=(bb * einv).astype(BF16), kt=(ke * einv).astype(BF16),
                bh=(bb * etail).astype(BF16), kh=(ke * etail).astype(BF16),
                v=v, v_b=v.astype(BF16), wtot=jnp.exp(tot))


def _rwkv_chunk_pair(offs, in_refs, y_refs, s_sc):
    c = RWKV_CHUNK
    hd = HEAD_DIM
    ti = lax.broadcasted_iota(jnp.int32, (c, c), 0)
    tj = lax.broadcasted_iota(jnp.int32, (c, c), 1)
    eye = (ti == tj).astype(F32)
    incl = [(tj <= ti).astype(F32), (tj >= ti).astype(F32)]
    strict = [m - eye for m in incl]
    diag_mask = ((ti // RWKV_INV_BASE) == (tj // RWKV_INV_BASE)).astype(F32)
    off_masks = []
    sz = RWKV_INV_BASE
    while sz < c:
        off_masks.append((((ti // (2 * sz)) == (tj // (2 * sz))) & ((ti // sz) != (tj // sz))).astype(F32))
        sz *= 2

    pre = [_rwkv_chunk_prep(d, offs[d], *in_refs[d], incl[d]) for d in range(2)]
    lanes = [(d, h) for d in range(2) for h in range(RWKV_HEADS)]
    sl = lambda h: slice(h * hd, (h + 1) * hd)
    get = lambda name: [pre[d][name][:, sl(h)] for d, h in lanes]
    at, rt, rt_b, bt, kt, bh, kh, v, v_b = (get(n) for n in ('at', 'rt', 'rt_b', 'bt', 'kt', 'bh', 'kh', 'v', 'v_b'))
    nl = range(len(lanes))
    nt = (((1,), (1,)), ((), ()))
    bdot = lambda x, y: jnp.dot(x.astype(BF16), y.astype(BF16), preferred_element_type=F32)

    gm = [lax.dot_general(jnp.concatenate([at[i], rt_b[i]], axis=0), jnp.concatenate([bt[i], kt[i]], axis=0), nt,
                          preferred_element_type=F32) for i in nl]
    zz0 = [bdot(v[i].T, kh[i]) for i in nl]
    aab = [gm[i][:c, :c] * strict[lanes[i][0]] for i in nl]
    aak = [gm[i][:c, c:] * strict[lanes[i][0]] for i in nl]
    arb = [gm[i][c:, :c] * incl[lanes[i][0]] for i in nl]
    ark = [gm[i][c:, c:] * incl[lanes[i][0]] for i in nl]
    av = [bdot(jnp.concatenate([aak[i], ark[i]], axis=0), v_b[i]) for i in nl]
    pw = [aab[i] * diag_mask for i in nl]
    tm_ = [eye + pw[i] for i in nl]
    for _ in range(int(math.log2(RWKV_INV_BASE)) - 1):
        pw = [bdot(pw[i], pw[i]) for i in nl]
        tm_ = [tm_[i] + bdot(tm_[i], pw[i]) for i in nl]
    for off_mask in off_masks:
        tn = [bdot(tm_[i], aab[i] * off_mask) for i in nl]
        tm_ = [tm_[i] + bdot(tn[i], tm_[i]) for i in nl]
    au = [bdot(tm_[i], jnp.concatenate([at[i], av[i][:c].astype(BF16)], axis=1)) for i in nl]
    ry = [bdot(arb[i], au[i]) for i in nl]
    mz = [bdot(au[i].T, bh[i]) for i in nl]
    s_old = [s_sc[d, h].astype(BF16) for d, h in lanes]
    ys = [lax.dot_general((rt[i] + ry[i][:, :hd]).astype(BF16), s_old[i], nt, preferred_element_type=F32)
          + ry[i][:, hd:] + av[i][c:] for i in nl]
    s_new = [jnp.dot(s_old[i], (eye * pre[lanes[i][0]]['wtot'][:, sl(lanes[i][1])] + mz[i][:hd]).astype(BF16),
                     preferred_element_type=F32) + mz[i][hd:] + zz0[i] for i in nl]
    for i, (d, h) in enumerate(lanes):
        s_sc[d, h] = s_new[i]
    for d in range(2):
        y_refs[d][0, pl.ds(offs[d], c), :] = jnp.concatenate(
            [ys[i] for i in nl if lanes[i][0] == d], axis=1)


def _rwkv_scan_kernel(rf, vf, kf, ldf, kef, bbf, rb, vb, kb, ldb, keb, bbb, yf, yb, s_sc, *, n_chunks):
    @pl.when(pl.program_id(1) == 0)
    def _():
        s_sc[...] = jnp.zeros(s_sc.shape, F32)

    def body(cc, carry):
        off_f = pl.multiple_of(cc * RWKV_CHUNK, RWKV_CHUNK)
        off_b = pl.multiple_of((n_chunks - 1 - cc) * RWKV_CHUNK, RWKV_CHUNK)
        _rwkv_chunk_pair((off_f, off_b), ((rf, vf, kf, ldf, kef, bbf), (rb, vb, kb, ldb, keb, bbb)),
                         (yf, yb), s_sc)
        return carry

    lax.fori_loop(0, n_chunks, body, 0)


def _rwkv_scan(r, v, kk, ld, ke, bb, n_batch, seq, ctx_len):
    m, w = r.shape
    ts = RWKV_STEP_ROWS
    assert ctx_len == ts
    lpb = seq // ts
    ctx0 = n_batch * seq // ts
    nj = 1 + lpb

    def fwd(b, j):
        return jnp.where(j == 0, ctx0 + b, b * lpb + j - 1)

    def bwd(b, j):
        return jnp.where(j == 0, ctx0 + b, b * lpb + lpb - j)

    shared = lambda f: pl.BlockSpec((ts, w), lambda b, j: (f(b, j), 0))
    per_dir = lambda f, d: pl.BlockSpec((1, ts, w), lambda b, j: (d, f(b, j), 0))
    y = pl.pallas_call(
        functools.partial(_rwkv_scan_kernel, n_chunks=ts // RWKV_CHUNK),
        out_shape=[jax.ShapeDtypeStruct((1, m, w), F32)] * 2,
        grid=(n_batch, nj),
        in_specs=[shared(fwd), shared(fwd), shared(fwd), per_dir(fwd, 0), per_dir(fwd, 0), per_dir(fwd, 0),
                  shared(bwd), shared(bwd), shared(bwd), per_dir(bwd, 1), per_dir(bwd, 1), per_dir(bwd, 1)],
        out_specs=[pl.BlockSpec((1, ts, w), lambda b, j: (0, fwd(b, j), 0)),
                   pl.BlockSpec((1, ts, w), lambda b, j: (0, bwd(b, j), 0))],
        scratch_shapes=[pltpu.VMEM((2, RWKV_HEADS, HEAD_DIM, HEAD_DIM), F32)],
        compiler_params=_cp(("parallel", "arbitrary")),
        name="rwkv_scan",
    )(r, v, kk, ld, ke, bb, r, v, kk, ld, ke, bb)
    return y


def _rwkv_out_kernel(yf_ref, yb_ref, bonus_ref, g_ref, lg_ref, lb_ref, ones_ref, o_ref):
    y = yf_ref[0] + yb_ref[0]
    ones_bd = ones_ref[...]
    inv = 1.0 / HEAD_DIM
    mu = _group_sum(y, ones_bd) * inv
    yc = y - mu
    var = _group_sum(yc * yc, ones_bd) * inv
    yn = yc * lax.rsqrt(var + RWKV_GN_EPS) * lg_ref[...] + lb_ref[...]
    o_ref[...] = ((yn + bonus_ref[...]) * g_ref[...]).astype(o_ref.dtype)


def _rwkv_out(yf, yb, bonus, g, lnx_g, lnx_b):
    m, w = bonus.shape
    tm = ROW_TILE
    head = jnp.arange(w) // HEAD_DIM
    ones_bd = (head[:, None] == head[None, :]).astype(BF16)
    blk = pl.BlockSpec((tm, w), lambda i: (i, 0))
    blk3 = pl.BlockSpec((1, tm, w), lambda i: (0, i, 0))
    one = pl.BlockSpec((1, w), lambda i: (0, 0))
    return pl.pallas_call(
        _rwkv_out_kernel,
        out_shape=jax.ShapeDtypeStruct((m, w), BF16),
        grid=(m // tm,),
        in_specs=[blk3, blk3, blk, blk, one, one, pl.BlockSpec((w, w), lambda i: (0, 0))],
        out_specs=blk,
        compiler_params=_cp(("parallel",)),
        name="rwkv_out",
    )(yf, yb, bonus, g, lnx_g.reshape(1, w), lnx_b.reshape(1, w), ones_bd)


def _merge_kernel(ya_ref, yb_ref, yc_ref, gt_ref, h_ref, pa_ref, pb_ref, pc_ref, wo_ref, gpost_ref, gpre_ref,
                  mod_ref, h_out, f_out, *, d):
    gates = gt_ref[...].astype(F32)
    merged = (_sigmoid(gates[:, 0:d]) * jnp.dot(ya_ref[...], pa_ref[...], preferred_element_type=F32)
              + _sigmoid(gates[:, d:2 * d]) * jnp.dot(yb_ref[...], pb_ref[...], preferred_element_type=F32)
              + _sigmoid(gates[:, 2 * d:3 * d]) * jnp.dot(yc_ref[...], pc_ref[...], preferred_element_type=F32))
    out = jnp.dot(merged.astype(BF16), wo_ref[...], preferred_element_type=F32)
    mod = lambda idx: mod_ref[0, :, idx * d:(idx + 1) * d]
    hn = h_ref[...] + mod(2) * (_rms(out, NORM_EPS) * gpost_ref[...])
    h_out[...] = hn
    f = (_rms(hn, NORM_EPS) * gpre_ref[...]) * (1.0 + mod(4)) + mod(3)
    f_out[...] = f.astype(f_out.dtype)


def _merge(ya, yb, yc, gates, h, pa, pb, pc, wo, g_post, g_pre, mods, f_dtype, lat_bpb, n_batch):
    m, d = h.shape
    tm = 256
    bpb = lat_bpb * (ROW_TILE // tm)
    seg = functools.partial(_seg_of_block, lat_blocks_per_batch=bpb, n_batch=n_batch)
    rows = lambda width: pl.BlockSpec((tm, width), lambda i: (i, 0))
    full = lambda a: pl.BlockSpec(a.shape, lambda i: (0, 0))
    return pl.pallas_call(
        functools.partial(_merge_kernel, d=d),
        out_shape=[jax.ShapeDtypeStruct((m, d), F32), jax.ShapeDtypeStruct((m, d), f_dtype)],
        grid=(m // tm,),
        in_specs=[rows(ya.shape[1]), rows(yb.shape[1]), rows(yc.shape[1]), rows(gates.shape[1]), rows(d),
                  full(pa), full(pb), full(pc), full(wo),
                  pl.BlockSpec((1, d), lambda i: (0, 0)), pl.BlockSpec((1, d), lambda i: (0, 0)),
                  pl.BlockSpec((1, 1, mods.shape[2]), lambda i: (seg(i), 0, 0))],
        out_specs=[rows(d), rows(d)],
        compiler_params=_cp(("parallel",)),
        name="merge",
    )(ya, yb, yc, gates, h, pa, pb, pc, wo, g_post, g_pre, mods)


def _swiglu_hidden(x, wg, wu):
    hg = jnp.dot(x, wg, preferred_element_type=F32)
    hu = jnp.dot(x, wu, preferred_element_type=F32)
    return (hg * _sigmoid(hg) * hu).astype(BF16)


def _ffn_kernel(f_ref, wg_ref, wu_ref, wd_ref, h_ref, gpost_ref, mod_ref, o_ref, acc, *, d):
    j = pl.program_id(1)

    @pl.when(j == 0)
    def _():
        acc[...] = jnp.zeros(acc.shape, F32)

    hid = _swiglu_hidden(f_ref[...], wg_ref[...], wu_ref[...])
    acc[...] += jnp.dot(hid, wd_ref[...], preferred_element_type=F32)

    @pl.when(j == pl.num_programs(1) - 1)
    def _():
        gate = mod_ref[0, :, 5 * d:6 * d]
        o_ref[...] = h_ref[...] + gate * (_rms(acc[...], NORM_EPS) * gpost_ref[...])


def _ffn(f, wg, wu, wd, h, g_post, mods, lat_bpb, n_batch):
    m, d = h.shape
    ff = wg.shape[1]
    tm, tf = ROW_TILE, FFN_TF
    seg = functools.partial(_seg_of_block, lat_blocks_per_batch=lat_bpb, n_batch=n_batch)
    return pl.pallas_call(
        functools.partial(_ffn_kernel, d=d),
        out_shape=jax.ShapeDtypeStruct((m, d), F32),
        grid=(m // tm, ff // tf),
        in_specs=[pl.BlockSpec((tm, d), lambda i, j: (i, 0)),
                  pl.BlockSpec((d, tf), lambda i, j: (0, j)),
                  pl.BlockSpec((d, tf), lambda i, j: (0, j)),
                  pl.BlockSpec((tf, d), lambda i, j: (j, 0)),
                  pl.BlockSpec((tm, d), lambda i, j: (i, 0)),
                  pl.BlockSpec((1, d), lambda i, j: (0, 0)),
                  pl.BlockSpec((1, 1, mods.shape[2]), lambda i, j: (seg(i), 0, 0))],
        out_specs=pl.BlockSpec((tm, d), lambda i, j: (i, 0)),
        scratch_shapes=[pltpu.VMEM((tm, d), F32)],
        compiler_params=_cp(("parallel", "arbitrary")),
        name="ffn_dense",
    )(f, wg, wu, wd, h, g_post, mods)


def _router_kernel(f_ref, w_ref, idx_ref, wt_ref):
    logits = jnp.dot(f_ref[...], w_ref[...], precision=HI, preferred_element_type=F32)
    lane = lax.broadcasted_iota(jnp.int32, logits.shape, 1)
    logits = jnp.where(lane < N_EXPERTS, logits, -jnp.inf)
    m1 = jnp.max(logits, axis=-1, keepdims=True)
    i1 = jnp.min(jnp.where(logits == m1, lane, LANES), axis=-1, keepdims=True)
    rest = jnp.where(lane == i1, -jnp.inf, logits)
    m2 = jnp.max(rest, axis=-1, keepdims=True)
    i2 = jnp.min(jnp.where(rest == m2, lane, LANES), axis=-1, keepdims=True)
    e = jnp.exp(m2 - m1)
    w1 = 1.0 / (1.0 + e)
    w2 = e / (1.0 + e)
    idx_ref[...] = jnp.where(lane == 0, i1, jnp.where(lane == 1, i2, 0))
    wt_ref[...] = jnp.where(lane == 0, w1, jnp.where(lane == 1, w2, 0.0))


def _router(f, router_w, n_rows):
    d = f.shape[1]
    tm = ROW_TILE
    w_pad = jnp.zeros((d, LANES), F32).at[:, :N_EXPERTS].set(router_w)
    return pl.pallas_call(
        _router_kernel,
        out_shape=[jax.ShapeDtypeStruct((n_rows, LANES), jnp.int32), jax.ShapeDtypeStruct((n_rows, LANES), F32)],
        grid=(n_rows // tm,),
        in_specs=[pl.BlockSpec((tm, d), lambda i: (i, 0)), pl.BlockSpec((d, LANES), lambda i: (0, 0))],
        out_specs=[pl.BlockSpec((tm, LANES), lambda i: (i, 0))] * 2,
        compiler_params=_cp(("parallel",)),
        name="router",
    )(f, w_pad)


def _moe_gather_copy(f_hbm, xbuf, sem, src_row, dst_row):
    return pltpu.make_async_copy(f_hbm.at[pl.ds(src_row, 1), :], xbuf.at[pl.ds(dst_row, 1), :], sem)


def _moe_ffn_kernel(blk_e_ref, nused_ref, tok_ref, f_hbm, wg_ref, wu_ref, wd_ref, y_ref, xbuf, xb, acc, sem,
                    *, tm):
    i = pl.program_id(0)
    j = pl.program_id(1)
    active = i < nused_ref[0]

    @pl.when(active & (j == 0))
    def _():
        def issue(r, carry):
            _moe_gather_copy(f_hbm, xbuf, sem, tok_ref[i * tm + r], r).start()
            return carry

        lax.fori_loop(0, tm, issue, 0)
        pltpu.make_async_copy(f_hbm.at[pl.ds(0, tm), :], xbuf, sem).wait()
        xb[...] = xbuf[...].astype(BF16)
        acc[...] = jnp.zeros(acc.shape, F32)

    @pl.when(active)
    def _():
        hid = _swiglu_hidden(xb[...], wg_ref[0], wu_ref[0])
        acc[...] += jnp.dot(hid, wd_ref[0], preferred_element_type=F32)

    @pl.when(j == pl.num_programs(1) - 1)
    def _():
        y_ref[...] = acc[...]


def _moe_ffn(f, blk_e, nused, tok, wg, wu, wd, n_blocks):
    d = f.shape[1]
    ff = wg.shape[2]
    tm, tf = MOE_TILE, MOE_TF

    def e_of(i, be, nu):
        return be[jnp.minimum(i, nu[0] - 1)]

    grid_spec = pltpu.PrefetchScalarGridSpec(
        num_scalar_prefetch=3,
        grid=(n_blocks, ff // tf),
        in_specs=[pl.BlockSpec(memory_space=pl.ANY),
                  pl.BlockSpec((1, d, tf), lambda i, j, be, nu, tk: (e_of(i, be, nu), 0, j)),
                  pl.BlockSpec((1, d, tf), lambda i, j, be, nu, tk: (e_of(i, be, nu), 0, j)),
                  pl.BlockSpec((1, tf, d), lambda i, j, be, nu, tk: (e_of(i, be, nu), j, 0))],
        out_specs=pl.BlockSpec((tm, d), lambda i, j, be, nu, tk: (i, 0)),
        scratch_shapes=[pltpu.VMEM((tm, d), F32), pltpu.VMEM((tm, d), BF16), pltpu.VMEM((tm, d), F32),
                        pltpu.SemaphoreType.DMA(())],
    )
    return pl.pallas_call(
        functools.partial(_moe_ffn_kernel, tm=tm),
        out_shape=jax.ShapeDtypeStruct((n_blocks * tm, d), F32),
        grid_spec=grid_spec,
        compiler_params=_cp(("arbitrary", "arbitrary")),
        name="moe_ffn",
    )(blk_e, nused, tok, f, wg, wu, wd)


def _moe_combine_kernel(p0_ref, p1_ref, y_hbm, wt_ref, h_ref, gpost_ref, mod_ref, o_ref, b0, b1, sem, *, tm, d):
    i = pl.program_id(0)

    def issue(r, carry):
        pltpu.make_async_copy(y_hbm.at[pl.ds(p0_ref[i * tm + r], 1), :], b0.at[pl.ds(r, 1), :], sem.at[0]).start()
        pltpu.make_async_copy(y_hbm.at[pl.ds(p1_ref[i * tm + r], 1), :], b1.at[pl.ds(r, 1), :], sem.at[1]).start()
        return carry

    lax.fori_loop(0, tm, issue, 0)
    pltpu.make_async_copy(y_hbm.at[pl.ds(0, tm), :], b0, sem.at[0]).wait()
    pltpu.make_async_copy(y_hbm.at[pl.ds(0, tm), :], b1, sem.at[1]).wait()
    wt = wt_ref[...]
    y = b0[...] * wt[:, 0:1] + b1[...] * wt[:, 1:2]
    gate = mod_ref[0, :, 5 * d:6 * d]
    o_ref[...] = h_ref[...] + gate * (_rms(y, NORM_EPS) * gpost_ref[...])


def _moe_combine(pos0, pos1, y, wt, h, g_post, mods, n_rows, rows_per_batch, n_batch):
    d = h.shape[1]
    tm = COMBINE_TILE
    seg = functools.partial(_seg_of_block, lat_blocks_per_batch=rows_per_batch // tm, n_batch=n_batch)
    grid_spec = pltpu.PrefetchScalarGridSpec(
        num_scalar_prefetch=2,
        grid=(n_rows // tm,),
        in_specs=[pl.BlockSpec(memory_space=pl.ANY),
                  pl.BlockSpec((tm, LANES), lambda i, a, b: (i, 0)),
                  pl.BlockSpec((tm, d), lambda i, a, b: (i, 0)),
                  pl.BlockSpec((1, d), lambda i, a, b: (0, 0)),
                  pl.BlockSpec((1, 1, mods.shape[2]), lambda i, a, b: (seg(i), 0, 0))],
        out_specs=pl.BlockSpec((tm, d), lambda i, a, b: (i, 0)),
        scratch_shapes=[pltpu.VMEM((tm, d), F32), pltpu.VMEM((tm, d), F32), pltpu.SemaphoreType.DMA((2,))],
    )
    return pl.pallas_call(
        functools.partial(_moe_combine_kernel, tm=tm, d=d),
        out_shape=jax.ShapeDtypeStruct((n_rows, d), F32),
        grid_spec=grid_spec,
        compiler_params=_cp(("arbitrary",)),
        name="moe_combine",
    )(pos0, pos1, y, wt, h, g_post, mods)


def _moe_slots(top_i, tile):
    n = top_i.shape[0]
    a = n * 2
    e_flat = top_i.reshape(a)
    onehot = (e_flat[:, None] == jnp.arange(N_EXPERTS, dtype=jnp.int32)[None, :]).astype(jnp.int32)
    csum = jnp.cumsum(onehot, axis=0)
    rank = jnp.sum(csum * onehot, axis=1) - 1
    counts = csum[-1]
    padded = (counts + tile - 1) // tile * tile
    pends = jnp.cumsum(padded)
    pstarts = pends - padded
    dest = (jnp.sum(onehot * pstarts[None, :], axis=1) + rank).astype(jnp.int32)
    n_blocks = a // tile + N_EXPERTS
    tok = jnp.zeros((n_blocks * tile,), jnp.int32).at[dest].set(jnp.arange(a, dtype=jnp.int32) // 2)
    block_start = jnp.arange(n_blocks, dtype=jnp.int32) * tile
    blk_e = jnp.minimum(jnp.searchsorted(pends, block_start, side='right'), N_EXPERTS - 1).astype(jnp.int32)
    nused = (pends[-1:] // tile).astype(jnp.int32)
    return tok, blk_e, nused, dest.reshape(n, 2), n_blocks


def _swa_head_order(w, axis):
    group = SWA_HEADS // SWA_KV_HEADS
    order = [g * group + t for t in range(group) for g in range(SWA_KV_HEADS)]
    shape = w.shape
    w = w.reshape(shape[:axis] + (SWA_HEADS, HEAD_DIM) + shape[axis + 1:])
    return jnp.take(w, jnp.array(order), axis=axis).reshape(shape)


def _rope_tables(n_batch, seq, ctx_len):
    t = jnp.arange(seq, dtype=jnp.int32)
    row = (t // GRID_W).astype(F32)
    col = (t % GRID_W).astype(F32)
    axis_dim = HEAD_DIM // 2
    inv_freq = ROPE_THETA ** (-jnp.arange(0, axis_dim, 2, dtype=F32) / axis_dim)
    dd = jnp.arange(LANES) % HEAD_DIM
    pos = jnp.where((dd // axis_dim)[None, :] == 0, row[:, None], col[:, None])
    ang = pos * inv_freq[dd % (axis_dim // 2)][None, :]
    cos = jnp.cos(ang)
    sin = jnp.where(((dd % axis_dim) < axis_dim // 2)[None, :], -jnp.sin(ang), jnp.sin(ang))
    n_ctx = n_batch * ctx_len
    cos = jnp.concatenate([jnp.tile(cos, (n_batch, 1)), jnp.ones((n_ctx, LANES), F32)], axis=0)
    sin = jnp.concatenate([jnp.tile(sin, (n_batch, 1)), jnp.zeros((n_ctx, LANES), F32)], axis=0)
    return cos, sin


def kernel(x, c, ctx, c_ctx, ada_w, ada_b, pre_mix_g, post_mix_g, pre_ffn_g, post_ffn_g, w_in, swa_sink,
           rwkv_mu_prev, rwkv_mu_next, rwkv_w0, rwkv_w_up, rwkv_a0, rwkv_a_up, rwkv_g_up, rwkv_k_k, rwkv_k_a,
           rwkv_r_k, rwkv_lnx_g, rwkv_lnx_b, diff_lambda, diff_subln_g, proj_swa, proj_rwkv, proj_diff, w_out,
           ffn_w_gate, ffn_w_up, ffn_w_down, router_w, moe_w_gate, moe_w_up, moe_w_down):
    n_batch, seq, d = x.shape
    ctx_len = ctx.shape[1]
    depth = w_in.shape[0]
    lat_rows = n_batch * seq
    lat_bpb = seq // ROW_TILE
    assert seq % ROW_TILE == 0 and (n_batch * ctx_len) % ROW_TILE == 0

    h = jnp.concatenate([x.reshape(lat_rows, d), ctx.reshape(n_batch * ctx_len, d)], axis=0)
    m = h.shape[0]
    cond = jnp.zeros((8, d), F32).at[:n_batch].set(c).at[n_batch].set(c_ctx)
    rope = _rope_tables(n_batch, seq, ctx_len)
    row = lambda a: a.reshape(1, -1)

    o_swa = 0
    o_rwkv = o_swa + SWA_Q + 2 * SWA_KV
    o_diff = o_rwkv + RWKV_COLS
    o_gate = o_diff + 2 * DIFF_QK + DIFF_V
    o_end = o_gate + 3 * d

    for layer in range(depth):
        mods = _ada(cond, ada_w[layer], ada_b[layer].reshape(1, -1))[:, None, :]
        wl = w_in[layer].astype(BF16)
        a = _prenorm(h, row(pre_mix_g[layer]), mods, 0, 1, lat_bpb, n_batch)

        w_swa = jnp.concatenate([_swa_head_order(wl[:, o_swa:o_swa + SWA_Q], axis=1),
                                 wl[:, o_swa + SWA_Q:o_rwkv]], axis=1)
        q_swa, kv_swa = _proj(a, w_swa, (SWA_Q, 2 * SWA_KV), (BF16, BF16), rope=rope,
                              rope_cols=SWA_Q + SWA_KV, scale_cols=SWA_Q, q_scale=HEAD_DIM ** -0.5)
        (p_rwkv,) = _proj(a, wl[:, o_rwkv:o_diff], (RWKV_COLS,), (F32,))
        q_diff, k_diff, v_diff = _proj(a, wl[:, o_diff:o_gate], (DIFF_QK, DIFF_QK, DIFF_V), (BF16,) * 3,
                                       rope=rope, rope_cols=2 * DIFF_QK, scale_cols=DIFF_QK,
                                       q_scale=HEAD_DIM ** -0.5 * math.log2(math.e))
        (gates,) = _proj(a, wl[:, o_gate:o_end], (d,), (BF16,), tn=d)

        ya = _swa(q_swa, kv_swa, swa_sink[layer].astype(F32), n_batch, seq, ctx_len)

        lp = {'rwkv_mu_prev': rwkv_mu_prev[layer], 'rwkv_mu_next': rwkv_mu_next[layer],
              'rwkv_w0': rwkv_w0[layer], 'rwkv_w_up': rwkv_w_up[layer], 'rwkv_a0': rwkv_a0[layer],
              'rwkv_a_up': rwkv_a_up[layer], 'rwkv_g_up': rwkv_g_up[layer], 'rwkv_k_k': rwkv_k_k[layer],
              'rwkv_k_a': rwkv_k_a[layer], 'rwkv_r_k': rwkv_r_k[layer]}
        r_, v_, kk_, ld_, ke_, bb_, g_, bonus_ = _rwkv_prep(p_rwkv, lp, n_batch, seq, ctx_len)
        y_f, y_b = _rwkv_scan(r_, v_, kk_, ld_, ke_, bb_, n_batch, seq, ctx_len)
        yb = _rwkv_out(y_f, y_b, bonus_, g_, rwkv_lnx_g[layer], rwkv_lnx_b[layer])

        lam_vec = diff_lambda[layer].astype(F32)
        lam_init = 0.8 - 0.6 * math.exp(-0.3 * layer)
        lam = (jnp.exp(jnp.sum(lam_vec[0] * lam_vec[1])) - jnp.exp(jnp.sum(lam_vec[2] * lam_vec[3]))
               + lam_init).reshape(1)
        yc = _diff(q_diff, k_diff, v_diff, lam, row(diff_subln_g[layer]), lam_init, n_batch, seq, ctx_len)

        moe_layer = layer % 2 == 1
        jj = layer // 2
        h, f = _merge(ya, yb, yc, gates, h, _swa_head_order(proj_swa[layer], axis=0).astype(BF16),
                      proj_rwkv[layer].astype(BF16),
                      proj_diff[layer].astype(BF16), w_out[layer].astype(BF16), row(post_mix_g[layer]),
                      row(pre_ffn_g[layer]), mods, F32 if moe_layer else BF16, lat_bpb, n_batch)
        need_ctx = layer < depth - 1
        if not moe_layer:
            h = _ffn(f, ffn_w_gate[jj].astype(BF16), ffn_w_up[jj].astype(BF16), ffn_w_down[jj].astype(BF16),
                     h, row(post_ffn_g[layer]), mods, lat_bpb, n_batch)
        else:
            n_tok = m if need_ctx else lat_rows
            top_i, top_w = _router(f, router_w[jj], n_tok)
            tok, blk_e, nused, dest, n_blocks = _moe_slots(top_i[:, :2], MOE_TILE)
            y = _moe_ffn(f, blk_e, nused, tok, moe_w_gate[jj].astype(BF16), moe_w_up[jj].astype(BF16),
                         moe_w_down[jj].astype(BF16), n_blocks)
            h = _moe_combine(dest[:, 0], dest[:, 1], y, top_w, h, row(post_ffn_g[layer]), mods, n_tok, seq,
                             n_batch)
    return h[:lat_rows].reshape(n_batch, seq, d)
```

```python
import functools
import math

import jax
import jax.numpy as jnp
from jax import lax
from jax.experimental import pallas as pl
from jax.experimental.pallas import tpu as pltpu

F32 = jnp.float32
BF16 = jnp.bfloat16
HI = lax.Precision.HIGHEST

HEAD_DIM = 64
GRID_W = 64
ROPE_THETA = 10000.0
NORM_EPS = 1e-6
NEG_INF = -1e30
SWA_HEADS = 8
SWA_KV_HEADS = 2
SWA_BLOCK = 128
RWKV_HEADS = 8
RWKV_WIDTH = RWKV_HEADS * HEAD_DIM
DECAY_LORA = 64
AAA_LORA = 64
GATE_LORA = 128
RWKV_GN_EPS = 64e-5
DIFF_HEADS = 4
DIFF_V_DIM = 2 * HEAD_DIM
DIFF_SUBLN_EPS = 1e-5
N_EXPERTS = 8
SWA_Q = SWA_HEADS * HEAD_DIM
SWA_KV = SWA_KV_HEADS * HEAD_DIM
assert SWA_KV_HEADS == 2 and SWA_KV == 128
RWKV_COLS = 3 * RWKV_WIDTH + 2 * DECAY_LORA + 2 * AAA_LORA + GATE_LORA
DIFF_QK = DIFF_HEADS * 2 * HEAD_DIM
DIFF_V = DIFF_HEADS * DIFF_V_DIM

LANES = 128
VMEM_LIMIT = 48 * 1024 * 1024
ROW_TILE = 512
RWKV_CHUNK = 64
RWKV_INV_BASE = 8
RWKV_STEP_ROWS = 256
DIFF_TQ = 256
DIFF_TK = 1024
DIFF_UNROLL = 2
DIFF_ONES_ROWS = 16
MOE_TILE = 512
MOE_TF = 1792
FFN_TF = 1408
COMBINE_TILE = 256


def _cp(sem, **kw):
    return pltpu.CompilerParams(dimension_semantics=sem, vmem_limit_bytes=VMEM_LIMIT, **kw)


def _seg_of_block(i, lat_blocks_per_batch, n_batch):
    return jnp.minimum(i // lat_blocks_per_batch, n_batch)


def _rms(x, eps):
    return x * lax.rsqrt(jnp.mean(x * x, axis=-1, keepdims=True) + eps)


def _sigmoid(x):
    return 1.0 / (1.0 + jnp.exp(-x))


def _group_sum(x, ones_bd):
    hi = x.astype(BF16)
    r1 = x - hi.astype(F32)
    mid = r1.astype(BF16)
    lo = (r1 - mid.astype(F32)).astype(BF16)
    rows = x.shape[0]
    parts = jnp.dot(jnp.concatenate([hi, mid, lo], axis=0), ones_bd, preferred_element_type=F32)
    return parts[:rows] + parts[rows:2 * rows] + parts[2 * rows:]


def _ada_kernel(x_ref, w_ref, b_ref, o_ref):
    x = x_ref[...]
    s = x * _sigmoid(x)
    o_ref[...] = jnp.dot(s, w_ref[...], precision=HI, preferred_element_type=F32) + b_ref[...]


def _ada(cond, w, b):
    rows, d = cond.shape
    n = w.shape[1]
    return pl.pallas_call(
        _ada_kernel,
        out_shape=jax.ShapeDtypeStruct((rows, n), F32),
        grid=(n // d,),
        in_specs=[pl.BlockSpec((rows, d), lambda j: (0, 0)),
                  pl.BlockSpec((d, d), lambda j: (0, j)),
                  pl.BlockSpec((1, d), lambda j: (0, j))],
        out_specs=pl.BlockSpec((rows, d), lambda j: (0, j)),
        compiler_params=_cp(("parallel",)),
        name="ada_mod",
    )(cond, w, b)


def _prenorm_kernel(h_ref, g_ref, mod_ref, o_ref, *, d, shift_idx, scale_idx):
    y = _rms(h_ref[...], NORM_EPS) * g_ref[...]
    shift = mod_ref[0, :, shift_idx * d:(shift_idx + 1) * d]
    scale = mod_ref[0, :, scale_idx * d:(scale_idx + 1) * d]
    o_ref[...] = (y * (1.0 + scale) + shift).astype(o_ref.dtype)


def _prenorm(h, g, mods, shift_idx, scale_idx, lat_bpb, n_batch):
    m, d = h.shape
    tm = ROW_TILE
    seg = functools.partial(_seg_of_block, lat_blocks_per_batch=lat_bpb, n_batch=n_batch)
    return pl.pallas_call(
        functools.partial(_prenorm_kernel, d=d, shift_idx=shift_idx, scale_idx=scale_idx),
        out_shape=jax.ShapeDtypeStruct((m, d), BF16),
        grid=(m // tm,),
        in_specs=[pl.BlockSpec((tm, d), lambda i: (i, 0)),
                  pl.BlockSpec((1, d), lambda i: (0, 0)),
                  pl.BlockSpec((1, 1, mods.shape[2]), lambda i: (seg(i), 0, 0))],
        out_specs=pl.BlockSpec((tm, d), lambda i: (i, 0)),
        compiler_params=_cp(("parallel",)),
        name="prenorm",
    )(h, g, mods)


def _proj_kernel(*refs, splits, rope_cols, scale_cols, q_scale):
    if rope_cols:
        a_ref, w_ref, cos_ref, sin_ref = refs[:4]
        outs = refs[4:]
    else:
        a_ref, w_ref = refs[:2]
        outs = refs[2:]
    y = jnp.dot(a_ref[...], w_ref[...], preferred_element_type=F32)
    tm, tn = y.shape
    if rope_cols:
        cos = cos_ref[...]
        sin = sin_ref[...]
        lane = lax.broadcasted_iota(jnp.int32, (tm, LANES), 1)
        first_half = (lane % 32) < 16
        pieces = []
        for c in range(tn // LANES):
            yc = y[:, c * LANES:(c + 1) * LANES]
            if c * LANES < rope_cols:
                partner = jnp.where(first_half, pltpu.roll(yc, LANES - 16, 1), pltpu.roll(yc, 16, 1))
                yc = yc * cos + partner * sin
            if c * LANES < scale_cols:
                yc = yc * q_scale
            pieces.append(yc)
        y = jnp.concatenate(pieces, axis=1)
    start = 0
    for o_ref, width in zip(outs, splits):
        o_ref[...] = y[:, start:start + width].astype(o_ref.dtype)
        start += width


def _proj(a, w, splits, dtypes, rope=None, rope_cols=0, scale_cols=0, q_scale=1.0, tn=None):
    m, k = a.shape
    n = w.shape[1]
    tm = ROW_TILE
    tn = n if tn is None else tn
    assert sum(splits) == tn and (len(splits) == 1 or tn == n)
    in_specs = [pl.BlockSpec((tm, k), lambda i, j: (i, 0)),
                pl.BlockSpec((k, tn), lambda i, j: (0, j))]
    args = [a, w]
    if rope_cols:
        in_specs += [pl.BlockSpec((tm, LANES), lambda i, j: (i, 0))] * 2
        args += list(rope)
    out_specs = []
    out_shape = []
    if len(splits) == 1:
        out_specs.append(pl.BlockSpec((tm, tn), lambda i, j: (i, j)))
        out_shape.append(jax.ShapeDtypeStruct((m, n), dtypes[0]))
    else:
        for width, dt in zip(splits, dtypes):
            out_specs.append(pl.BlockSpec((tm, width), lambda i, j: (i, 0)))
            out_shape.append(jax.ShapeDtypeStruct((m, width), dt))
    res = pl.pallas_call(
        functools.partial(_proj_kernel, splits=tuple(splits), rope_cols=rope_cols, scale_cols=scale_cols,
                          q_scale=q_scale),
        out_shape=out_shape,
        grid=(m // tm, n // tn),
        in_specs=in_specs,
        out_specs=out_specs,
        compiler_params=_cp(("parallel", "parallel")),
        name="proj",
    )(*args)
    return res


def _swa_kernel(sink_ref, q_ref, kp_ref, kc_ref, kn_ref, kx_ref, o_ref, *, nb, n_lat_blocks):
    i = pl.program_id(0)
    is_lat = i < n_lat_blocks
    n = i % nb
    blk = SWA_BLOCK
    q = q_ref[...]
    kv = jnp.concatenate([kp_ref[...], kc_ref[...], kn_ref[...], kx_ref[...]], axis=0)
    nkeys = kv.shape[0]
    k_t = kv[:, :SWA_KV]
    v_aug = jnp.concatenate([kv[:, SWA_KV:], jnp.ones((nkeys, SWA_KV), kv.dtype)], axis=1)
    r = lax.broadcasted_iota(jnp.int32, (blk, nkeys), 0)
    j = lax.broadcasted_iota(jnp.int32, (blk, nkeys), 1)
    lo = jnp.where(n > 0, 0, blk)
    hi = jnp.where(n < nb - 1, 3 * blk, 2 * blk)
    valid_loc = (j >= r) & (j <= r + 2 * blk) & (j >= lo) & (j < hi) & is_lat
    bias = jnp.where(valid_loc | (j >= 3 * blk), 0.0, NEG_INF)
    low_half = lax.broadcasted_iota(jnp.int32, (blk, SWA_KV), 1) < HEAD_DIM
    zero = jnp.zeros((blk, SWA_KV), q.dtype)
    group = SWA_HEADS // SWA_KV_HEADS
    q_rows, sinks = [], []
    for t in range(group):
        qt = q[:, t * SWA_KV:(t + 1) * SWA_KV]
        q_rows += [jnp.where(low_half, qt, zero), jnp.where(low_half, zero, qt)]
        sinks += [jnp.full((blk, 1), sink_ref[t], F32), jnp.full((blk, 1), sink_ref[group + t], F32)]
    scores = [lax.dot_general(jnp.concatenate(q_rows[2 * t:2 * t + 2], axis=0), k_t, (((1,), (1,)), ((), ())),
                              preferred_element_type=F32) for t in range(group)]
    tiles = []
    for t in range(group):
        sk = jnp.concatenate(sinks[2 * t:2 * t + 2], axis=0)
        s = (scores[t].reshape(2, blk, nkeys) + bias[None]).reshape(2 * blk, nkeys)
        mx = jnp.maximum(jnp.max(s, axis=-1, keepdims=True), sk)
        p = jnp.exp((s - mx).astype(BF16))
        oa = jnp.dot(p, v_aug, preferred_element_type=F32)
        den = oa[:, SWA_KV:] + jnp.exp(sk - mx)
        on = oa[:, :SWA_KV] * (1.0 / den)
        tiles.append(jnp.where(low_half, on[:blk], on[blk:]))
    o_ref[...] = jnp.concatenate(tiles, axis=1).astype(o_ref.dtype)


def _swa(q, kv, sink, n_batch, seq, ctx_len):
    m = q.shape[0]
    blk = SWA_BLOCK
    nb = seq // blk
    n_lat = n_batch * nb
    cpb = ctx_len // blk

    def batch_of(i):
        return jnp.where(i < n_lat, i // nb, (i - n_lat) // cpb)

    def prev_idx(i, s):
        return (jnp.where(i < n_lat, batch_of(i) * nb + jnp.maximum(i % nb - 1, 0), i), 0)

    def next_idx(i, s):
        return (jnp.where(i < n_lat, batch_of(i) * nb + jnp.minimum(i % nb + 1, nb - 1), i), 0)

    def ctx_idx(i, s):
        return (n_batch * seq // ctx_len + batch_of(i), 0)

    grid_spec = pltpu.PrefetchScalarGridSpec(
        num_scalar_prefetch=1,
        grid=(m // blk,),
        in_specs=[pl.BlockSpec((blk, SWA_Q), lambda i, s: (i, 0)),
                  pl.BlockSpec((blk, 2 * SWA_KV), prev_idx),
                  pl.BlockSpec((blk, 2 * SWA_KV), lambda i, s: (i, 0)),
                  pl.BlockSpec((blk, 2 * SWA_KV), next_idx),
                  pl.BlockSpec((ctx_len, 2 * SWA_KV), ctx_idx)],
        out_specs=pl.BlockSpec((blk, SWA_Q), lambda i, s: (i, 0)),
    )
    return pl.pallas_call(
        functools.partial(_swa_kernel, nb=nb, n_lat_blocks=n_lat),
        out_shape=jax.ShapeDtypeStruct((m, SWA_Q), BF16),
        grid_spec=grid_spec,
        compiler_params=_cp(("parallel",)),
        name="swa_attn",
    )(sink, q, kv, kv, kv, kv)


def _diff_kernel(*refs, n_lat_chunks, coef):
    if n_lat_chunks:
        lam_ref, gcol_ref, q_ref, kc_ref, vtc_ref, kl_ref, vtl_ref, o_ref, m_sc, acc_sc, st_a, st_b = refs
    else:
        lam_ref, gcol_ref, q_ref, kc_ref, vtc_ref, o_ref, m_sc, acc_sc, st_a, st_b = refs
    q = q_ref[...]
    tq = q.shape[0]
    dv = DIFF_V_DIM
    lane = lax.broadcasted_iota(jnp.int32, q.shape, 1)
    zero = jnp.zeros_like(q)
    qq = jnp.concatenate([jnp.where(lane < HEAD_DIM, q, zero), jnp.where(lane >= HEAD_DIM, q, zero)], axis=0)

    def scores(k):
        return lax.dot_general(k, qq, (((1,), (1,)), ((), ())), preferred_element_type=F32)

    def accumulate(st_ref, vt):
        st = st_ref[0:vt.shape[1], :]
        m_old = m_sc[...]
        m_new = jnp.maximum(m_old, jnp.max(st, axis=0, keepdims=True))
        alpha = jnp.exp2(m_old - m_new)
        pt = jnp.exp2((st - m_new).astype(BF16))
        acc_sc[...] = alpha * acc_sc[...] + jnp.dot(vt, pt, preferred_element_type=F32)
        m_sc[...] = m_new

    tk = st_a.shape[0]
    k_lat = lambda c: kl_ref[pl.ds(pl.multiple_of(c * tk, tk), tk), :]
    m_sc[...] = jnp.full(m_sc.shape, NEG_INF, F32)
    acc_sc[...] = jnp.zeros(acc_sc.shape, F32)
    bufs = (st_a, st_b)
    n = n_lat_chunks
    if n:
        unroll = DIFF_UNROLL
        st_a[...] = scores(k_lat(0))
        n_trips = (n - 1) // unroll

        def body(j, carry):
            for u in range(unroll):
                c = j * unroll + u
                bufs[(u + 1) % 2][...] = scores(k_lat(c + 1))
                accumulate(bufs[u % 2], vtl_ref[0, c])
            return carry

        lax.fori_loop(0, n_trips, body, 0)
        for c in range(n_trips * unroll, n):
            if c + 1 < n:
                bufs[(c + 1) % 2][...] = scores(k_lat(c + 1))
            else:
                bufs[(c + 1) % 2][0:kc_ref.shape[0], :] = scores(kc_ref[...])
            accumulate(bufs[c % 2], vtl_ref[0, c])
    else:
        st_a[0:kc_ref.shape[0], :] = scores(kc_ref[...])
    accumulate(bufs[n % 2], vtc_ref[0, 0])
    acc = acc_sc[...]
    ot = acc[:dv] * (1.0 / acc[dv:dv + 1])
    odt = ot[:, :tq] - lam_ref[0] * ot[:, tq:]
    ms = jnp.mean(odt * odt, axis=0, keepdims=True)
    yt = odt * lax.rsqrt(ms + DIFF_SUBLN_EPS) * (gcol_ref[...] * coef)
    o_ref[...] = yt.T.astype(o_ref.dtype)


def _diff(q, k, v, lam, subln_g, lam_init, n_batch, seq, ctx_len):
    m = q.shape[0]
    tk = min(DIFF_TK, seq)
    assert seq % tk == 0 and tk % ctx_len == 0
    dv = DIFF_V_DIM
    dva = dv + DIFF_ONES_ROWS
    lat_rows = n_batch * seq

    def transposed_chunks(rows, size):
        t = rows.reshape(rows.shape[0] // size, size, DIFF_HEADS, dv).transpose(2, 0, 3, 1)
        return jnp.concatenate([t, jnp.ones(t.shape[:2] + (DIFF_ONES_ROWS, size), t.dtype)], axis=2)

    vt_lat = transposed_chunks(v[:lat_rows], tk)
    vt_ctx = transposed_chunks(v[lat_rows:], ctx_len)
    gcol = subln_g.reshape(dv, 1)
    lat_chunks = seq // tk
    ctx0 = n_batch * seq // ctx_len

    def call(tq, n_q, q_block0, batch_of, with_lat):
        in_specs = [pl.BlockSpec((dv, 1), lambda h, i, s: (0, 0)),
                    pl.BlockSpec((tq, LANES), lambda h, i, s: (q_block0 + i, h)),
                    pl.BlockSpec((ctx_len, LANES), lambda h, i, s: (ctx0 + batch_of(i), h)),
                    pl.BlockSpec((1, 1, dva, ctx_len), lambda h, i, s: (h, batch_of(i), 0, 0))]
        args = [lam, gcol, q, k, vt_ctx]
        if with_lat:
            in_specs += [pl.BlockSpec((seq, LANES), lambda h, i, s: (batch_of(i), h)),
                         pl.BlockSpec((1, lat_chunks, dva, tk), lambda h, i, s: (h, batch_of(i), 0, 0))]
            args += [k, vt_lat]
        grid_spec = pltpu.PrefetchScalarGridSpec(
            num_scalar_prefetch=1,
            grid=(DIFF_HEADS, n_q),
            in_specs=in_specs,
            out_specs=pl.BlockSpec((tq, LANES), lambda h, i, s: (i, h)),
            scratch_shapes=[pltpu.VMEM((1, 2 * tq), F32), pltpu.VMEM((dva, 2 * tq), F32),
                            pltpu.VMEM((tk, 2 * tq), F32), pltpu.VMEM((tk, 2 * tq), F32)],
        )
        return pl.pallas_call(
            functools.partial(_diff_kernel, n_lat_chunks=lat_chunks if with_lat else 0, coef=1.0 - lam_init),
            out_shape=jax.ShapeDtypeStruct((n_q * tq, DIFF_V), BF16),
            grid_spec=grid_spec,
            compiler_params=_cp(("parallel", "arbitrary")),
            name="diff_attn" if with_lat else "diff_attn_ctx",
        )(*args)

    tq = DIFF_TQ
    y_lat = call(tq, n_batch * seq // tq, 0, lambda i: i // (seq // tq), True)
    y_ctx = call(ctx_len, n_batch, ctx0, lambda i: i, False)
    return jnp.concatenate([y_lat, y_ctx], axis=0)


def _rwkv_prep_kernel(p_ref, hp_ref, hn_ref, mup_ref, mun_ref, kk_w_ref, ka_ref, rk_ref, w0_ref, a0_ref,
                      wup_ref, aup_ref, gup_ref, ones_ref,
                      r_ref, v_ref, kk_ref, ld_ref, ke_ref, bb_ref, g_ref, bonus_ref, sc,
                      *, tm, lat_rows, seq, ctx_len):
    i = pl.program_id(0)
    w = RWKV_WIDTH
    sc[0:8, :] = hp_ref[...]
    sc[8:8 + tm, :] = p_ref[...]
    sc[8 + tm:16 + tm, :] = hn_ref[...]
    p = p_ref[...]
    prev = sc[7:7 + tm, :]
    nxt = sc[9:9 + tm, :]
    row = i * tm + lax.broadcasted_iota(jnp.int32, (tm, 1), 0)
    pos = jnp.where(row < lat_rows, row % seq, (row - lat_rows) % ctx_len)
    seg_len = jnp.where(row < lat_rows, seq, ctx_len)
    prev = jnp.where(pos == 0, 0.0, prev)
    nxt = jnp.where(pos == seg_len - 1, 0.0, nxt)
    ps = p + mup_ref[...] * (prev - p) + mun_ref[...] * (nxt - p)

    r = ps[:, 0:w]
    k = ps[:, w:2 * w]
    v = ps[:, 2 * w:3 * w]
    wd = ps[:, 3 * w:3 * w + 2 * DECAY_LORA]
    ad = ps[:, 3 * w + 2 * DECAY_LORA:3 * w + 2 * DECAY_LORA + 2 * AAA_LORA]
    gd = ps[:, 3 * w + 2 * DECAY_LORA + 2 * AAA_LORA:]

    ones_bd = ones_ref[...]
    g = jnp.dot(_sigmoid(gd).astype(BF16), gup_ref[...], preferred_element_type=F32)
    kk = k * kk_w_ref[...]
    ss = _group_sum(kk * kk, ones_bd)
    kk = kk / jnp.maximum(jnp.sqrt(ss), 1e-12)
    w_raw = w0_ref[...] + jnp.dot(jnp.tanh(wd).astype(BF16), wup_ref[...], preferred_element_type=F32)
    a_raw = a0_ref[...] + jnp.dot(ad.astype(BF16), aup_ref[...], preferred_element_type=F32)
    ld = -math.exp(-0.5) * _sigmoid(w_raw)
    a = _sigmoid(a_raw)
    ka = ka_ref[...]
    ke_sum = jnp.zeros_like(k)
    for d in range(2):
        a_d = a[:, d * w:(d + 1) * w]
        ke = k * (1.0 + (a_d - 1.0) * ka)
        ld_ref[d] = ld[:, d * w:(d + 1) * w]
        ke_ref[d] = ke
        bb_ref[d] = kk * a_d
        ke_sum = ke_sum + ke
    rk = _group_sum(r * ke_sum * rk_ref[...], ones_bd)
    r_ref[...] = r
    v_ref[...] = v
    kk_ref[...] = kk
    g_ref[...] = g
    bonus_ref[...] = rk * v


def _rwkv_prep(p, lp, n_batch, seq, ctx_len):
    m, cols = p.shape
    tm = 256
    w = RWKV_WIDTH
    lat_rows = n_batch * seq
    row = lambda a: a.reshape(1, -1).astype(F32)

    def blockdiag(u):
        z = jnp.zeros_like(u[0])
        return jnp.concatenate([jnp.concatenate([u[0], z], axis=1), jnp.concatenate([z, u[1]], axis=1)], axis=0)

    head = jnp.arange(w) // HEAD_DIM
    ones_bd = (head[:, None] == head[None, :]).astype(BF16)
    full = lambda shape: pl.BlockSpec(shape, lambda i: (0,) * len(shape))
    nb8 = m // 8
    outs = pl.pallas_call(
        functools.partial(_rwkv_prep_kernel, tm=tm, lat_rows=lat_rows, seq=seq, ctx_len=ctx_len),
        out_shape=[jax.ShapeDtypeStruct((m, w), F32)] * 3
        + [jax.ShapeDtypeStruct((2, m, w), F32)] * 3
        + [jax.ShapeDtypeStruct((m, w), F32)] * 2,
        grid=(m // tm,),
        in_specs=[pl.BlockSpec((tm, cols), lambda i: (i, 0)),
                  pl.BlockSpec((8, cols), lambda i: (jnp.maximum(i * (tm // 8) - 1, 0), 0)),
                  pl.BlockSpec((8, cols), lambda i: (jnp.minimum((i + 1) * (tm // 8), nb8 - 1), 0)),
                  full((1, cols)), full((1, cols)), full((1, w)), full((1, w)), full((1, w)),
                  full((1, 2 * w)), full((1, 2 * w)),
                  full((2 * DECAY_LORA, 2 * w)), full((2 * AAA_LORA, 2 * w)), full((GATE_LORA, w)),
                  full((w, w))],
        out_specs=[pl.BlockSpec((tm, w), lambda i: (i, 0))] * 3
        + [pl.BlockSpec((2, tm, w), lambda i: (0, i, 0))] * 3
        + [pl.BlockSpec((tm, w), lambda i: (i, 0))] * 2,
        scratch_shapes=[pltpu.VMEM((tm + 16, cols), F32)],
        compiler_params=_cp(("parallel",)),
        name="rwkv_prep",
    )(p, p, p, row(lp['rwkv_mu_prev']), row(lp['rwkv_mu_next']), row(lp['rwkv_k_k']), row(lp['rwkv_k_a']),
      row(lp['rwkv_r_k']), row(lp['rwkv_w0']), row(lp['rwkv_a0']),
      blockdiag(lp['rwkv_w_up']).astype(BF16), blockdiag(lp['rwkv_a_up']).astype(BF16),
      lp['rwkv_g_up'].astype(BF16), ones_bd)
    return outs


def _rwkv_chunk_prep(d, off, r_ref, v_ref, kk_ref, ld_ref, ke_ref, bb_ref, incl):
    c = RWKV_CHUNK
    ld = ld_ref[0, pl.ds(off, c), :]
    r = r_ref[pl.ds(off, c), :]
    v = v_ref[pl.ds(off, c), :]
    kk = kk_ref[pl.ds(off, c), :]
    ke = ke_ref[0, pl.ds(off, c), :]
    bb = bb_ref[0, pl.ds(off, c), :]
    cum = jnp.dot(incl, ld, precision=HI, preferred_element_type=F32)
    tot = jnp.sum(ld, axis=0, keepdims=True)
    rt = r * jnp.exp(cum)
    einv = jnp.exp(-cum)
    etail = jnp.exp(tot - cum)
    return dict(at=(-kk * jnp.exp(cum - ld)).astype(BF16), rt=rt, rt_b=rt.astype(BF16),
                bt=(bb * einv).astype(BF16), kt=(ke * einv).astype(BF16),
                bh=(bb * etail).astype(BF16), kh=(ke * etail).astype(BF16),
                v=v, v_b=v.astype(BF16), wtot=jnp.exp(tot))


def _rwkv_chunk_pair(offs, in_refs, y_refs, s_sc):
    c = RWKV_CHUNK
    hd = HEAD_DIM
    ti = lax.broadcasted_iota(jnp.int32, (c, c), 0)
    tj = lax.broadcasted_iota(jnp.int32, (c, c), 1)
    eye = (ti == tj).astype(F32)
    incl = [(tj <= ti).astype(F32), (tj >= ti).astype(F32)]
    strict = [m - eye for m in incl]
    diag_mask = ((ti // RWKV_INV_BASE) == (tj // RWKV_INV_BASE)).astype(F32)
    off_masks = []
    sz = RWKV_INV_BASE
    while sz < c:
        off_masks.append((((ti // (2 * sz)) == (tj // (2 * sz))) & ((ti // sz) != (tj // sz))).astype(F32))
        sz *= 2

    pre = [_rwkv_chunk_prep(d, offs[d], *in_refs[d], incl[d]) for d in range(2)]
    lanes = [(d, h) for d in range(2) for h in range(RWKV_HEADS)]
    sl = lambda h: slice(h * hd, (h + 1) * hd)
    get = lambda name: [pre[d][name][:, sl(h)] for d, h in lanes]
    at, rt, rt_b, bt, kt, bh, kh, v, v_b = (get(n) for n in ('at', 'rt', 'rt_b', 'bt', 'kt', 'bh', 'kh', 'v', 'v_b'))
    nl = range(len(lanes))
    nt = (((1,), (1,)), ((), ()))
    bdot = lambda x, y: jnp.dot(x.astype(BF16), y.astype(BF16), preferred_element_type=F32)

    gm = [lax.dot_general(jnp.concatenate([at[i], rt_b[i]], axis=0), jnp.concatenate([bt[i], kt[i]], axis=0), nt,
                          preferred_element_type=F32) for i in nl]
    zz0 = [bdot(v[i].T, kh[i]) for i in nl]
    aab = [gm[i][:c, :c] * strict[lanes[i][0]] for i in nl]
    aak = [gm[i][:c, c:] * strict[lanes[i][0]] for i in nl]
    arb = [gm[i][c:, :c] * incl[lanes[i][0]] for i in nl]
    ark = [gm[i][c:, c:] * incl[lanes[i][0]] for i in nl]
    av = [bdot(jnp.concatenate([aak[i], ark[i]], axis=0), v_b[i]) for i in nl]
    pw = [aab[i] * diag_mask for i in nl]
    tm_ = [eye + pw[i] for i in nl]
    for _ in range(int(math.log2(RWKV_INV_BASE)) - 1):
        pw = [bdot(pw[i], pw[i]) for i in nl]
        tm_ = [tm_[i] + bdot(tm_[i], pw[i]) for i in nl]
    for off_mask in off_masks:
        tn = [bdot(tm_[i], aab[i] * off_mask) for i in nl]
        tm_ = [tm_[i] + bdot(tn[i], tm_[i]) for i in nl]
    au = [bdot(tm_[i], jnp.concatenate([at[i], av[i][:c].astype(BF16)], axis=1)) for i in nl]
    ry = [bdot(arb[i], au[i]) for i in nl]
    mz = [bdot(au[i].T, bh[i]) for i in nl]
    s_old = [s_sc[d, h].astype(BF16) for d, h in lanes]
    ys = [lax.dot_general((rt[i] + ry[i][:, :hd]).astype(BF16), s_old[i], nt, preferred_element_type=F32)
          + ry[i][:, hd:] + av[i][c:] for i in nl]
    s_new = [jnp.dot(s_old[i], (eye * pre[lanes[i][0]]['wtot'][:, sl(lanes[i][1])] + mz[i][:hd]).astype(BF16),
                     preferred_element_type=F32) + mz[i][hd:] + zz0[i] for i in nl]
    for i, (d, h) in enumerate(lanes):
        s_sc[d, h] = s_new[i]
    for d in range(2):
        y_refs[d][0, pl.ds(offs[d], c), :] = jnp.concatenate(
            [ys[i] for i in nl if lanes[i][0] == d], axis=1)


def _rwkv_scan_kernel(rf, vf, kf, ldf, kef, bbf, rb, vb, kb, ldb, keb, bbb, yf, yb, s_sc, *, n_chunks):
    @pl.when(pl.program_id(1) == 0)
    def _():
        s_sc[...] = jnp.zeros(s_sc.shape, F32)

    def body(cc, carry):
        off_f = pl.multiple_of(cc * RWKV_CHUNK, RWKV_CHUNK)
        off_b = pl.multiple_of((n_chunks - 1 - cc) * RWKV_CHUNK, RWKV_CHUNK)
        _rwkv_chunk_pair((off_f, off_b), ((rf, vf, kf, ldf, kef, bbf), (rb, vb, kb, ldb, keb, bbb)),
                         (yf, yb), s_sc)
        return carry

    lax.fori_loop(0, n_chunks, body, 0)


def _rwkv_scan(r, v, kk, ld, ke, bb, n_batch, seq, ctx_len):
    m, w = r.shape
    ts = RWKV_STEP_ROWS
    assert ctx_len == ts
    lpb = seq // ts
    ctx0 = n_batch * seq // ts
    nj = 1 + lpb

    def fwd(b, j):
        return jnp.where(j == 0, ctx0 + b, b * lpb + j - 1)

    def bwd(b, j):
        return jnp.where(j == 0, ctx0 + b, b * lpb + lpb - j)

    shared = lambda f: pl.BlockSpec((ts, w), lambda b, j: (f(b, j), 0))
    per_dir = lambda f, d: pl.BlockSpec((1, ts, w), lambda b, j: (d, f(b, j), 0))
    y = pl.pallas_call(
        functools.partial(_rwkv_scan_kernel, n_chunks=ts // RWKV_CHUNK),
        out_shape=[jax.ShapeDtypeStruct((1, m, w), F32)] * 2,
        grid=(n_batch, nj),
        in_specs=[shared(fwd), shared(fwd), shared(fwd), per_dir(fwd, 0), per_dir(fwd, 0), per_dir(fwd, 0),
                  shared(bwd), shared(bwd), shared(bwd), per_dir(bwd, 1), per_dir(bwd, 1), per_dir(bwd, 1)],
        out_specs=[pl.BlockSpec((1, ts, w), lambda b, j: (0, fwd(b, j), 0)),
                   pl.BlockSpec((1, ts, w), lambda b, j: (0, bwd(b, j), 0))],
        scratch_shapes=[pltpu.VMEM((2, RWKV_HEADS, HEAD_DIM, HEAD_DIM), F32)],
        compiler_params=_cp(("parallel", "arbitrary")),
        name="rwkv_scan",
    )(r, v, kk, ld, ke, bb, r, v, kk, ld, ke, bb)
    return y


def _rwkv_out_kernel(yf_ref, yb_ref, bonus_ref, g_ref, lg_ref, lb_ref, ones_ref, o_ref):
    y = yf_ref[0] + yb_ref[0]
    ones_bd = ones_ref[...]
    inv = 1.0 / HEAD_DIM
    mu = _group_sum(y, ones_bd) * inv
    yc = y - mu
    var = _group_sum(yc * yc, ones_bd) * inv
    yn = yc * lax.rsqrt(var + RWKV_GN_EPS) * lg_ref[...] + lb_ref[...]
    o_ref[...] = ((yn + bonus_ref[...]) * g_ref[...]).astype(o_ref.dtype)


def _rwkv_out(yf, yb, bonus, g, lnx_g, lnx_b):
    m, w = bonus.shape
    tm = ROW_TILE
    head = jnp.arange(w) // HEAD_DIM
    ones_bd = (head[:, None] == head[None, :]).astype(BF16)
    blk = pl.BlockSpec((tm, w), lambda i: (i, 0))
    blk3 = pl.BlockSpec((1, tm, w), lambda i: (0, i, 0))
    one = pl.BlockSpec((1, w), lambda i: (0, 0))
    return pl.pallas_call(
        _rwkv_out_kernel,
        out_shape=jax.ShapeDtypeStruct((m, w), BF16),
        grid=(m // tm,),
        in_specs=[blk3, blk3, blk, blk, one, one, pl.BlockSpec((w, w), lambda i: (0, 0))],
        out_specs=blk,
        compiler_params=_cp(("parallel",)),
        name="rwkv_out",
    )(yf, yb, bonus, g, lnx_g.reshape(1, w), lnx_b.reshape(1, w), ones_bd)


def _merge_kernel(ya_ref, yb_ref, yc_ref, gt_ref, h_ref, pa_ref, pb_ref, pc_ref, wo_ref, gpost_ref, gpre_ref,
                  mod_ref, h_out, f_out, *, d):
    gates = gt_ref[...].astype(F32)
    merged = (_sigmoid(gates[:, 0:d]) * jnp.dot(ya_ref[...], pa_ref[...], preferred_element_type=F32)
              + _sigmoid(gates[:, d:2 * d]) * jnp.dot(yb_ref[...], pb_ref[...], preferred_element_type=F32)
              + _sigmoid(gates[:, 2 * d:3 * d]) * jnp.dot(yc_ref[...], pc_ref[...], preferred_element_type=F32))
    out = jnp.dot(merged.astype(BF16), wo_ref[...], preferred_element_type=F32)
    mod = lambda idx: mod_ref[0, :, idx * d:(idx + 1) * d]
    hn = h_ref[...] + mod(2) * (_rms(out, NORM_EPS) * gpost_ref[...])
    h_out[...] = hn
    f = (_rms(hn, NORM_EPS) * gpre_ref[...]) * (1.0 + mod(4)) + mod(3)
    f_out[...] = f.astype(f_out.dtype)


def _merge(ya, yb, yc, gates, h, pa, pb, pc, wo, g_post, g_pre, mods, f_dtype, lat_bpb, n_batch):
    m, d = h.shape
    tm = 256
    bpb = lat_bpb * (ROW_TILE // tm)
    seg = functools.partial(_seg_of_block, lat_blocks_per_batch=bpb, n_batch=n_batch)
    rows = lambda width: pl.BlockSpec((tm, width), lambda i: (i, 0))
    full = lambda a: pl.BlockSpec(a.shape, lambda i: (0, 0))
    return pl.pallas_call(
        functools.partial(_merge_kernel, d=d),
        out_shape=[jax.ShapeDtypeStruct((m, d), F32), jax.ShapeDtypeStruct((m, d), f_dtype)],
        grid=(m // tm,),
        in_specs=[rows(ya.shape[1]), rows(yb.shape[1]), rows(yc.shape[1]), rows(gates.shape[1]), rows(d),
                  full(pa), full(pb), full(pc), full(wo),
                  pl.BlockSpec((1, d), lambda i: (0, 0)), pl.BlockSpec((1, d), lambda i: (0, 0)),
                  pl.BlockSpec((1, 1, mods.shape[2]), lambda i: (seg(i), 0, 0))],
        out_specs=[rows(d), rows(d)],
        compiler_params=_cp(("parallel",)),
        name="merge",
    )(ya, yb, yc, gates, h, pa, pb, pc, wo, g_post, g_pre, mods)


def _swiglu_hidden(x, wg, wu):
    hg = jnp.dot(x, wg, preferred_element_type=F32)
    hu = jnp.dot(x, wu, preferred_element_type=F32)
    return (hg * _sigmoid(hg) * hu).astype(BF16)


def _ffn_kernel(f_ref, wg_ref, wu_ref, wd_ref, h_ref, gpost_ref, mod_ref, o_ref, acc, *, d):
    j = pl.program_id(1)

    @pl.when(j == 0)
    def _():
        acc[...] = jnp.zeros(acc.shape, F32)

    hid = _swiglu_hidden(f_ref[...], wg_ref[...], wu_ref[...])
    acc[...] += jnp.dot(hid, wd_ref[...], preferred_element_type=F32)

    @pl.when(j == pl.num_programs(1) - 1)
    def _():
        gate = mod_ref[0, :, 5 * d:6 * d]
        o_ref[...] = h_ref[...] + gate * (_rms(acc[...], NORM_EPS) * gpost_ref[...])


def _ffn(f, wg, wu, wd, h, g_post, mods, lat_bpb, n_batch):
    m, d = h.shape
    ff = wg.shape[1]
    tm, tf = ROW_TILE, FFN_TF
    seg = functools.partial(_seg_of_block, lat_blocks_per_batch=lat_bpb, n_batch=n_batch)
    return pl.pallas_call(
        functools.partial(_ffn_kernel, d=d),
        out_shape=jax.ShapeDtypeStruct((m, d), F32),
        grid=(m // tm, ff // tf),
        in_specs=[pl.BlockSpec((tm, d), lambda i, j: (i, 0)),
                  pl.BlockSpec((d, tf), lambda i, j: (0, j)),
                  pl.BlockSpec((d, tf), lambda i, j: (0, j)),
                  pl.BlockSpec((tf, d), lambda i, j: (j, 0)),
                  pl.BlockSpec((tm, d), lambda i, j: (i, 0)),
                  pl.BlockSpec((1, d), lambda i, j: (0, 0)),
                  pl.BlockSpec((1, 1, mods.shape[2]), lambda i, j: (seg(i), 0, 0))],
        out_specs=pl.BlockSpec((tm, d), lambda i, j: (i, 0)),
        scratch_shapes=[pltpu.VMEM((tm, d), F32)],
        compiler_params=_cp(("parallel", "arbitrary")),
        name="ffn_dense",
    )(f, wg, wu, wd, h, g_post, mods)


def _router_kernel(f_ref, w_ref, idx_ref, wt_ref):
    logits = jnp.dot(f_ref[...], w_ref[...], precision=HI, preferred_element_type=F32)
    lane = lax.broadcasted_iota(jnp.int32, logits.shape, 1)
    logits = jnp.where(lane < N_EXPERTS, logits, -jnp.inf)
    m1 = jnp.max(logits, axis=-1, keepdims=True)
    i1 = jnp.min(jnp.where(logits == m1, lane, LANES), axis=-1, keepdims=True)
    rest = jnp.where(lane == i1, -jnp.inf, logits)
    m2 = jnp.max(rest, axis=-1, keepdims=True)
    i2 = jnp.min(jnp.where(rest == m2, lane, LANES), axis=-1, keepdims=True)
    e = jnp.exp(m2 - m1)
    w1 = 1.0 / (1.0 + e)
    w2 = e / (1.0 + e)
    idx_ref[...] = jnp.where(lane == 0, i1, jnp.where(lane == 1, i2, 0))
    wt_ref[...] = jnp.where(lane == 0, w1, jnp.where(lane == 1, w2, 0.0))


def _router(f, router_w, n_rows):
    d = f.shape[1]
    tm = ROW_TILE
    w_pad = jnp.zeros((d, LANES), F32).at[:, :N_EXPERTS].set(router_w)
    return pl.pallas_call(
        _router_kernel,
        out_shape=[jax.ShapeDtypeStruct((n_rows, LANES), jnp.int32), jax.ShapeDtypeStruct((n_rows, LANES), F32)],
        grid=(n_rows // tm,),
        in_specs=[pl.BlockSpec((tm, d), lambda i: (i, 0)), pl.BlockSpec((d, LANES), lambda i: (0, 0))],
        out_specs=[pl.BlockSpec((tm, LANES), lambda i: (i, 0))] * 2,
        compiler_params=_cp(("parallel",)),
        name="router",
    )(f, w_pad)


def _moe_gather_copy(f_hbm, xbuf, sem, slot, src_row, dst_row):
    return pltpu.make_async_copy(f_hbm.at[pl.ds(src_row, 1), :], xbuf.at[slot, pl.ds(dst_row, 1), :],
                                 sem.at[slot])


def _moe_ffn_kernel(blk_e_ref, nused_ref, tok_ref, f_hbm, wg_ref, wu_ref, wd_ref, y_ref, xbuf, xb, acc, sem,
                    *, tm, nj):
    i = pl.program_id(0)
    j = pl.program_id(1)
    nused = nused_ref[0]
    active = i < nused
    slot = i % 2
    per_step = tm // nj

    @pl.when((i == 0) & (j == 0))
    def _():
        def issue(r, carry):
            _moe_gather_copy(f_hbm, xbuf, sem, 0, tok_ref[r], r).start()
            return carry

        lax.fori_loop(0, tm, issue, 0)

    @pl.when((i <= nused) & (j == 0))
    def _():
        pltpu.make_async_copy(f_hbm.at[pl.ds(0, tm), :], xbuf.at[slot], sem.at[slot]).wait()

    @pl.when(active & (j == 0))
    def _():
        xb[...] = xbuf[slot].astype(BF16)
        acc[...] = jnp.zeros(acc.shape, F32)

    @pl.when(active)
    def _():
        base = (i + 1) * tm + j * per_step
        for r in range(per_step):
            _moe_gather_copy(f_hbm, xbuf, sem, 1 - slot, tok_ref[base + r], j * per_step + r).start()
        hid = _swiglu_hidden(xb[...], wg_ref[0], wu_ref[0])
        acc[...] += jnp.dot(hid, wd_ref[0], preferred_element_type=F32)

    @pl.when(j == nj - 1)
    def _():
        y_ref[...] = acc[...]


def _moe_ffn(f, blk_e, nused, tok, wg, wu, wd, n_blocks):
    d = f.shape[1]
    ff = wg.shape[2]
    tm, tf = MOE_TILE, MOE_TF

    def e_of(i, be, nu):
        return be[jnp.minimum(i, nu[0] - 1)]

    grid_spec = pltpu.PrefetchScalarGridSpec(
        num_scalar_prefetch=3,
        grid=(n_blocks, ff // tf),
        in_specs=[pl.BlockSpec(memory_space=pl.ANY),
                  pl.BlockSpec((1, d, tf), lambda i, j, be, nu, tk: (e_of(i, be, nu), 0, j)),
                  pl.BlockSpec((1, d, tf), lambda i, j, be, nu, tk: (e_of(i, be, nu), 0, j)),
                  pl.BlockSpec((1, tf, d), lambda i, j, be, nu, tk: (e_of(i, be, nu), j, 0))],
        out_specs=pl.BlockSpec((tm, d), lambda i, j, be, nu, tk: (i, 0)),
        scratch_shapes=[pltpu.VMEM((2, tm, d), F32), pltpu.VMEM((tm, d), BF16), pltpu.VMEM((tm, d), F32),
                        pltpu.SemaphoreType.DMA((2,))],
    )
    return pl.pallas_call(
        functools.partial(_moe_ffn_kernel, tm=tm, nj=ff // tf),
        out_shape=jax.ShapeDtypeStruct((n_blocks * tm, d), F32),
        grid_spec=grid_spec,
        compiler_params=_cp(("arbitrary", "arbitrary")),
        name="moe_ffn",
    )(blk_e, nused, tok, f, wg, wu, wd)


def _moe_combine_kernel(p0_ref, p1_ref, y_hbm, wt_ref, h_ref, gpost_ref, mod_ref, o_ref, b0, b1, sem, *, tm, d):
    i = pl.program_id(0)

    def issue(r, carry):
        pltpu.make_async_copy(y_hbm.at[pl.ds(p0_ref[i * tm + r], 1), :], b0.at[pl.ds(r, 1), :], sem.at[0]).start()
        pltpu.make_async_copy(y_hbm.at[pl.ds(p1_ref[i * tm + r], 1), :], b1.at[pl.ds(r, 1), :], sem.at[1]).start()
        return carry

    lax.fori_loop(0, tm, issue, 0)
    pltpu.make_async_copy(y_hbm.at[pl.ds(0, tm), :], b0, sem.at[0]).wait()
    pltpu.make_async_copy(y_hbm.at[pl.ds(0, tm), :], b1, sem.at[1]).wait()
    wt = wt_ref[...]
    y = b0[...] * wt[:, 0:1] + b1[...] * wt[:, 1:2]
    gate = mod_ref[0, :, 5 * d:6 * d]
    o_ref[...] = h_ref[...] + gate * (_rms(y, NORM_EPS) * gpost_ref[...])


def _moe_combine(pos0, pos1, y, wt, h, g_post, mods, n_rows, rows_per_batch, n_batch):
    d = h.shape[1]
    tm = COMBINE_TILE
    seg = functools.partial(_seg_of_block, lat_blocks_per_batch=rows_per_batch // tm, n_batch=n_batch)
    grid_spec = pltpu.PrefetchScalarGridSpec(
        num_scalar_prefetch=2,
        grid=(n_rows // tm,),
        in_specs=[pl.BlockSpec(memory_space=pl.ANY),
                  pl.BlockSpec((tm, LANES), lambda i, a, b: (i, 0)),
                  pl.BlockSpec((tm, d), lambda i, a, b: (i, 0)),
                  pl.BlockSpec((1, d), lambda i, a, b: (0, 0)),
                  pl.BlockSpec((1, 1, mods.shape[2]), lambda i, a, b: (seg(i), 0, 0))],
        out_specs=pl.BlockSpec((tm, d), lambda i, a, b: (i, 0)),
        scratch_shapes=[pltpu.VMEM((tm, d), F32), pltpu.VMEM((tm, d), F32), pltpu.SemaphoreType.DMA((2,))],
    )
    return pl.pallas_call(
        functools.partial(_moe_combine_kernel, tm=tm, d=d),
        out_shape=jax.ShapeDtypeStruct((n_rows, d), F32),
        grid_spec=grid_spec,
        compiler_params=_cp(("arbitrary",)),
        name="moe_combine",
    )(pos0, pos1, y, wt, h, g_post, mods)


def _moe_slots(top_i, tile):
    n = top_i.shape[0]
    a = n * 2
    e_flat = top_i.reshape(a)
    onehot = (e_flat[:, None] == jnp.arange(N_EXPERTS, dtype=jnp.int32)[None, :]).astype(jnp.int32)
    sub = LANES
    blocks = onehot.reshape(a // sub, sub, N_EXPERTS).astype(F32)
    tri = (jnp.arange(sub)[:, None] >= jnp.arange(sub)[None, :]).astype(F32)
    within = jnp.einsum('ij,bjk->bik', tri, blocks).astype(jnp.int32)
    totals = within[:, -1, :]
    csum = (within + (jnp.cumsum(totals, axis=0) - totals)[:, None, :]).reshape(a, N_EXPERTS)
    rank = jnp.sum(csum * onehot, axis=1) - 1
    counts = csum[-1]
    padded = (counts + tile - 1) // tile * tile
    pends = jnp.cumsum(padded)
    pstarts = pends - padded
    dest = (jnp.sum(onehot * pstarts[None, :], axis=1) + rank).astype(jnp.int32)
    n_blocks = a // tile + N_EXPERTS
    tok = jnp.zeros((n_blocks * tile,), jnp.int32).at[dest].set(jnp.arange(a, dtype=jnp.int32) // 2)
    block_start = jnp.arange(n_blocks, dtype=jnp.int32) * tile
    blk_e = jnp.minimum(jnp.sum((block_start[:, None] >= pends[None, :]).astype(jnp.int32), axis=1),
                        N_EXPERTS - 1)
    nused = (pends[-1:] // tile).astype(jnp.int32)
    return tok, blk_e, nused, dest.reshape(n, 2), n_blocks


def _swa_head_order(w, axis):
    group = SWA_HEADS // SWA_KV_HEADS
    order = [g * group + t for t in range(group) for g in range(SWA_KV_HEADS)]
    shape = w.shape
    w = w.reshape(shape[:axis] + (SWA_HEADS, HEAD_DIM) + shape[axis + 1:])
    return jnp.take(w, jnp.array(order), axis=axis).reshape(shape)


def _rope_tables(n_batch, seq, ctx_len):
    t = jnp.arange(seq, dtype=jnp.int32)
    row = (t // GRID_W).astype(F32)
    col = (t % GRID_W).astype(F32)
    axis_dim = HEAD_DIM // 2
    inv_freq = ROPE_THETA ** (-jnp.arange(0, axis_dim, 2, dtype=F32) / axis_dim)
    dd = jnp.arange(LANES) % HEAD_DIM
    pos = jnp.where((dd // axis_dim)[None, :] == 0, row[:, None], col[:, None])
    ang = pos * inv_freq[dd % (axis_dim // 2)][None, :]
    cos = jnp.cos(ang)
    sin = jnp.where(((dd % axis_dim) < axis_dim // 2)[None, :], -jnp.sin(ang), jnp.sin(ang))
    n_ctx = n_batch * ctx_len
    cos = jnp.concatenate([jnp.tile(cos, (n_batch, 1)), jnp.ones((n_ctx, LANES), F32)], axis=0)
    sin = jnp.concatenate([jnp.tile(sin, (n_batch, 1)), jnp.zeros((n_ctx, LANES), F32)], axis=0)
    return cos, sin


def kernel(x, c, ctx, c_ctx, ada_w, ada_b, pre_mix_g, post_mix_g, pre_ffn_g, post_ffn_g, w_in, swa_sink,
           rwkv_mu_prev, rwkv_mu_next, rwkv_w0, rwkv_w_up, rwkv_a0, rwkv_a_up, rwkv_g_up, rwkv_k_k, rwkv_k_a,
           rwkv_r_k, rwkv_lnx_g, rwkv_lnx_b, diff_lambda, diff_subln_g, proj_swa, proj_rwkv, proj_diff, w_out,
           ffn_w_gate, ffn_w_up, ffn_w_down, router_w, moe_w_gate, moe_w_up, moe_w_down):
    n_batch, seq, d = x.shape
    ctx_len = ctx.shape[1]
    depth = w_in.shape[0]
    lat_rows = n_batch * seq
    lat_bpb = seq // ROW_TILE
    assert seq % ROW_TILE == 0 and (n_batch * ctx_len) % ROW_TILE == 0

    h = jnp.concatenate([x.reshape(lat_rows, d), ctx.reshape(n_batch * ctx_len, d)], axis=0)
    m = h.shape[0]
    cond = jnp.zeros((8, d), F32).at[:n_batch].set(c).at[n_batch].set(c_ctx)
    rope = _rope_tables(n_batch, seq, ctx_len)
    row = lambda a: a.reshape(1, -1)

    o_swa = 0
    o_rwkv = o_swa + SWA_Q + 2 * SWA_KV
    o_diff = o_rwkv + RWKV_COLS
    o_gate = o_diff + 2 * DIFF_QK + DIFF_V
    o_end = o_gate + 3 * d

    for layer in range(depth):
        mods = _ada(cond, ada_w[layer], ada_b[layer].reshape(1, -1))[:, None, :]
        wl = w_in[layer].astype(BF16)
        a = _prenorm(h, row(pre_mix_g[layer]), mods, 0, 1, lat_bpb, n_batch)

        w_swa = jnp.concatenate([_swa_head_order(wl[:, o_swa:o_swa + SWA_Q], axis=1),
                                 wl[:, o_swa + SWA_Q:o_rwkv]], axis=1)
        q_swa, kv_swa = _proj(a, w_swa, (SWA_Q, 2 * SWA_KV), (BF16, BF16), rope=rope,
                              rope_cols=SWA_Q + SWA_KV, scale_cols=SWA_Q, q_scale=HEAD_DIM ** -0.5)
        (p_rwkv,) = _proj(a, wl[:, o_rwkv:o_diff], (RWKV_COLS,), (F32,))
        q_diff, k_diff, v_diff = _proj(a, wl[:, o_diff:o_gate], (DIFF_QK, DIFF_QK, DIFF_V), (BF16,) * 3,
                                       rope=rope, rope_cols=2 * DIFF_QK, scale_cols=DIFF_QK,
                                       q_scale=HEAD_DIM ** -0.5 * math.log2(math.e))
        (gates,) = _proj(a, wl[:, o_gate:o_end], (d,), (BF16,), tn=d)

        ya = _swa(q_swa, kv_swa, swa_sink[layer].astype(F32), n_batch, seq, ctx_len)

        lp = {'rwkv_mu_prev': rwkv_mu_prev[layer], 'rwkv_mu_next': rwkv_mu_next[layer],
              'rwkv_w0': rwkv_w0[layer], 'rwkv_w_up': rwkv_w_up[layer], 'rwkv_a0': rwkv_a0[layer],
              'rwkv_a_up': rwkv_a_up[layer], 'rwkv_g_up': rwkv_g_up[layer], 'rwkv_k_k': rwkv_k_k[layer],
              'rwkv_k_a': rwkv_k_a[layer], 'rwkv_r_k': rwkv_r_k[layer]}
        r_, v_, kk_, ld_, ke_, bb_, g_, bonus_ = _rwkv_prep(p_rwkv, lp, n_batch, seq, ctx_len)
        y_f, y_b = _rwkv_scan(r_, v_, kk_, ld_, ke_, bb_, n_batch, seq, ctx_len)
        yb = _rwkv_out(y_f, y_b, bonus_, g_, rwkv_lnx_g[layer], rwkv_lnx_b[layer])

        lam_vec = diff_lambda[layer].astype(F32)
        lam_init = 0.8 - 0.6 * math.exp(-0.3 * layer)
        lam = (jnp.exp(jnp.sum(lam_vec[0] * lam_vec[1])) - jnp.exp(jnp.sum(lam_vec[2] * lam_vec[3]))
               + lam_init).reshape(1)
        yc = _diff(q_diff, k_diff, v_diff, lam, row(diff_subln_g[layer]), lam_init, n_batch, seq, ctx_len)

        moe_layer = layer % 2 == 1
        jj = layer // 2
        h, f = _merge(ya, yb, yc, gates, h, _swa_head_order(proj_swa[layer], axis=0).astype(BF16),
                      proj_rwkv[layer].astype(BF16),
                      proj_diff[layer].astype(BF16), w_out[layer].astype(BF16), row(post_mix_g[layer]),
                      row(pre_ffn_g[layer]), mods, F32 if moe_layer else BF16, lat_bpb, n_batch)
        need_ctx = layer < depth - 1
        if not moe_layer:
            h = _ffn(f, ffn_w_gate[jj].astype(BF16), ffn_w_up[jj].astype(BF16), ffn_w_down[jj].astype(BF16),
                     h, row(post_ffn_g[layer]), mods, lat_bpb, n_batch)
        else:
            n_tok = m if need_ctx else lat_rows
            top_i, top_w = _router(f, router_w[jj], n_tok)
            tok, blk_e, nused, dest, n_blocks = _moe_slots(top_i[:, :2], MOE_TILE)
            y = _moe_ffn(f, blk_e, nused, tok, moe_w_gate[jj].astype(BF16), moe_w_up[jj].astype(BF16),
                         moe_w_down[jj].astype(BF16), n_blocks)
            h = _moe_combine(dest[:, 0], dest[:, 1], y, top_w, h, row(post_ffn_g[layer]), mods, n_tok, seq,
                             n_batch)
    return h[:lat_rows].reshape(n_batch, seq, d)
```

```python
import functools
import math

import jax
import jax.numpy as jnp
from jax import lax
from jax.experimental import pallas as pl
from jax.experimental.pallas import tpu as pltpu

F32 = jnp.float32
BF16 = jnp.bfloat16
HI = lax.Precision.HIGHEST

HEAD_DIM = 64
GRID_W = 64
ROPE_THETA = 10000.0
NORM_EPS = 1e-6
NEG_INF = -1e30
SWA_HEADS = 8
SWA_KV_HEADS = 2
SWA_BLOCK = 128
RWKV_HEADS = 8
RWKV_WIDTH = RWKV_HEADS * HEAD_DIM
DECAY_LORA = 64
AAA_LORA = 64
GATE_LORA = 128
RWKV_GN_EPS = 64e-5
DIFF_HEADS = 4
DIFF_V_DIM = 2 * HEAD_DIM
DIFF_SUBLN_EPS = 1e-5
N_EXPERTS = 8
SWA_Q = SWA_HEADS * HEAD_DIM
SWA_KV = SWA_KV_HEADS * HEAD_DIM
assert SWA_KV_HEADS == 2 and SWA_KV == 128
RWKV_COLS = 3 * RWKV_WIDTH + 2 * DECAY_LORA + 2 * AAA_LORA + GATE_LORA
DIFF_QK = DIFF_HEADS * 2 * HEAD_DIM
DIFF_V = DIFF_HEADS * DIFF_V_DIM

LANES = 128
VMEM_LIMIT = 48 * 1024 * 1024
ROW_TILE = 512
RWKV_CHUNK = 64
RWKV_INV_BASE = 8
RWKV_GROUP = 2
RWKV_STEP_ROWS = 256
DIFF_TQ = 256
DIFF_TK = 1024
DIFF_UNROLL = 2
DIFF_ONES_ROWS = 16
MOE_TILE = 512
MOE_TF = 1792
FFN_TF = 1408
COMBINE_TILE = 256


def _cp(sem, **kw):
    return pltpu.CompilerParams(dimension_semantics=sem, vmem_limit_bytes=VMEM_LIMIT, **kw)


def _seg_of_block(i, lat_blocks_per_batch, n_batch):
    return jnp.minimum(i // lat_blocks_per_batch, n_batch)


def _rms(x, eps):
    return x * lax.rsqrt(jnp.mean(x * x, axis=-1, keepdims=True) + eps)


def _sigmoid(x):
    return 1.0 / (1.0 + jnp.exp(-x))


def _group_sum(x, ones_bd):
    hi = x.astype(BF16)
    mid = (x - hi.astype(F32)).astype(BF16)
    rows = x.shape[0]
    parts = jnp.dot(jnp.concatenate([hi, mid], axis=0), ones_bd, preferred_element_type=F32)
    return parts[:rows] + parts[rows:]


def _ada_kernel(x_ref, w_ref, b_ref, o_ref):
    x = x_ref[...]
    s = x * _sigmoid(x)
    o_ref[...] = jnp.dot(s, w_ref[...], precision=HI, preferred_element_type=F32) + b_ref[...]


def _ada(cond, w, b):
    rows, d = cond.shape
    n = w.shape[1]
    return pl.pallas_call(
        _ada_kernel,
        out_shape=jax.ShapeDtypeStruct((rows, n), F32),
        grid=(n // d,),
        in_specs=[pl.BlockSpec((rows, d), lambda j: (0, 0)),
                  pl.BlockSpec((d, d), lambda j: (0, j)),
                  pl.BlockSpec((1, d), lambda j: (0, j))],
        out_specs=pl.BlockSpec((rows, d), lambda j: (0, j)),
        compiler_params=_cp(("parallel",)),
        name="ada_mod",
    )(cond, w, b)


def _prenorm_kernel(h_ref, g_ref, mod_ref, o_ref, *, d, shift_idx, scale_idx):
    y = _rms(h_ref[...], NORM_EPS) * g_ref[...]
    shift = mod_ref[0, :, shift_idx * d:(shift_idx + 1) * d]
    scale = mod_ref[0, :, scale_idx * d:(scale_idx + 1) * d]
    o_ref[...] = (y * (1.0 + scale) + shift).astype(o_ref.dtype)


def _prenorm(h, g, mods, shift_idx, scale_idx, lat_bpb, n_batch):
    m, d = h.shape
    tm = ROW_TILE
    seg = functools.partial(_seg_of_block, lat_blocks_per_batch=lat_bpb, n_batch=n_batch)
    return pl.pallas_call(
        functools.partial(_prenorm_kernel, d=d, shift_idx=shift_idx, scale_idx=scale_idx),
        out_shape=jax.ShapeDtypeStruct((m, d), BF16),
        grid=(m // tm,),
        in_specs=[pl.BlockSpec((tm, d), lambda i: (i, 0)),
                  pl.BlockSpec((1, d), lambda i: (0, 0)),
                  pl.BlockSpec((1, 1, mods.shape[2]), lambda i: (seg(i), 0, 0))],
        out_specs=pl.BlockSpec((tm, d), lambda i: (i, 0)),
        compiler_params=_cp(("parallel",)),
        name="prenorm",
    )(h, g, mods)


def _proj_kernel(*refs, splits, rope_cols, scale_cols, q_scale):
    if rope_cols:
        a_ref, w_ref, cos_ref, sin_ref = refs[:4]
        outs = refs[4:]
    else:
        a_ref, w_ref = refs[:2]
        outs = refs[2:]
    y = jnp.dot(a_ref[...], w_ref[...], preferred_element_type=F32)
    tm, tn = y.shape
    if rope_cols:
        cos = cos_ref[...]
        sin = sin_ref[...]
        lane = lax.broadcasted_iota(jnp.int32, (tm, LANES), 1)
        first_half = (lane % 32) < 16
        pieces = []
        for c in range(tn // LANES):
            yc = y[:, c * LANES:(c + 1) * LANES]
            if c * LANES < rope_cols:
                partner = jnp.where(first_half, pltpu.roll(yc, LANES - 16, 1), pltpu.roll(yc, 16, 1))
                yc = yc * cos + partner * sin
            if c * LANES < scale_cols:
                yc = yc * q_scale
            pieces.append(yc)
        y = jnp.concatenate(pieces, axis=1)
    start = 0
    for o_ref, width in zip(outs, splits):
        o_ref[...] = y[:, start:start + width].astype(o_ref.dtype)
        start += width


def _proj(a, w, splits, dtypes, rope=None, rope_cols=0, scale_cols=0, q_scale=1.0, tn=None):
    m, k = a.shape
    n = w.shape[1]
    tm = ROW_TILE
    tn = n if tn is None else tn
    assert sum(splits) == tn and (len(splits) == 1 or tn == n)
    in_specs = [pl.BlockSpec((tm, k), lambda i, j: (i, 0)),
                pl.BlockSpec((k, tn), lambda i, j: (0, j))]
    args = [a, w]
    if rope_cols:
        in_specs += [pl.BlockSpec((tm, LANES), lambda i, j: (i, 0))] * 2
        args += list(rope)
    out_specs = []
    out_shape = []
    if len(splits) == 1:
        out_specs.append(pl.BlockSpec((tm, tn), lambda i, j: (i, j)))
        out_shape.append(jax.ShapeDtypeStruct((m, n), dtypes[0]))
    else:
        for width, dt in zip(splits, dtypes):
            out_specs.append(pl.BlockSpec((tm, width), lambda i, j: (i, 0)))
            out_shape.append(jax.ShapeDtypeStruct((m, width), dt))
    res = pl.pallas_call(
        functools.partial(_proj_kernel, splits=tuple(splits), rope_cols=rope_cols, scale_cols=scale_cols,
                          q_scale=q_scale),
        out_shape=out_shape,
        grid=(m // tm, n // tn),
        in_specs=in_specs,
        out_specs=out_specs,
        compiler_params=_cp(("parallel", "parallel")),
        name="proj",
    )(*args)
    return res


def _swa_kernel(sink_ref, q_ref, kp_ref, kc_ref, kn_ref, kx_ref, o_ref, *, nb, n_lat_blocks):
    i = pl.program_id(0)
    is_lat = i < n_lat_blocks
    n = i % nb
    blk = SWA_BLOCK
    q = q_ref[...]
    kv = jnp.concatenate([kp_ref[...], kc_ref[...], kn_ref[...], kx_ref[...]], axis=0)
    nkeys = kv.shape[0]
    k_t = kv[:, :SWA_KV]
    v_aug = jnp.concatenate([kv[:, SWA_KV:], jnp.ones((nkeys, SWA_KV), kv.dtype)], axis=1)
    r = lax.broadcasted_iota(jnp.int32, (blk, nkeys), 0)
    j = lax.broadcasted_iota(jnp.int32, (blk, nkeys), 1)
    lo = jnp.where(n > 0, 0, blk)
    hi = jnp.where(n < nb - 1, 3 * blk, 2 * blk)
    valid_loc = (j >= r) & (j <= r + 2 * blk) & (j >= lo) & (j < hi) & is_lat
    bias = jnp.where(valid_loc | (j >= 3 * blk), 0.0, NEG_INF)
    low_half = lax.broadcasted_iota(jnp.int32, (blk, SWA_KV), 1) < HEAD_DIM
    zero = jnp.zeros((blk, SWA_KV), q.dtype)
    group = SWA_HEADS // SWA_KV_HEADS
    q_rows, sinks = [], []
    for t in range(group):
        qt = q[:, t * SWA_KV:(t + 1) * SWA_KV]
        q_rows += [jnp.where(low_half, qt, zero), jnp.where(low_half, zero, qt)]
        sinks += [jnp.full((blk, 1), sink_ref[t], F32), jnp.full((blk, 1), sink_ref[group + t], F32)]
    scores = [lax.dot_general(jnp.concatenate(q_rows[2 * t:2 * t + 2], axis=0), k_t, (((1,), (1,)), ((), ())),
                              preferred_element_type=F32) for t in range(group)]
    tiles = []
    for t in range(group):
        sk = jnp.concatenate(sinks[2 * t:2 * t + 2], axis=0)
        s = (scores[t].reshape(2, blk, nkeys) + bias[None]).reshape(2 * blk, nkeys)
        mx = jnp.maximum(jnp.max(s, axis=-1, keepdims=True), sk)
        p = jnp.exp((s - mx).astype(BF16))
        oa = jnp.dot(p, v_aug, preferred_element_type=F32)
        den = oa[:, SWA_KV:] + jnp.exp(sk - mx)
        on = oa[:, :SWA_KV] * (1.0 / den)
        tiles.append(jnp.where(low_half, on[:blk], on[blk:]))
    o_ref[...] = jnp.concatenate(tiles, axis=1).astype(o_ref.dtype)


def _swa(q, kv, sink, n_batch, seq, ctx_len):
    m = q.shape[0]
    blk = SWA_BLOCK
    nb = seq // blk
    n_lat = n_batch * nb
    cpb = ctx_len // blk

    def batch_of(i):
        return jnp.where(i < n_lat, i // nb, (i - n_lat) // cpb)

    def prev_idx(i, s):
        return (jnp.where(i < n_lat, batch_of(i) * nb + jnp.maximum(i % nb - 1, 0), i), 0)

    def next_idx(i, s):
        return (jnp.where(i < n_lat, batch_of(i) * nb + jnp.minimum(i % nb + 1, nb - 1), i), 0)

    def ctx_idx(i, s):
        return (n_batch * seq // ctx_len + batch_of(i), 0)

    grid_spec = pltpu.PrefetchScalarGridSpec(
        num_scalar_prefetch=1,
        grid=(m // blk,),
        in_specs=[pl.BlockSpec((blk, SWA_Q), lambda i, s: (i, 0)),
                  pl.BlockSpec((blk, 2 * SWA_KV), prev_idx),
                  pl.BlockSpec((blk, 2 * SWA_KV), lambda i, s: (i, 0)),
                  pl.BlockSpec((blk, 2 * SWA_KV), next_idx),
                  pl.BlockSpec((ctx_len, 2 * SWA_KV), ctx_idx)],
        out_specs=pl.BlockSpec((blk, SWA_Q), lambda i, s: (i, 0)),
    )
    return pl.pallas_call(
        functools.partial(_swa_kernel, nb=nb, n_lat_blocks=n_lat),
        out_shape=jax.ShapeDtypeStruct((m, SWA_Q), BF16),
        grid_spec=grid_spec,
        compiler_params=_cp(("parallel",)),
        name="swa_attn",
    )(sink, q, kv, kv, kv, kv)


def _diff_kernel(*refs, n_lat_chunks, coef):
    if n_lat_chunks:
        lam_ref, gcol_ref, q_ref, kc_ref, vtc_ref, kl_ref, vtl_ref, o_ref, m_sc, acc_sc, st_a, st_b = refs
    else:
        lam_ref, gcol_ref, q_ref, kc_ref, vtc_ref, o_ref, m_sc, acc_sc, st_a, st_b = refs
    q = q_ref[...]
    tq = q.shape[0]
    dv = DIFF_V_DIM
    lane = lax.broadcasted_iota(jnp.int32, q.shape, 1)
    zero = jnp.zeros_like(q)
    qq = jnp.concatenate([jnp.where(lane < HEAD_DIM, q, zero), jnp.where(lane >= HEAD_DIM, q, zero)], axis=0)

    def scores(k):
        return lax.dot_general(k, qq, (((1,), (1,)), ((), ())), preferred_element_type=F32)

    def accumulate(st_ref, vt):
        st = st_ref[0:vt.shape[1], :]
        m_old = m_sc[...]
        m_new = jnp.maximum(m_old, jnp.max(st, axis=0, keepdims=True))
        alpha = jnp.exp2(m_old - m_new)
        pt = jnp.exp2((st - m_new).astype(BF16))
        acc_sc[...] = alpha * acc_sc[...] + jnp.dot(vt, pt, preferred_element_type=F32)
        m_sc[...] = m_new

    tk = st_a.shape[0]
    k_lat = lambda c: kl_ref[pl.ds(pl.multiple_of(c * tk, tk), tk), :]
    m_sc[...] = jnp.full(m_sc.shape, NEG_INF, F32)
    acc_sc[...] = jnp.zeros(acc_sc.shape, F32)
    bufs = (st_a, st_b)
    n = n_lat_chunks
    if n:
        unroll = DIFF_UNROLL
        st_a[...] = scores(k_lat(0))
        n_trips = (n - 1) // unroll

        def body(j, carry):
            for u in range(unroll):
                c = j * unroll + u
                bufs[(u + 1) % 2][...] = scores(k_lat(c + 1))
                accumulate(bufs[u % 2], vtl_ref[0, c])
            return carry

        lax.fori_loop(0, n_trips, body, 0)
        for c in range(n_trips * unroll, n):
            if c + 1 < n:
                bufs[(c + 1) % 2][...] = scores(k_lat(c + 1))
            else:
                bufs[(c + 1) % 2][0:kc_ref.shape[0], :] = scores(kc_ref[...])
            accumulate(bufs[c % 2], vtl_ref[0, c])
    else:
        st_a[0:kc_ref.shape[0], :] = scores(kc_ref[...])
    accumulate(bufs[n % 2], vtc_ref[0, 0])
    acc = acc_sc[...]
    ot = acc[:dv] * (1.0 / acc[dv:dv + 1])
    odt = ot[:, :tq] - lam_ref[0] * ot[:, tq:]
    ms = jnp.mean(odt * odt, axis=0, keepdims=True)
    yt = odt * lax.rsqrt(ms + DIFF_SUBLN_EPS) * (gcol_ref[...] * coef)
    o_ref[...] = yt.T.astype(o_ref.dtype)


def _diff(q, k, v, lam, subln_g, lam_init, n_batch, seq, ctx_len):
    m = q.shape[0]
    tk = min(DIFF_TK, seq)
    assert seq % tk == 0 and tk % ctx_len == 0
    dv = DIFF_V_DIM
    dva = dv + DIFF_ONES_ROWS
    lat_rows = n_batch * seq

    def transposed_chunks(rows, size):
        t = rows.reshape(rows.shape[0] // size, size, DIFF_HEADS, dv).transpose(2, 0, 3, 1)
        return jnp.concatenate([t, jnp.ones(t.shape[:2] + (DIFF_ONES_ROWS, size), t.dtype)], axis=2)

    vt_lat = transposed_chunks(v[:lat_rows], tk)
    vt_ctx = transposed_chunks(v[lat_rows:], ctx_len)
    gcol = subln_g.reshape(dv, 1)
    lat_chunks = seq // tk
    ctx0 = n_batch * seq // ctx_len

    def call(tq, n_q, q_block0, batch_of, with_lat):
        in_specs = [pl.BlockSpec((dv, 1), lambda h, i, s: (0, 0)),
                    pl.BlockSpec((tq, LANES), lambda h, i, s: (q_block0 + i, h)),
                    pl.BlockSpec((ctx_len, LANES), lambda h, i, s: (ctx0 + batch_of(i), h)),
                    pl.BlockSpec((1, 1, dva, ctx_len), lambda h, i, s: (h, batch_of(i), 0, 0))]
        args = [lam, gcol, q, k, vt_ctx]
        if with_lat:
            in_specs += [pl.BlockSpec((seq, LANES), lambda h, i, s: (batch_of(i), h)),
                         pl.BlockSpec((1, lat_chunks, dva, tk), lambda h, i, s: (h, batch_of(i), 0, 0))]
            args += [k, vt_lat]
        grid_spec = pltpu.PrefetchScalarGridSpec(
            num_scalar_prefetch=1,
            grid=(DIFF_HEADS, n_q),
            in_specs=in_specs,
            out_specs=pl.BlockSpec((tq, LANES), lambda h, i, s: (i, h)),
            scratch_shapes=[pltpu.VMEM((1, 2 * tq), F32), pltpu.VMEM((dva, 2 * tq), F32),
                            pltpu.VMEM((tk, 2 * tq), F32), pltpu.VMEM((tk, 2 * tq), F32)],
        )
        return pl.pallas_call(
            functools.partial(_diff_kernel, n_lat_chunks=lat_chunks if with_lat else 0, coef=1.0 - lam_init),
            out_shape=jax.ShapeDtypeStruct((n_q * tq, DIFF_V), BF16),
            grid_spec=grid_spec,
            compiler_params=_cp(("parallel", "arbitrary")),
            name="diff_attn" if with_lat else "diff_attn_ctx",
        )(*args)

    tq = DIFF_TQ
    y_lat = call(tq, n_batch * seq // tq, 0, lambda i: i // (seq // tq), True)
    y_ctx = call(ctx_len, n_batch, ctx0, lambda i: i, False)
    return jnp.concatenate([y_lat, y_ctx], axis=0)


def _rwkv_prep_kernel(p_ref, hp_ref, hn_ref, mup_ref, mun_ref, kk_w_ref, ka_ref, rk_ref, w0_ref, a0_ref,
                      wup_ref, aup_ref, gup_ref, ones_ref,
                      r_ref, v_ref, kk_ref, ld_ref, ke_ref, bb_ref, g_ref, bonus_ref, sc,
                      *, tm, lat_rows, seq, ctx_len):
    i = pl.program_id(0)
    w = RWKV_WIDTH
    sc[0:8, :] = hp_ref[...]
    sc[8:8 + tm, :] = p_ref[...]
    sc[8 + tm:16 + tm, :] = hn_ref[...]
    p = p_ref[...]
    prev = sc[7:7 + tm, :]
    nxt = sc[9:9 + tm, :]
    row = i * tm + lax.broadcasted_iota(jnp.int32, (tm, 1), 0)
    pos = jnp.where(row < lat_rows, row % seq, (row - lat_rows) % ctx_len)
    seg_len = jnp.where(row < lat_rows, seq, ctx_len)
    prev = jnp.where(pos == 0, 0.0, prev)
    nxt = jnp.where(pos == seg_len - 1, 0.0, nxt)
    ps = p + mup_ref[...] * (prev - p) + mun_ref[...] * (nxt - p)

    r = ps[:, 0:w]
    k = ps[:, w:2 * w]
    v = ps[:, 2 * w:3 * w]
    wd = ps[:, 3 * w:3 * w + 2 * DECAY_LORA]
    ad = ps[:, 3 * w + 2 * DECAY_LORA:3 * w + 2 * DECAY_LORA + 2 * AAA_LORA]
    gd = ps[:, 3 * w + 2 * DECAY_LORA + 2 * AAA_LORA:]

    ones_bd = ones_ref[...]
    g = jnp.dot(_sigmoid(gd).astype(BF16), gup_ref[...], preferred_element_type=F32)
    kk = k * kk_w_ref[...]
    ss = _group_sum(kk * kk, ones_bd)
    kk = kk / jnp.maximum(jnp.sqrt(ss), 1e-12)
    w_raw = w0_ref[...] + jnp.dot(jnp.tanh(wd).astype(BF16), wup_ref[...], preferred_element_type=F32)
    a_raw = a0_ref[...] + jnp.dot(ad.astype(BF16), aup_ref[...], preferred_element_type=F32)
    ld = -math.exp(-0.5) * _sigmoid(w_raw)
    a = _sigmoid(a_raw)
    ka = ka_ref[...]
    ke_sum = jnp.zeros_like(k)
    for d in range(2):
        a_d = a[:, d * w:(d + 1) * w]
        ke = k * (1.0 + (a_d - 1.0) * ka)
        ld_ref[d] = ld[:, d * w:(d + 1) * w]
        ke_ref[d] = ke
        bb_ref[d] = kk * a_d
        ke_sum = ke_sum + ke
    rk = _group_sum(r * ke_sum * rk_ref[...], ones_bd)
    r_ref[...] = r
    v_ref[...] = v
    kk_ref[...] = kk
    g_ref[...] = g
    bonus_ref[...] = rk * v


def _rwkv_prep(p, lp, n_batch, seq, ctx_len):
    m, cols = p.shape
    tm = 256
    w = RWKV_WIDTH
    lat_rows = n_batch * seq
    row = lambda a: a.reshape(1, -1).astype(F32)

    def blockdiag(u):
        z = jnp.zeros_like(u[0])
        return jnp.concatenate([jnp.concatenate([u[0], z], axis=1), jnp.concatenate([z, u[1]], axis=1)], axis=0)

    head = jnp.arange(w) // HEAD_DIM
    ones_bd = (head[:, None] == head[None, :]).astype(BF16)
    full = lambda shape: pl.BlockSpec(shape, lambda i: (0,) * len(shape))
    nb8 = m // 8
    outs = pl.pallas_call(
        functools.partial(_rwkv_prep_kernel, tm=tm, lat_rows=lat_rows, seq=seq, ctx_len=ctx_len),
        out_shape=[jax.ShapeDtypeStruct((m, w), F32)] * 3
        + [jax.ShapeDtypeStruct((2, m, w), F32)] * 3
        + [jax.ShapeDtypeStruct((m, w), F32)] * 2,
        grid=(m // tm,),
        in_specs=[pl.BlockSpec((tm, cols), lambda i: (i, 0)),
                  pl.BlockSpec((8, cols), lambda i: (jnp.maximum(i * (tm // 8) - 1, 0), 0)),
                  pl.BlockSpec((8, cols), lambda i: (jnp.minimum((i + 1) * (tm // 8), nb8 - 1), 0)),
                  full((1, cols)), full((1, cols)), full((1, w)), full((1, w)), full((1, w)),
                  full((1, 2 * w)), full((1, 2 * w)),
                  full((2 * DECAY_LORA, 2 * w)), full((2 * AAA_LORA, 2 * w)), full((GATE_LORA, w)),
                  full((w, w))],
        out_specs=[pl.BlockSpec((tm, w), lambda i: (i, 0))] * 3
        + [pl.BlockSpec((2, tm, w), lambda i: (0, i, 0))] * 3
        + [pl.BlockSpec((tm, w), lambda i: (i, 0))] * 2,
        scratch_shapes=[pltpu.VMEM((tm + 16, cols), F32)],
        compiler_params=_cp(("parallel",)),
        name="rwkv_prep",
    )(p, p, p, row(lp['rwkv_mu_prev']), row(lp['rwkv_mu_next']), row(lp['rwkv_k_k']), row(lp['rwkv_k_a']),
      row(lp['rwkv_r_k']), row(lp['rwkv_w0']), row(lp['rwkv_a0']),
      blockdiag(lp['rwkv_w_up']).astype(BF16), blockdiag(lp['rwkv_a_up']).astype(BF16),
      lp['rwkv_g_up'].astype(BF16), ones_bd)
    return outs


def _rwkv_chunk_prep(d, off, r_ref, v_ref, kk_ref, ld_ref, ke_ref, bb_ref, incl):
    c = RWKV_CHUNK
    ld = ld_ref[0, pl.ds(off, c), :]
    r = r_ref[pl.ds(off, c), :]
    v = v_ref[pl.ds(off, c), :]
    kk = kk_ref[pl.ds(off, c), :]
    ke = ke_ref[0, pl.ds(off, c), :]
    bb = bb_ref[0, pl.ds(off, c), :]
    cum = jnp.dot(incl, ld, precision=HI, preferred_element_type=F32)
    tot = jnp.sum(ld, axis=0, keepdims=True)
    rt = r * jnp.exp(cum)
    einv = jnp.exp(-cum)
    etail = jnp.exp(tot - cum)
    return dict(at=(-kk * jnp.exp(cum - ld)).astype(BF16), rt=rt, rt_b=rt.astype(BF16),
                bt=(bb * einv).astype(BF16), kt=(ke * einv).astype(BF16),
                bh=(bb * etail).astype(BF16), kh=(ke * etail).astype(BF16),
                v=v, v_b=v.astype(BF16), wtot=jnp.exp(tot))


def _rwkv_chunk_group(offs, in_refs, y_refs, s_sc):
    c = RWKV_CHUNK
    hd = HEAD_DIM
    n_group = len(offs[0])
    ti = lax.broadcasted_iota(jnp.int32, (c, c), 0)
    tj = lax.broadcasted_iota(jnp.int32, (c, c), 1)
    eye = (ti == tj).astype(F32)
    incl = [(tj <= ti).astype(F32), (tj >= ti).astype(F32)]
    strict = [m - eye for m in incl]
    diag_mask = ((ti // RWKV_INV_BASE) == (tj // RWKV_INV_BASE)).astype(F32)
    off_masks = []
    sz = RWKV_INV_BASE
    while sz < c:
        off_masks.append((((ti // (2 * sz)) == (tj // (2 * sz))) & ((ti // sz) != (tj // sz))).astype(F32))
        sz *= 2

    pre = {(d, g): _rwkv_chunk_prep(d, offs[d][g], *in_refs[d], incl[d])
           for d in range(2) for g in range(n_group)}
    lanes = [(d, g, h) for g in range(n_group) for d in range(2) for h in range(RWKV_HEADS)]
    sl = lambda h: slice(h * hd, (h + 1) * hd)
    get = lambda name: [pre[d, g][name][:, sl(h)] for d, g, h in lanes]
    at, rt, rt_b, bt, kt, bh, kh, v, v_b = (get(n) for n in ('at', 'rt', 'rt_b', 'bt', 'kt', 'bh', 'kh', 'v', 'v_b'))
    nl = range(len(lanes))
    nt = (((1,), (1,)), ((), ()))
    bdot = lambda x, y: jnp.dot(x.astype(BF16), y.astype(BF16), preferred_element_type=F32)

    gm = [lax.dot_general(jnp.concatenate([at[i], rt_b[i]], axis=0), jnp.concatenate([bt[i], kt[i]], axis=0), nt,
                          preferred_element_type=F32) for i in nl]
    zz0 = [bdot(v[i].T, kh[i]) for i in nl]
    aab = [gm[i][:c, :c] * strict[lanes[i][0]] for i in nl]
    aak = [gm[i][:c, c:] * strict[lanes[i][0]] for i in nl]
    arb = [gm[i][c:, :c] * incl[lanes[i][0]] for i in nl]
    ark = [gm[i][c:, c:] * incl[lanes[i][0]] for i in nl]
    av = [bdot(jnp.concatenate([aak[i], ark[i]], axis=0), v_b[i]) for i in nl]
    pw = [aab[i] * diag_mask for i in nl]
    tm_ = [eye + pw[i] for i in nl]
    for _ in range(int(math.log2(RWKV_INV_BASE)) - 1):
        pw = [bdot(pw[i], pw[i]) for i in nl]
        tm_ = [tm_[i] + bdot(tm_[i], pw[i]) for i in nl]
    for off_mask in off_masks:
        tn = [bdot(tm_[i], aab[i] * off_mask) for i in nl]
        tm_ = [tm_[i] + bdot(tn[i], tm_[i]) for i in nl]
    au = [bdot(tm_[i], jnp.concatenate([at[i], av[i][:c].astype(BF16)], axis=1)) for i in nl]
    ry = [bdot(arb[i], au[i]) for i in nl]
    mz = [bdot(au[i].T, bh[i]) for i in nl]
    rbar = [(rt[i] + ry[i][:, :hd]).astype(BF16) for i in nl]
    ybar = [ry[i][:, hd:] + av[i][c:] for i in nl]
    mm = [(eye * pre[lanes[i][0], lanes[i][1]]['wtot'][:, sl(lanes[i][2])] + mz[i][:hd]).astype(BF16) for i in nl]
    zz = [mz[i][hd:] + zz0[i] for i in nl]
    state = {(d, h): s_sc[d, h] for d in range(2) for h in range(RWKV_HEADS)}
    ys = {}
    for g in range(n_group):
        idx = [i for i in nl if lanes[i][1] == g]
        s_b = {i: state[lanes[i][0], lanes[i][2]].astype(BF16) for i in idx}
        for i in idx:
            ys[i] = lax.dot_general(rbar[i], s_b[i], nt, preferred_element_type=F32) + ybar[i]
        for i in idx:
            state[lanes[i][0], lanes[i][2]] = jnp.dot(s_b[i], mm[i], preferred_element_type=F32) + zz[i]
    for (d, h), val in state.items():
        s_sc[d, h] = val
    for d in range(2):
        for g in range(n_group):
            y_refs[d][0, pl.ds(offs[d][g], c), :] = jnp.concatenate(
                [ys[i] for i in nl if lanes[i][0] == d and lanes[i][1] == g], axis=1)


def _rwkv_scan_kernel(rf, vf, kf, ldf, kef, bbf, rb, vb, kb, ldb, keb, bbb, yf, yb, s_sc, *, n_chunks):
    @pl.when(pl.program_id(1) == 0)
    def _():
        s_sc[...] = jnp.zeros(s_sc.shape, F32)

    group = RWKV_GROUP

    def body(t, carry):
        off_f = [pl.multiple_of((t * group + g) * RWKV_CHUNK, RWKV_CHUNK) for g in range(group)]
        off_b = [pl.multiple_of((n_chunks - 1 - t * group - g) * RWKV_CHUNK, RWKV_CHUNK) for g in range(group)]
        _rwkv_chunk_group((off_f, off_b), ((rf, vf, kf, ldf, kef, bbf), (rb, vb, kb, ldb, keb, bbb)),
                          (yf, yb), s_sc)
        return carry

    lax.fori_loop(0, n_chunks // group, body, 0)


def _rwkv_scan(r, v, kk, ld, ke, bb, n_batch, seq, ctx_len):
    m, w = r.shape
    ts = RWKV_STEP_ROWS
    assert ctx_len == ts
    lpb = seq // ts
    ctx0 = n_batch * seq // ts
    nj = 1 + lpb

    def fwd(b, j):
        return jnp.where(j == 0, ctx0 + b, b * lpb + j - 1)

    def bwd(b, j):
        return jnp.where(j == 0, ctx0 + b, b * lpb + lpb - j)

    shared = lambda f: pl.BlockSpec((ts, w), lambda b, j: (f(b, j), 0))
    per_dir = lambda f, d: pl.BlockSpec((1, ts, w), lambda b, j: (d, f(b, j), 0))
    y = pl.pallas_call(
        functools.partial(_rwkv_scan_kernel, n_chunks=ts // RWKV_CHUNK),
        out_shape=[jax.ShapeDtypeStruct((1, m, w), F32)] * 2,
        grid=(n_batch, nj),
        in_specs=[shared(fwd), shared(fwd), shared(fwd), per_dir(fwd, 0), per_dir(fwd, 0), per_dir(fwd, 0),
                  shared(bwd), shared(bwd), shared(bwd), per_dir(bwd, 1), per_dir(bwd, 1), per_dir(bwd, 1)],
        out_specs=[pl.BlockSpec((1, ts, w), lambda b, j: (0, fwd(b, j), 0)),
                   pl.BlockSpec((1, ts, w), lambda b, j: (0, bwd(b, j), 0))],
        scratch_shapes=[pltpu.VMEM((2, RWKV_HEADS, HEAD_DIM, HEAD_DIM), F32)],
        compiler_params=_cp(("parallel", "arbitrary")),
        name="rwkv_scan",
    )(r, v, kk, ld, ke, bb, r, v, kk, ld, ke, bb)
    return y


def _rwkv_finish(y, bonus, g, lnx_g, lnx_b, ones_bd):
    inv = 1.0 / HEAD_DIM
    mu = _group_sum(y, ones_bd) * inv
    yc = y - mu
    var = _group_sum(yc * yc, ones_bd) * inv
    yn = yc * lax.rsqrt(var + RWKV_GN_EPS) * lnx_g + lnx_b
    return (yn + bonus) * g


def _top2(logits):
    lane = lax.broadcasted_iota(jnp.int32, logits.shape, 1)
    logits = jnp.where(lane < N_EXPERTS, logits, -jnp.inf)
    m1 = jnp.max(logits, axis=-1, keepdims=True)
    i1 = jnp.min(jnp.where(logits == m1, lane, LANES), axis=-1, keepdims=True)
    rest = jnp.where(lane == i1, -jnp.inf, logits)
    m2 = jnp.max(rest, axis=-1, keepdims=True)
    i2 = jnp.min(jnp.where(rest == m2, lane, LANES), axis=-1, keepdims=True)
    e = jnp.exp(m2 - m1)
    w1 = 1.0 / (1.0 + e)
    w2 = e / (1.0 + e)
    idx = jnp.where(lane == 0, i1, jnp.where(lane == 1, i2, 0))
    wts = jnp.where(lane == 0, w1, jnp.where(lane == 1, w2, 0.0))
    return idx, wts


def _merge_kernel(*refs, d, with_router):
    (ya_ref, yf_ref, yb_ref, bonus_ref, g_ref, lg_ref, lb_ref, ones_ref, yc_ref, gt_ref, h_ref,
     pa_ref, pb_ref, pc_ref, wo_ref, gpost_ref, gpre_ref, mod_ref) = refs[:18]
    if with_router:
        rwh_ref, rwl_ref, h_out, f_out, idx_out, wt_out = refs[18:]
    else:
        h_out, f_out = refs[18:]
    yb = _rwkv_finish(yf_ref[0] + yb_ref[0], bonus_ref[...], g_ref[...], lg_ref[...], lb_ref[...], ones_ref[...])
    gates = gt_ref[...].astype(F32)
    merged = (_sigmoid(gates[:, 0:d]) * jnp.dot(ya_ref[...], pa_ref[...], preferred_element_type=F32)
              + _sigmoid(gates[:, d:2 * d]) * jnp.dot(yb.astype(BF16), pb_ref[...], preferred_element_type=F32)
              + _sigmoid(gates[:, 2 * d:3 * d]) * jnp.dot(yc_ref[...], pc_ref[...], preferred_element_type=F32))
    out = jnp.dot(merged.astype(BF16), wo_ref[...], preferred_element_type=F32)
    mod = lambda idx: mod_ref[0, :, idx * d:(idx + 1) * d]
    hn = h_ref[...] + mod(2) * (_rms(out, NORM_EPS) * gpost_ref[...])
    h_out[...] = hn
    f = (_rms(hn, NORM_EPS) * gpre_ref[...]) * (1.0 + mod(4)) + mod(3)
    f_out[...] = f.astype(f_out.dtype)
    if with_router:
        rows = f.shape[0]
        hi = f.astype(BF16)
        mid = (f - hi.astype(F32)).astype(BF16)
        part = jnp.dot(jnp.concatenate([hi, mid], axis=0), rwh_ref[...], preferred_element_type=F32)
        logits = part[:rows] + part[rows:] + jnp.dot(hi, rwl_ref[...], preferred_element_type=F32)
        idx, wts = _top2(logits)
        idx_out[...] = idx
        wt_out[...] = wts


def _merge(ya, yf, yb, bonus, g, lnx_g, lnx_b, yc, gates, h, pa, pb, pc, wo, g_post, g_pre, mods, f_dtype,
           lat_bpb, n_batch, router_w=None):
    m, d = h.shape
    w = bonus.shape[1]
    tm = 256
    bpb = lat_bpb * (ROW_TILE // tm)
    seg = functools.partial(_seg_of_block, lat_blocks_per_batch=bpb, n_batch=n_batch)
    rows = lambda width: pl.BlockSpec((tm, width), lambda i: (i, 0))
    rows3 = pl.BlockSpec((1, tm, w), lambda i: (0, i, 0))
    full = lambda a: pl.BlockSpec(a.shape, lambda i: (0, 0))
    head = jnp.arange(w) // HEAD_DIM
    ones_bd = (head[:, None] == head[None, :]).astype(BF16)
    args = [ya, yf, yb, bonus, g, lnx_g.reshape(1, w), lnx_b.reshape(1, w), ones_bd, yc, gates, h,
            pa, pb, pc, wo, g_post, g_pre, mods]
    in_specs = [rows(ya.shape[1]), rows3, rows3, rows(w), rows(w), full(args[5]), full(args[6]), full(ones_bd),
                rows(yc.shape[1]), rows(gates.shape[1]), rows(d), full(pa), full(pb), full(pc), full(wo),
                full(g_post), full(g_pre), pl.BlockSpec((1, 1, mods.shape[2]), lambda i: (seg(i), 0, 0))]
    out_shape = [jax.ShapeDtypeStruct((m, d), F32), jax.ShapeDtypeStruct((m, d), f_dtype)]
    out_specs = [rows(d), rows(d)]
    if router_w is not None:
        w_pad = jnp.zeros((d, LANES), F32).at[:, :N_EXPERTS].set(router_w)
        w_hi = w_pad.astype(BF16)
        w_lo = (w_pad - w_hi.astype(F32)).astype(BF16)
        args += [w_hi, w_lo]
        in_specs += [full(w_hi), full(w_lo)]
        out_shape += [jax.ShapeDtypeStruct((m, LANES), jnp.int32), jax.ShapeDtypeStruct((m, LANES), F32)]
        out_specs += [rows(LANES), rows(LANES)]
    return pl.pallas_call(
        functools.partial(_merge_kernel, d=d, with_router=router_w is not None),
        out_shape=out_shape,
        grid=(m // tm,),
        in_specs=in_specs,
        out_specs=out_specs,
        compiler_params=_cp(("parallel",)),
        name="merge",
    )(*args)


def _swiglu_hidden(x, wg, wu):
    hg = jnp.dot(x, wg, preferred_element_type=F32)
    hu = jnp.dot(x, wu, preferred_element_type=F32)
    return (hg * _sigmoid(hg) * hu).astype(BF16)


def _ffn_kernel(f_ref, wg_ref, wu_ref, wd_ref, h_ref, gpost_ref, mod_ref, o_ref, acc, *, d):
    j = pl.program_id(1)

    @pl.when(j == 0)
    def _():
        acc[...] = jnp.zeros(acc.shape, F32)

    hid = _swiglu_hidden(f_ref[...], wg_ref[...], wu_ref[...])
    acc[...] += jnp.dot(hid, wd_ref[...], preferred_element_type=F32)

    @pl.when(j == pl.num_programs(1) - 1)
    def _():
        gate = mod_ref[0, :, 5 * d:6 * d]
        o_ref[...] = h_ref[...] + gate * (_rms(acc[...], NORM_EPS) * gpost_ref[...])


def _ffn(f, wg, wu, wd, h, g_post, mods, lat_bpb, n_batch):
    m, d = h.shape
    ff = wg.shape[1]
    tm, tf = ROW_TILE, FFN_TF
    seg = functools.partial(_seg_of_block, lat_blocks_per_batch=lat_bpb, n_batch=n_batch)
    return pl.pallas_call(
        functools.partial(_ffn_kernel, d=d),
        out_shape=jax.ShapeDtypeStruct((m, d), F32),
        grid=(m // tm, ff // tf),
        in_specs=[pl.BlockSpec((tm, d), lambda i, j: (i, 0)),
                  pl.BlockSpec((d, tf), lambda i, j: (0, j)),
                  pl.BlockSpec((d, tf), lambda i, j: (0, j)),
                  pl.BlockSpec((tf, d), lambda i, j: (j, 0)),
                  pl.BlockSpec((tm, d), lambda i, j: (i, 0)),
                  pl.BlockSpec((1, d), lambda i, j: (0, 0)),
                  pl.BlockSpec((1, 1, mods.shape[2]), lambda i, j: (seg(i), 0, 0))],
        out_specs=pl.BlockSpec((tm, d), lambda i, j: (i, 0)),
        scratch_shapes=[pltpu.VMEM((tm, d), F32)],
        compiler_params=_cp(("parallel", "arbitrary")),
        name="ffn_dense",
    )(f, wg, wu, wd, h, g_post, mods)


def _moe_gather_copy(f_hbm, xbuf, sem, slot, src_row, dst_row):
    return pltpu.make_async_copy(f_hbm.at[pl.ds(src_row, 1), :], xbuf.at[slot, pl.ds(dst_row, 1), :],
                                 sem.at[slot])


def _moe_ffn_kernel(blk_e_ref, nused_ref, tok_ref, f_hbm, wg_ref, wu_ref, wd_ref, y_ref, xbuf, xb, acc, sem,
                    *, tm, nj):
    i = pl.program_id(0)
    j = pl.program_id(1)
    nused = nused_ref[0]
    active = i < nused
    slot = i % 2
    per_step = tm // nj

    @pl.when((i == 0) & (j == 0))
    def _():
        def issue(r, carry):
            _moe_gather_copy(f_hbm, xbuf, sem, 0, tok_ref[r], r).start()
            return carry

        lax.fori_loop(0, tm, issue, 0)

    @pl.when((i <= nused) & (j == 0))
    def _():
        pltpu.make_async_copy(f_hbm.at[pl.ds(0, tm), :], xbuf.at[slot], sem.at[slot]).wait()

    @pl.when(active & (j == 0))
    def _():
        xb[...] = xbuf[slot].astype(BF16)
        acc[...] = jnp.zeros(acc.shape, F32)

    @pl.when(active)
    def _():
        base = (i + 1) * tm + j * per_step
        for r in range(per_step):
            _moe_gather_copy(f_hbm, xbuf, sem, 1 - slot, tok_ref[base + r], j * per_step + r).start()
        hid = _swiglu_hidden(xb[...], wg_ref[0], wu_ref[0])
        acc[...] += jnp.dot(hid, wd_ref[0], preferred_element_type=F32)

    @pl.when(j == nj - 1)
    def _():
        y_ref[...] = acc[...]


def _moe_ffn(f, blk_e, nused, tok, wg, wu, wd, n_blocks):
    d = f.shape[1]
    ff = wg.shape[2]
    tm, tf = MOE_TILE, MOE_TF

    def e_of(i, be, nu):
        return be[jnp.minimum(i, nu[0] - 1)]

    grid_spec = pltpu.PrefetchScalarGridSpec(
        num_scalar_prefetch=3,
        grid=(n_blocks, ff // tf),
        in_specs=[pl.BlockSpec(memory_space=pl.ANY),
                  pl.BlockSpec((1, d, tf), lambda i, j, be, nu, tk: (e_of(i, be, nu), 0, j)),
                  pl.BlockSpec((1, d, tf), lambda i, j, be, nu, tk: (e_of(i, be, nu), 0, j)),
                  pl.BlockSpec((1, tf, d), lambda i, j, be, nu, tk: (e_of(i, be, nu), j, 0))],
        out_specs=pl.BlockSpec((tm, d), lambda i, j, be, nu, tk: (i, 0)),
        scratch_shapes=[pltpu.VMEM((2, tm, d), F32), pltpu.VMEM((tm, d), BF16), pltpu.VMEM((tm, d), F32),
                        pltpu.SemaphoreType.DMA((2,))],
    )
    return pl.pallas_call(
        functools.partial(_moe_ffn_kernel, tm=tm, nj=ff // tf),
        out_shape=jax.ShapeDtypeStruct((n_blocks * tm, d), F32),
        grid_spec=grid_spec,
        compiler_params=_cp(("arbitrary", "arbitrary")),
        name="moe_ffn",
    )(blk_e, nused, tok, f, wg, wu, wd)


def _moe_combine_kernel(p0_ref, p1_ref, y_hbm, wt_ref, h_ref, gpost_ref, mod_ref, o_ref, b0, b1, sem, *, tm, d):
    i = pl.program_id(0)

    def issue(r, carry):
        pltpu.make_async_copy(y_hbm.at[pl.ds(p0_ref[i * tm + r], 1), :], b0.at[pl.ds(r, 1), :], sem.at[0]).start()
        pltpu.make_async_copy(y_hbm.at[pl.ds(p1_ref[i * tm + r], 1), :], b1.at[pl.ds(r, 1), :], sem.at[1]).start()
        return carry

    lax.fori_loop(0, tm, issue, 0)
    pltpu.make_async_copy(y_hbm.at[pl.ds(0, tm), :], b0, sem.at[0]).wait()
    pltpu.make_async_copy(y_hbm.at[pl.ds(0, tm), :], b1, sem.at[1]).wait()
    wt = wt_ref[...]
    y = b0[...] * wt[:, 0:1] + b1[...] * wt[:, 1:2]
    gate = mod_ref[0, :, 5 * d:6 * d]
    o_ref[...] = h_ref[...] + gate * (_rms(y, NORM_EPS) * gpost_ref[...])


def _moe_combine(pos0, pos1, y, wt, h, g_post, mods, n_rows, rows_per_batch, n_batch):
    d = h.shape[1]
    tm = COMBINE_TILE
    seg = functools.partial(_seg_of_block, lat_blocks_per_batch=rows_per_batch // tm, n_batch=n_batch)
    grid_spec = pltpu.PrefetchScalarGridSpec(
        num_scalar_prefetch=2,
        grid=(n_rows // tm,),
        in_specs=[pl.BlockSpec(memory_space=pl.ANY),
                  pl.BlockSpec((tm, LANES), lambda i, a, b: (i, 0)),
                  pl.BlockSpec((tm, d), lambda i, a, b: (i, 0)),
                  pl.BlockSpec((1, d), lambda i, a, b: (0, 0)),
                  pl.BlockSpec((1, 1, mods.shape[2]), lambda i, a, b: (seg(i), 0, 0))],
        out_specs=pl.BlockSpec((tm, d), lambda i, a, b: (i, 0)),
        scratch_shapes=[pltpu.VMEM((tm, d), F32), pltpu.VMEM((tm, d), F32), pltpu.SemaphoreType.DMA((2,))],
    )
    return pl.pallas_call(
        functools.partial(_moe_combine_kernel, tm=tm, d=d),
        out_shape=jax.ShapeDtypeStruct((n_rows, d), F32),
        grid_spec=grid_spec,
        compiler_params=_cp(("arbitrary",)),
        name="moe_combine",
    )(pos0, pos1, y, wt, h, g_post, mods)


def _moe_slots(top_i, tile):
    n = top_i.shape[0]
    a = n * 2
    e_flat = top_i.reshape(a)
    onehot = (e_flat[:, None] == jnp.arange(N_EXPERTS, dtype=jnp.int32)[None, :]).astype(jnp.int32)
    sub = LANES
    blocks = onehot.reshape(a // sub, sub, N_EXPERTS).astype(F32)
    tri = (jnp.arange(sub)[:, None] >= jnp.arange(sub)[None, :]).astype(F32)
    within = jnp.einsum('ij,bjk->bik', tri, blocks).astype(jnp.int32)
    totals = within[:, -1, :]
    csum = (within + (jnp.cumsum(totals, axis=0) - totals)[:, None, :]).reshape(a, N_EXPERTS)
    rank = jnp.sum(csum * onehot, axis=1) - 1
    counts = csum[-1]
    padded = (counts + tile - 1) // tile * tile
    pends = jnp.cumsum(padded)
    pstarts = pends - padded
    dest = (jnp.sum(onehot * pstarts[None, :], axis=1) + rank).astype(jnp.int32)
    n_blocks = a // tile + N_EXPERTS
    tok = jnp.zeros((n_blocks * tile,), jnp.int32).at[dest].set(jnp.arange(a, dtype=jnp.int32) // 2)
    block_start = jnp.arange(n_blocks, dtype=jnp.int32) * tile
    blk_e = jnp.minimum(jnp.sum((block_start[:, None] >= pends[None, :]).astype(jnp.int32), axis=1),
                        N_EXPERTS - 1)
    nused = (pends[-1:] // tile).astype(jnp.int32)
    return tok, blk_e, nused, dest.reshape(n, 2), n_blocks


def _swa_head_order(w, axis):
    group = SWA_HEADS // SWA_KV_HEADS
    order = [g * group + t for t in range(group) for g in range(SWA_KV_HEADS)]
    shape = w.shape
    w = w.reshape(shape[:axis] + (SWA_HEADS, HEAD_DIM) + shape[axis + 1:])
    return jnp.take(w, jnp.array(order), axis=axis).reshape(shape)


def _rope_tables(n_batch, seq, ctx_len):
    t = jnp.arange(seq, dtype=jnp.int32)
    row = (t // GRID_W).astype(F32)
    col = (t % GRID_W).astype(F32)
    axis_dim = HEAD_DIM // 2
    inv_freq = ROPE_THETA ** (-jnp.arange(0, axis_dim, 2, dtype=F32) / axis_dim)
    dd = jnp.arange(LANES) % HEAD_DIM
    pos = jnp.where((dd // axis_dim)[None, :] == 0, row[:, None], col[:, None])
    ang = pos * inv_freq[dd % (axis_dim // 2)][None, :]
    cos = jnp.cos(ang)
    sin = jnp.where(((dd % axis_dim) < axis_dim // 2)[None, :], -jnp.sin(ang), jnp.sin(ang))
    n_ctx = n_batch * ctx_len
    cos = jnp.concatenate([jnp.tile(cos, (n_batch, 1)), jnp.ones((n_ctx, LANES), F32)], axis=0)
    sin = jnp.concatenate([jnp.tile(sin, (n_batch, 1)), jnp.zeros((n_ctx, LANES), F32)], axis=0)
    return cos, sin


def kernel(x, c, ctx, c_ctx, ada_w, ada_b, pre_mix_g, post_mix_g, pre_ffn_g, post_ffn_g, w_in, swa_sink,
           rwkv_mu_prev, rwkv_mu_next, rwkv_w0, rwkv_w_up, rwkv_a0, rwkv_a_up, rwkv_g_up, rwkv_k_k, rwkv_k_a,
           rwkv_r_k, rwkv_lnx_g, rwkv_lnx_b, diff_lambda, diff_subln_g, proj_swa, proj_rwkv, proj_diff, w_out,
           ffn_w_gate, ffn_w_up, ffn_w_down, router_w, moe_w_gate, moe_w_up, moe_w_down):
    n_batch, seq, d = x.shape
    ctx_len = ctx.shape[1]
    depth = w_in.shape[0]
    lat_rows = n_batch * seq
    lat_bpb = seq // ROW_TILE
    assert seq % ROW_TILE == 0 and (n_batch * ctx_len) % ROW_TILE == 0

    h = jnp.concatenate([x.reshape(lat_rows, d), ctx.reshape(n_batch * ctx_len, d)], axis=0)
    m = h.shape[0]
    cond = jnp.zeros((8, d), F32).at[:n_batch].set(c).at[n_batch].set(c_ctx)
    rope = _rope_tables(n_batch, seq, ctx_len)
    row = lambda a: a.reshape(1, -1)

    o_swa = 0
    o_rwkv = o_swa + SWA_Q + 2 * SWA_KV
    o_diff = o_rwkv + RWKV_COLS
    o_gate = o_diff + 2 * DIFF_QK + DIFF_V
    o_end = o_gate + 3 * d

    for layer in range(depth):
        mods = _ada(cond, ada_w[layer], ada_b[layer].reshape(1, -1))[:, None, :]
        wl = w_in[layer].astype(BF16)
        a = _prenorm(h, row(pre_mix_g[layer]), mods, 0, 1, lat_bpb, n_batch)

        w_swa = jnp.concatenate([_swa_head_order(wl[:, o_swa:o_swa + SWA_Q], axis=1),
                                 wl[:, o_swa + SWA_Q:o_rwkv]], axis=1)
        q_swa, kv_swa = _proj(a, w_swa, (SWA_Q, 2 * SWA_KV), (BF16, BF16), rope=rope,
                              rope_cols=SWA_Q + SWA_KV, scale_cols=SWA_Q, q_scale=HEAD_DIM ** -0.5)
        (p_rwkv,) = _proj(a, wl[:, o_rwkv:o_diff], (RWKV_COLS,), (F32,))
        q_diff, k_diff, v_diff = _proj(a, wl[:, o_diff:o_gate], (DIFF_QK, DIFF_QK, DIFF_V), (BF16,) * 3,
                                       rope=rope, rope_cols=2 * DIFF_QK, scale_cols=DIFF_QK,
                                       q_scale=HEAD_DIM ** -0.5 * math.log2(math.e))
        (gates,) = _proj(a, wl[:, o_gate:o_end], (d,), (BF16,), tn=d)

        ya = _swa(q_swa, kv_swa, swa_sink[layer].astype(F32), n_batch, seq, ctx_len)

        lp = {'rwkv_mu_prev': rwkv_mu_prev[layer], 'rwkv_mu_next': rwkv_mu_next[layer],
              'rwkv_w0': rwkv_w0[layer], 'rwkv_w_up': rwkv_w_up[layer], 'rwkv_a0': rwkv_a0[layer],
              'rwkv_a_up': rwkv_a_up[layer], 'rwkv_g_up': rwkv_g_up[layer], 'rwkv_k_k': rwkv_k_k[layer],
              'rwkv_k_a': rwkv_k_a[layer], 'rwkv_r_k': rwkv_r_k[layer]}
        r_, v_, kk_, ld_, ke_, bb_, g_, bonus_ = _rwkv_prep(p_rwkv, lp, n_batch, seq, ctx_len)
        y_f, y_b = _rwkv_scan(r_, v_, kk_, ld_, ke_, bb_, n_batch, seq, ctx_len)

        lam_vec = diff_lambda[layer].astype(F32)
        lam_init = 0.8 - 0.6 * math.exp(-0.3 * layer)
        lam = (jnp.exp(jnp.sum(lam_vec[0] * lam_vec[1])) - jnp.exp(jnp.sum(lam_vec[2] * lam_vec[3]))
               + lam_init).reshape(1)
        yc = _diff(q_diff, k_diff, v_diff, lam, row(diff_subln_g[layer]), lam_init, n_batch, seq, ctx_len)

        moe_layer = layer % 2 == 1
        jj = layer // 2
        merged = _merge(ya, y_f, y_b, bonus_, g_, rwkv_lnx_g[layer], rwkv_lnx_b[layer], yc, gates, h,
                        _swa_head_order(proj_swa[layer], axis=0).astype(BF16), proj_rwkv[layer].astype(BF16),
                        proj_diff[layer].astype(BF16), w_out[layer].astype(BF16), row(post_mix_g[layer]),
                        row(pre_ffn_g[layer]), mods, F32 if moe_layer else BF16, lat_bpb, n_batch,
                        router_w=router_w[jj] if moe_layer else None)
        h, f = merged[:2]
        need_ctx = layer < depth - 1
        if not moe_layer:
            h = _ffn(f, ffn_w_gate[jj].astype(BF16), ffn_w_up[jj].astype(BF16), ffn_w_down[jj].astype(BF16),
                     h, row(post_ffn_g[layer]), mods, lat_bpb, n_batch)
        else:
            n_tok = m if need_ctx else lat_rows
            top_i, top_w = merged[2][:n_tok], merged[3][:n_tok]
            tok, blk_e, nused, dest, n_blocks = _moe_slots(top_i[:, :2], MOE_TILE)
            y = _moe_ffn(f, blk_e, nused, tok, moe_w_gate[jj].astype(BF16), moe_w_up[jj].astype(BF16),
                         moe_w_down[jj].astype(BF16), n_blocks)
            h = _moe_combine(dest[:, 0], dest[:, 1], y, top_w, h, row(post_ffn_g[layer]), mods, n_tok, seq,
                             n_batch)
    return h[:lat_rows].reshape(n_batch, seq, d)
```

```python
import functools
import math

import jax
import jax.numpy as jnp
from jax import lax
from jax.experimental import pallas as pl
from jax.experimental.pallas import tpu as pltpu

F32 = jnp.float32
BF16 = jnp.bfloat16
HI = lax.Precision.HIGHEST

HEAD_DIM = 64
GRID_W = 64
ROPE_THETA = 10000.0
NORM_EPS = 1e-6
NEG_INF = -1e30
SWA_HEADS = 8
SWA_KV_HEADS = 2
SWA_BLOCK = 128
RWKV_HEADS = 8
RWKV_WIDTH = RWKV_HEADS * HEAD_DIM
DECAY_LORA = 64
AAA_LORA = 64
GATE_LORA = 128
RWKV_GN_EPS = 64e-5
DIFF_HEADS = 4
DIFF_V_DIM = 2 * HEAD_DIM
DIFF_SUBLN_EPS = 1e-5
N_EXPERTS = 8
SWA_Q = SWA_HEADS * HEAD_DIM
SWA_KV = SWA_KV_HEADS * HEAD_DIM
assert SWA_KV_HEADS == 2 and SWA_KV == 128
RWKV_COLS = 3 * RWKV_WIDTH + 2 * DECAY_LORA + 2 * AAA_LORA + GATE_LORA
DIFF_QK = DIFF_HEADS * 2 * HEAD_DIM
DIFF_V = DIFF_HEADS * DIFF_V_DIM

LANES = 128
VMEM_LIMIT = 48 * 1024 * 1024
ROW_TILE = 512
RWKV_CHUNK = 64
RWKV_INV_BASE = 8
RWKV_GROUP = 2
RWKV_STEP_ROWS = 256
DIFF_TQ = 512
DIFF_TK = 1024
DIFF_UNROLL = 2
DIFF_ONES_ROWS = 16
MOE_TILE = 512
MOE_TF = 1792
FFN_TF = 1408
COMBINE_TILE = 256


def _cp(sem, **kw):
    return pltpu.CompilerParams(dimension_semantics=sem, vmem_limit_bytes=VMEM_LIMIT, **kw)


def _seg_of_block(i, lat_blocks_per_batch, n_batch):
    return jnp.minimum(i // lat_blocks_per_batch, n_batch)


def _rms(x, eps):
    return x * lax.rsqrt(jnp.mean(x * x, axis=-1, keepdims=True) + eps)


def _sigmoid(x):
    return 1.0 / (1.0 + jnp.exp(-x))


def _group_sum(x, ones_bd):
    hi = x.astype(BF16)
    mid = (x - hi.astype(F32)).astype(BF16)
    rows = x.shape[0]
    parts = jnp.dot(jnp.concatenate([hi, mid], axis=0), ones_bd, preferred_element_type=F32)
    return parts[:rows] + parts[rows:]


def _ada_kernel(x_ref, w_ref, b_ref, o_ref):
    x = x_ref[...]
    s = x * _sigmoid(x)
    o_ref[...] = jnp.dot(s, w_ref[...], precision=HI, preferred_element_type=F32) + b_ref[...]


def _ada(cond, w, b):
    rows, d = cond.shape
    n = w.shape[1]
    return pl.pallas_call(
        _ada_kernel,
        out_shape=jax.ShapeDtypeStruct((rows, n), F32),
        grid=(n // d,),
        in_specs=[pl.BlockSpec((rows, d), lambda j: (0, 0)),
                  pl.BlockSpec((d, d), lambda j: (0, j)),
                  pl.BlockSpec((1, d), lambda j: (0, j))],
        out_specs=pl.BlockSpec((rows, d), lambda j: (0, j)),
        compiler_params=_cp(("parallel",)),
        name="ada_mod",
    )(cond, w, b)


def _prenorm_kernel(h_ref, g_ref, mod_ref, o_ref, *, d, shift_idx, scale_idx):
    y = _rms(h_ref[...], NORM_EPS) * g_ref[...]
    shift = mod_ref[0, :, shift_idx * d:(shift_idx + 1) * d]
    scale = mod_ref[0, :, scale_idx * d:(scale_idx + 1) * d]
    o_ref[...] = (y * (1.0 + scale) + shift).astype(o_ref.dtype)


def _prenorm(h, g, mods, shift_idx, scale_idx, lat_bpb, n_batch):
    m, d = h.shape
    tm = ROW_TILE
    seg = functools.partial(_seg_of_block, lat_blocks_per_batch=lat_bpb, n_batch=n_batch)
    return pl.pallas_call(
        functools.partial(_prenorm_kernel, d=d, shift_idx=shift_idx, scale_idx=scale_idx),
        out_shape=jax.ShapeDtypeStruct((m, d), BF16),
        grid=(m // tm,),
        in_specs=[pl.BlockSpec((tm, d), lambda i: (i, 0)),
                  pl.BlockSpec((1, d), lambda i: (0, 0)),
                  pl.BlockSpec((1, 1, mods.shape[2]), lambda i: (seg(i), 0, 0))],
        out_specs=pl.BlockSpec((tm, d), lambda i: (i, 0)),
        compiler_params=_cp(("parallel",)),
        name="prenorm",
    )(h, g, mods)


def _proj_kernel(*refs, splits, rope_cols, scale_cols, q_scale):
    if rope_cols:
        a_ref, w_ref, cos_ref, sin_ref = refs[:4]
        outs = refs[4:]
    else:
        a_ref, w_ref = refs[:2]
        outs = refs[2:]
    y = jnp.dot(a_ref[...], w_ref[...], preferred_element_type=F32)
    tm, tn = y.shape
    if rope_cols:
        cos = cos_ref[...]
        sin = sin_ref[...]
        lane = lax.broadcasted_iota(jnp.int32, (tm, LANES), 1)
        first_half = (lane % 32) < 16
        pieces = []
        for c in range(tn // LANES):
            yc = y[:, c * LANES:(c + 1) * LANES]
            if c * LANES < rope_cols:
                partner = jnp.where(first_half, pltpu.roll(yc, LANES - 16, 1), pltpu.roll(yc, 16, 1))
                yc = yc * cos + partner * sin
            if c * LANES < scale_cols:
                yc = yc * q_scale
            pieces.append(yc)
        y = jnp.concatenate(pieces, axis=1)
    start = 0
    for o_ref, width in zip(outs, splits):
        o_ref[...] = y[:, start:start + width].astype(o_ref.dtype)
        start += width


def _proj(a, w, splits, dtypes, rope=None, rope_cols=0, scale_cols=0, q_scale=1.0, tn=None):
    m, k = a.shape
    n = w.shape[1]
    tm = ROW_TILE
    tn = n if tn is None else tn
    assert sum(splits) == tn and (len(splits) == 1 or tn == n)
    in_specs = [pl.BlockSpec((tm, k), lambda i, j: (i, 0)),
                pl.BlockSpec((k, tn), lambda i, j: (0, j))]
    args = [a, w]
    if rope_cols:
        in_specs += [pl.BlockSpec((tm, LANES), lambda i, j: (i, 0))] * 2
        args += list(rope)
    out_specs = []
    out_shape = []
    if len(splits) == 1:
        out_specs.append(pl.BlockSpec((tm, tn), lambda i, j: (i, j)))
        out_shape.append(jax.ShapeDtypeStruct((m, n), dtypes[0]))
    else:
        for width, dt in zip(splits, dtypes):
            out_specs.append(pl.BlockSpec((tm, width), lambda i, j: (i, 0)))
            out_shape.append(jax.ShapeDtypeStruct((m, width), dt))
    res = pl.pallas_call(
        functools.partial(_proj_kernel, splits=tuple(splits), rope_cols=rope_cols, scale_cols=scale_cols,
                          q_scale=q_scale),
        out_shape=out_shape,
        grid=(m // tm, n // tn),
        in_specs=in_specs,
        out_specs=out_specs,
        compiler_params=_cp(("parallel", "parallel")),
        name="proj",
    )(*args)
    return res


def _swa_kernel(sink_ref, q_ref, kp_ref, kc_ref, kn_ref, kx_ref, o_ref, *, nb, n_lat_blocks):
    i = pl.program_id(0)
    is_lat = i < n_lat_blocks
    n = i % nb
    blk = SWA_BLOCK
    q = q_ref[...]
    kv = jnp.concatenate([kp_ref[...], kc_ref[...], kn_ref[...], kx_ref[...]], axis=0)
    nkeys = kv.shape[0]
    k_t = kv[:, :SWA_KV]
    v_aug = jnp.concatenate([kv[:, SWA_KV:], jnp.ones((nkeys, SWA_KV), kv.dtype)], axis=1)
    r = lax.broadcasted_iota(jnp.int32, (blk, nkeys), 0)
    j = lax.broadcasted_iota(jnp.int32, (blk, nkeys), 1)
    lo = jnp.where(n > 0, 0, blk)
    hi = jnp.where(n < nb - 1, 3 * blk, 2 * blk)
    valid_loc = (j >= r) & (j <= r + 2 * blk) & (j >= lo) & (j < hi) & is_lat
    bias = jnp.where(valid_loc | (j >= 3 * blk), 0.0, NEG_INF)
    low_half = lax.broadcasted_iota(jnp.int32, (blk, SWA_KV), 1) < HEAD_DIM
    zero = jnp.zeros((blk, SWA_KV), q.dtype)
    group = SWA_HEADS // SWA_KV_HEADS
    q_rows, sinks = [], []
    for t in range(group):
        qt = q[:, t * SWA_KV:(t + 1) * SWA_KV]
        q_rows += [jnp.where(low_half, qt, zero), jnp.where(low_half, zero, qt)]
        sinks += [jnp.full((blk, 1), sink_ref[t], F32), jnp.full((blk, 1), sink_ref[group + t], F32)]
    scores = [lax.dot_general(jnp.concatenate(q_rows[2 * t:2 * t + 2], axis=0), k_t, (((1,), (1,)), ((), ())),
                              preferred_element_type=F32) for t in range(group)]
    tiles = []
    for t in range(group):
        sk = jnp.concatenate(sinks[2 * t:2 * t + 2], axis=0)
        s = (scores[t].reshape(2, blk, nkeys) + bias[None]).reshape(2 * blk, nkeys)
        mx = jnp.maximum(jnp.max(s, axis=-1, keepdims=True), sk)
        p = jnp.exp((s - mx).astype(BF16))
        oa = jnp.dot(p, v_aug, preferred_element_type=F32)
        den = oa[:, SWA_KV:] + jnp.exp(sk - mx)
        on = oa[:, :SWA_KV] * (1.0 / den)
        tiles.append(jnp.where(low_half, on[:blk], on[blk:]))
    o_ref[...] = jnp.concatenate(tiles, axis=1).astype(o_ref.dtype)


def _swa(q, kv, sink, n_batch, seq, ctx_len):
    m = q.shape[0]
    blk = SWA_BLOCK
    nb = seq // blk
    n_lat = n_batch * nb
    cpb = ctx_len // blk

    def batch_of(i):
        return jnp.where(i < n_lat, i // nb, (i - n_lat) // cpb)

    def prev_idx(i, s):
        return (jnp.where(i < n_lat, batch_of(i) * nb + jnp.maximum(i % nb - 1, 0), i), 0)

    def next_idx(i, s):
        return (jnp.where(i < n_lat, batch_of(i) * nb + jnp.minimum(i % nb + 1, nb - 1), i), 0)

    def ctx_idx(i, s):
        return (n_batch * seq // ctx_len + batch_of(i), 0)

    grid_spec = pltpu.PrefetchScalarGridSpec(
        num_scalar_prefetch=1,
        grid=(m // blk,),
        in_specs=[pl.BlockSpec((blk, SWA_Q), lambda i, s: (i, 0)),
                  pl.BlockSpec((blk, 2 * SWA_KV), prev_idx),
                  pl.BlockSpec((blk, 2 * SWA_KV), lambda i, s: (i, 0)),
                  pl.BlockSpec((blk, 2 * SWA_KV), next_idx),
                  pl.BlockSpec((ctx_len, 2 * SWA_KV), ctx_idx)],
        out_specs=pl.BlockSpec((blk, SWA_Q), lambda i, s: (i, 0)),
    )
    return pl.pallas_call(
        functools.partial(_swa_kernel, nb=nb, n_lat_blocks=n_lat),
        out_shape=jax.ShapeDtypeStruct((m, SWA_Q), BF16),
        grid_spec=grid_spec,
        compiler_params=_cp(("parallel",)),
        name="swa_attn",
    )(sink, q, kv, kv, kv, kv)


def _diff_kernel(*refs, n_lat_chunks, coef):
    if n_lat_chunks:
        lam_ref, gcol_ref, q_ref, kc_ref, vtc_ref, kl_ref, vtl_ref, o_ref, m_sc, acc_sc, st_a, st_b = refs
    else:
        lam_ref, gcol_ref, q_ref, kc_ref, vtc_ref, o_ref, m_sc, acc_sc, st_a, st_b = refs
    q = q_ref[...]
    tq = q.shape[0]
    dv = DIFF_V_DIM
    lane = lax.broadcasted_iota(jnp.int32, q.shape, 1)
    zero = jnp.zeros_like(q)
    qq = jnp.concatenate([jnp.where(lane < HEAD_DIM, q, zero), jnp.where(lane >= HEAD_DIM, q, zero)], axis=0)

    def scores(k):
        return lax.dot_general(k, qq, (((1,), (1,)), ((), ())), preferred_element_type=F32)

    def accumulate(st_ref, vt):
        st = st_ref[0:vt.shape[1], :]
        m_old = m_sc[...]
        m_new = jnp.maximum(m_old, jnp.max(st, axis=0, keepdims=True))
        alpha = jnp.exp2(m_old - m_new)
        pt = jnp.exp2((st - m_new).astype(BF16))
        acc_sc[...] = alpha * acc_sc[...] + jnp.dot(vt, pt, preferred_element_type=F32)
        m_sc[...] = m_new

    tk = st_a.shape[0]
    k_lat = lambda c: kl_ref[pl.ds(pl.multiple_of(c * tk, tk), tk), :]
    m_sc[...] = jnp.full(m_sc.shape, NEG_INF, F32)
    acc_sc[...] = jnp.zeros(acc_sc.shape, F32)
    bufs = (st_a, st_b)
    n = n_lat_chunks
    if n:
        unroll = DIFF_UNROLL
        st_a[...] = scores(k_lat(0))
        n_trips = (n - 1) // unroll

        def body(j, carry):
            for u in range(unroll):
                c = j * unroll + u
                bufs[(u + 1) % 2][...] = scores(k_lat(c + 1))
                accumulate(bufs[u % 2], vtl_ref[0, c])
            return carry

        lax.fori_loop(0, n_trips, body, 0)
        for c in range(n_trips * unroll, n):
            if c + 1 < n:
                bufs[(c + 1) % 2][...] = scores(k_lat(c + 1))
            else:
                bufs[(c + 1) % 2][0:kc_ref.shape[0], :] = scores(kc_ref[...])
            accumulate(bufs[c % 2], vtl_ref[0, c])
    else:
        st_a[0:kc_ref.shape[0], :] = scores(kc_ref[...])
    accumulate(bufs[n % 2], vtc_ref[0, 0])
    acc = acc_sc[...]
    ot = acc[:dv] * (1.0 / acc[dv:dv + 1])
    odt = ot[:, :tq] - lam_ref[0] * ot[:, tq:]
    ms = jnp.mean(odt * odt, axis=0, keepdims=True)
    yt = odt * lax.rsqrt(ms + DIFF_SUBLN_EPS) * (gcol_ref[...] * coef)
    o_ref[...] = yt.T.astype(o_ref.dtype)


def _diff(q, k, v, lam, subln_g, lam_init, n_batch, seq, ctx_len):
    m = q.shape[0]
    tk = min(DIFF_TK, seq)
    assert seq % tk == 0 and tk % ctx_len == 0
    dv = DIFF_V_DIM
    dva = dv + DIFF_ONES_ROWS
    lat_rows = n_batch * seq

    def transposed_chunks(rows, size):
        t = rows.reshape(rows.shape[0] // size, size, DIFF_HEADS, dv).transpose(2, 0, 3, 1)
        return jnp.concatenate([t, jnp.ones(t.shape[:2] + (DIFF_ONES_ROWS, size), t.dtype)], axis=2)

    vt_lat = transposed_chunks(v[:lat_rows], tk)
    vt_ctx = transposed_chunks(v[lat_rows:], ctx_len)
    gcol = subln_g.reshape(dv, 1)
    lat_chunks = seq // tk
    ctx0 = n_batch * seq // ctx_len

    def call(tq, n_q, q_block0, batch_of, with_lat):
        in_specs = [pl.BlockSpec((dv, 1), lambda h, i, s: (0, 0)),
                    pl.BlockSpec((tq, LANES), lambda h, i, s: (q_block0 + i, h)),
                    pl.BlockSpec((ctx_len, LANES), lambda h, i, s: (ctx0 + batch_of(i), h)),
                    pl.BlockSpec((1, 1, dva, ctx_len), lambda h, i, s: (h, batch_of(i), 0, 0))]
        args = [lam, gcol, q, k, vt_ctx]
        if with_lat:
            in_specs += [pl.BlockSpec((seq, LANES), lambda h, i, s: (batch_of(i), h)),
                         pl.BlockSpec((1, lat_chunks, dva, tk), lambda h, i, s: (h, batch_of(i), 0, 0))]
            args += [k, vt_lat]
        grid_spec = pltpu.PrefetchScalarGridSpec(
            num_scalar_prefetch=1,
            grid=(DIFF_HEADS, n_q),
            in_specs=in_specs,
            out_specs=pl.BlockSpec((tq, LANES), lambda h, i, s: (i, h)),
            scratch_shapes=[pltpu.VMEM((1, 2 * tq), F32), pltpu.VMEM((dva, 2 * tq), F32),
                            pltpu.VMEM((tk, 2 * tq), F32), pltpu.VMEM((tk, 2 * tq), F32)],
        )
        return pl.pallas_call(
            functools.partial(_diff_kernel, n_lat_chunks=lat_chunks if with_lat else 0, coef=1.0 - lam_init),
            out_shape=jax.ShapeDtypeStruct((n_q * tq, DIFF_V), BF16),
            grid_spec=grid_spec,
            compiler_params=_cp(("parallel", "arbitrary")),
            name="diff_attn" if with_lat else "diff_attn_ctx",
        )(*args)

    tq = DIFF_TQ
    y_lat = call(tq, n_batch * seq // tq, 0, lambda i: i // (seq // tq), True)
    y_ctx = call(ctx_len, n_batch, ctx0, lambda i: i, False)
    return jnp.concatenate([y_lat, y_ctx], axis=0)


def _rwkv_prep_kernel(p_ref, hp_ref, hn_ref, mup_ref, mun_ref, kk_w_ref, ka_ref, rk_ref, w0_ref, a0_ref,
                      wup_ref, aup_ref, gup_ref, ones_ref,
                      r_ref, v_ref, kk_ref, ld_ref, ke_ref, bb_ref, g_ref, bonus_ref, sc,
                      *, tm, lat_rows, seq, ctx_len):
    i = pl.program_id(0)
    w = RWKV_WIDTH
    sc[0:8, :] = hp_ref[...]
    sc[8:8 + tm, :] = p_ref[...]
    sc[8 + tm:16 + tm, :] = hn_ref[...]
    p = p_ref[...]
    prev = sc[7:7 + tm, :]
    nxt = sc[9:9 + tm, :]
    row = i * tm + lax.broadcasted_iota(jnp.int32, (tm, 1), 0)
    pos = jnp.where(row < lat_rows, row % seq, (row - lat_rows) % ctx_len)
    seg_len = jnp.where(row < lat_rows, seq, ctx_len)
    prev = jnp.where(pos == 0, 0.0, prev)
    nxt = jnp.where(pos == seg_len - 1, 0.0, nxt)
    ps = p + mup_ref[...] * (prev - p) + mun_ref[...] * (nxt - p)

    r = ps[:, 0:w]
    k = ps[:, w:2 * w]
    v = ps[:, 2 * w:3 * w]
    wd = ps[:, 3 * w:3 * w + 2 * DECAY_LORA]
    ad = ps[:, 3 * w + 2 * DECAY_LORA:3 * w + 2 * DECAY_LORA + 2 * AAA_LORA]
    gd = ps[:, 3 * w + 2 * DECAY_LORA + 2 * AAA_LORA:]

    ones_bd = ones_ref[...]
    g = jnp.dot(_sigmoid(gd).astype(BF16), gup_ref[...], preferred_element_type=F32)
    kk = k * kk_w_ref[...]
    ss = _group_sum(kk * kk, ones_bd)
    kk = kk / jnp.maximum(jnp.sqrt(ss), 1e-12)
    w_raw = w0_ref[...] + jnp.dot(jnp.tanh(wd).astype(BF16), wup_ref[...], preferred_element_type=F32)
    a_raw = a0_ref[...] + jnp.dot(ad.astype(BF16), aup_ref[...], preferred_element_type=F32)
    ld = -math.exp(-0.5) * _sigmoid(w_raw)
    a = _sigmoid(a_raw)
    ka = ka_ref[...]
    ke_sum = jnp.zeros_like(k)
    for d in range(2):
        a_d = a[:, d * w:(d + 1) * w]
        ke = k * (1.0 + (a_d - 1.0) * ka)
        ld_ref[d] = ld[:, d * w:(d + 1) * w]
        ke_ref[d] = ke
        bb_ref[d] = kk * a_d
        ke_sum = ke_sum + ke
    rk = _group_sum(r * ke_sum * rk_ref[...], ones_bd)
    r_ref[...] = r
    v_ref[...] = v
    kk_ref[...] = kk
    g_ref[...] = g
    bonus_ref[...] = rk * v


def _rwkv_prep(p, lp, n_batch, seq, ctx_len):
    m, cols = p.shape
    tm = 256
    w = RWKV_WIDTH
    lat_rows = n_batch * seq
    row = lambda a: a.reshape(1, -1).astype(F32)

    def blockdiag(u):
        z = jnp.zeros_like(u[0])
        return jnp.concatenate([jnp.concatenate([u[0], z], axis=1), jnp.concatenate([z, u[1]], axis=1)], axis=0)

    head = jnp.arange(w) // HEAD_DIM
    ones_bd = (head[:, None] == head[None, :]).astype(BF16)
    full = lambda shape: pl.BlockSpec(shape, lambda i: (0,) * len(shape))
    nb8 = m // 8
    outs = pl.pallas_call(
        functools.partial(_rwkv_prep_kernel, tm=tm, lat_rows=lat_rows, seq=seq, ctx_len=ctx_len),
        out_shape=[jax.ShapeDtypeStruct((m, w), F32)] * 3
        + [jax.ShapeDtypeStruct((2, m, w), F32)] * 3
        + [jax.ShapeDtypeStruct((m, w), F32)] * 2,
        grid=(m // tm,),
        in_specs=[pl.BlockSpec((tm, cols), lambda i: (i, 0)),
                  pl.BlockSpec((8, cols), lambda i: (jnp.maximum(i * (tm // 8) - 1, 0), 0)),
                  pl.BlockSpec((8, cols), lambda i: (jnp.minimum((i + 1) * (tm // 8), nb8 - 1), 0)),
                  full((1, cols)), full((1, cols)), full((1, w)), full((1, w)), full((1, w)),
                  full((1, 2 * w)), full((1, 2 * w)),
                  full((2 * DECAY_LORA, 2 * w)), full((2 * AAA_LORA, 2 * w)), full((GATE_LORA, w)),
                  full((w, w))],
        out_specs=[pl.BlockSpec((tm, w), lambda i: (i, 0))] * 3
        + [pl.BlockSpec((2, tm, w), lambda i: (0, i, 0))] * 3
        + [pl.BlockSpec((tm, w), lambda i: (i, 0))] * 2,
        scratch_shapes=[pltpu.VMEM((tm + 16, cols), F32)],
        compiler_params=_cp(("parallel",)),
        name="rwkv_prep",
    )(p, p, p, row(lp['rwkv_mu_prev']), row(lp['rwkv_mu_next']), row(lp['rwkv_k_k']), row(lp['rwkv_k_a']),
      row(lp['rwkv_r_k']), row(lp['rwkv_w0']), row(lp['rwkv_a0']),
      blockdiag(lp['rwkv_w_up']).astype(BF16), blockdiag(lp['rwkv_a_up']).astype(BF16),
      lp['rwkv_g_up'].astype(BF16), ones_bd)
    return outs


def _rwkv_chunk_prep(d, off, r_ref, v_ref, kk_ref, ld_ref, ke_ref, bb_ref, incl):
    c = RWKV_CHUNK
    ld = ld_ref[0, pl.ds(off, c), :]
    r = r_ref[pl.ds(off, c), :]
    v = v_ref[pl.ds(off, c), :]
    kk = kk_ref[pl.ds(off, c), :]
    ke = ke_ref[0, pl.ds(off, c), :]
    bb = bb_ref[0, pl.ds(off, c), :]
    cum = jnp.dot(incl, ld, precision=HI, preferred_element_type=F32)
    tot = jnp.sum(ld, axis=0, keepdims=True)
    rt = r * jnp.exp(cum)
    einv = jnp.exp(-cum)
    etail = jnp.exp(tot - cum)
    return dict(at=(-kk * jnp.exp(cum - ld)).astype(BF16), rt=rt, rt_b=rt.astype(BF16),
                bt=(bb * einv).astype(BF16), kt=(ke * einv).astype(BF16),
                bh=(bb * etail).astype(BF16), kh=(ke * etail).astype(BF16),
                v=v, v_b=v.astype(BF16), wtot=jnp.exp(tot))


def _rwkv_chunk_group(offs, in_refs, y_refs, s_sc):
    c = RWKV_CHUNK
    hd = HEAD_DIM
    n_group = len(offs[0])
    ti = lax.broadcasted_iota(jnp.int32, (c, c), 0)
    tj = lax.broadcasted_iota(jnp.int32, (c, c), 1)
    eye = (ti == tj).astype(F32)
    incl = [(tj <= ti).astype(F32), (tj >= ti).astype(F32)]
    strict = [m - eye for m in incl]
    diag_mask = ((ti // RWKV_INV_BASE) == (tj // RWKV_INV_BASE)).astype(F32)
    off_masks = []
    sz = RWKV_INV_BASE
    while sz < c:
        off_masks.append((((ti // (2 * sz)) == (tj // (2 * sz))) & ((ti // sz) != (tj // sz))).astype(F32))
        sz *= 2

    pre = {(d, g): _rwkv_chunk_prep(d, offs[d][g], *in_refs[d], incl[d])
           for d in range(2) for g in range(n_group)}
    lanes = [(d, g, h) for g in range(n_group) for d in range(2) for h in range(RWKV_HEADS)]
    sl = lambda h: slice(h * hd, (h + 1) * hd)
    get = lambda name: [pre[d, g][name][:, sl(h)] for d, g, h in lanes]
    at, rt, rt_b, bt, kt, bh, kh, v, v_b = (get(n) for n in ('at', 'rt', 'rt_b', 'bt', 'kt', 'bh', 'kh', 'v', 'v_b'))
    nl = range(len(lanes))
    nt = (((1,), (1,)), ((), ()))
    bdot = lambda x, y: jnp.dot(x.astype(BF16), y.astype(BF16), preferred_element_type=F32)

    gm = [lax.dot_general(jnp.concatenate([at[i], rt_b[i]], axis=0), jnp.concatenate([bt[i], kt[i]], axis=0), nt,
                          preferred_element_type=F32) for i in nl]
    zz0 = [bdot(v[i].T, kh[i]) for i in nl]
    aab = [gm[i][:c, :c] * strict[lanes[i][0]] for i in nl]
    aak = [gm[i][:c, c:] * strict[lanes[i][0]] for i in nl]
    arb = [gm[i][c:, :c] * incl[lanes[i][0]] for i in nl]
    ark = [gm[i][c:, c:] * incl[lanes[i][0]] for i in nl]
    av = [bdot(jnp.concatenate([aak[i], ark[i]], axis=0), v_b[i]) for i in nl]
    pw = [aab[i] * diag_mask for i in nl]
    tm_ = [eye + pw[i] for i in nl]
    for _ in range(int(math.log2(RWKV_INV_BASE)) - 1):
        pw = [bdot(pw[i], pw[i]) for i in nl]
        tm_ = [tm_[i] + bdot(tm_[i], pw[i]) for i in nl]
    for off_mask in off_masks:
        tn = [bdot(tm_[i], aab[i] * off_mask) for i in nl]
        tm_ = [tm_[i] + bdot(tn[i], tm_[i]) for i in nl]
    au = [bdot(tm_[i], jnp.concatenate([at[i], av[i][:c].astype(BF16)], axis=1)) for i in nl]
    ry = [bdot(arb[i], au[i]) for i in nl]
    mz = [bdot(au[i].T, bh[i]) for i in nl]
    rbar = [(rt[i] + ry[i][:, :hd]).astype(BF16) for i in nl]
    ybar = [ry[i][:, hd:] + av[i][c:] for i in nl]
    mm = [(eye * pre[lanes[i][0], lanes[i][1]]['wtot'][:, sl(lanes[i][2])] + mz[i][:hd]).astype(BF16) for i in nl]
    zz = [mz[i][hd:] + zz0[i] for i in nl]
    state = {(d, h): s_sc[d, h] for d in range(2) for h in range(RWKV_HEADS)}
    ys = {}
    for g in range(n_group):
        idx = [i for i in nl if lanes[i][1] == g]
        s_b = {i: state[lanes[i][0], lanes[i][2]].astype(BF16) for i in idx}
        for i in idx:
            ys[i] = lax.dot_general(rbar[i], s_b[i], nt, preferred_element_type=F32) + ybar[i]
        for i in idx:
            state[lanes[i][0], lanes[i][2]] = jnp.dot(s_b[i], mm[i], preferred_element_type=F32) + zz[i]
    for (d, h), val in state.items():
        s_sc[d, h] = val
    for d in range(2):
        for g in range(n_group):
            y_refs[d][0, pl.ds(offs[d][g], c), :] = jnp.concatenate(
                [ys[i] for i in nl if lanes[i][0] == d and lanes[i][1] == g], axis=1)


def _rwkv_scan_kernel(rf, vf, kf, ldf, kef, bbf, rb, vb, kb, ldb, keb, bbb, yf, yb, s_sc, *, n_chunks):
    @pl.when(pl.program_id(1) == 0)
    def _():
        s_sc[...] = jnp.zeros(s_sc.shape, F32)

    group = RWKV_GROUP

    def body(t, carry):
        off_f = [pl.multiple_of((t * group + g) * RWKV_CHUNK, RWKV_CHUNK) for g in range(group)]
        off_b = [pl.multiple_of((n_chunks - 1 - t * group - g) * RWKV_CHUNK, RWKV_CHUNK) for g in range(group)]
        _rwkv_chunk_group((off_f, off_b), ((rf, vf, kf, ldf, kef, bbf), (rb, vb, kb, ldb, keb, bbb)),
                          (yf, yb), s_sc)
        return carry

    lax.fori_loop(0, n_chunks // group, body, 0)


def _rwkv_scan(r, v, kk, ld, ke, bb, n_batch, seq, ctx_len):
    m, w = r.shape
    ts = RWKV_STEP_ROWS
    assert ctx_len == ts
    lpb = seq // ts
    ctx0 = n_batch * seq // ts
    nj = 1 + lpb

    def fwd(b, j):
        return jnp.where(j == 0, ctx0 + b, b * lpb + j - 1)

    def bwd(b, j):
        return jnp.where(j == 0, ctx0 + b, b * lpb + lpb - j)

    shared = lambda f: pl.BlockSpec((ts, w), lambda b, j: (f(b, j), 0))
    per_dir = lambda f, d: pl.BlockSpec((1, ts, w), lambda b, j: (d, f(b, j), 0))
    y = pl.pallas_call(
        functools.partial(_rwkv_scan_kernel, n_chunks=ts // RWKV_CHUNK),
        out_shape=[jax.ShapeDtypeStruct((1, m, w), F32)] * 2,
        grid=(n_batch, nj),
        in_specs=[shared(fwd), shared(fwd), shared(fwd), per_dir(fwd, 0), per_dir(fwd, 0), per_dir(fwd, 0),
                  shared(bwd), shared(bwd), shared(bwd), per_dir(bwd, 1), per_dir(bwd, 1), per_dir(bwd, 1)],
        out_specs=[pl.BlockSpec((1, ts, w), lambda b, j: (0, fwd(b, j), 0)),
                   pl.BlockSpec((1, ts, w), lambda b, j: (0, bwd(b, j), 0))],
        scratch_shapes=[pltpu.VMEM((2, RWKV_HEADS, HEAD_DIM, HEAD_DIM), F32)],
        compiler_params=_cp(("parallel", "arbitrary")),
        name="rwkv_scan",
    )(r, v, kk, ld, ke, bb, r, v, kk, ld, ke, bb)
    return y


def _rwkv_finish(y, bonus, g, lnx_g, lnx_b, ones_bd):
    inv = 1.0 / HEAD_DIM
    mu = _group_sum(y, ones_bd) * inv
    yc = y - mu
    var = _group_sum(yc * yc, ones_bd) * inv
    yn = yc * lax.rsqrt(var + RWKV_GN_EPS) * lnx_g + lnx_b
    return (yn + bonus) * g


def _top2(logits):
    lane = lax.broadcasted_iota(jnp.int32, logits.shape, 1)
    logits = jnp.where(lane < N_EXPERTS, logits, -jnp.inf)
    m1 = jnp.max(logits, axis=-1, keepdims=True)
    i1 = jnp.min(jnp.where(logits == m1, lane, LANES), axis=-1, keepdims=True)
    rest = jnp.where(lane == i1, -jnp.inf, logits)
    m2 = jnp.max(rest, axis=-1, keepdims=True)
    i2 = jnp.min(jnp.where(rest == m2, lane, LANES), axis=-1, keepdims=True)
    e = jnp.exp(m2 - m1)
    w1 = 1.0 / (1.0 + e)
    w2 = e / (1.0 + e)
    idx = jnp.where(lane == 0, i1, jnp.where(lane == 1, i2, 0))
    wts = jnp.where(lane == 0, w1, jnp.where(lane == 1, w2, 0.0))
    return idx, wts


def _merge_kernel(*refs, d, with_router):
    (ya_ref, yf_ref, yb_ref, bonus_ref, g_ref, lg_ref, lb_ref, ones_ref, yc_ref, gt_ref, h_ref,
     pa_ref, pb_ref, pc_ref, wo_ref, gpost_ref, gpre_ref, mod_ref) = refs[:18]
    if with_router:
        rwh_ref, rwl_ref, h_out, f_out, idx_out, wt_out = refs[18:]
    else:
        h_out, f_out = refs[18:]
    yb = _rwkv_finish(yf_ref[0] + yb_ref[0], bonus_ref[...], g_ref[...], lg_ref[...], lb_ref[...], ones_ref[...])
    gates = gt_ref[...].astype(F32)
    merged = (_sigmoid(gates[:, 0:d]) * jnp.dot(ya_ref[...], pa_ref[...], preferred_element_type=F32)
              + _sigmoid(gates[:, d:2 * d]) * jnp.dot(yb.astype(BF16), pb_ref[...], preferred_element_type=F32)
              + _sigmoid(gates[:, 2 * d:3 * d]) * jnp.dot(yc_ref[...], pc_ref[...], preferred_element_type=F32))
    out = jnp.dot(merged.astype(BF16), wo_ref[...], preferred_element_type=F32)
    mod = lambda idx: mod_ref[0, :, idx * d:(idx + 1) * d]
    hn = h_ref[...] + mod(2) * (_rms(out, NORM_EPS) * gpost_ref[...])
    h_out[...] = hn
    f = (_rms(hn, NORM_EPS) * gpre_ref[...]) * (1.0 + mod(4)) + mod(3)
    f_out[...] = f.astype(f_out.dtype)
    if with_router:
        rows = f.shape[0]
        hi = f.astype(BF16)
        mid = (f - hi.astype(F32)).astype(BF16)
        part = jnp.dot(jnp.concatenate([hi, mid], axis=0), rwh_ref[...], preferred_element_type=F32)
        logits = part[:rows] + part[rows:] + jnp.dot(hi, rwl_ref[...], preferred_element_type=F32)
        idx, wts = _top2(logits)
        idx_out[...] = idx
        wt_out[...] = wts


def _merge(ya, yf, yb, bonus, g, lnx_g, lnx_b, yc, gates, h, pa, pb, pc, wo, g_post, g_pre, mods, f_dtype,
           lat_bpb, n_batch, router_w=None):
    m, d = h.shape
    w = bonus.shape[1]
    tm = 256
    bpb = lat_bpb * (ROW_TILE // tm)
    seg = functools.partial(_seg_of_block, lat_blocks_per_batch=bpb, n_batch=n_batch)
    rows = lambda width: pl.BlockSpec((tm, width), lambda i: (i, 0))
    rows3 = pl.BlockSpec((1, tm, w), lambda i: (0, i, 0))
    full = lambda a: pl.BlockSpec(a.shape, lambda i: (0, 0))
    head = jnp.arange(w) // HEAD_DIM
    ones_bd = (head[:, None] == head[None, :]).astype(BF16)
    args = [ya, yf, yb, bonus, g, lnx_g.reshape(1, w), lnx_b.reshape(1, w), ones_bd, yc, gates, h,
            pa, pb, pc, wo, g_post, g_pre, mods]
    in_specs = [rows(ya.shape[1]), rows3, rows3, rows(w), rows(w), full(args[5]), full(args[6]), full(ones_bd),
                rows(yc.shape[1]), rows(gates.shape[1]), rows(d), full(pa), full(pb), full(pc), full(wo),
                full(g_post), full(g_pre), pl.BlockSpec((1, 1, mods.shape[2]), lambda i: (seg(i), 0, 0))]
    out_shape = [jax.ShapeDtypeStruct((m, d), F32), jax.ShapeDtypeStruct((m, d), f_dtype)]
    out_specs = [rows(d), rows(d)]
    if router_w is not None:
        w_pad = jnp.zeros((d, LANES), F32).at[:, :N_EXPERTS].set(router_w)
        w_hi = w_pad.astype(BF16)
        w_lo = (w_pad - w_hi.astype(F32)).astype(BF16)
        args += [w_hi, w_lo]
        in_specs += [full(w_hi), full(w_lo)]
        out_shape += [jax.ShapeDtypeStruct((m, LANES), jnp.int32), jax.ShapeDtypeStruct((m, LANES), F32)]
        out_specs += [rows(LANES), rows(LANES)]
    return pl.pallas_call(
        functools.partial(_merge_kernel, d=d, with_router=router_w is not None),
        out_shape=out_shape,
        grid=(m // tm,),
        in_specs=in_specs,
        out_specs=out_specs,
        compiler_params=_cp(("parallel",)),
        name="merge",
    )(*args)


def _swiglu_hidden(x, wg, wu):
    hg = jnp.dot(x, wg, preferred_element_type=F32)
    hu = jnp.dot(x, wu, preferred_element_type=F32)
    return (hg * _sigmoid(hg) * hu).astype(BF16)


def _ffn_kernel(f_ref, wg_ref, wu_ref, wd_ref, h_ref, gpost_ref, mod_ref, o_ref, acc, *, d):
    j = pl.program_id(1)

    @pl.when(j == 0)
    def _():
        acc[...] = jnp.zeros(acc.shape, F32)

    hid = _swiglu_hidden(f_ref[...], wg_ref[...], wu_ref[...])
    acc[...] += jnp.dot(hid, wd_ref[...], preferred_element_type=F32)

    @pl.when(j == pl.num_programs(1) - 1)
    def _():
        gate = mod_ref[0, :, 5 * d:6 * d]
        o_ref[...] = h_ref[...] + gate * (_rms(acc[...], NORM_EPS) * gpost_ref[...])


def _ffn(f, wg, wu, wd, h, g_post, mods, lat_bpb, n_batch):
    m, d = h.shape
    ff = wg.shape[1]
    tm, tf = ROW_TILE, FFN_TF
    seg = functools.partial(_seg_of_block, lat_blocks_per_batch=lat_bpb, n_batch=n_batch)
    return pl.pallas_call(
        functools.partial(_ffn_kernel, d=d),
        out_shape=jax.ShapeDtypeStruct((m, d), F32),
        grid=(m // tm, ff // tf),
        in_specs=[pl.BlockSpec((tm, d), lambda i, j: (i, 0)),
                  pl.BlockSpec((d, tf), lambda i, j: (0, j)),
                  pl.BlockSpec((d, tf), lambda i, j: (0, j)),
                  pl.BlockSpec((tf, d), lambda i, j: (j, 0)),
                  pl.BlockSpec((tm, d), lambda i, j: (i, 0)),
                  pl.BlockSpec((1, d), lambda i, j: (0, 0)),
                  pl.BlockSpec((1, 1, mods.shape[2]), lambda i, j: (seg(i), 0, 0))],
        out_specs=pl.BlockSpec((tm, d), lambda i, j: (i, 0)),
        scratch_shapes=[pltpu.VMEM((tm, d), F32)],
        compiler_params=_cp(("parallel", "arbitrary")),
        name="ffn_dense",
    )(f, wg, wu, wd, h, g_post, mods)


def _moe_gather_copy(f_hbm, xbuf, sem, slot, src_row, dst_row):
    return pltpu.make_async_copy(f_hbm.at[pl.ds(src_row, 1), :], xbuf.at[slot, pl.ds(dst_row, 1), :],
                                 sem.at[slot])


def _moe_ffn_kernel(blk_e_ref, nused_ref, tok_ref, f_hbm, wg_ref, wu_ref, wd_ref, y_ref, xbuf, xb, acc, sem,
                    *, tm, nj):
    i = pl.program_id(0)
    j = pl.program_id(1)
    nused = nused_ref[0]
    active = i < nused
    slot = i % 2
    per_step = tm // nj

    @pl.when((i == 0) & (j == 0))
    def _():
        def issue(r, carry):
            _moe_gather_copy(f_hbm, xbuf, sem, 0, tok_ref[r], r).start()
            return carry

        lax.fori_loop(0, tm, issue, 0)

    @pl.when((i <= nused) & (j == 0))
    def _():
        pltpu.make_async_copy(f_hbm.at[pl.ds(0, tm), :], xbuf.at[slot], sem.at[slot]).wait()

    @pl.when(active & (j == 0))
    def _():
        xb[...] = xbuf[slot].astype(BF16)
        acc[...] = jnp.zeros(acc.shape, F32)

    @pl.when(active)
    def _():
        base = (i + 1) * tm + j * per_step
        for r in range(per_step):
            _moe_gather_copy(f_hbm, xbuf, sem, 1 - slot, tok_ref[base + r], j * per_step + r).start()
        hid = _swiglu_hidden(xb[...], wg_ref[0], wu_ref[0])
        acc[...] += jnp.dot(hid, wd_ref[0], preferred_element_type=F32)

    @pl.when(j == nj - 1)
    def _():
        y_ref[...] = acc[...]


def _moe_ffn(f, blk_e, nused, tok, wg, wu, wd, n_blocks):
    d = f.shape[1]
    ff = wg.shape[2]
    tm, tf = MOE_TILE, MOE_TF

    nj = ff // tf

    def e_of(i, be, nu):
        return be[jnp.minimum(i, nu[0] - 1)]

    def j_of(i, j, nu):
        return jnp.where(i < nu[0], j, nj - 1)

    grid_spec = pltpu.PrefetchScalarGridSpec(
        num_scalar_prefetch=3,
        grid=(n_blocks, nj),
        in_specs=[pl.BlockSpec(memory_space=pl.ANY),
                  pl.BlockSpec((1, d, tf), lambda i, j, be, nu, tk: (e_of(i, be, nu), 0, j_of(i, j, nu))),
                  pl.BlockSpec((1, d, tf), lambda i, j, be, nu, tk: (e_of(i, be, nu), 0, j_of(i, j, nu))),
                  pl.BlockSpec((1, tf, d), lambda i, j, be, nu, tk: (e_of(i, be, nu), j_of(i, j, nu), 0))],
        out_specs=pl.BlockSpec((tm, d), lambda i, j, be, nu, tk: (i, 0)),
        scratch_shapes=[pltpu.VMEM((2, tm, d), F32), pltpu.VMEM((tm, d), BF16), pltpu.VMEM((tm, d), F32),
                        pltpu.SemaphoreType.DMA((2,))],
    )
    return pl.pallas_call(
        functools.partial(_moe_ffn_kernel, tm=tm, nj=nj),
        out_shape=jax.ShapeDtypeStruct((n_blocks * tm, d), F32),
        grid_spec=grid_spec,
        compiler_params=_cp(("arbitrary", "arbitrary")),
        name="moe_ffn",
    )(blk_e, nused, tok, f, wg, wu, wd)


def _moe_combine_kernel(p0_ref, p1_ref, y_hbm, wt_ref, h_ref, gpost_ref, mod_ref, o_ref, b0, b1, sem, *, tm, d):
    i = pl.program_id(0)

    def issue(r, carry):
        pltpu.make_async_copy(y_hbm.at[pl.ds(p0_ref[i * tm + r], 1), :], b0.at[pl.ds(r, 1), :], sem.at[0]).start()
        pltpu.make_async_copy(y_hbm.at[pl.ds(p1_ref[i * tm + r], 1), :], b1.at[pl.ds(r, 1), :], sem.at[1]).start()
        return carry

    lax.fori_loop(0, tm, issue, 0)
    pltpu.make_async_copy(y_hbm.at[pl.ds(0, tm), :], b0, sem.at[0]).wait()
    pltpu.make_async_copy(y_hbm.at[pl.ds(0, tm), :], b1, sem.at[1]).wait()
    wt = wt_ref[...]
    y = b0[...] * wt[:, 0:1] + b1[...] * wt[:, 1:2]
    gate = mod_ref[0, :, 5 * d:6 * d]
    o_ref[...] = h_ref[...] + gate * (_rms(y, NORM_EPS) * gpost_ref[...])


def _moe_combine(pos0, pos1, y, wt, h, g_post, mods, n_rows, rows_per_batch, n_batch):
    d = h.shape[1]
    tm = COMBINE_TILE
    seg = functools.partial(_seg_of_block, lat_blocks_per_batch=rows_per_batch // tm, n_batch=n_batch)
    grid_spec = pltpu.PrefetchScalarGridSpec(
        num_scalar_prefetch=2,
        grid=(n_rows // tm,),
        in_specs=[pl.BlockSpec(memory_space=pl.ANY),
                  pl.BlockSpec((tm, LANES), lambda i, a, b: (i, 0)),
                  pl.BlockSpec((tm, d), lambda i, a, b: (i, 0)),
                  pl.BlockSpec((1, d), lambda i, a, b: (0, 0)),
                  pl.BlockSpec((1, 1, mods.shape[2]), lambda i, a, b: (seg(i), 0, 0))],
        out_specs=pl.BlockSpec((tm, d), lambda i, a, b: (i, 0)),
        scratch_shapes=[pltpu.VMEM((tm, d), F32), pltpu.VMEM((tm, d), F32), pltpu.SemaphoreType.DMA((2,))],
    )
    return pl.pallas_call(
        functools.partial(_moe_combine_kernel, tm=tm, d=d),
        out_shape=jax.ShapeDtypeStruct((n_rows, d), F32),
        grid_spec=grid_spec,
        compiler_params=_cp(("arbitrary",)),
        name="moe_combine",
    )(pos0, pos1, y, wt, h, g_post, mods)


def _moe_slots(top_i, tile):
    n = top_i.shape[0]
    a = n * 2
    e_flat = top_i.reshape(a)
    onehot = (e_flat[:, None] == jnp.arange(N_EXPERTS, dtype=jnp.int32)[None, :]).astype(jnp.int32)
    sub = LANES
    blocks = onehot.reshape(a // sub, sub, N_EXPERTS).astype(F32)
    tri = (jnp.arange(sub)[:, None] >= jnp.arange(sub)[None, :]).astype(F32)
    within = jnp.einsum('ij,bjk->bik', tri, blocks).astype(jnp.int32)
    totals = within[:, -1, :]
    csum = (within + (jnp.cumsum(totals, axis=0) - totals)[:, None, :]).reshape(a, N_EXPERTS)
    rank = jnp.sum(csum * onehot, axis=1) - 1
    counts = csum[-1]
    padded = (counts + tile - 1) // tile * tile
    pends = jnp.cumsum(padded)
    pstarts = pends - padded
    dest = (jnp.sum(onehot * pstarts[None, :], axis=1) + rank).astype(jnp.int32)
    n_blocks = a // tile + N_EXPERTS
    tok = jnp.zeros((n_blocks * tile,), jnp.int32).at[dest].set(jnp.arange(a, dtype=jnp.int32) // 2)
    block_start = jnp.arange(n_blocks, dtype=jnp.int32) * tile
    blk_e = jnp.minimum(jnp.sum((block_start[:, None] >= pends[None, :]).astype(jnp.int32), axis=1),
                        N_EXPERTS - 1)
    nused = (pends[-1:] // tile).astype(jnp.int32)
    return tok, blk_e, nused, dest.reshape(n, 2), n_blocks


def _swa_head_order(w, axis):
    group = SWA_HEADS // SWA_KV_HEADS
    order = [g * group + t for t in range(group) for g in range(SWA_KV_HEADS)]
    shape = w.shape
    w = w.reshape(shape[:axis] + (SWA_HEADS, HEAD_DIM) + shape[axis + 1:])
    return jnp.take(w, jnp.array(order), axis=axis).reshape(shape)


def _rope_tables(n_batch, seq, ctx_len):
    t = jnp.arange(seq, dtype=jnp.int32)
    row = (t // GRID_W).astype(F32)
    col = (t % GRID_W).astype(F32)
    axis_dim = HEAD_DIM // 2
    inv_freq = ROPE_THETA ** (-jnp.arange(0, axis_dim, 2, dtype=F32) / axis_dim)
    dd = jnp.arange(LANES) % HEAD_DIM
    pos = jnp.where((dd // axis_dim)[None, :] == 0, row[:, None], col[:, None])
    ang = pos * inv_freq[dd % (axis_dim // 2)][None, :]
    cos = jnp.cos(ang)
    sin = jnp.where(((dd % axis_dim) < axis_dim // 2)[None, :], -jnp.sin(ang), jnp.sin(ang))
    n_ctx = n_batch * ctx_len
    cos = jnp.concatenate([jnp.tile(cos, (n_batch, 1)), jnp.ones((n_ctx, LANES), F32)], axis=0)
    sin = jnp.concatenate([jnp.tile(sin, (n_batch, 1)), jnp.zeros((n_ctx, LANES), F32)], axis=0)
    return cos, sin


def kernel(x, c, ctx, c_ctx, ada_w, ada_b, pre_mix_g, post_mix_g, pre_ffn_g, post_ffn_g, w_in, swa_sink,
           rwkv_mu_prev, rwkv_mu_next, rwkv_w0, rwkv_w_up, rwkv_a0, rwkv_a_up, rwkv_g_up, rwkv_k_k, rwkv_k_a,
           rwkv_r_k, rwkv_lnx_g, rwkv_lnx_b, diff_lambda, diff_subln_g, proj_swa, proj_rwkv, proj_diff, w_out,
           ffn_w_gate, ffn_w_up, ffn_w_down, router_w, moe_w_gate, moe_w_up, moe_w_down):
    n_batch, seq, d = x.shape
    ctx_len = ctx.shape[1]
    depth = w_in.shape[0]
    lat_rows = n_batch * seq
    lat_bpb = seq // ROW_TILE
    assert seq % ROW_TILE == 0 and (n_batch * ctx_len) % ROW_TILE == 0

    h = jnp.concatenate([x.reshape(lat_rows, d), ctx.reshape(n_batch * ctx_len, d)], axis=0)
    m = h.shape[0]
    cond = jnp.zeros((8, d), F32).at[:n_batch].set(c).at[n_batch].set(c_ctx)
    rope = _rope_tables(n_batch, seq, ctx_len)
    row = lambda a: a.reshape(1, -1)

    o_swa = 0
    o_rwkv = o_swa + SWA_Q + 2 * SWA_KV
    o_diff = o_rwkv + RWKV_COLS
    o_gate = o_diff + 2 * DIFF_QK + DIFF_V
    o_end = o_gate + 3 * d

    for layer in range(depth):
        mods = _ada(cond, ada_w[layer], ada_b[layer].reshape(1, -1))[:, None, :]
        wl = w_in[layer].astype(BF16)
        a = _prenorm(h, row(pre_mix_g[layer]), mods, 0, 1, lat_bpb, n_batch)

        w_swa = jnp.concatenate([_swa_head_order(wl[:, o_swa:o_swa + SWA_Q], axis=1),
                                 wl[:, o_swa + SWA_Q:o_rwkv]], axis=1)
        q_swa, kv_swa = _proj(a, w_swa, (SWA_Q, 2 * SWA_KV), (BF16, BF16), rope=rope,
                              rope_cols=SWA_Q + SWA_KV, scale_cols=SWA_Q, q_scale=HEAD_DIM ** -0.5)
        (p_rwkv,) = _proj(a, wl[:, o_rwkv:o_diff], (RWKV_COLS,), (F32,))
        q_diff, k_diff, v_diff = _proj(a, wl[:, o_diff:o_gate], (DIFF_QK, DIFF_QK, DIFF_V), (BF16,) * 3,
                                       rope=rope, rope_cols=2 * DIFF_QK, scale_cols=DIFF_QK,
                                       q_scale=HEAD_DIM ** -0.5 * math.log2(math.e))
        (gates,) = _proj(a, wl[:, o_gate:o_end], (d,), (BF16,), tn=d)

        ya = _swa(q_swa, kv_swa, swa_sink[layer].astype(F32), n_batch, seq, ctx_len)

        lp = {'rwkv_mu_prev': rwkv_mu_prev[layer], 'rwkv_mu_next': rwkv_mu_next[layer],
              'rwkv_w0': rwkv_w0[layer], 'rwkv_w_up': rwkv_w_up[layer], 'rwkv_a0': rwkv_a0[layer],
              'rwkv_a_up': rwkv_a_up[layer], 'rwkv_g_up': rwkv_g_up[layer], 'rwkv_k_k': rwkv_k_k[layer],
              'rwkv_k_a': rwkv_k_a[layer], 'rwkv_r_k': rwkv_r_k[layer]}
        r_, v_, kk_, ld_, ke_, bb_, g_, bonus_ = _rwkv_prep(p_rwkv, lp, n_batch, seq, ctx_len)
        y_f, y_b = _rwkv_scan(r_, v_, kk_, ld_, ke_, bb_, n_batch, seq, ctx_len)

        lam_vec = diff_lambda[layer].astype(F32)
        lam_init = 0.8 - 0.6 * math.exp(-0.3 * layer)
        lam = (jnp.exp(jnp.sum(lam_vec[0] * lam_vec[1])) - jnp.exp(jnp.sum(lam_vec[2] * lam_vec[3]))
               + lam_init).reshape(1)
        yc = _diff(q_diff, k_diff, v_diff, lam, row(diff_subln_g[layer]), lam_init, n_batch, seq, ctx_len)

        moe_layer = layer % 2 == 1
        jj = layer // 2
        merged = _merge(ya, y_f, y_b, bonus_, g_, rwkv_lnx_g[layer], rwkv_lnx_b[layer], yc, gates, h,
                        _swa_head_order(proj_swa[layer], axis=0).astype(BF16), proj_rwkv[layer].astype(BF16),
                        proj_diff[layer].astype(BF16), w_out[layer].astype(BF16), row(post_mix_g[layer]),
                        row(pre_ffn_g[layer]), mods, F32 if moe_layer else BF16, lat_bpb, n_batch,
                        router_w=router_w[jj] if moe_layer else None)
        h, f = merged[:2]
        need_ctx = layer < depth - 1
        if not moe_layer:
            h = _ffn(f, ffn_w_gate[jj].astype(BF16), ffn_w_up[jj].astype(BF16), ffn_w_down[jj].astype(BF16),
                     h, row(post_ffn_g[layer]), mods, lat_bpb, n_batch)
        else:
            n_tok = m if need_ctx else lat_rows
            top_i, top_w = merged[2][:n_tok], merged[3][:n_tok]
            tok, blk_e, nused, dest, n_blocks = _moe_slots(top_i[:, :2], MOE_TILE)
            y = _moe_ffn(f, blk_e, nused, tok, moe_w_gate[jj].astype(BF16), moe_w_up[jj].astype(BF16),
                         moe_w_down[jj].astype(BF16), n_blocks)
            h = _moe_combine(dest[:, 0], dest[:, 1], y, top_w, h, row(post_ffn_g[layer]), mods, n_tok, seq,
                             n_batch)
    return h[:lat_rows].reshape(n_batch, seq, d)
```

```python
import functools
import math

import jax
import jax.numpy as jnp
from jax import lax
from jax.experimental import pallas as pl
from jax.experimental.pallas import tpu as pltpu

F32 = jnp.float32
BF16 = jnp.bfloat16
HI = lax.Precision.HIGHEST

HEAD_DIM = 64
GRID_W = 64
ROPE_THETA = 10000.0
NORM_EPS = 1e-6
NEG_INF = -1e30
SWA_HEADS = 8
SWA_KV_HEADS = 2
SWA_BLOCK = 128
RWKV_HEADS = 8
RWKV_WIDTH = RWKV_HEADS * HEAD_DIM
DECAY_LORA = 64
AAA_LORA = 64
GATE_LORA = 128
RWKV_GN_EPS = 64e-5
DIFF_HEADS = 4
DIFF_V_DIM = 2 * HEAD_DIM
DIFF_SUBLN_EPS = 1e-5
N_EXPERTS = 8
SWA_Q = SWA_HEADS * HEAD_DIM
SWA_KV = SWA_KV_HEADS * HEAD_DIM
assert SWA_KV_HEADS == 2 and SWA_KV == 128
RWKV_COLS = 3 * RWKV_WIDTH + 2 * DECAY_LORA + 2 * AAA_LORA + GATE_LORA
DIFF_QK = DIFF_HEADS * 2 * HEAD_DIM
DIFF_V = DIFF_HEADS * DIFF_V_DIM

LANES = 128
VMEM_LIMIT = 48 * 1024 * 1024
ROW_TILE = 512
RWKV_CHUNK = 64
RWKV_INV_BASE = 8
RWKV_GROUP = 2
RWKV_STEP_ROWS = 256
DIFF_TQ = 1024
DIFF_TK = 512
DIFF_UNROLL = 2
DIFF_ONES_ROWS = 16
MOE_TILE = 512
MOE_TF = 1792
FFN_TF = 1408
COMBINE_TILE = 256


def _cp(sem, **kw):
    return pltpu.CompilerParams(dimension_semantics=sem, vmem_limit_bytes=VMEM_LIMIT, **kw)


def _seg_of_block(i, lat_blocks_per_batch, n_batch):
    return jnp.minimum(i // lat_blocks_per_batch, n_batch)


def _rms(x, eps):
    return x * lax.rsqrt(jnp.mean(x * x, axis=-1, keepdims=True) + eps)


def _sigmoid(x):
    return 1.0 / (1.0 + jnp.exp(-x))


def _group_sum(x, ones_bd):
    hi = x.astype(BF16)
    mid = (x - hi.astype(F32)).astype(BF16)
    rows = x.shape[0]
    parts = jnp.dot(jnp.concatenate([hi, mid], axis=0), ones_bd, preferred_element_type=F32)
    return parts[:rows] + parts[rows:]


def _ada_kernel(x_ref, w_ref, b_ref, o_ref):
    x = x_ref[...]
    s = x * _sigmoid(x)
    o_ref[...] = jnp.dot(s, w_ref[...], precision=HI, preferred_element_type=F32) + b_ref[...]


def _ada(cond, w, b):
    rows, d = cond.shape
    n = w.shape[1]
    return pl.pallas_call(
        _ada_kernel,
        out_shape=jax.ShapeDtypeStruct((rows, n), F32),
        grid=(n // d,),
        in_specs=[pl.BlockSpec((rows, d), lambda j: (0, 0)),
                  pl.BlockSpec((d, d), lambda j: (0, j)),
                  pl.BlockSpec((1, d), lambda j: (0, j))],
        out_specs=pl.BlockSpec((rows, d), lambda j: (0, j)),
        compiler_params=_cp(("parallel",)),
        name="ada_mod",
    )(cond, w, b)


def _prenorm_kernel(h_ref, g_ref, mod_ref, o_ref, *, d, shift_idx, scale_idx):
    y = _rms(h_ref[...], NORM_EPS) * g_ref[...]
    shift = mod_ref[0, :, shift_idx * d:(shift_idx + 1) * d]
    scale = mod_ref[0, :, scale_idx * d:(scale_idx + 1) * d]
    o_ref[...] = (y * (1.0 + scale) + shift).astype(o_ref.dtype)


def _prenorm(h, g, mods, shift_idx, scale_idx, lat_bpb, n_batch):
    m, d = h.shape
    tm = ROW_TILE
    seg = functools.partial(_seg_of_block, lat_blocks_per_batch=lat_bpb, n_batch=n_batch)
    return pl.pallas_call(
        functools.partial(_prenorm_kernel, d=d, shift_idx=shift_idx, scale_idx=scale_idx),
        out_shape=jax.ShapeDtypeStruct((m, d), BF16),
        grid=(m // tm,),
        in_specs=[pl.BlockSpec((tm, d), lambda i: (i, 0)),
                  pl.BlockSpec((1, d), lambda i: (0, 0)),
                  pl.BlockSpec((1, 1, mods.shape[2]), lambda i: (seg(i), 0, 0))],
        out_specs=pl.BlockSpec((tm, d), lambda i: (i, 0)),
        compiler_params=_cp(("parallel",)),
        name="prenorm",
    )(h, g, mods)


def _proj_kernel(*refs, splits, rope_cols, scale_cols, q_scale):
    if rope_cols:
        a_ref, w_ref, cos_ref, sin_ref = refs[:4]
        outs = refs[4:]
    else:
        a_ref, w_ref = refs[:2]
        outs = refs[2:]
    y = jnp.dot(a_ref[...], w_ref[...], preferred_element_type=F32)
    tm, tn = y.shape
    if rope_cols:
        cos = cos_ref[...]
        sin = sin_ref[...]
        lane = lax.broadcasted_iota(jnp.int32, (tm, LANES), 1)
        first_half = (lane % 32) < 16
        pieces = []
        for c in range(tn // LANES):
            yc = y[:, c * LANES:(c + 1) * LANES]
            if c * LANES < rope_cols:
                partner = jnp.where(first_half, pltpu.roll(yc, LANES - 16, 1), pltpu.roll(yc, 16, 1))
                yc = yc * cos + partner * sin
            if c * LANES < scale_cols:
                yc = yc * q_scale
            pieces.append(yc)
        y = jnp.concatenate(pieces, axis=1)
    start = 0
    for o_ref, width in zip(outs, splits):
        o_ref[...] = y[:, start:start + width].astype(o_ref.dtype)
        start += width


def _proj(a, w, splits, dtypes, rope=None, rope_cols=0, scale_cols=0, q_scale=1.0, tn=None):
    m, k = a.shape
    n = w.shape[1]
    tm = ROW_TILE
    tn = n if tn is None else tn
    assert sum(splits) == tn and (len(splits) == 1 or tn == n)
    in_specs = [pl.BlockSpec((tm, k), lambda i, j: (i, 0)),
                pl.BlockSpec((k, tn), lambda i, j: (0, j))]
    args = [a, w]
    if rope_cols:
        in_specs += [pl.BlockSpec((tm, LANES), lambda i, j: (i, 0))] * 2
        args += list(rope)
    out_specs = []
    out_shape = []
    if len(splits) == 1:
        out_specs.append(pl.BlockSpec((tm, tn), lambda i, j: (i, j)))
        out_shape.append(jax.ShapeDtypeStruct((m, n), dtypes[0]))
    else:
        for width, dt in zip(splits, dtypes):
            out_specs.append(pl.BlockSpec((tm, width), lambda i, j: (i, 0)))
            out_shape.append(jax.ShapeDtypeStruct((m, width), dt))
    res = pl.pallas_call(
        functools.partial(_proj_kernel, splits=tuple(splits), rope_cols=rope_cols, scale_cols=scale_cols,
                          q_scale=q_scale),
        out_shape=out_shape,
        grid=(m // tm, n // tn),
        in_specs=in_specs,
        out_specs=out_specs,
        compiler_params=_cp(("parallel", "parallel")),
        name="proj",
    )(*args)
    return res


def _swa_kernel(sink_ref, q_ref, kp_ref, kc_ref, kn_ref, kx_ref, o_ref, *, nb, n_lat_blocks):
    i = pl.program_id(0)
    is_lat = i < n_lat_blocks
    n = i % nb
    blk = SWA_BLOCK
    q = q_ref[...]
    kv = jnp.concatenate([kp_ref[...], kc_ref[...], kn_ref[...], kx_ref[...]], axis=0)
    nkeys = kv.shape[0]
    k_t = kv[:, :SWA_KV]
    v_aug = jnp.concatenate([kv[:, SWA_KV:], jnp.ones((nkeys, SWA_KV), kv.dtype)], axis=1)
    r = lax.broadcasted_iota(jnp.int32, (blk, nkeys), 0)
    j = lax.broadcasted_iota(jnp.int32, (blk, nkeys), 1)
    lo = jnp.where(n > 0, 0, blk)
    hi = jnp.where(n < nb - 1, 3 * blk, 2 * blk)
    valid_loc = (j >= r) & (j <= r + 2 * blk) & (j >= lo) & (j < hi) & is_lat
    bias = jnp.where(valid_loc | (j >= 3 * blk), 0.0, NEG_INF)
    low_half = lax.broadcasted_iota(jnp.int32, (blk, SWA_KV), 1) < HEAD_DIM
    zero = jnp.zeros((blk, SWA_KV), q.dtype)
    group = SWA_HEADS // SWA_KV_HEADS
    q_rows, sinks = [], []
    for t in range(group):
        qt = q[:, t * SWA_KV:(t + 1) * SWA_KV]
        q_rows += [jnp.where(low_half, qt, zero), jnp.where(low_half, zero, qt)]
        sinks += [jnp.full((blk, 1), sink_ref[t], F32), jnp.full((blk, 1), sink_ref[group + t], F32)]
    scores = [lax.dot_general(jnp.concatenate(q_rows[2 * t:2 * t + 2], axis=0), k_t, (((1,), (1,)), ((), ())),
                              preferred_element_type=F32) for t in range(group)]
    tiles = []
    for t in range(group):
        sk = jnp.concatenate(sinks[2 * t:2 * t + 2], axis=0)
        s = (scores[t].reshape(2, blk, nkeys) + bias[None]).reshape(2 * blk, nkeys)
        mx = jnp.maximum(jnp.max(s, axis=-1, keepdims=True), sk)
        p = jnp.exp((s - mx).astype(BF16))
        oa = jnp.dot(p, v_aug, preferred_element_type=F32)
        den = oa[:, SWA_KV:] + jnp.exp(sk - mx)
        on = oa[:, :SWA_KV] * (1.0 / den)
        tiles.append(jnp.where(low_half, on[:blk], on[blk:]))
    o_ref[...] = jnp.concatenate(tiles, axis=1).astype(o_ref.dtype)


def _swa(q, kv, sink, n_batch, seq, ctx_len):
    m = q.shape[0]
    blk = SWA_BLOCK
    nb = seq // blk
    n_lat = n_batch * nb
    cpb = ctx_len // blk

    def batch_of(i):
        return jnp.where(i < n_lat, i // nb, (i - n_lat) // cpb)

    def prev_idx(i, s):
        return (jnp.where(i < n_lat, batch_of(i) * nb + jnp.maximum(i % nb - 1, 0), i), 0)

    def next_idx(i, s):
        return (jnp.where(i < n_lat, batch_of(i) * nb + jnp.minimum(i % nb + 1, nb - 1), i), 0)

    def ctx_idx(i, s):
        return (n_batch * seq // ctx_len + batch_of(i), 0)

    grid_spec = pltpu.PrefetchScalarGridSpec(
        num_scalar_prefetch=1,
        grid=(m // blk,),
        in_specs=[pl.BlockSpec((blk, SWA_Q), lambda i, s: (i, 0)),
                  pl.BlockSpec((blk, 2 * SWA_KV), prev_idx),
                  pl.BlockSpec((blk, 2 * SWA_KV), lambda i, s: (i, 0)),
                  pl.BlockSpec((blk, 2 * SWA_KV), next_idx),
                  pl.BlockSpec((ctx_len, 2 * SWA_KV), ctx_idx)],
        out_specs=pl.BlockSpec((blk, SWA_Q), lambda i, s: (i, 0)),
    )
    return pl.pallas_call(
        functools.partial(_swa_kernel, nb=nb, n_lat_blocks=n_lat),
        out_shape=jax.ShapeDtypeStruct((m, SWA_Q), BF16),
        grid_spec=grid_spec,
        compiler_params=_cp(("parallel",)),
        name="swa_attn",
    )(sink, q, kv, kv, kv, kv)


def _diff_kernel(*refs, n_lat_chunks, coef):
    if n_lat_chunks:
        lam_ref, gcol_ref, q_ref, kc_ref, vtc_ref, kl_ref, vtl_ref, o_ref, m_sc, acc_sc, st_a, st_b = refs
    else:
        lam_ref, gcol_ref, q_ref, kc_ref, vtc_ref, o_ref, m_sc, acc_sc, st_a, st_b = refs
    q = q_ref[...]
    tq = q.shape[0]
    dv = DIFF_V_DIM
    lane = lax.broadcasted_iota(jnp.int32, q.shape, 1)
    zero = jnp.zeros_like(q)
    qq = jnp.concatenate([jnp.where(lane < HEAD_DIM, q, zero), jnp.where(lane >= HEAD_DIM, q, zero)], axis=0)

    def scores(k):
        return lax.dot_general(k, qq, (((1,), (1,)), ((), ())), preferred_element_type=F32)

    def accumulate(st_ref, vt):
        st = st_ref[0:vt.shape[1], :]
        m_old = m_sc[...]
        m_new = jnp.maximum(m_old, jnp.max(st, axis=0, keepdims=True))
        alpha = jnp.exp2(m_old - m_new)
        pt = jnp.exp2((st - m_new).astype(BF16))
        acc_sc[...] = alpha * acc_sc[...] + jnp.dot(vt, pt, preferred_element_type=F32)
        m_sc[...] = m_new

    tk = st_a.shape[0]
    k_lat = lambda c: kl_ref[pl.ds(pl.multiple_of(c * tk, tk), tk), :]
    m_sc[...] = jnp.full(m_sc.shape, NEG_INF, F32)
    acc_sc[...] = jnp.zeros(acc_sc.shape, F32)
    bufs = (st_a, st_b)
    n = n_lat_chunks
    if n:
        unroll = DIFF_UNROLL
        st_a[...] = scores(k_lat(0))
        n_trips = (n - 1) // unroll

        def body(j, carry):
            for u in range(unroll):
                c = j * unroll + u
                bufs[(u + 1) % 2][...] = scores(k_lat(c + 1))
                accumulate(bufs[u % 2], vtl_ref[0, c])
            return carry

        lax.fori_loop(0, n_trips, body, 0)
        for c in range(n_trips * unroll, n):
            if c + 1 < n:
                bufs[(c + 1) % 2][...] = scores(k_lat(c + 1))
            else:
                bufs[(c + 1) % 2][0:kc_ref.shape[0], :] = scores(kc_ref[...])
            accumulate(bufs[c % 2], vtl_ref[0, c])
    else:
        st_a[0:kc_ref.shape[0], :] = scores(kc_ref[...])
    accumulate(bufs[n % 2], vtc_ref[0, 0])
    acc = acc_sc[...]
    ot = acc[:dv] * (1.0 / acc[dv:dv + 1])
    odt = ot[:, :tq] - lam_ref[0] * ot[:, tq:]
    ms = jnp.mean(odt * odt, axis=0, keepdims=True)
    yt = odt * lax.rsqrt(ms + DIFF_SUBLN_EPS) * (gcol_ref[...] * coef)
    o_ref[...] = yt.T.astype(o_ref.dtype)


def _diff(q, k, v, lam, subln_g, lam_init, n_batch, seq, ctx_len):
    m = q.shape[0]
    tk = min(DIFF_TK, seq)
    assert seq % tk == 0 and tk % ctx_len == 0
    dv = DIFF_V_DIM
    dva = dv + DIFF_ONES_ROWS
    lat_rows = n_batch * seq

    def transposed_chunks(rows, size):
        t = rows.reshape(rows.shape[0] // size, size, DIFF_HEADS, dv).transpose(2, 0, 3, 1)
        return jnp.concatenate([t, jnp.ones(t.shape[:2] + (DIFF_ONES_ROWS, size), t.dtype)], axis=2)

    vt_lat = transposed_chunks(v[:lat_rows], tk)
    vt_ctx = transposed_chunks(v[lat_rows:], ctx_len)
    gcol = subln_g.reshape(dv, 1)
    lat_chunks = seq // tk
    ctx0 = n_batch * seq // ctx_len

    def call(tq, n_q, q_block0, batch_of, with_lat):
        in_specs = [pl.BlockSpec((dv, 1), lambda h, i, s: (0, 0)),
                    pl.BlockSpec((tq, LANES), lambda h, i, s: (q_block0 + i, h)),
                    pl.BlockSpec((ctx_len, LANES), lambda h, i, s: (ctx0 + batch_of(i), h)),
                    pl.BlockSpec((1, 1, dva, ctx_len), lambda h, i, s: (h, batch_of(i), 0, 0))]
        args = [lam, gcol, q, k, vt_ctx]
        if with_lat:
            in_specs += [pl.BlockSpec((seq, LANES), lambda h, i, s: (batch_of(i), h)),
                         pl.BlockSpec((1, lat_chunks, dva, tk), lambda h, i, s: (h, batch_of(i), 0, 0))]
            args += [k, vt_lat]
        grid_spec = pltpu.PrefetchScalarGridSpec(
            num_scalar_prefetch=1,
            grid=(DIFF_HEADS, n_q),
            in_specs=in_specs,
            out_specs=pl.BlockSpec((tq, LANES), lambda h, i, s: (i, h)),
            scratch_shapes=[pltpu.VMEM((1, 2 * tq), F32), pltpu.VMEM((dva, 2 * tq), F32),
                            pltpu.VMEM((tk, 2 * tq), F32), pltpu.VMEM((tk, 2 * tq), F32)],
        )
        return pl.pallas_call(
            functools.partial(_diff_kernel, n_lat_chunks=lat_chunks if with_lat else 0, coef=1.0 - lam_init),
            out_shape=jax.ShapeDtypeStruct((n_q * tq, DIFF_V), BF16),
            grid_spec=grid_spec,
            compiler_params=_cp(("parallel", "arbitrary")),
            name="diff_attn" if with_lat else "diff_attn_ctx",
        )(*args)

    tq = DIFF_TQ
    y_lat = call(tq, n_batch * seq // tq, 0, lambda i: i // (seq // tq), True)
    y_ctx = call(ctx_len, n_batch, ctx0, lambda i: i, False)
    return jnp.concatenate([y_lat, y_ctx], axis=0)


def _rwkv_prep_kernel(p_ref, hp_ref, hn_ref, mup_ref, mun_ref, kk_w_ref, ka_ref, rk_ref, w0_ref, a0_ref,
                      wup_ref, aup_ref, gup_ref, ones_ref,
                      r_ref, v_ref, kk_ref, ld_ref, ke_ref, bb_ref, g_ref, bonus_ref, sc,
                      *, tm, lat_rows, seq, ctx_len):
    i = pl.program_id(0)
    w = RWKV_WIDTH
    sc[0:8, :] = hp_ref[...]
    sc[8:8 + tm, :] = p_ref[...]
    sc[8 + tm:16 + tm, :] = hn_ref[...]
    p = p_ref[...]
    prev = sc[7:7 + tm, :]
    nxt = sc[9:9 + tm, :]
    row = i * tm + lax.broadcasted_iota(jnp.int32, (tm, 1), 0)
    pos = jnp.where(row < lat_rows, row % seq, (row - lat_rows) % ctx_len)
    seg_len = jnp.where(row < lat_rows, seq, ctx_len)
    prev = jnp.where(pos == 0, 0.0, prev)
    nxt = jnp.where(pos == seg_len - 1, 0.0, nxt)
    ps = p + mup_ref[...] * (prev - p) + mun_ref[...] * (nxt - p)

    r = ps[:, 0:w]
    k = ps[:, w:2 * w]
    v = ps[:, 2 * w:3 * w]
    wd = ps[:, 3 * w:3 * w + 2 * DECAY_LORA]
    ad = ps[:, 3 * w + 2 * DECAY_LORA:3 * w + 2 * DECAY_LORA + 2 * AAA_LORA]
    gd = ps[:, 3 * w + 2 * DECAY_LORA + 2 * AAA_LORA:]

    ones_bd = ones_ref[...]
    g = jnp.dot(_sigmoid(gd).astype(BF16), gup_ref[...], preferred_element_type=F32)
    kk = k * kk_w_ref[...]
    ss = _group_sum(kk * kk, ones_bd)
    kk = kk / jnp.maximum(jnp.sqrt(ss), 1e-12)
    w_raw = w0_ref[...] + jnp.dot(jnp.tanh(wd).astype(BF16), wup_ref[...], preferred_element_type=F32)
    a_raw = a0_ref[...] + jnp.dot(ad.astype(BF16), aup_ref[...], preferred_element_type=F32)
    ld = -math.exp(-0.5) * _sigmoid(w_raw)
    a = _sigmoid(a_raw)
    ka = ka_ref[...]
    ke_sum = jnp.zeros_like(k)
    for d in range(2):
        a_d = a[:, d * w:(d + 1) * w]
        ke = k * (1.0 + (a_d - 1.0) * ka)
        ld_ref[d] = ld[:, d * w:(d + 1) * w]
        ke_ref[d] = ke
        bb_ref[d] = kk * a_d
        ke_sum = ke_sum + ke
    rk = _group_sum(r * ke_sum * rk_ref[...], ones_bd)
    r_ref[...] = r
    v_ref[...] = v
    kk_ref[...] = kk
    g_ref[...] = g
    bonus_ref[...] = rk * v


def _rwkv_prep(p, lp, n_batch, seq, ctx_len):
    m, cols = p.shape
    tm = 256
    w = RWKV_WIDTH
    lat_rows = n_batch * seq
    row = lambda a: a.reshape(1, -1).astype(F32)

    def blockdiag(u):
        z = jnp.zeros_like(u[0])
        return jnp.concatenate([jnp.concatenate([u[0], z], axis=1), jnp.concatenate([z, u[1]], axis=1)], axis=0)

    head = jnp.arange(w) // HEAD_DIM
    ones_bd = (head[:, None] == head[None, :]).astype(BF16)
    full = lambda shape: pl.BlockSpec(shape, lambda i: (0,) * len(shape))
    nb8 = m // 8
    outs = pl.pallas_call(
        functools.partial(_rwkv_prep_kernel, tm=tm, lat_rows=lat_rows, seq=seq, ctx_len=ctx_len),
        out_shape=[jax.ShapeDtypeStruct((m, w), F32)] * 3
        + [jax.ShapeDtypeStruct((2, m, w), F32)] * 3
        + [jax.ShapeDtypeStruct((m, w), F32)] * 2,
        grid=(m // tm,),
        in_specs=[pl.BlockSpec((tm, cols), lambda i: (i, 0)),
                  pl.BlockSpec((8, cols), lambda i: (jnp.maximum(i * (tm // 8) - 1, 0), 0)),
                  pl.BlockSpec((8, cols), lambda i: (jnp.minimum((i + 1) * (tm // 8), nb8 - 1), 0)),
                  full((1, cols)), full((1, cols)), full((1, w)), full((1, w)), full((1, w)),
                  full((1, 2 * w)), full((1, 2 * w)),
                  full((2 * DECAY_LORA, 2 * w)), full((2 * AAA_LORA, 2 * w)), full((GATE_LORA, w)),
                  full((w, w))],
        out_specs=[pl.BlockSpec((tm, w), lambda i: (i, 0))] * 3
        + [pl.BlockSpec((2, tm, w), lambda i: (0, i, 0))] * 3
        + [pl.BlockSpec((tm, w), lambda i: (i, 0))] * 2,
        scratch_shapes=[pltpu.VMEM((tm + 16, cols), F32)],
        compiler_params=_cp(("parallel",)),
        name="rwkv_prep",
    )(p, p, p, row(lp['rwkv_mu_prev']), row(lp['rwkv_mu_next']), row(lp['rwkv_k_k']), row(lp['rwkv_k_a']),
      row(lp['rwkv_r_k']), row(lp['rwkv_w0']), row(lp['rwkv_a0']),
      blockdiag(lp['rwkv_w_up']).astype(BF16), blockdiag(lp['rwkv_a_up']).astype(BF16),
      lp['rwkv_g_up'].astype(BF16), ones_bd)
    return outs


def _rwkv_chunk_prep(d, off, r_ref, v_ref, kk_ref, ld_ref, ke_ref, bb_ref, incl):
    c = RWKV_CHUNK
    ld = ld_ref[0, pl.ds(off, c), :]
    r = r_ref[pl.ds(off, c), :]
    v = v_ref[pl.ds(off, c), :]
    kk = kk_ref[pl.ds(off, c), :]
    ke = ke_ref[0, pl.ds(off, c), :]
    bb = bb_ref[0, pl.ds(off, c), :]
    cum = jnp.dot(incl, ld, precision=HI, preferred_element_type=F32)
    tot = jnp.sum(ld, axis=0, keepdims=True)
    rt = r * jnp.exp(cum)
    einv = jnp.exp(-cum)
    etail = jnp.exp(tot - cum)
    return dict(at=(-kk * jnp.exp(cum - ld)).astype(BF16), rt=rt, rt_b=rt.astype(BF16),
                bt=(bb * einv).astype(BF16), kt=(ke * einv).astype(BF16),
                bh=(bb * etail).astype(BF16), kh=(ke * etail).astype(BF16),
                v=v, v_b=v.astype(BF16), wtot=jnp.exp(tot))


def _rwkv_chunk_group(offs, in_refs, y_refs, s_sc):
    c = RWKV_CHUNK
    hd = HEAD_DIM
    n_group = len(offs[0])
    ti = lax.broadcasted_iota(jnp.int32, (c, c), 0)
    tj = lax.broadcasted_iota(jnp.int32, (c, c), 1)
    eye = (ti == tj).astype(F32)
    incl = [(tj <= ti).astype(F32), (tj >= ti).astype(F32)]
    strict = [m - eye for m in incl]
    diag_mask = ((ti // RWKV_INV_BASE) == (tj // RWKV_INV_BASE)).astype(F32)
    off_masks = []
    sz = RWKV_INV_BASE
    while sz < c:
        off_masks.append((((ti // (2 * sz)) == (tj // (2 * sz))) & ((ti // sz) != (tj // sz))).astype(F32))
        sz *= 2

    pre = {(d, g): _rwkv_chunk_prep(d, offs[d][g], *in_refs[d], incl[d])
           for d in range(2) for g in range(n_group)}
    lanes = [(d, g, h) for g in range(n_group) for d in range(2) for h in range(RWKV_HEADS)]
    sl = lambda h: slice(h * hd, (h + 1) * hd)
    get = lambda name: [pre[d, g][name][:, sl(h)] for d, g, h in lanes]
    at, rt, rt_b, bt, kt, bh, kh, v, v_b = (get(n) for n in ('at', 'rt', 'rt_b', 'bt', 'kt', 'bh', 'kh', 'v', 'v_b'))
    nl = range(len(lanes))
    nt = (((1,), (1,)), ((), ()))
    bdot = lambda x, y: jnp.dot(x.astype(BF16), y.astype(BF16), preferred_element_type=F32)

    gm = [lax.dot_general(jnp.concatenate([at[i], rt_b[i]], axis=0), jnp.concatenate([bt[i], kt[i]], axis=0), nt,
                          preferred_element_type=F32) for i in nl]
    zz0 = [bdot(v[i].T, kh[i]) for i in nl]
    aab = [gm[i][:c, :c] * strict[lanes[i][0]] for i in nl]
    aak = [gm[i][:c, c:] * strict[lanes[i][0]] for i in nl]
    arb = [gm[i][c:, :c] * incl[lanes[i][0]] for i in nl]
    ark = [gm[i][c:, c:] * incl[lanes[i][0]] for i in nl]
    av = [bdot(jnp.concatenate([aak[i], ark[i]], axis=0), v_b[i]) for i in nl]
    pw = [aab[i] * diag_mask for i in nl]
    tm_ = [eye + pw[i] for i in nl]
    for _ in range(int(math.log2(RWKV_INV_BASE)) - 1):
        pw = [bdot(pw[i], pw[i]) for i in nl]
        tm_ = [tm_[i] + bdot(tm_[i], pw[i]) for i in nl]
    for off_mask in off_masks:
        tn = [bdot(tm_[i], aab[i] * off_mask) for i in nl]
        tm_ = [tm_[i] + bdot(tn[i], tm_[i]) for i in nl]
    au = [bdot(tm_[i], jnp.concatenate([at[i], av[i][:c].astype(BF16)], axis=1)) for i in nl]
    ry = [bdot(arb[i], au[i]) for i in nl]
    mz = [bdot(au[i].T, bh[i]) for i in nl]
    rbar = [(rt[i] + ry[i][:, :hd]).astype(BF16) for i in nl]
    ybar = [ry[i][:, hd:] + av[i][c:] for i in nl]
    mm = [(eye * pre[lanes[i][0], lanes[i][1]]['wtot'][:, sl(lanes[i][2])] + mz[i][:hd]).astype(BF16) for i in nl]
    zz = [mz[i][hd:] + zz0[i] for i in nl]
    state = {(d, h): s_sc[d, h] for d in range(2) for h in range(RWKV_HEADS)}
    ys = {}
    for g in range(n_group):
        idx = [i for i in nl if lanes[i][1] == g]
        s_b = {i: state[lanes[i][0], lanes[i][2]].astype(BF16) for i in idx}
        for i in idx:
            ys[i] = lax.dot_general(rbar[i], s_b[i], nt, preferred_element_type=F32) + ybar[i]
        for i in idx:
            state[lanes[i][0], lanes[i][2]] = jnp.dot(s_b[i], mm[i], preferred_element_type=F32) + zz[i]
    for (d, h), val in state.items():
        s_sc[d, h] = val
    for d in range(2):
        for g in range(n_group):
            y_refs[d][0, pl.ds(offs[d][g], c), :] = jnp.concatenate(
                [ys[i] for i in nl if lanes[i][0] == d and lanes[i][1] == g], axis=1)


def _rwkv_scan_kernel(rf, vf, kf, ldf, kef, bbf, rb, vb, kb, ldb, keb, bbb, yf, yb, s_sc, *, n_chunks):
    @pl.when(pl.program_id(1) == 0)
    def _():
        s_sc[...] = jnp.zeros(s_sc.shape, F32)

    group = RWKV_GROUP

    def body(t, carry):
        off_f = [pl.multiple_of((t * group + g) * RWKV_CHUNK, RWKV_CHUNK) for g in range(group)]
        off_b = [pl.multiple_of((n_chunks - 1 - t * group - g) * RWKV_CHUNK, RWKV_CHUNK) for g in range(group)]
        _rwkv_chunk_group((off_f, off_b), ((rf, vf, kf, ldf, kef, bbf), (rb, vb, kb, ldb, keb, bbb)),
                          (yf, yb), s_sc)
        return carry

    lax.fori_loop(0, n_chunks // group, body, 0)


def _rwkv_scan(r, v, kk, ld, ke, bb, n_batch, seq, ctx_len):
    m, w = r.shape
    ts = RWKV_STEP_ROWS
    assert ctx_len == ts
    lpb = seq // ts
    ctx0 = n_batch * seq // ts
    nj = 1 + lpb

    def fwd(b, j):
        return jnp.where(j == 0, ctx0 + b, b * lpb + j - 1)

    def bwd(b, j):
        return jnp.where(j == 0, ctx0 + b, b * lpb + lpb - j)

    shared = lambda f: pl.BlockSpec((ts, w), lambda b, j: (f(b, j), 0))
    per_dir = lambda f, d: pl.BlockSpec((1, ts, w), lambda b, j: (d, f(b, j), 0))
    y = pl.pallas_call(
        functools.partial(_rwkv_scan_kernel, n_chunks=ts // RWKV_CHUNK),
        out_shape=[jax.ShapeDtypeStruct((1, m, w), F32)] * 2,
        grid=(n_batch, nj),
        in_specs=[shared(fwd), shared(fwd), shared(fwd), per_dir(fwd, 0), per_dir(fwd, 0), per_dir(fwd, 0),
                  shared(bwd), shared(bwd), shared(bwd), per_dir(bwd, 1), per_dir(bwd, 1), per_dir(bwd, 1)],
        out_specs=[pl.BlockSpec((1, ts, w), lambda b, j: (0, fwd(b, j), 0)),
                   pl.BlockSpec((1, ts, w), lambda b, j: (0, bwd(b, j), 0))],
        scratch_shapes=[pltpu.VMEM((2, RWKV_HEADS, HEAD_DIM, HEAD_DIM), F32)],
        compiler_params=_cp(("parallel", "arbitrary")),
        name="rwkv_scan",
    )(r, v, kk, ld, ke, bb, r, v, kk, ld, ke, bb)
    return y


def _rwkv_finish(y, bonus, g, lnx_g, lnx_b, ones_bd):
    inv = 1.0 / HEAD_DIM
    mu = _group_sum(y, ones_bd) * inv
    yc = y - mu
    var = _group_sum(yc * yc, ones_bd) * inv
    yn = yc * lax.rsqrt(var + RWKV_GN_EPS) * lnx_g + lnx_b
    return (yn + bonus) * g


def _top2(logits):
    lane = lax.broadcasted_iota(jnp.int32, logits.shape, 1)
    logits = jnp.where(lane < N_EXPERTS, logits, -jnp.inf)
    m1 = jnp.max(logits, axis=-1, keepdims=True)
    i1 = jnp.min(jnp.where(logits == m1, lane, LANES), axis=-1, keepdims=True)
    rest = jnp.where(lane == i1, -jnp.inf, logits)
    m2 = jnp.max(rest, axis=-1, keepdims=True)
    i2 = jnp.min(jnp.where(rest == m2, lane, LANES), axis=-1, keepdims=True)
    e = jnp.exp(m2 - m1)
    w1 = 1.0 / (1.0 + e)
    w2 = e / (1.0 + e)
    idx = jnp.where(lane == 0, i1, jnp.where(lane == 1, i2, 0))
    wts = jnp.where(lane == 0, w1, jnp.where(lane == 1, w2, 0.0))
    return idx, wts


def _merge_kernel(*refs, d, with_router):
    (ya_ref, yf_ref, yb_ref, bonus_ref, g_ref, lg_ref, lb_ref, ones_ref, yc_ref, gt_ref, h_ref,
     pa_ref, pb_ref, pc_ref, wo_ref, gpost_ref, gpre_ref, mod_ref) = refs[:18]
    if with_router:
        rwh_ref, rwl_ref, h_out, f_out, idx_out, wt_out = refs[18:]
    else:
        h_out, f_out = refs[18:]
    yb = _rwkv_finish(yf_ref[0] + yb_ref[0], bonus_ref[...], g_ref[...], lg_ref[...], lb_ref[...], ones_ref[...])
    gates = gt_ref[...].astype(F32)
    merged = (_sigmoid(gates[:, 0:d]) * jnp.dot(ya_ref[...], pa_ref[...], preferred_element_type=F32)
              + _sigmoid(gates[:, d:2 * d]) * jnp.dot(yb.astype(BF16), pb_ref[...], preferred_element_type=F32)
              + _sigmoid(gates[:, 2 * d:3 * d]) * jnp.dot(yc_ref[...], pc_ref[...], preferred_element_type=F32))
    out = jnp.dot(merged.astype(BF16), wo_ref[...], preferred_element_type=F32)
    mod = lambda idx: mod_ref[0, :, idx * d:(idx + 1) * d]
    hn = h_ref[...] + mod(2) * (_rms(out, NORM_EPS) * gpost_ref[...])
    h_out[...] = hn
    f = (_rms(hn, NORM_EPS) * gpre_ref[...]) * (1.0 + mod(4)) + mod(3)
    f_out[...] = f.astype(f_out.dtype)
    if with_router:
        rows = f.shape[0]
        hi = f.astype(BF16)
        mid = (f - hi.astype(F32)).astype(BF16)
        part = jnp.dot(jnp.concatenate([hi, mid], axis=0), rwh_ref[...], preferred_element_type=F32)
        logits = part[:rows] + part[rows:] + jnp.dot(hi, rwl_ref[...], preferred_element_type=F32)
        idx, wts = _top2(logits)
        idx_out[...] = idx
        wt_out[...] = wts


def _merge(ya, yf, yb, bonus, g, lnx_g, lnx_b, yc, gates, h, pa, pb, pc, wo, g_post, g_pre, mods, f_dtype,
           lat_bpb, n_batch, router_w=None):
    m, d = h.shape
    w = bonus.shape[1]
    tm = 256
    bpb = lat_bpb * (ROW_TILE // tm)
    seg = functools.partial(_seg_of_block, lat_blocks_per_batch=bpb, n_batch=n_batch)
    rows = lambda width: pl.BlockSpec((tm, width), lambda i: (i, 0))
    rows3 = pl.BlockSpec((1, tm, w), lambda i: (0, i, 0))
    full = lambda a: pl.BlockSpec(a.shape, lambda i: (0, 0))
    head = jnp.arange(w) // HEAD_DIM
    ones_bd = (head[:, None] == head[None, :]).astype(BF16)
    args = [ya, yf, yb, bonus, g, lnx_g.reshape(1, w), lnx_b.reshape(1, w), ones_bd, yc, gates, h,
            pa, pb, pc, wo, g_post, g_pre, mods]
    in_specs = [rows(ya.shape[1]), rows3, rows3, rows(w), rows(w), full(args[5]), full(args[6]), full(ones_bd),
                rows(yc.shape[1]), rows(gates.shape[1]), rows(d), full(pa), full(pb), full(pc), full(wo),
                full(g_post), full(g_pre), pl.BlockSpec((1, 1, mods.shape[2]), lambda i: (seg(i), 0, 0))]
    out_shape = [jax.ShapeDtypeStruct((m, d), F32), jax.ShapeDtypeStruct((m, d), f_dtype)]
    out_specs = [rows(d), rows(d)]
    if router_w is not None:
        w_pad = jnp.zeros((d, LANES), F32).at[:, :N_EXPERTS].set(router_w)
        w_hi = w_pad.astype(BF16)
        w_lo = (w_pad - w_hi.astype(F32)).astype(BF16)
        args += [w_hi, w_lo]
        in_specs += [full(w_hi), full(w_lo)]
        out_shape += [jax.ShapeDtypeStruct((m, LANES), jnp.int32), jax.ShapeDtypeStruct((m, LANES), F32)]
        out_specs += [rows(LANES), rows(LANES)]
    return pl.pallas_call(
        functools.partial(_merge_kernel, d=d, with_router=router_w is not None),
        out_shape=out_shape,
        grid=(m // tm,),
        in_specs=in_specs,
        out_specs=out_specs,
        compiler_params=_cp(("parallel",)),
        name="merge",
    )(*args)


def _swiglu_hidden(x, wg, wu):
    hg = jnp.dot(x, wg, preferred_element_type=F32)
    hu = jnp.dot(x, wu, preferred_element_type=F32)
    return (hg * _sigmoid(hg) * hu).astype(BF16)


def _ffn_kernel(f_ref, wg_ref, wu_ref, wd_ref, h_ref, gpost_ref, mod_ref, o_ref, acc, *, d):
    j = pl.program_id(1)

    @pl.when(j == 0)
    def _():
        acc[...] = jnp.zeros(acc.shape, F32)

    hid = _swiglu_hidden(f_ref[...], wg_ref[...], wu_ref[...])
    acc[...] += jnp.dot(hid, wd_ref[...], preferred_element_type=F32)

    @pl.when(j == pl.num_programs(1) - 1)
    def _():
        gate = mod_ref[0, :, 5 * d:6 * d]
        o_ref[...] = h_ref[...] + gate * (_rms(acc[...], NORM_EPS) * gpost_ref[...])


def _ffn(f, wg, wu, wd, h, g_post, mods, lat_bpb, n_batch):
    m, d = h.shape
    ff = wg.shape[1]
    tm, tf = ROW_TILE, FFN_TF
    seg = functools.partial(_seg_of_block, lat_blocks_per_batch=lat_bpb, n_batch=n_batch)
    return pl.pallas_call(
        functools.partial(_ffn_kernel, d=d),
        out_shape=jax.ShapeDtypeStruct((m, d), F32),
        grid=(m // tm, ff // tf),
        in_specs=[pl.BlockSpec((tm, d), lambda i, j: (i, 0)),
                  pl.BlockSpec((d, tf), lambda i, j: (0, j)),
                  pl.BlockSpec((d, tf), lambda i, j: (0, j)),
                  pl.BlockSpec((tf, d), lambda i, j: (j, 0)),
                  pl.BlockSpec((tm, d), lambda i, j: (i, 0)),
                  pl.BlockSpec((1, d), lambda i, j: (0, 0)),
                  pl.BlockSpec((1, 1, mods.shape[2]), lambda i, j: (seg(i), 0, 0))],
        out_specs=pl.BlockSpec((tm, d), lambda i, j: (i, 0)),
        scratch_shapes=[pltpu.VMEM((tm, d), F32)],
        compiler_params=_cp(("parallel", "arbitrary")),
        name="ffn_dense",
    )(f, wg, wu, wd, h, g_post, mods)


def _moe_gather_copy(f_hbm, xbuf, sem, slot, src_row, dst_row):
    return pltpu.make_async_copy(f_hbm.at[pl.ds(src_row, 1), :], xbuf.at[slot, pl.ds(dst_row, 1), :],
                                 sem.at[slot])


def _moe_ffn_kernel(blk_e_ref, nused_ref, tok_ref, f_hbm, wg_ref, wu_ref, wd_ref, y_ref, xbuf, xb, acc, sem,
                    *, tm, nj):
    i = pl.program_id(0)
    j = pl.program_id(1)
    nused = nused_ref[0]
    active = i < nused
    slot = i % 2
    per_step = tm // nj

    @pl.when((i == 0) & (j == 0))
    def _():
        def issue(r, carry):
            _moe_gather_copy(f_hbm, xbuf, sem, 0, tok_ref[r], r).start()
            return carry

        lax.fori_loop(0, tm, issue, 0)

    @pl.when((i <= nused) & (j == 0))
    def _():
        pltpu.make_async_copy(f_hbm.at[pl.ds(0, tm), :], xbuf.at[slot], sem.at[slot]).wait()

    @pl.when(active & (j == 0))
    def _():
        xb[...] = xbuf[slot].astype(BF16)
        acc[...] = jnp.zeros(acc.shape, F32)

    @pl.when(active)
    def _():
        base = (i + 1) * tm + j * per_step
        for r in range(per_step):
            _moe_gather_copy(f_hbm, xbuf, sem, 1 - slot, tok_ref[base + r], j * per_step + r).start()
        hid = _swiglu_hidden(xb[...], wg_ref[0], wu_ref[0])
        acc[...] += jnp.dot(hid, wd_ref[0], preferred_element_type=F32)

    @pl.when(j == nj - 1)
    def _():
        y_ref[...] = acc[...]


def _moe_ffn(f, blk_e, nused, tok, wg, wu, wd, n_blocks):
    d = f.shape[1]
    ff = wg.shape[2]
    tm, tf = MOE_TILE, MOE_TF

    nj = ff // tf

    def e_of(i, be, nu):
        return be[jnp.minimum(i, nu[0] - 1)]

    def j_of(i, j, nu):
        return jnp.where(i < nu[0], j, nj - 1)

    grid_spec = pltpu.PrefetchScalarGridSpec(
        num_scalar_prefetch=3,
        grid=(n_blocks, nj),
        in_specs=[pl.BlockSpec(memory_space=pl.ANY),
                  pl.BlockSpec((1, d, tf), lambda i, j, be, nu, tk: (e_of(i, be, nu), 0, j_of(i, j, nu))),
                  pl.BlockSpec((1, d, tf), lambda i, j, be, nu, tk: (e_of(i, be, nu), 0, j_of(i, j, nu))),
                  pl.BlockSpec((1, tf, d), lambda i, j, be, nu, tk: (e_of(i, be, nu), j_of(i, j, nu), 0))],
        out_specs=pl.BlockSpec((tm, d), lambda i, j, be, nu, tk: (i, 0)),
        scratch_shapes=[pltpu.VMEM((2, tm, d), F32), pltpu.VMEM((tm, d), BF16), pltpu.VMEM((tm, d), F32),
                        pltpu.SemaphoreType.DMA((2,))],
    )
    return pl.pallas_call(
        functools.partial(_moe_ffn_kernel, tm=tm, nj=nj),
        out_shape=jax.ShapeDtypeStruct((n_blocks * tm, d), F32),
        grid_spec=grid_spec,
        compiler_params=_cp(("arbitrary", "arbitrary")),
        name="moe_ffn",
    )(blk_e, nused, tok, f, wg, wu, wd)


def _moe_combine_kernel(p0_ref, p1_ref, y_hbm, wt_ref, h_ref, gpost_ref, mod_ref, o_ref, b0, b1, sem, *, tm, d):
    i = pl.program_id(0)
    slot = i % 2

    def issue_block(blk, s):
        def issue(r, carry):
            pltpu.make_async_copy(y_hbm.at[pl.ds(p0_ref[blk * tm + r], 1), :], b0.at[s, pl.ds(r, 1), :],
                                  sem.at[0, s]).start()
            pltpu.make_async_copy(y_hbm.at[pl.ds(p1_ref[blk * tm + r], 1), :], b1.at[s, pl.ds(r, 1), :],
                                  sem.at[1, s]).start()
            return carry

        lax.fori_loop(0, tm, issue, 0)

    @pl.when(i == 0)
    def _():
        issue_block(0, 0)

    @pl.when(i + 1 < pl.num_programs(0))
    def _():
        issue_block(i + 1, 1 - slot)

    pltpu.make_async_copy(y_hbm.at[pl.ds(0, tm), :], b0.at[slot], sem.at[0, slot]).wait()
    pltpu.make_async_copy(y_hbm.at[pl.ds(0, tm), :], b1.at[slot], sem.at[1, slot]).wait()
    wt = wt_ref[...]
    y = b0[slot] * wt[:, 0:1] + b1[slot] * wt[:, 1:2]
    gate = mod_ref[0, :, 5 * d:6 * d]
    o_ref[...] = h_ref[...] + gate * (_rms(y, NORM_EPS) * gpost_ref[...])


def _moe_combine(pos0, pos1, y, wt, h, g_post, mods, n_rows, rows_per_batch, n_batch):
    d = h.shape[1]
    tm = COMBINE_TILE
    seg = functools.partial(_seg_of_block, lat_blocks_per_batch=rows_per_batch // tm, n_batch=n_batch)
    grid_spec = pltpu.PrefetchScalarGridSpec(
        num_scalar_prefetch=2,
        grid=(n_rows // tm,),
        in_specs=[pl.BlockSpec(memory_space=pl.ANY),
                  pl.BlockSpec((tm, LANES), lambda i, a, b: (i, 0)),
                  pl.BlockSpec((tm, d), lambda i, a, b: (i, 0)),
                  pl.BlockSpec((1, d), lambda i, a, b: (0, 0)),
                  pl.BlockSpec((1, 1, mods.shape[2]), lambda i, a, b: (seg(i), 0, 0))],
        out_specs=pl.BlockSpec((tm, d), lambda i, a, b: (i, 0)),
        scratch_shapes=[pltpu.VMEM((2, tm, d), F32), pltpu.VMEM((2, tm, d), F32),
                        pltpu.SemaphoreType.DMA((2, 2))],
    )
    return pl.pallas_call(
        functools.partial(_moe_combine_kernel, tm=tm, d=d),
        out_shape=jax.ShapeDtypeStruct((n_rows, d), F32),
        grid_spec=grid_spec,
        compiler_params=_cp(("arbitrary",)),
        name="moe_combine",
    )(pos0, pos1, y, wt, h, g_post, mods)


def _moe_slots(top_i, tile):
    n = top_i.shape[0]
    a = n * 2
    e_flat = top_i.reshape(a)
    onehot = (e_flat[:, None] == jnp.arange(N_EXPERTS, dtype=jnp.int32)[None, :]).astype(jnp.int32)
    sub = LANES
    blocks = onehot.reshape(a // sub, sub, N_EXPERTS).astype(F32)
    tri = (jnp.arange(sub)[:, None] >= jnp.arange(sub)[None, :]).astype(F32)
    within = jnp.einsum('ij,bjk->bik', tri, blocks).astype(jnp.int32)
    totals = within[:, -1, :]
    csum = (within + (jnp.cumsum(totals, axis=0) - totals)[:, None, :]).reshape(a, N_EXPERTS)
    rank = jnp.sum(csum * onehot, axis=1) - 1
    counts = csum[-1]
    padded = (counts + tile - 1) // tile * tile
    pends = jnp.cumsum(padded)
    pstarts = pends - padded
    dest = (jnp.sum(onehot * pstarts[None, :], axis=1) + rank).astype(jnp.int32)
    n_blocks = a // tile + N_EXPERTS
    tok = jnp.zeros((n_blocks * tile,), jnp.int32).at[dest].set(
        jnp.arange(a, dtype=jnp.int32) // 2, unique_indices=True, mode='promise_in_bounds')
    block_start = jnp.arange(n_blocks, dtype=jnp.int32) * tile
    blk_e = jnp.minimum(jnp.sum((block_start[:, None] >= pends[None, :]).astype(jnp.int32), axis=1),
                        N_EXPERTS - 1)
    nused = (pends[-1:] // tile).astype(jnp.int32)
    return tok, blk_e, nused, dest.reshape(n, 2), n_blocks


def _swa_head_order(w, axis):
    group = SWA_HEADS // SWA_KV_HEADS
    order = [g * group + t for t in range(group) for g in range(SWA_KV_HEADS)]
    shape = w.shape
    w = w.reshape(shape[:axis] + (SWA_HEADS, HEAD_DIM) + shape[axis + 1:])
    return jnp.take(w, jnp.array(order), axis=axis).reshape(shape)


def _rope_tables(n_batch, seq, ctx_len):
    t = jnp.arange(seq, dtype=jnp.int32)
    row = (t // GRID_W).astype(F32)
    col = (t % GRID_W).astype(F32)
    axis_dim = HEAD_DIM // 2
    inv_freq = ROPE_THETA ** (-jnp.arange(0, axis_dim, 2, dtype=F32) / axis_dim)
    dd = jnp.arange(LANES) % HEAD_DIM
    pos = jnp.where((dd // axis_dim)[None, :] == 0, row[:, None], col[:, None])
    ang = pos * inv_freq[dd % (axis_dim // 2)][None, :]
    cos = jnp.cos(ang)
    sin = jnp.where(((dd % axis_dim) < axis_dim // 2)[None, :], -jnp.sin(ang), jnp.sin(ang))
    n_ctx = n_batch * ctx_len
    cos = jnp.concatenate([jnp.tile(cos, (n_batch, 1)), jnp.ones((n_ctx, LANES), F32)], axis=0)
    sin = jnp.concatenate([jnp.tile(sin, (n_batch, 1)), jnp.zeros((n_ctx, LANES), F32)], axis=0)
    return cos, sin


def kernel(x, c, ctx, c_ctx, ada_w, ada_b, pre_mix_g, post_mix_g, pre_ffn_g, post_ffn_g, w_in, swa_sink,
           rwkv_mu_prev, rwkv_mu_next, rwkv_w0, rwkv_w_up, rwkv_a0, rwkv_a_up, rwkv_g_up, rwkv_k_k, rwkv_k_a,
           rwkv_r_k, rwkv_lnx_g, rwkv_lnx_b, diff_lambda, diff_subln_g, proj_swa, proj_rwkv, proj_diff, w_out,
           ffn_w_gate, ffn_w_up, ffn_w_down, router_w, moe_w_gate, moe_w_up, moe_w_down):
    n_batch, seq, d = x.shape
    ctx_len = ctx.shape[1]
    depth = w_in.shape[0]
    lat_rows = n_batch * seq
    lat_bpb = seq // ROW_TILE
    assert seq % ROW_TILE == 0 and (n_batch * ctx_len) % ROW_TILE == 0

    h = jnp.concatenate([x.reshape(lat_rows, d), ctx.reshape(n_batch * ctx_len, d)], axis=0)
    m = h.shape[0]
    cond = jnp.zeros((8, d), F32).at[:n_batch].set(c).at[n_batch].set(c_ctx)
    rope = _rope_tables(n_batch, seq, ctx_len)
    row = lambda a: a.reshape(1, -1)

    o_swa = 0
    o_rwkv = o_swa + SWA_Q + 2 * SWA_KV
    o_diff = o_rwkv + RWKV_COLS
    o_gate = o_diff + 2 * DIFF_QK + DIFF_V
    o_end = o_gate + 3 * d

    for layer in range(depth):
        mods = _ada(cond, ada_w[layer], ada_b[layer].reshape(1, -1))[:, None, :]
        wl = w_in[layer].astype(BF16)
        a = _prenorm(h, row(pre_mix_g[layer]), mods, 0, 1, lat_bpb, n_batch)

        w_swa = jnp.concatenate([_swa_head_order(wl[:, o_swa:o_swa + SWA_Q], axis=1),
                                 wl[:, o_swa + SWA_Q:o_rwkv]], axis=1)
        q_swa, kv_swa = _proj(a, w_swa, (SWA_Q, 2 * SWA_KV), (BF16, BF16), rope=rope,
                              rope_cols=SWA_Q + SWA_KV, scale_cols=SWA_Q, q_scale=HEAD_DIM ** -0.5)
        (p_rwkv,) = _proj(a, wl[:, o_rwkv:o_diff], (RWKV_COLS,), (F32,))
        q_diff, k_diff, v_diff = _proj(a, wl[:, o_diff:o_gate], (DIFF_QK, DIFF_QK, DIFF_V), (BF16,) * 3,
                                       rope=rope, rope_cols=2 * DIFF_QK, scale_cols=DIFF_QK,
                                       q_scale=HEAD_DIM ** -0.5 * math.log2(math.e))
        (gates,) = _proj(a, wl[:, o_gate:o_end], (d,), (BF16,), tn=d)

        ya = _swa(q_swa, kv_swa, swa_sink[layer].astype(F32), n_batch, seq, ctx_len)

        lp = {'rwkv_mu_prev': rwkv_mu_prev[layer], 'rwkv_mu_next': rwkv_mu_next[layer],
              'rwkv_w0': rwkv_w0[layer], 'rwkv_w_up': rwkv_w_up[layer], 'rwkv_a0': rwkv_a0[layer],
              'rwkv_a_up': rwkv_a_up[layer], 'rwkv_g_up': rwkv_g_up[layer], 'rwkv_k_k': rwkv_k_k[layer],
              'rwkv_k_a': rwkv_k_a[layer], 'rwkv_r_k': rwkv_r_k[layer]}
        r_, v_, kk_, ld_, ke_, bb_, g_, bonus_ = _rwkv_prep(p_rwkv, lp, n_batch, seq, ctx_len)
        y_f, y_b = _rwkv_scan(r_, v_, kk_, ld_, ke_, bb_, n_batch, seq, ctx_len)

        lam_vec = diff_lambda[layer].astype(F32)
        lam_init = 0.8 - 0.6 * math.exp(-0.3 * layer)
        lam = (jnp.exp(jnp.sum(lam_vec[0] * lam_vec[1])) - jnp.exp(jnp.sum(lam_vec[2] * lam_vec[3]))
               + lam_init).reshape(1)
        yc = _diff(q_diff, k_diff, v_diff, lam, row(diff_subln_g[layer]), lam_init, n_batch, seq, ctx_len)

        moe_layer = layer % 2 == 1
        jj = layer // 2
        merged = _merge(ya, y_f, y_b, bonus_, g_, rwkv_lnx_g[layer], rwkv_lnx_b[layer], yc, gates, h,
                        _swa_head_order(proj_swa[layer], axis=0).astype(BF16), proj_rwkv[layer].astype(BF16),
                        proj_diff[layer].astype(BF16), w_out[layer].astype(BF16), row(post_mix_g[layer]),
                        row(pre_ffn_g[layer]), mods, F32 if moe_layer else BF16, lat_bpb, n_batch,
                        router_w=router_w[jj] if moe_layer else None)
        h, f = merged[:2]
        need_ctx = layer < depth - 1
        if not moe_layer:
            h = _ffn(f, ffn_w_gate[jj].astype(BF16), ffn_w_up[jj].astype(BF16), ffn_w_down[jj].astype(BF16),
                     h, row(post_ffn_g[layer]), mods, lat_bpb, n_batch)
        else:
            n_tok = m if need_ctx else lat_rows
            top_i, top_w = merged[2][:n_tok], merged[3][:n_tok]
            tok, blk_e, nused, dest, n_blocks = _moe_slots(top_i[:, :2], MOE_TILE)
            y = _moe_ffn(f, blk_e, nused, tok, moe_w_gate[jj].astype(BF16), moe_w_up[jj].astype(BF16),
                         moe_w_down[jj].astype(BF16), n_blocks)
            h = _moe_combine(dest[:, 0], dest[:, 1], y, top_w, h, row(post_ffn_g[layer]), mods, n_tok, seq,
                             n_batch)
    return h[:lat_rows].reshape(n_batch, seq, d)
```

```python
import functools
import math

import jax
import jax.numpy as jnp
from jax import lax
from jax.experimental import pallas as pl
from jax.experimental.pallas import tpu as pltpu

F32 = jnp.float32
BF16 = jnp.bfloat16
HI = lax.Precision.HIGHEST

HEAD_DIM = 64
GRID_W = 64
ROPE_THETA = 10000.0
NORM_EPS = 1e-6
NEG_INF = -1e30
SWA_HEADS = 8
SWA_KV_HEADS = 2
SWA_BLOCK = 128
RWKV_HEADS = 8
RWKV_WIDTH = RWKV_HEADS * HEAD_DIM
DECAY_LORA = 64
AAA_LORA = 64
GATE_LORA = 128
RWKV_GN_EPS = 64e-5
DIFF_HEADS = 4
DIFF_V_DIM = 2 * HEAD_DIM
DIFF_SUBLN_EPS = 1e-5
N_EXPERTS = 8
SWA_Q = SWA_HEADS * HEAD_DIM
SWA_KV = SWA_KV_HEADS * HEAD_DIM
assert SWA_KV_HEADS == 2 and SWA_KV == 128
RWKV_COLS = 3 * RWKV_WIDTH + 2 * DECAY_LORA + 2 * AAA_LORA + GATE_LORA
DIFF_QK = DIFF_HEADS * 2 * HEAD_DIM
DIFF_V = DIFF_HEADS * DIFF_V_DIM

LANES = 128
VMEM_LIMIT = 48 * 1024 * 1024
ROW_TILE = 512
RWKV_CHUNK = 64
RWKV_INV_BASE = 8
RWKV_GROUP = 2
RWKV_STEP_ROWS = 256
DIFF_TQ = 1024
DIFF_TK = 512
DIFF_UNROLL = 2
DIFF_ONES_ROWS = 16
MOE_TILE = 512
MOE_TF = 1792
FFN_TF = 1408
COMBINE_TILE = 256
PREP_TILE = 256
MERGE_TILE = 256


def _cp(sem, **kw):
    return pltpu.CompilerParams(dimension_semantics=sem, vmem_limit_bytes=VMEM_LIMIT, **kw)


def _seg_of_block(i, lat_blocks_per_batch, n_batch):
    return jnp.minimum(i // lat_blocks_per_batch, n_batch)


def _rms(x, eps):
    return x * lax.rsqrt(jnp.mean(x * x, axis=-1, keepdims=True) + eps)


def _sigmoid(x):
    return 1.0 / (1.0 + jnp.exp(-x))


def _group_sum(x, ones_bd):
    hi = x.astype(BF16)
    mid = (x - hi.astype(F32)).astype(BF16)
    rows = x.shape[0]
    parts = jnp.dot(jnp.concatenate([hi, mid], axis=0), ones_bd, preferred_element_type=F32)
    return parts[:rows] + parts[rows:]


def _ada_kernel(x_ref, w_ref, b_ref, o_ref):
    x = x_ref[...]
    s = x * _sigmoid(x)
    o_ref[...] = jnp.dot(s, w_ref[...], precision=HI, preferred_element_type=F32) + b_ref[...]


def _ada(cond, w, b):
    rows, d = cond.shape
    n = w.shape[1]
    return pl.pallas_call(
        _ada_kernel,
        out_shape=jax.ShapeDtypeStruct((rows, n), F32),
        grid=(n // d,),
        in_specs=[pl.BlockSpec((rows, d), lambda j: (0, 0)),
                  pl.BlockSpec((d, d), lambda j: (0, j)),
                  pl.BlockSpec((1, d), lambda j: (0, j))],
        out_specs=pl.BlockSpec((rows, d), lambda j: (0, j)),
        compiler_params=_cp(("parallel",)),
        name="ada_mod",
    )(cond, w, b)


def _prenorm_kernel(h_ref, g_ref, mod_ref, o_ref, *, d, shift_idx, scale_idx):
    y = _rms(h_ref[...], NORM_EPS) * g_ref[...]
    shift = mod_ref[0, :, shift_idx * d:(shift_idx + 1) * d]
    scale = mod_ref[0, :, scale_idx * d:(scale_idx + 1) * d]
    o_ref[...] = (y * (1.0 + scale) + shift).astype(o_ref.dtype)


def _prenorm(h, g, mods, shift_idx, scale_idx, lat_bpb, n_batch):
    m, d = h.shape
    tm = ROW_TILE
    seg = functools.partial(_seg_of_block, lat_blocks_per_batch=lat_bpb, n_batch=n_batch)
    return pl.pallas_call(
        functools.partial(_prenorm_kernel, d=d, shift_idx=shift_idx, scale_idx=scale_idx),
        out_shape=jax.ShapeDtypeStruct((m, d), BF16),
        grid=(m // tm,),
        in_specs=[pl.BlockSpec((tm, d), lambda i: (i, 0)),
                  pl.BlockSpec((1, d), lambda i: (0, 0)),
                  pl.BlockSpec((1, 1, mods.shape[2]), lambda i: (seg(i), 0, 0))],
        out_specs=pl.BlockSpec((tm, d), lambda i: (i, 0)),
        compiler_params=_cp(("parallel",)),
        name="prenorm",
    )(h, g, mods)


def _proj_kernel(*refs, splits, rope_cols, scale_cols, q_scale):
    if rope_cols:
        a_ref, w_ref, cos_ref, sin_ref = refs[:4]
        outs = refs[4:]
    else:
        a_ref, w_ref = refs[:2]
        outs = refs[2:]
    y = jnp.dot(a_ref[...], w_ref[...], preferred_element_type=F32)
    tm, tn = y.shape
    if rope_cols:
        cos = cos_ref[...]
        sin = sin_ref[...]
        lane = lax.broadcasted_iota(jnp.int32, (tm, LANES), 1)
        first_half = (lane % 32) < 16
        pieces = []
        for c in range(tn // LANES):
            yc = y[:, c * LANES:(c + 1) * LANES]
            if c * LANES < rope_cols:
                partner = jnp.where(first_half, pltpu.roll(yc, LANES - 16, 1), pltpu.roll(yc, 16, 1))
                yc = yc * cos + partner * sin
            if c * LANES < scale_cols:
                yc = yc * q_scale
            pieces.append(yc)
        y = jnp.concatenate(pieces, axis=1)
    start = 0
    for o_ref, width in zip(outs, splits):
        o_ref[...] = y[:, start:start + width].astype(o_ref.dtype)
        start += width


def _proj(a, w, splits, dtypes, rope=None, rope_cols=0, scale_cols=0, q_scale=1.0, tn=None):
    m, k = a.shape
    n = w.shape[1]
    tm = ROW_TILE
    tn = n if tn is None else tn
    assert sum(splits) == tn and (len(splits) == 1 or tn == n)
    in_specs = [pl.BlockSpec((tm, k), lambda i, j: (i, 0)),
                pl.BlockSpec((k, tn), lambda i, j: (0, j))]
    args = [a, w]
    if rope_cols:
        in_specs += [pl.BlockSpec((tm, LANES), lambda i, j: (i, 0))] * 2
        args += list(rope)
    out_specs = []
    out_shape = []
    if len(splits) == 1:
        out_specs.append(pl.BlockSpec((tm, tn), lambda i, j: (i, j)))
        out_shape.append(jax.ShapeDtypeStruct((m, n), dtypes[0]))
    else:
        for width, dt in zip(splits, dtypes):
            out_specs.append(pl.BlockSpec((tm, width), lambda i, j: (i, 0)))
            out_shape.append(jax.ShapeDtypeStruct((m, width), dt))
    res = pl.pallas_call(
        functools.partial(_proj_kernel, splits=tuple(splits), rope_cols=rope_cols, scale_cols=scale_cols,
                          q_scale=q_scale),
        out_shape=out_shape,
        grid=(m // tm, n // tn),
        in_specs=in_specs,
        out_specs=out_specs,
        compiler_params=_cp(("parallel", "parallel")),
        name="proj",
    )(*args)
    return res


def _swa_kernel(sink_ref, q_ref, kp_ref, kc_ref, kn_ref, kx_ref, o_ref, *, nb, n_lat_blocks):
    i = pl.program_id(0)
    is_lat = i < n_lat_blocks
    n = i % nb
    blk = SWA_BLOCK
    q = q_ref[...]
    kv = jnp.concatenate([kp_ref[...], kc_ref[...], kn_ref[...], kx_ref[...]], axis=0)
    nkeys = kv.shape[0]
    k_t = kv[:, :SWA_KV]
    v_aug = jnp.concatenate([kv[:, SWA_KV:], jnp.ones((nkeys, SWA_KV), kv.dtype)], axis=1)
    r = lax.broadcasted_iota(jnp.int32, (blk, nkeys), 0)
    j = lax.broadcasted_iota(jnp.int32, (blk, nkeys), 1)
    lo = jnp.where(n > 0, 0, blk)
    hi = jnp.where(n < nb - 1, 3 * blk, 2 * blk)
    valid_loc = (j >= r) & (j <= r + 2 * blk) & (j >= lo) & (j < hi) & is_lat
    bias = jnp.where(valid_loc | (j >= 3 * blk), 0.0, NEG_INF)
    low_half = lax.broadcasted_iota(jnp.int32, (blk, SWA_KV), 1) < HEAD_DIM
    zero = jnp.zeros((blk, SWA_KV), q.dtype)
    group = SWA_HEADS // SWA_KV_HEADS
    q_rows, sinks = [], []
    for t in range(group):
        qt = q[:, t * SWA_KV:(t + 1) * SWA_KV]
        q_rows += [jnp.where(low_half, qt, zero), jnp.where(low_half, zero, qt)]
        sinks += [jnp.full((blk, 1), sink_ref[t], F32), jnp.full((blk, 1), sink_ref[group + t], F32)]
    scores = [lax.dot_general(jnp.concatenate(q_rows[2 * t:2 * t + 2], axis=0), k_t, (((1,), (1,)), ((), ())),
                              preferred_element_type=F32) for t in range(group)]
    tiles = []
    for t in range(group):
        sk = jnp.concatenate(sinks[2 * t:2 * t + 2], axis=0)
        s = (scores[t].reshape(2, blk, nkeys) + bias[None]).reshape(2 * blk, nkeys)
        mx = jnp.maximum(jnp.max(s, axis=-1, keepdims=True), sk)
        p = jnp.exp((s - mx).astype(BF16))
        oa = jnp.dot(p, v_aug, preferred_element_type=F32)
        den = oa[:, SWA_KV:] + jnp.exp(sk - mx)
        on = oa[:, :SWA_KV] * (1.0 / den)
        tiles.append(jnp.where(low_half, on[:blk], on[blk:]))
    o_ref[...] = jnp.concatenate(tiles, axis=1).astype(o_ref.dtype)


def _swa(q, kv, sink, n_batch, seq, ctx_len):
    m = q.shape[0]
    blk = SWA_BLOCK
    nb = seq // blk
    n_lat = n_batch * nb
    cpb = ctx_len // blk

    def batch_of(i):
        return jnp.where(i < n_lat, i // nb, (i - n_lat) // cpb)

    def prev_idx(i, s):
        return (jnp.where(i < n_lat, batch_of(i) * nb + jnp.maximum(i % nb - 1, 0), i), 0)

    def next_idx(i, s):
        return (jnp.where(i < n_lat, batch_of(i) * nb + jnp.minimum(i % nb + 1, nb - 1), i), 0)

    def ctx_idx(i, s):
        return (n_batch * seq // ctx_len + batch_of(i), 0)

    grid_spec = pltpu.PrefetchScalarGridSpec(
        num_scalar_prefetch=1,
        grid=(m // blk,),
        in_specs=[pl.BlockSpec((blk, SWA_Q), lambda i, s: (i, 0)),
                  pl.BlockSpec((blk, 2 * SWA_KV), prev_idx),
                  pl.BlockSpec((blk, 2 * SWA_KV), lambda i, s: (i, 0)),
                  pl.BlockSpec((blk, 2 * SWA_KV), next_idx),
                  pl.BlockSpec((ctx_len, 2 * SWA_KV), ctx_idx)],
        out_specs=pl.BlockSpec((blk, SWA_Q), lambda i, s: (i, 0)),
    )
    return pl.pallas_call(
        functools.partial(_swa_kernel, nb=nb, n_lat_blocks=n_lat),
        out_shape=jax.ShapeDtypeStruct((m, SWA_Q), BF16),
        grid_spec=grid_spec,
        compiler_params=_cp(("parallel",)),
        name="swa_attn",
    )(sink, q, kv, kv, kv, kv)


def _diff_kernel(*refs, n_lat_chunks, coef):
    if n_lat_chunks:
        lam_ref, gcol_ref, q_ref, kc_ref, vtc_ref, kl_ref, vtl_ref, o_ref, m_sc, acc_sc, st_a, st_b = refs
    else:
        lam_ref, gcol_ref, q_ref, kc_ref, vtc_ref, o_ref, m_sc, acc_sc, st_a, st_b = refs
    q = q_ref[...]
    tq = q.shape[0]
    dv = DIFF_V_DIM
    lane = lax.broadcasted_iota(jnp.int32, q.shape, 1)
    zero = jnp.zeros_like(q)
    qq = jnp.concatenate([jnp.where(lane < HEAD_DIM, q, zero), jnp.where(lane >= HEAD_DIM, q, zero)], axis=0)

    def scores(k):
        return lax.dot_general(k, qq, (((1,), (1,)), ((), ())), preferred_element_type=F32)

    def accumulate(st_ref, vt):
        st = st_ref[0:vt.shape[1], :]
        m_old = m_sc[...]
        m_new = jnp.maximum(m_old, jnp.max(st, axis=0, keepdims=True))
        alpha = jnp.exp2(m_old - m_new)
        pt = jnp.exp2((st - m_new).astype(BF16))
        acc_sc[...] = alpha * acc_sc[...] + jnp.dot(vt, pt, preferred_element_type=F32)
        m_sc[...] = m_new

    tk = st_a.shape[0]
    k_lat = lambda c: kl_ref[pl.ds(pl.multiple_of(c * tk, tk), tk), :]
    m_sc[...] = jnp.full(m_sc.shape, NEG_INF, F32)
    acc_sc[...] = jnp.zeros(acc_sc.shape, F32)
    bufs = (st_a, st_b)
    n = n_lat_chunks
    if n:
        unroll = DIFF_UNROLL
        st_a[...] = scores(k_lat(0))
        n_trips = (n - 1) // unroll

        def body(j, carry):
            for u in range(unroll):
                c = j * unroll + u
                bufs[(u + 1) % 2][...] = scores(k_lat(c + 1))
                accumulate(bufs[u % 2], vtl_ref[0, c])
            return carry

        lax.fori_loop(0, n_trips, body, 0)
        for c in range(n_trips * unroll, n):
            if c + 1 < n:
                bufs[(c + 1) % 2][...] = scores(k_lat(c + 1))
            else:
                bufs[(c + 1) % 2][0:kc_ref.shape[0], :] = scores(kc_ref[...])
            accumulate(bufs[c % 2], vtl_ref[0, c])
    else:
        st_a[0:kc_ref.shape[0], :] = scores(kc_ref[...])
    accumulate(bufs[n % 2], vtc_ref[0, 0])
    acc = acc_sc[...]
    ot = acc[:dv] * (1.0 / acc[dv:dv + 1])
    odt = ot[:, :tq] - lam_ref[0] * ot[:, tq:]
    ms = jnp.mean(odt * odt, axis=0, keepdims=True)
    yt = odt * lax.rsqrt(ms + DIFF_SUBLN_EPS) * (gcol_ref[...] * coef)
    o_ref[...] = yt.T.astype(o_ref.dtype)


def _diff(q, k, v, lam, subln_g, lam_init, n_batch, seq, ctx_len):
    tk = min(DIFF_TK, seq)
    assert seq % tk == 0 and tk % ctx_len == 0
    dv = DIFF_V_DIM
    dva = dv + DIFF_ONES_ROWS
    lat_rows = n_batch * seq

    def transposed_chunks(rows, size):
        t = rows.reshape(rows.shape[0] // size, size, DIFF_HEADS, dv).transpose(2, 0, 3, 1)
        return jnp.concatenate([t, jnp.ones(t.shape[:2] + (DIFF_ONES_ROWS, size), t.dtype)], axis=2)

    vt_lat = transposed_chunks(v[:lat_rows], tk)
    vt_ctx = transposed_chunks(v[lat_rows:], ctx_len)
    gcol = subln_g.reshape(dv, 1)
    lat_chunks = seq // tk
    ctx0 = n_batch * seq // ctx_len

    def call(tq, n_q, q_block0, batch_of, with_lat):
        in_specs = [pl.BlockSpec((dv, 1), lambda h, i, s: (0, 0)),
                    pl.BlockSpec((tq, LANES), lambda h, i, s: (q_block0 + i, h)),
                    pl.BlockSpec((ctx_len, LANES), lambda h, i, s: (ctx0 + batch_of(i), h)),
                    pl.BlockSpec((1, 1, dva, ctx_len), lambda h, i, s: (h, batch_of(i), 0, 0))]
        args = [lam, gcol, q, k, vt_ctx]
        if with_lat:
            in_specs += [pl.BlockSpec((seq, LANES), lambda h, i, s: (batch_of(i), h)),
                         pl.BlockSpec((1, lat_chunks, dva, tk), lambda h, i, s: (h, batch_of(i), 0, 0))]
            args += [k, vt_lat]
        grid_spec = pltpu.PrefetchScalarGridSpec(
            num_scalar_prefetch=1,
            grid=(DIFF_HEADS, n_q),
            in_specs=in_specs,
            out_specs=pl.BlockSpec((tq, LANES), lambda h, i, s: (i, h)),
            scratch_shapes=[pltpu.VMEM((1, 2 * tq), F32), pltpu.VMEM((dva, 2 * tq), F32),
                            pltpu.VMEM((tk, 2 * tq), F32), pltpu.VMEM((tk, 2 * tq), F32)],
        )
        return pl.pallas_call(
            functools.partial(_diff_kernel, n_lat_chunks=lat_chunks if with_lat else 0, coef=1.0 - lam_init),
            out_shape=jax.ShapeDtypeStruct((n_q * tq, DIFF_V), BF16),
            grid_spec=grid_spec,
            compiler_params=_cp(("parallel", "arbitrary")),
            name="diff_attn" if with_lat else "diff_attn_ctx",
        )(*args)

    tq = DIFF_TQ
    y_lat = call(tq, n_batch * seq // tq, 0, lambda i: i // (seq // tq), True)
    y_ctx = call(ctx_len, n_batch, ctx0, lambda i: i, False)
    return jnp.concatenate([y_lat, y_ctx], axis=0)


def _rwkv_prep_kernel(p_ref, hp_ref, hn_ref, mup_ref, mun_ref, kk_w_ref, ka_ref, rk_ref, w0_ref, a0_ref,
                      wup_ref, aup_ref, gup_ref, ones_ref,
                      r_ref, v_ref, kk_ref, ld_ref, ke_ref, bb_ref, g_ref, bonus_ref, sc,
                      *, tm, lat_rows, seq, ctx_len):
    i = pl.program_id(0)
    w = RWKV_WIDTH
    sc[0:8, :] = hp_ref[...]
    sc[8:8 + tm, :] = p_ref[...]
    sc[8 + tm:16 + tm, :] = hn_ref[...]
    p = p_ref[...]
    prev = sc[7:7 + tm, :]
    nxt = sc[9:9 + tm, :]
    row = i * tm + lax.broadcasted_iota(jnp.int32, (tm, 1), 0)
    pos = jnp.where(row < lat_rows, row % seq, (row - lat_rows) % ctx_len)
    seg_len = jnp.where(row < lat_rows, seq, ctx_len)
    prev = jnp.where(pos == 0, 0.0, prev)
    nxt = jnp.where(pos == seg_len - 1, 0.0, nxt)
    ps = p + mup_ref[...] * (prev - p) + mun_ref[...] * (nxt - p)

    r = ps[:, 0:w]
    k = ps[:, w:2 * w]
    v = ps[:, 2 * w:3 * w]
    wd = ps[:, 3 * w:3 * w + 2 * DECAY_LORA]
    ad = ps[:, 3 * w + 2 * DECAY_LORA:3 * w + 2 * DECAY_LORA + 2 * AAA_LORA]
    gd = ps[:, 3 * w + 2 * DECAY_LORA + 2 * AAA_LORA:]

    ones_bd = ones_ref[...]
    g = jnp.dot(_sigmoid(gd).astype(BF16), gup_ref[...], preferred_element_type=F32)
    kk = k * kk_w_ref[...]
    ss = _group_sum(kk * kk, ones_bd)
    kk = kk / jnp.maximum(jnp.sqrt(ss), 1e-12)
    w_raw = w0_ref[...] + jnp.dot(jnp.tanh(wd).astype(BF16), wup_ref[...], preferred_element_type=F32)
    a_raw = a0_ref[...] + jnp.dot(ad.astype(BF16), aup_ref[...], preferred_element_type=F32)
    ld = -math.exp(-0.5) * _sigmoid(w_raw)
    a = _sigmoid(a_raw)
    ka = ka_ref[...]
    ke_sum = jnp.zeros_like(k)
    for d in range(2):
        a_d = a[:, d * w:(d + 1) * w]
        ke = k * (1.0 + (a_d - 1.0) * ka)
        ld_ref[d] = ld[:, d * w:(d + 1) * w]
        ke_ref[d] = ke
        bb_ref[d] = kk * a_d
        ke_sum = ke_sum + ke
    rk = _group_sum(r * ke_sum * rk_ref[...], ones_bd)
    r_ref[...] = r
    v_ref[...] = v
    kk_ref[...] = kk
    g_ref[...] = g
    bonus_ref[...] = rk * v


def _rwkv_prep(p, lp, n_batch, seq, ctx_len):
    m, cols = p.shape
    tm = PREP_TILE
    w = RWKV_WIDTH
    lat_rows = n_batch * seq
    row = lambda a: a.reshape(1, -1).astype(F32)

    def blockdiag(u):
        z = jnp.zeros_like(u[0])
        return jnp.concatenate([jnp.concatenate([u[0], z], axis=1), jnp.concatenate([z, u[1]], axis=1)], axis=0)

    head = jnp.arange(w) // HEAD_DIM
    ones_bd = (head[:, None] == head[None, :]).astype(BF16)
    full = lambda shape: pl.BlockSpec(shape, lambda i: (0,) * len(shape))
    nb8 = m // 8
    outs = pl.pallas_call(
        functools.partial(_rwkv_prep_kernel, tm=tm, lat_rows=lat_rows, seq=seq, ctx_len=ctx_len),
        out_shape=[jax.ShapeDtypeStruct((m, w), F32)] * 3
        + [jax.ShapeDtypeStruct((2, m, w), F32)] * 3
        + [jax.ShapeDtypeStruct((m, w), F32)] * 2,
        grid=(m // tm,),
        in_specs=[pl.BlockSpec((tm, cols), lambda i: (i, 0)),
                  pl.BlockSpec((8, cols), lambda i: (jnp.maximum(i * (tm // 8) - 1, 0), 0)),
                  pl.BlockSpec((8, cols), lambda i: (jnp.minimum((i + 1) * (tm // 8), nb8 - 1), 0)),
                  full((1, cols)), full((1, cols)), full((1, w)), full((1, w)), full((1, w)),
                  full((1, 2 * w)), full((1, 2 * w)),
                  full((2 * DECAY_LORA, 2 * w)), full((2 * AAA_LORA, 2 * w)), full((GATE_LORA, w)),
                  full((w, w))],
        out_specs=[pl.BlockSpec((tm, w), lambda i: (i, 0))] * 3
        + [pl.BlockSpec((2, tm, w), lambda i: (0, i, 0))] * 3
        + [pl.BlockSpec((tm, w), lambda i: (i, 0))] * 2,
        scratch_shapes=[pltpu.VMEM((tm + 16, cols), F32)],
        compiler_params=_cp(("parallel",)),
        name="rwkv_prep",
    )(p, p, p, row(lp['rwkv_mu_prev']), row(lp['rwkv_mu_next']), row(lp['rwkv_k_k']), row(lp['rwkv_k_a']),
      row(lp['rwkv_r_k']), row(lp['rwkv_w0']), row(lp['rwkv_a0']),
      blockdiag(lp['rwkv_w_up']).astype(BF16), blockdiag(lp['rwkv_a_up']).astype(BF16),
      lp['rwkv_g_up'].astype(BF16), ones_bd)
    return outs


def _rwkv_chunk_prep(off, r_ref, v_ref, kk_ref, ld_ref, ke_ref, bb_ref, incl):
    c = RWKV_CHUNK
    ld = ld_ref[0, pl.ds(off, c), :]
    r = r_ref[pl.ds(off, c), :]
    v = v_ref[pl.ds(off, c), :]
    kk = kk_ref[pl.ds(off, c), :]
    ke = ke_ref[0, pl.ds(off, c), :]
    bb = bb_ref[0, pl.ds(off, c), :]
    cum = jnp.dot(incl, ld, precision=HI, preferred_element_type=F32)
    tot = jnp.sum(ld, axis=0, keepdims=True)
    rt = r * jnp.exp(cum)
    einv = jnp.exp(-cum)
    etail = jnp.exp(tot - cum)
    return dict(at=(-kk * jnp.exp(cum - ld)).astype(BF16), rt=rt, rt_b=rt.astype(BF16),
                bt=(bb * einv).astype(BF16), kt=(ke * einv).astype(BF16),
                bh=(bb * etail).astype(BF16), kh=(ke * etail).astype(BF16),
                v=v, v_b=v.astype(BF16), wtot=jnp.exp(tot))


def _rwkv_chunk_group(offs, in_refs, y_refs, s_sc):
    c = RWKV_CHUNK
    hd = HEAD_DIM
    n_group = len(offs[0])
    ti = lax.broadcasted_iota(jnp.int32, (c, c), 0)
    tj = lax.broadcasted_iota(jnp.int32, (c, c), 1)
    eye = (ti == tj).astype(F32)
    incl = [(tj <= ti).astype(F32), (tj >= ti).astype(F32)]
    strict = [m - eye for m in incl]
    diag_mask = ((ti // RWKV_INV_BASE) == (tj // RWKV_INV_BASE)).astype(F32)
    off_masks = []
    sz = RWKV_INV_BASE
    while sz < c:
        off_masks.append((((ti // (2 * sz)) == (tj // (2 * sz))) & ((ti // sz) != (tj // sz))).astype(F32))
        sz *= 2

    pre = {(d, g): _rwkv_chunk_prep(offs[d][g], *in_refs[d], incl[d])
           for d in range(2) for g in range(n_group)}
    lanes = [(d, g, h) for g in range(n_group) for d in range(2) for h in range(RWKV_HEADS)]
    sl = lambda h: slice(h * hd, (h + 1) * hd)
    get = lambda name: [pre[d, g][name][:, sl(h)] for d, g, h in lanes]
    at, rt, rt_b, bt, kt, bh, kh, v, v_b = (get(n) for n in ('at', 'rt', 'rt_b', 'bt', 'kt', 'bh', 'kh', 'v', 'v_b'))
    nl = range(len(lanes))
    nt = (((1,), (1,)), ((), ()))
    bdot = lambda x, y: jnp.dot(x.astype(BF16), y.astype(BF16), preferred_element_type=F32)

    gm = [lax.dot_general(jnp.concatenate([at[i], rt_b[i]], axis=0), jnp.concatenate([bt[i], kt[i]], axis=0), nt,
                          preferred_element_type=F32) for i in nl]
    zz0 = [bdot(v[i].T, kh[i]) for i in nl]
    aab = [gm[i][:c, :c] * strict[lanes[i][0]] for i in nl]
    aak = [gm[i][:c, c:] * strict[lanes[i][0]] for i in nl]
    arb = [gm[i][c:, :c] * incl[lanes[i][0]] for i in nl]
    ark = [gm[i][c:, c:] * incl[lanes[i][0]] for i in nl]
    av = [bdot(jnp.concatenate([aak[i], ark[i]], axis=0), v_b[i]) for i in nl]
    pw = [aab[i] * diag_mask for i in nl]
    tm_ = [eye + pw[i] for i in nl]
    for _ in range(int(math.log2(RWKV_INV_BASE)) - 1):
        pw = [bdot(pw[i], pw[i]) for i in nl]
        tm_ = [tm_[i] + bdot(tm_[i], pw[i]) for i in nl]
    for off_mask in off_masks:
        tn = [bdot(tm_[i], aab[i] * off_mask) for i in nl]
        tm_ = [tm_[i] + bdot(tn[i], tm_[i]) for i in nl]
    au = [bdot(tm_[i], jnp.concatenate([at[i], av[i][:c].astype(BF16)], axis=1)) for i in nl]
    ry = [bdot(arb[i], au[i]) for i in nl]
    mz = [bdot(au[i].T, bh[i]) for i in nl]
    rbar = [(rt[i] + ry[i][:, :hd]).astype(BF16) for i in nl]
    ybar = [ry[i][:, hd:] + av[i][c:] for i in nl]
    mm = [(eye * pre[lanes[i][0], lanes[i][1]]['wtot'][:, sl(lanes[i][2])] + mz[i][:hd]).astype(BF16) for i in nl]
    zz = [mz[i][hd:] + zz0[i] for i in nl]
    state = {(d, h): s_sc[d, h] for d in range(2) for h in range(RWKV_HEADS)}
    ys = {}
    for g in range(n_group):
        idx = [i for i in nl if lanes[i][1] == g]
        s_b = {i: state[lanes[i][0], lanes[i][2]].astype(BF16) for i in idx}
        for i in idx:
            ys[i] = lax.dot_general(rbar[i], s_b[i], nt, preferred_element_type=F32) + ybar[i]
        for i in idx:
            state[lanes[i][0], lanes[i][2]] = jnp.dot(s_b[i], mm[i], preferred_element_type=F32) + zz[i]
    for (d, h), val in state.items():
        s_sc[d, h] = val
    for d in range(2):
        for g in range(n_group):
            y_refs[d][0, pl.ds(offs[d][g], c), :] = jnp.concatenate(
                [ys[i] for i in nl if lanes[i][0] == d and lanes[i][1] == g], axis=1)


def _rwkv_scan_kernel(rf, vf, kf, ldf, kef, bbf, rb, vb, kb, ldb, keb, bbb, yf, yb, s_sc, *, n_chunks):
    @pl.when(pl.program_id(1) == 0)
    def _():
        s_sc[...] = jnp.zeros(s_sc.shape, F32)

    group = RWKV_GROUP

    def body(t, carry):
        off_f = [pl.multiple_of((t * group + g) * RWKV_CHUNK, RWKV_CHUNK) for g in range(group)]
        off_b = [pl.multiple_of((n_chunks - 1 - t * group - g) * RWKV_CHUNK, RWKV_CHUNK) for g in range(group)]
        _rwkv_chunk_group((off_f, off_b), ((rf, vf, kf, ldf, kef, bbf), (rb, vb, kb, ldb, keb, bbb)),
                          (yf, yb), s_sc)
        return carry

    lax.fori_loop(0, n_chunks // group, body, 0)


def _rwkv_scan(r, v, kk, ld, ke, bb, n_batch, seq, ctx_len):
    m, w = r.shape
    ts = RWKV_STEP_ROWS
    assert ctx_len == ts
    lpb = seq // ts
    ctx0 = n_batch * seq // ts
    nj = 1 + lpb

    def fwd(b, j):
        return jnp.where(j == 0, ctx0 + b, b * lpb + j - 1)

    def bwd(b, j):
        return jnp.where(j == 0, ctx0 + b, b * lpb + lpb - j)

    shared = lambda f: pl.BlockSpec((ts, w), lambda b, j: (f(b, j), 0))
    per_dir = lambda f, d: pl.BlockSpec((1, ts, w), lambda b, j: (d, f(b, j), 0))
    y = pl.pallas_call(
        functools.partial(_rwkv_scan_kernel, n_chunks=ts // RWKV_CHUNK),
        out_shape=[jax.ShapeDtypeStruct((1, m, w), F32)] * 2,
        grid=(n_batch, nj),
        in_specs=[shared(fwd), shared(fwd), shared(fwd), per_dir(fwd, 0), per_dir(fwd, 0), per_dir(fwd, 0),
                  shared(bwd), shared(bwd), shared(bwd), per_dir(bwd, 1), per_dir(bwd, 1), per_dir(bwd, 1)],
        out_specs=[pl.BlockSpec((1, ts, w), lambda b, j: (0, fwd(b, j), 0)),
                   pl.BlockSpec((1, ts, w), lambda b, j: (0, bwd(b, j), 0))],
        scratch_shapes=[pltpu.VMEM((2, RWKV_HEADS, HEAD_DIM, HEAD_DIM), F32)],
        compiler_params=_cp(("parallel", "arbitrary")),
        name="rwkv_scan",
    )(r, v, kk, ld, ke, bb, r, v, kk, ld, ke, bb)
    return y


def _rwkv_finish(y, bonus, g, lnx_g, lnx_b, ones_bd):
    inv = 1.0 / HEAD_DIM
    mu = _group_sum(y, ones_bd) * inv
    yc = y - mu
    var = _group_sum(yc * yc, ones_bd) * inv
    yn = yc * lax.rsqrt(var + RWKV_GN_EPS) * lnx_g + lnx_b
    return (yn + bonus) * g


def _top2(logits):
    lane = lax.broadcasted_iota(jnp.int32, logits.shape, 1)
    logits = jnp.where(lane < N_EXPERTS, logits, -jnp.inf)
    m1 = jnp.max(logits, axis=-1, keepdims=True)
    i1 = jnp.min(jnp.where(logits == m1, lane, LANES), axis=-1, keepdims=True)
    rest = jnp.where(lane == i1, -jnp.inf, logits)
    m2 = jnp.max(rest, axis=-1, keepdims=True)
    i2 = jnp.min(jnp.where(rest == m2, lane, LANES), axis=-1, keepdims=True)
    e = jnp.exp(m2 - m1)
    w1 = 1.0 / (1.0 + e)
    w2 = e / (1.0 + e)
    idx = jnp.where(lane == 0, i1, jnp.where(lane == 1, i2, 0))
    wts = jnp.where(lane == 0, w1, jnp.where(lane == 1, w2, 0.0))
    return idx, wts


def _merge_kernel(*refs, d, with_router):
    (ya_ref, yf_ref, yb_ref, bonus_ref, g_ref, lg_ref, lb_ref, ones_ref, yc_ref, gt_ref, h_ref,
     pa_ref, pb_ref, pc_ref, wo_ref, gpost_ref, gpre_ref, mod_ref) = refs[:18]
    if with_router:
        rwh_ref, rwl_ref, h_out, f_out, idx_out, wt_out = refs[18:]
    else:
        h_out, f_out = refs[18:]
    yb = _rwkv_finish(yf_ref[0] + yb_ref[0], bonus_ref[...], g_ref[...], lg_ref[...], lb_ref[...], ones_ref[...])
    gates = gt_ref[...].astype(F32)
    merged = (_sigmoid(gates[:, 0:d]) * jnp.dot(ya_ref[...], pa_ref[...], preferred_element_type=F32)
              + _sigmoid(gates[:, d:2 * d]) * jnp.dot(yb.astype(BF16), pb_ref[...], preferred_element_type=F32)
              + _sigmoid(gates[:, 2 * d:3 * d]) * jnp.dot(yc_ref[...], pc_ref[...], preferred_element_type=F32))
    out = jnp.dot(merged.astype(BF16), wo_ref[...], preferred_element_type=F32)
    mod = lambda idx: mod_ref[0, :, idx * d:(idx + 1) * d]
    hn = h_ref[...] + mod(2) * (_rms(out, NORM_EPS) * gpost_ref[...])
    h_out[...] = hn
    f = (_rms(hn, NORM_EPS) * gpre_ref[...]) * (1.0 + mod(4)) + mod(3)
    f_out[...] = f.astype(f_out.dtype)
    if with_router:
        rows = f.shape[0]
        hi = f.astype(BF16)
        mid = (f - hi.astype(F32)).astype(BF16)
        part = jnp.dot(jnp.concatenate([hi, mid], axis=0), rwh_ref[...], preferred_element_type=F32)
        logits = part[:rows] + part[rows:] + jnp.dot(hi, rwl_ref[...], preferred_element_type=F32)
        idx, wts = _top2(logits)
        idx_out[...] = idx
        wt_out[...] = wts


def _merge(ya, yf, yb, bonus, g, lnx_g, lnx_b, yc, gates, h, pa, pb, pc, wo, g_post, g_pre, mods, f_dtype,
           lat_bpb, n_batch, router_w=None):
    m, d = h.shape
    w = bonus.shape[1]
    tm = MERGE_TILE
    bpb = lat_bpb * (ROW_TILE // tm)
    seg = functools.partial(_seg_of_block, lat_blocks_per_batch=bpb, n_batch=n_batch)
    rows = lambda width: pl.BlockSpec((tm, width), lambda i: (i, 0))
    rows3 = pl.BlockSpec((1, tm, w), lambda i: (0, i, 0))
    full = lambda a: pl.BlockSpec(a.shape, lambda i: (0, 0))
    head = jnp.arange(w) // HEAD_DIM
    ones_bd = (head[:, None] == head[None, :]).astype(BF16)
    args = [ya, yf, yb, bonus, g, lnx_g.reshape(1, w), lnx_b.reshape(1, w), ones_bd, yc, gates, h,
            pa, pb, pc, wo, g_post, g_pre, mods]
    in_specs = [rows(ya.shape[1]), rows3, rows3, rows(w), rows(w), full(args[5]), full(args[6]), full(ones_bd),
                rows(yc.shape[1]), rows(gates.shape[1]), rows(d), full(pa), full(pb), full(pc), full(wo),
                full(g_post), full(g_pre), pl.BlockSpec((1, 1, mods.shape[2]), lambda i: (seg(i), 0, 0))]
    out_shape = [jax.ShapeDtypeStruct((m, d), F32), jax.ShapeDtypeStruct((m, d), f_dtype)]
    out_specs = [rows(d), rows(d)]
    if router_w is not None:
        w_pad = jnp.zeros((d, LANES), F32).at[:, :N_EXPERTS].set(router_w)
        w_hi = w_pad.astype(BF16)
        w_lo = (w_pad - w_hi.astype(F32)).astype(BF16)
        args += [w_hi, w_lo]
        in_specs += [full(w_hi), full(w_lo)]
        out_shape += [jax.ShapeDtypeStruct((m, LANES), jnp.int32), jax.ShapeDtypeStruct((m, LANES), F32)]
        out_specs += [rows(LANES), rows(LANES)]
    return pl.pallas_call(
        functools.partial(_merge_kernel, d=d, with_router=router_w is not None),
        out_shape=out_shape,
        grid=(m // tm,),
        in_specs=in_specs,
        out_specs=out_specs,
        compiler_params=_cp(("parallel",)),
        name="merge",
    )(*args)


def _swiglu_hidden(x, wg, wu):
    hg = jnp.dot(x, wg, preferred_element_type=F32)
    hu = jnp.dot(x, wu, preferred_element_type=F32)
    return (hg * _sigmoid(hg) * hu).astype(BF16)


def _ffn_kernel(f_ref, wg_ref, wu_ref, wd_ref, h_ref, gpost_ref, mod_ref, o_ref, acc, *, d):
    j = pl.program_id(1)

    @pl.when(j == 0)
    def _():
        acc[...] = jnp.zeros(acc.shape, F32)

    hid = _swiglu_hidden(f_ref[...], wg_ref[...], wu_ref[...])
    acc[...] += jnp.dot(hid, wd_ref[...], preferred_element_type=F32)

    @pl.when(j == pl.num_programs(1) - 1)
    def _():
        gate = mod_ref[0, :, 5 * d:6 * d]
        o_ref[...] = h_ref[...] + gate * (_rms(acc[...], NORM_EPS) * gpost_ref[...])


def _ffn(f, wg, wu, wd, h, g_post, mods, lat_bpb, n_batch):
    m, d = h.shape
    ff = wg.shape[1]
    tm, tf = ROW_TILE, FFN_TF
    seg = functools.partial(_seg_of_block, lat_blocks_per_batch=lat_bpb, n_batch=n_batch)
    return pl.pallas_call(
        functools.partial(_ffn_kernel, d=d),
        out_shape=jax.ShapeDtypeStruct((m, d), F32),
        grid=(m // tm, ff // tf),
        in_specs=[pl.BlockSpec((tm, d), lambda i, j: (i, 0)),
                  pl.BlockSpec((d, tf), lambda i, j: (0, j)),
                  pl.BlockSpec((d, tf), lambda i, j: (0, j)),
                  pl.BlockSpec((tf, d), lambda i, j: (j, 0)),
                  pl.BlockSpec((tm, d), lambda i, j: (i, 0)),
                  pl.BlockSpec((1, d), lambda i, j: (0, 0)),
                  pl.BlockSpec((1, 1, mods.shape[2]), lambda i, j: (seg(i), 0, 0))],
        out_specs=pl.BlockSpec((tm, d), lambda i, j: (i, 0)),
        scratch_shapes=[pltpu.VMEM((tm, d), F32)],
        compiler_params=_cp(("parallel", "arbitrary")),
        name="ffn_dense",
    )(f, wg, wu, wd, h, g_post, mods)


def _moe_gather_copy(f_hbm, xbuf, sem, slot, src_row, dst_row):
    return pltpu.make_async_copy(f_hbm.at[pl.ds(src_row, 1), :], xbuf.at[slot, pl.ds(dst_row, 1), :],
                                 sem.at[slot])


def _moe_ffn_kernel(blk_e_ref, nused_ref, tok_ref, f_hbm, wg_ref, wu_ref, wd_ref, y_ref, xbuf, xb, acc, sem,
                    *, tm, nj):
    i = pl.program_id(0)
    j = pl.program_id(1)
    nused = nused_ref[0]
    active = i < nused
    slot = i % 2
    per_step = tm // nj

    @pl.when((i == 0) & (j == 0))
    def _():
        def issue(r, carry):
            _moe_gather_copy(f_hbm, xbuf, sem, 0, tok_ref[r], r).start()
            return carry

        lax.fori_loop(0, tm, issue, 0)

    @pl.when((i <= nused) & (j == 0))
    def _():
        pltpu.make_async_copy(f_hbm.at[pl.ds(0, tm), :], xbuf.at[slot], sem.at[slot]).wait()

    @pl.when(active & (j == 0))
    def _():
        xb[...] = xbuf[slot].astype(BF16)
        acc[...] = jnp.zeros(acc.shape, F32)

    @pl.when(active)
    def _():
        base = (i + 1) * tm + j * per_step
        for r in range(per_step):
            _moe_gather_copy(f_hbm, xbuf, sem, 1 - slot, tok_ref[base + r], j * per_step + r).start()
        hid = _swiglu_hidden(xb[...], wg_ref[0], wu_ref[0])
        acc[...] += jnp.dot(hid, wd_ref[0], preferred_element_type=F32)

    @pl.when(j == nj - 1)
    def _():
        y_ref[...] = acc[...]


def _moe_ffn(f, blk_e, nused, tok, wg, wu, wd, n_blocks):
    d = f.shape[1]
    ff = wg.shape[2]
    tm, tf = MOE_TILE, MOE_TF

    nj = ff // tf

    def e_of(i, be, nu):
        return be[jnp.minimum(i, nu[0] - 1)]

    def j_of(i, j, nu):
        return jnp.where(i < nu[0], j, nj - 1)

    grid_spec = pltpu.PrefetchScalarGridSpec(
        num_scalar_prefetch=3,
        grid=(n_blocks, nj),
        in_specs=[pl.BlockSpec(memory_space=pl.ANY),
                  pl.BlockSpec((1, d, tf), lambda i, j, be, nu, tk: (e_of(i, be, nu), 0, j_of(i, j, nu))),
                  pl.BlockSpec((1, d, tf), lambda i, j, be, nu, tk: (e_of(i, be, nu), 0, j_of(i, j, nu))),
                  pl.BlockSpec((1, tf, d), lambda i, j, be, nu, tk: (e_of(i, be, nu), j_of(i, j, nu), 0))],
        out_specs=pl.BlockSpec((tm, d), lambda i, j, be, nu, tk: (i, 0)),
        scratch_shapes=[pltpu.VMEM((2, tm, d), F32), pltpu.VMEM((tm, d), BF16), pltpu.VMEM((tm, d), F32),
                        pltpu.SemaphoreType.DMA((2,))],
    )
    return pl.pallas_call(
        functools.partial(_moe_ffn_kernel, tm=tm, nj=nj),
        out_shape=jax.ShapeDtypeStruct((n_blocks * tm, d), F32),
        grid_spec=grid_spec,
        compiler_params=_cp(("arbitrary", "arbitrary")),
        name="moe_ffn",
    )(blk_e, nused, tok, f, wg, wu, wd)


def _moe_combine_kernel(p0_ref, p1_ref, y_hbm, wt_ref, h_ref, gpost_ref, mod_ref, o_ref, b0, b1, sem, *, tm, d):
    i = pl.program_id(0)
    slot = i % 2

    def issue_block(blk, s):
        def issue(r, carry):
            pltpu.make_async_copy(y_hbm.at[pl.ds(p0_ref[blk * tm + r], 1), :], b0.at[s, pl.ds(r, 1), :],
                                  sem.at[0, s]).start()
            pltpu.make_async_copy(y_hbm.at[pl.ds(p1_ref[blk * tm + r], 1), :], b1.at[s, pl.ds(r, 1), :],
                                  sem.at[1, s]).start()
            return carry

        lax.fori_loop(0, tm, issue, 0)

    @pl.when(i == 0)
    def _():
        issue_block(0, 0)

    @pl.when(i + 1 < pl.num_programs(0))
    def _():
        issue_block(i + 1, 1 - slot)

    pltpu.make_async_copy(y_hbm.at[pl.ds(0, tm), :], b0.at[slot], sem.at[0, slot]).wait()
    pltpu.make_async_copy(y_hbm.at[pl.ds(0, tm), :], b1.at[slot], sem.at[1, slot]).wait()
    wt = wt_ref[...]
    y = b0[slot] * wt[:, 0:1] + b1[slot] * wt[:, 1:2]
    gate = mod_ref[0, :, 5 * d:6 * d]
    o_ref[...] = h_ref[...] + gate * (_rms(y, NORM_EPS) * gpost_ref[...])


def _moe_combine(pos0, pos1, y, wt, h, g_post, mods, n_rows, rows_per_batch, n_batch):
    d = h.shape[1]
    tm = COMBINE_TILE
    seg = functools.partial(_seg_of_block, lat_blocks_per_batch=rows_per_batch // tm, n_batch=n_batch)
    grid_spec = pltpu.PrefetchScalarGridSpec(
        num_scalar_prefetch=2,
        grid=(n_rows // tm,),
        in_specs=[pl.BlockSpec(memory_space=pl.ANY),
                  pl.BlockSpec((tm, LANES), lambda i, a, b: (i, 0)),
                  pl.BlockSpec((tm, d), lambda i, a, b: (i, 0)),
                  pl.BlockSpec((1, d), lambda i, a, b: (0, 0)),
                  pl.BlockSpec((1, 1, mods.shape[2]), lambda i, a, b: (seg(i), 0, 0))],
        out_specs=pl.BlockSpec((tm, d), lambda i, a, b: (i, 0)),
        scratch_shapes=[pltpu.VMEM((2, tm, d), F32), pltpu.VMEM((2, tm, d), F32),
                        pltpu.SemaphoreType.DMA((2, 2))],
    )
    return pl.pallas_call(
        functools.partial(_moe_combine_kernel, tm=tm, d=d),
        out_shape=jax.ShapeDtypeStruct((n_rows, d), F32),
        grid_spec=grid_spec,
        compiler_params=_cp(("arbitrary",)),
        name="moe_combine",
    )(pos0, pos1, y, wt, h, g_post, mods)


def _moe_slots(top_i, tile):
    n = top_i.shape[0]
    a = n * 2
    e_flat = top_i.reshape(a)
    onehot = (e_flat[:, None] == jnp.arange(N_EXPERTS, dtype=jnp.int32)[None, :]).astype(jnp.int32)
    sub = LANES
    blocks = onehot.reshape(a // sub, sub, N_EXPERTS).astype(F32)
    tri = (jnp.arange(sub)[:, None] >= jnp.arange(sub)[None, :]).astype(F32)
    within = jnp.einsum('ij,bjk->bik', tri, blocks).astype(jnp.int32)
    totals = within[:, -1, :]
    csum = (within + (jnp.cumsum(totals, axis=0) - totals)[:, None, :]).reshape(a, N_EXPERTS)
    rank = jnp.sum(csum * onehot, axis=1) - 1
    counts = csum[-1]
    padded = (counts + tile - 1) // tile * tile
    pends = jnp.cumsum(padded)
    pstarts = pends - padded
    dest = (jnp.sum(onehot * pstarts[None, :], axis=1) + rank).astype(jnp.int32)
    n_blocks = a // tile + N_EXPERTS
    tok = jnp.zeros((n_blocks * tile,), jnp.int32).at[dest].set(
        jnp.arange(a, dtype=jnp.int32) // 2, unique_indices=True, mode='promise_in_bounds')
    block_start = jnp.arange(n_blocks, dtype=jnp.int32) * tile
    blk_e = jnp.minimum(jnp.sum((block_start[:, None] >= pends[None, :]).astype(jnp.int32), axis=1),
                        N_EXPERTS - 1)
    nused = (pends[-1:] // tile).astype(jnp.int32)
    return tok, blk_e, nused, dest.reshape(n, 2), n_blocks


def _swa_head_order(w, axis):
    group = SWA_HEADS // SWA_KV_HEADS
    order = [g * group + t for t in range(group) for g in range(SWA_KV_HEADS)]
    shape = w.shape
    w = w.reshape(shape[:axis] + (SWA_HEADS, HEAD_DIM) + shape[axis + 1:])
    return jnp.take(w, jnp.array(order), axis=axis).reshape(shape)


def _rope_tables(n_batch, seq, ctx_len):
    t = jnp.arange(seq, dtype=jnp.int32)
    row = (t // GRID_W).astype(F32)
    col = (t % GRID_W).astype(F32)
    axis_dim = HEAD_DIM // 2
    inv_freq = ROPE_THETA ** (-jnp.arange(0, axis_dim, 2, dtype=F32) / axis_dim)
    dd = jnp.arange(LANES) % HEAD_DIM
    pos = jnp.where((dd // axis_dim)[None, :] == 0, row[:, None], col[:, None])
    ang = pos * inv_freq[dd % (axis_dim // 2)][None, :]
    cos = jnp.cos(ang)
    sin = jnp.where(((dd % axis_dim) < axis_dim // 2)[None, :], -jnp.sin(ang), jnp.sin(ang))
    n_ctx = n_batch * ctx_len
    cos = jnp.concatenate([jnp.tile(cos, (n_batch, 1)), jnp.ones((n_ctx, LANES), F32)], axis=0)
    sin = jnp.concatenate([jnp.tile(sin, (n_batch, 1)), jnp.zeros((n_ctx, LANES), F32)], axis=0)
    return cos, sin


def kernel(x, c, ctx, c_ctx, ada_w, ada_b, pre_mix_g, post_mix_g, pre_ffn_g, post_ffn_g, w_in, swa_sink,
           rwkv_mu_prev, rwkv_mu_next, rwkv_w0, rwkv_w_up, rwkv_a0, rwkv_a_up, rwkv_g_up, rwkv_k_k, rwkv_k_a,
           rwkv_r_k, rwkv_lnx_g, rwkv_lnx_b, diff_lambda, diff_subln_g, proj_swa, proj_rwkv, proj_diff, w_out,
           ffn_w_gate, ffn_w_up, ffn_w_down, router_w, moe_w_gate, moe_w_up, moe_w_down):
    n_batch, seq, d = x.shape
    ctx_len = ctx.shape[1]
    depth = w_in.shape[0]
    lat_rows = n_batch * seq
    lat_bpb = seq // ROW_TILE
    assert seq % ROW_TILE == 0 and (n_batch * ctx_len) % ROW_TILE == 0

    h = jnp.concatenate([x.reshape(lat_rows, d), ctx.reshape(n_batch * ctx_len, d)], axis=0)
    m = h.shape[0]
    cond = jnp.zeros((8, d), F32).at[:n_batch].set(c).at[n_batch].set(c_ctx)
    rope = _rope_tables(n_batch, seq, ctx_len)
    row = lambda a: a.reshape(1, -1)

    o_swa = 0
    o_rwkv = o_swa + SWA_Q + 2 * SWA_KV
    o_diff = o_rwkv + RWKV_COLS
    o_gate = o_diff + 2 * DIFF_QK + DIFF_V
    o_end = o_gate + 3 * d

    for layer in range(depth):
        mods = _ada(cond, ada_w[layer], ada_b[layer].reshape(1, -1))[:, None, :]
        wl = w_in[layer].astype(BF16)
        a = _prenorm(h, row(pre_mix_g[layer]), mods, 0, 1, lat_bpb, n_batch)

        w_swa = jnp.concatenate([_swa_head_order(wl[:, o_swa:o_swa + SWA_Q], axis=1),
                                 wl[:, o_swa + SWA_Q:o_rwkv]], axis=1)
        q_swa, kv_swa = _proj(a, w_swa, (SWA_Q, 2 * SWA_KV), (BF16, BF16), rope=rope,
                              rope_cols=SWA_Q + SWA_KV, scale_cols=SWA_Q, q_scale=HEAD_DIM ** -0.5)
        (p_rwkv,) = _proj(a, wl[:, o_rwkv:o_diff], (RWKV_COLS,), (F32,))
        q_diff, k_diff, v_diff = _proj(a, wl[:, o_diff:o_gate], (DIFF_QK, DIFF_QK, DIFF_V), (BF16,) * 3,
                                       rope=rope, rope_cols=2 * DIFF_QK, scale_cols=DIFF_QK,
                                       q_scale=HEAD_DIM ** -0.5 * math.log2(math.e))
        (gates,) = _proj(a, wl[:, o_gate:o_end], (d,), (BF16,), tn=d)

        ya = _swa(q_swa, kv_swa, swa_sink[layer].astype(F32), n_batch, seq, ctx_len)

        lp = {'rwkv_mu_prev': rwkv_mu_prev[layer], 'rwkv_mu_next': rwkv_mu_next[layer],
              'rwkv_w0': rwkv_w0[layer], 'rwkv_w_up': rwkv_w_up[layer], 'rwkv_a0': rwkv_a0[layer],
              'rwkv_a_up': rwkv_a_up[layer], 'rwkv_g_up': rwkv_g_up[layer], 'rwkv_k_k': rwkv_k_k[layer],
              'rwkv_k_a': rwkv_k_a[layer], 'rwkv_r_k': rwkv_r_k[layer]}
        r_, v_, kk_, ld_, ke_, bb_, g_, bonus_ = _rwkv_prep(p_rwkv, lp, n_batch, seq, ctx_len)
        y_f, y_b = _rwkv_scan(r_, v_, kk_, ld_, ke_, bb_, n_batch, seq, ctx_len)

        lam_vec = diff_lambda[layer].astype(F32)
        lam_init = 0.8 - 0.6 * math.exp(-0.3 * layer)
        lam = (jnp.exp(jnp.sum(lam_vec[0] * lam_vec[1])) - jnp.exp(jnp.sum(lam_vec[2] * lam_vec[3]))
               + lam_init).reshape(1)
        yc = _diff(q_diff, k_diff, v_diff, lam, row(diff_subln_g[layer]), lam_init, n_batch, seq, ctx_len)

        moe_layer = layer % 2 == 1
        jj = layer // 2
        merged = _merge(ya, y_f, y_b, bonus_, g_, rwkv_lnx_g[layer], rwkv_lnx_b[layer], yc, gates, h,
                        _swa_head_order(proj_swa[layer], axis=0).astype(BF16), proj_rwkv[layer].astype(BF16),
                        proj_diff[layer].astype(BF16), w_out[layer].astype(BF16), row(post_mix_g[layer]),
                        row(pre_ffn_g[layer]), mods, F32 if moe_layer else BF16, lat_bpb, n_batch,
                        router_w=router_w[jj] if moe_layer else None)
        h, f = merged[:2]
        need_ctx = layer < depth - 1
        if not moe_layer:
            h = _ffn(f, ffn_w_gate[jj].astype(BF16), ffn_w_up[jj].astype(BF16), ffn_w_down[jj].astype(BF16),
                     h, row(post_ffn_g[layer]), mods, lat_bpb, n_batch)
        else:
            n_tok = m if need_ctx else lat_rows
            top_i, top_w = merged[2][:n_tok], merged[3][:n_tok]
            tok, blk_e, nused, dest, n_blocks = _moe_slots(top_i[:, :2], MOE_TILE)
            y = _moe_ffn(f, blk_e, nused, tok, moe_w_gate[jj].astype(BF16), moe_w_up[jj].astype(BF16),
                         moe_w_down[jj].astype(BF16), n_blocks)
            h = _moe_combine(dest[:, 0], dest[:, 1], y, top_w, h, row(post_ffn_g[layer]), mods, n_tok, seq,
                             n_batch)
    return h[:lat_rows].reshape(n_batch, seq, d)
```

```python
import functools
import math

import jax
import jax.numpy as jnp
from jax import lax
from jax.experimental import pallas as pl
from jax.experimental.pallas import tpu as pltpu

F32 = jnp.float32
BF16 = jnp.bfloat16
HI = lax.Precision.HIGHEST

HEAD_DIM = 64
GRID_W = 64
ROPE_THETA = 10000.0
NORM_EPS = 1e-6
NEG_INF = -1e30
SWA_HEADS = 8
SWA_KV_HEADS = 2
SWA_BLOCK = 128
RWKV_HEADS = 8
RWKV_WIDTH = RWKV_HEADS * HEAD_DIM
DECAY_LORA = 64
AAA_LORA = 64
GATE_LORA = 128
RWKV_GN_EPS = 64e-5
DIFF_HEADS = 4
DIFF_V_DIM = 2 * HEAD_DIM
DIFF_SUBLN_EPS = 1e-5
N_EXPERTS = 8
SWA_Q = SWA_HEADS * HEAD_DIM
SWA_KV = SWA_KV_HEADS * HEAD_DIM
assert SWA_KV_HEADS == 2 and SWA_KV == 128
RWKV_COLS = 3 * RWKV_WIDTH + 2 * DECAY_LORA + 2 * AAA_LORA + GATE_LORA
DIFF_QK = DIFF_HEADS * 2 * HEAD_DIM
DIFF_V = DIFF_HEADS * DIFF_V_DIM

LANES = 128
VMEM_LIMIT = 48 * 1024 * 1024
ROW_TILE = 512
RWKV_CHUNK = 64
RWKV_INV_BASE = 8
RWKV_GROUP = 2
RWKV_STEP_ROWS = 256
DIFF_TQ = 1024
DIFF_TK = 512
DIFF_UNROLL = 2
DIFF_ONES_ROWS = 16
MOE_TILE = 512
MOE_TF = 1792
FFN_TF = 1408
COMBINE_TILE = 256
PREP_TILE = 256
MERGE_TILE = 256


def _cp(sem, **kw):
    return pltpu.CompilerParams(dimension_semantics=sem, vmem_limit_bytes=VMEM_LIMIT, **kw)


def _seg_of_block(i, lat_blocks_per_batch, n_batch):
    return jnp.minimum(i // lat_blocks_per_batch, n_batch)


def _rms(x, eps):
    return x * lax.rsqrt(jnp.mean(x * x, axis=-1, keepdims=True) + eps)


def _sigmoid(x):
    return 1.0 / (1.0 + jnp.exp(-x))


def _group_sum(x, ones_bd):
    hi = x.astype(BF16)
    mid = (x - hi.astype(F32)).astype(BF16)
    rows = x.shape[0]
    parts = jnp.dot(jnp.concatenate([hi, mid], axis=0), ones_bd, preferred_element_type=F32)
    return parts[:rows] + parts[rows:]


def _ada_kernel(x_ref, w_ref, b_ref, o_ref):
    x = x_ref[...]
    s = x * _sigmoid(x)
    o_ref[...] = jnp.dot(s, w_ref[...], precision=HI, preferred_element_type=F32) + b_ref[...]


def _ada(cond, w, b):
    rows, d = cond.shape
    n = w.shape[1]
    return pl.pallas_call(
        _ada_kernel,
        out_shape=jax.ShapeDtypeStruct((rows, n), F32),
        grid=(n // d,),
        in_specs=[pl.BlockSpec((rows, d), lambda j: (0, 0)),
                  pl.BlockSpec((d, d), lambda j: (0, j)),
                  pl.BlockSpec((1, d), lambda j: (0, j))],
        out_specs=pl.BlockSpec((rows, d), lambda j: (0, j)),
        compiler_params=_cp(("parallel",)),
        name="ada_mod",
    )(cond, w, b)


def _prenorm_kernel(h_ref, g_ref, mod_ref, o_ref, *, d, shift_idx, scale_idx):
    y = _rms(h_ref[...], NORM_EPS) * g_ref[...]
    shift = mod_ref[0, :, shift_idx * d:(shift_idx + 1) * d]
    scale = mod_ref[0, :, scale_idx * d:(scale_idx + 1) * d]
    o_ref[...] = (y * (1.0 + scale) + shift).astype(o_ref.dtype)


def _prenorm(h, g, mods, shift_idx, scale_idx, lat_bpb, n_batch):
    m, d = h.shape
    tm = ROW_TILE
    seg = functools.partial(_seg_of_block, lat_blocks_per_batch=lat_bpb, n_batch=n_batch)
    return pl.pallas_call(
        functools.partial(_prenorm_kernel, d=d, shift_idx=shift_idx, scale_idx=scale_idx),
        out_shape=jax.ShapeDtypeStruct((m, d), BF16),
        grid=(m // tm,),
        in_specs=[pl.BlockSpec((tm, d), lambda i: (i, 0)),
                  pl.BlockSpec((1, d), lambda i: (0, 0)),
                  pl.BlockSpec((1, 1, mods.shape[2]), lambda i: (seg(i), 0, 0))],
        out_specs=pl.BlockSpec((tm, d), lambda i: (i, 0)),
        compiler_params=_cp(("parallel",)),
        name="prenorm",
    )(h, g, mods)


def _proj_kernel(*refs, splits, rope_cols, scale_cols, q_scale):
    if rope_cols:
        a_ref, w_ref, cos_ref, sin_ref = refs[:4]
        outs = refs[4:]
    else:
        a_ref, w_ref = refs[:2]
        outs = refs[2:]
    y = jnp.dot(a_ref[...], w_ref[...], preferred_element_type=F32)
    tm, tn = y.shape
    if rope_cols:
        cos = cos_ref[...]
        sin = sin_ref[...]
        lane = lax.broadcasted_iota(jnp.int32, (tm, LANES), 1)
        first_half = (lane % 32) < 16
        pieces = []
        for c in range(tn // LANES):
            yc = y[:, c * LANES:(c + 1) * LANES]
            if c * LANES < rope_cols:
                partner = jnp.where(first_half, pltpu.roll(yc, LANES - 16, 1), pltpu.roll(yc, 16, 1))
                yc = yc * cos + partner * sin
            if c * LANES < scale_cols:
                yc = yc * q_scale
            pieces.append(yc)
        y = jnp.concatenate(pieces, axis=1)
    start = 0
    for o_ref, width in zip(outs, splits):
        o_ref[...] = y[:, start:start + width].astype(o_ref.dtype)
        start += width


def _proj(a, w, splits, dtypes, rope=None, rope_cols=0, scale_cols=0, q_scale=1.0, tn=None):
    m, k = a.shape
    n = w.shape[1]
    tm = ROW_TILE
    tn = n if tn is None else tn
    assert sum(splits) == tn and (len(splits) == 1 or tn == n)
    in_specs = [pl.BlockSpec((tm, k), lambda i, j: (i, 0)),
                pl.BlockSpec((k, tn), lambda i, j: (0, j))]
    args = [a, w]
    if rope_cols:
        in_specs += [pl.BlockSpec((tm, LANES), lambda i, j: (i, 0))] * 2
        args += list(rope)
    out_specs = []
    out_shape = []
    if len(splits) == 1:
        out_specs.append(pl.BlockSpec((tm, tn), lambda i, j: (i, j)))
        out_shape.append(jax.ShapeDtypeStruct((m, n), dtypes[0]))
    else:
        for width, dt in zip(splits, dtypes):
            out_specs.append(pl.BlockSpec((tm, width), lambda i, j: (i, 0)))
            out_shape.append(jax.ShapeDtypeStruct((m, width), dt))
    res = pl.pallas_call(
        functools.partial(_proj_kernel, splits=tuple(splits), rope_cols=rope_cols, scale_cols=scale_cols,
                          q_scale=q_scale),
        out_shape=out_shape,
        grid=(m // tm, n // tn),
        in_specs=in_specs,
        out_specs=out_specs,
        compiler_params=_cp(("parallel", "parallel")),
        name="proj",
    )(*args)
    return res


def _swa_kernel(sink_ref, q_ref, kp_ref, km_ref, kn_ref, kx_ref, o_ref, *, nb, n_lat_pairs):
    i = pl.program_id(0)
    is_lat = i < n_lat_pairs
    n0 = (2 * i) % nb
    blk = SWA_BLOCK
    kvx = kx_ref[...]
    windows = [jnp.concatenate([kp_ref[...], km_ref[...], kvx], axis=0),
               jnp.concatenate([km_ref[...], kn_ref[...], kvx], axis=0)]
    nkeys = windows[0].shape[0]
    r = lax.broadcasted_iota(jnp.int32, (blk, nkeys), 0)
    j = lax.broadcasted_iota(jnp.int32, (blk, nkeys), 1)
    band = (j >= r) & (j <= r + 2 * blk) & is_lat
    ctx_keys = j >= 3 * blk
    edges = [j >= jnp.where(n0 > 0, 0, blk), j < jnp.where(n0 + 1 < nb - 1, 3 * blk, 2 * blk)]
    low_half = lax.broadcasted_iota(jnp.int32, (blk, SWA_KV), 1) < HEAD_DIM
    zero = jnp.zeros((blk, SWA_KV), q_ref.dtype)
    group = SWA_HEADS // SWA_KV_HEADS
    parts = []
    for b in range(2):
        kv = windows[b]
        k_t = kv[:, :SWA_KV]
        v_aug = jnp.concatenate([kv[:, SWA_KV:], jnp.ones((nkeys, SWA_KV), kv.dtype)], axis=1)
        bias = jnp.where((band & edges[b]) | ctx_keys, 0.0, NEG_INF)
        q = q_ref[b * blk:(b + 1) * blk, :]
        for t in range(group):
            qt = q[:, t * SWA_KV:(t + 1) * SWA_KV]
            rows = jnp.concatenate([jnp.where(low_half, qt, zero), jnp.where(low_half, zero, qt)], axis=0)
            sk = jnp.concatenate([jnp.full((blk, 1), sink_ref[t], F32),
                                  jnp.full((blk, 1), sink_ref[group + t], F32)], axis=0)
            parts.append((rows, k_t, v_aug, bias, sk))
    scores = [lax.dot_general(rows, k_t, (((1,), (1,)), ((), ())), preferred_element_type=F32)
              for rows, k_t, _, _, _ in parts]
    tiles = []
    for sc, (_, _, v_aug, bias, sk) in zip(scores, parts):
        s = (sc.reshape(2, blk, nkeys) + bias[None]).reshape(2 * blk, nkeys)
        mx = jnp.maximum(jnp.max(s, axis=-1, keepdims=True), sk)
        p = jnp.exp((s - mx).astype(BF16))
        oa = jnp.dot(p, v_aug, preferred_element_type=F32)
        den = oa[:, SWA_KV:] + jnp.exp(sk - mx)
        on = oa[:, :SWA_KV] * (1.0 / den)
        tiles.append(jnp.where(low_half, on[:blk], on[blk:]))
    o_ref[...] = jnp.concatenate([jnp.concatenate(tiles[:group], axis=1), jnp.concatenate(tiles[group:], axis=1)],
                                 axis=0).astype(o_ref.dtype)


def _swa(q, kv, sink, n_batch, seq, ctx_len):
    m = q.shape[0]
    blk = SWA_BLOCK
    nb = seq // blk
    n_lat = n_batch * nb
    cpb = ctx_len // blk
    assert nb % 2 == 0 and cpb % 2 == 0

    def batch_of(i):
        return jnp.where(2 * i < n_lat, (2 * i) // nb, (2 * i - n_lat) // cpb)

    def prev_idx(i, s):
        n0 = (2 * i) % nb
        return (jnp.where(2 * i < n_lat, batch_of(i) * nb + jnp.maximum(n0 - 1, 0), 2 * i), 0)

    def next_idx(i, s):
        n0 = (2 * i) % nb
        return (jnp.where(2 * i < n_lat, batch_of(i) * nb + jnp.minimum(n0 + 2, nb - 1), 2 * i), 0)

    def ctx_idx(i, s):
        return (n_batch * seq // ctx_len + batch_of(i), 0)

    grid_spec = pltpu.PrefetchScalarGridSpec(
        num_scalar_prefetch=1,
        grid=(m // (2 * blk),),
        in_specs=[pl.BlockSpec((2 * blk, SWA_Q), lambda i, s: (i, 0)),
                  pl.BlockSpec((blk, 2 * SWA_KV), prev_idx),
                  pl.BlockSpec((2 * blk, 2 * SWA_KV), lambda i, s: (i, 0)),
                  pl.BlockSpec((blk, 2 * SWA_KV), next_idx),
                  pl.BlockSpec((ctx_len, 2 * SWA_KV), ctx_idx)],
        out_specs=pl.BlockSpec((2 * blk, SWA_Q), lambda i, s: (i, 0)),
    )
    return pl.pallas_call(
        functools.partial(_swa_kernel, nb=nb, n_lat_pairs=n_lat // 2),
        out_shape=jax.ShapeDtypeStruct((m, SWA_Q), BF16),
        grid_spec=grid_spec,
        compiler_params=_cp(("parallel",)),
        name="swa_attn",
    )(sink, q, kv, kv, kv, kv)


def _diff_kernel(*refs, n_lat_chunks, coef):
    if n_lat_chunks:
        lam_ref, gcol_ref, q_ref, kc_ref, vtc_ref, kl_ref, vtl_ref, o_ref, m_sc, acc_sc, st_a, st_b = refs
    else:
        lam_ref, gcol_ref, q_ref, kc_ref, vtc_ref, o_ref, m_sc, acc_sc, st_a, st_b = refs
    q = q_ref[...]
    tq = q.shape[0]
    dv = DIFF_V_DIM
    lane = lax.broadcasted_iota(jnp.int32, q.shape, 1)
    zero = jnp.zeros_like(q)
    qq = jnp.concatenate([jnp.where(lane < HEAD_DIM, q, zero), jnp.where(lane >= HEAD_DIM, q, zero)], axis=0)

    def scores(k):
        return lax.dot_general(k, qq, (((1,), (1,)), ((), ())), preferred_element_type=F32)

    def accumulate(st_ref, vt):
        st = st_ref[0:vt.shape[1], :]
        m_old = m_sc[...]
        m_new = jnp.maximum(m_old, jnp.max(st, axis=0, keepdims=True))
        alpha = jnp.exp2(m_old - m_new)
        pt = jnp.exp2((st - m_new).astype(BF16))
        acc_sc[...] = alpha * acc_sc[...] + jnp.dot(vt, pt, preferred_element_type=F32)
        m_sc[...] = m_new

    tk = st_a.shape[0]
    k_lat = lambda c: kl_ref[pl.ds(pl.multiple_of(c * tk, tk), tk), :]
    m_sc[...] = jnp.full(m_sc.shape, NEG_INF, F32)
    acc_sc[...] = jnp.zeros(acc_sc.shape, F32)
    bufs = (st_a, st_b)
    n = n_lat_chunks
    if n:
        unroll = DIFF_UNROLL
        st_a[...] = scores(k_lat(0))
        n_trips = (n - 1) // unroll

        def body(j, carry):
            for u in range(unroll):
                c = j * unroll + u
                bufs[(u + 1) % 2][...] = scores(k_lat(c + 1))
                accumulate(bufs[u % 2], vtl_ref[0, c])
            return carry

        lax.fori_loop(0, n_trips, body, 0)
        for c in range(n_trips * unroll, n):
            if c + 1 < n:
                bufs[(c + 1) % 2][...] = scores(k_lat(c + 1))
            else:
                bufs[(c + 1) % 2][0:kc_ref.shape[0], :] = scores(kc_ref[...])
            accumulate(bufs[c % 2], vtl_ref[0, c])
    else:
        st_a[0:kc_ref.shape[0], :] = scores(kc_ref[...])
    accumulate(bufs[n % 2], vtc_ref[0, 0])
    acc = acc_sc[...]
    ot = acc[:dv] * (1.0 / acc[dv:dv + 1])
    odt = ot[:, :tq] - lam_ref[0] * ot[:, tq:]
    ms = jnp.mean(odt * odt, axis=0, keepdims=True)
    yt = odt * lax.rsqrt(ms + DIFF_SUBLN_EPS) * (gcol_ref[...] * coef)
    o_ref[...] = yt.T.astype(o_ref.dtype)


def _diff(q, k, v, lam, subln_g, lam_init, n_batch, seq, ctx_len):
    tk = min(DIFF_TK, seq)
    assert seq % tk == 0 and tk % ctx_len == 0
    dv = DIFF_V_DIM
    dva = dv + DIFF_ONES_ROWS
    lat_rows = n_batch * seq

    def transposed_chunks(rows, size):
        t = rows.reshape(rows.shape[0] // size, size, DIFF_HEADS, dv).transpose(2, 0, 3, 1)
        return jnp.concatenate([t, jnp.ones(t.shape[:2] + (DIFF_ONES_ROWS, size), t.dtype)], axis=2)

    vt_lat = transposed_chunks(v[:lat_rows], tk)
    vt_ctx = transposed_chunks(v[lat_rows:], ctx_len)
    gcol = subln_g.reshape(dv, 1)
    lat_chunks = seq // tk
    ctx0 = n_batch * seq // ctx_len

    def call(tq, n_q, q_block0, batch_of, with_lat):
        in_specs = [pl.BlockSpec((dv, 1), lambda h, i, s: (0, 0)),
                    pl.BlockSpec((tq, LANES), lambda h, i, s: (q_block0 + i, h)),
                    pl.BlockSpec((ctx_len, LANES), lambda h, i, s: (ctx0 + batch_of(i), h)),
                    pl.BlockSpec((1, 1, dva, ctx_len), lambda h, i, s: (h, batch_of(i), 0, 0))]
        args = [lam, gcol, q, k, vt_ctx]
        if with_lat:
            in_specs += [pl.BlockSpec((seq, LANES), lambda h, i, s: (batch_of(i), h)),
                         pl.BlockSpec((1, lat_chunks, dva, tk), lambda h, i, s: (h, batch_of(i), 0, 0))]
            args += [k, vt_lat]
        grid_spec = pltpu.PrefetchScalarGridSpec(
            num_scalar_prefetch=1,
            grid=(DIFF_HEADS, n_q),
            in_specs=in_specs,
            out_specs=pl.BlockSpec((tq, LANES), lambda h, i, s: (i, h)),
            scratch_shapes=[pltpu.VMEM((1, 2 * tq), F32), pltpu.VMEM((dva, 2 * tq), F32),
                            pltpu.VMEM((tk, 2 * tq), F32), pltpu.VMEM((tk, 2 * tq), F32)],
        )
        return pl.pallas_call(
            functools.partial(_diff_kernel, n_lat_chunks=lat_chunks if with_lat else 0, coef=1.0 - lam_init),
            out_shape=jax.ShapeDtypeStruct((n_q * tq, DIFF_V), BF16),
            grid_spec=grid_spec,
            compiler_params=_cp(("parallel", "arbitrary")),
            name="diff_attn" if with_lat else "diff_attn_ctx",
        )(*args)

    tq = min(DIFF_TQ, seq)
    y_lat = call(tq, n_batch * seq // tq, 0, lambda i: i // (seq // tq), True)
    y_ctx = call(ctx_len, n_batch, ctx0, lambda i: i, False)
    return jnp.concatenate([y_lat, y_ctx], axis=0)


def _rwkv_prep_kernel(p_ref, hp_ref, hn_ref, mup_ref, mun_ref, kk_w_ref, ka_ref, rk_ref, w0_ref, a0_ref,
                      wup_ref, aup_ref, gup_ref, ones_ref,
                      r_ref, v_ref, kk_ref, ld_ref, ke_ref, bb_ref, g_ref, bonus_ref, sc,
                      *, tm, lat_rows, seq, ctx_len):
    i = pl.program_id(0)
    w = RWKV_WIDTH
    sc[0:8, :] = hp_ref[...]
    sc[8:8 + tm, :] = p_ref[...]
    sc[8 + tm:16 + tm, :] = hn_ref[...]
    p = p_ref[...]
    prev = sc[7:7 + tm, :]
    nxt = sc[9:9 + tm, :]
    row = i * tm + lax.broadcasted_iota(jnp.int32, (tm, 1), 0)
    pos = jnp.where(row < lat_rows, row % seq, (row - lat_rows) % ctx_len)
    seg_len = jnp.where(row < lat_rows, seq, ctx_len)
    prev = jnp.where(pos == 0, 0.0, prev)
    nxt = jnp.where(pos == seg_len - 1, 0.0, nxt)
    ps = p + mup_ref[...] * (prev - p) + mun_ref[...] * (nxt - p)

    r = ps[:, 0:w]
    k = ps[:, w:2 * w]
    v = ps[:, 2 * w:3 * w]
    wd = ps[:, 3 * w:3 * w + 2 * DECAY_LORA]
    ad = ps[:, 3 * w + 2 * DECAY_LORA:3 * w + 2 * DECAY_LORA + 2 * AAA_LORA]
    gd = ps[:, 3 * w + 2 * DECAY_LORA + 2 * AAA_LORA:]

    ones_bd = ones_ref[...]
    g = jnp.dot(_sigmoid(gd).astype(BF16), gup_ref[...], preferred_element_type=F32)
    kk = k * kk_w_ref[...]
    ss = _group_sum(kk * kk, ones_bd)
    kk = kk / jnp.maximum(jnp.sqrt(ss), 1e-12)
    w_raw = w0_ref[...] + jnp.dot(jnp.tanh(wd).astype(BF16), wup_ref[...], preferred_element_type=F32)
    a_raw = a0_ref[...] + jnp.dot(ad.astype(BF16), aup_ref[...], preferred_element_type=F32)
    ld = -math.exp(-0.5) * _sigmoid(w_raw)
    a = _sigmoid(a_raw)
    ka = ka_ref[...]
    ke_sum = jnp.zeros_like(k)
    for d in range(2):
        a_d = a[:, d * w:(d + 1) * w]
        ke = k * (1.0 + (a_d - 1.0) * ka)
        ld_ref[d] = ld[:, d * w:(d + 1) * w]
        ke_ref[d] = ke
        bb_ref[d] = kk * a_d
        ke_sum = ke_sum + ke
    rk = _group_sum(r * ke_sum * rk_ref[...], ones_bd)
    r_ref[...] = r
    v_ref[...] = v
    kk_ref[...] = kk
    g_ref[...] = g
    bonus_ref[...] = rk * v


def _rwkv_prep(p, lp, n_batch, seq, ctx_len):
    m, cols = p.shape
    tm = PREP_TILE
    w = RWKV_WIDTH
    lat_rows = n_batch * seq
    row = lambda a: a.reshape(1, -1).astype(F32)

    def blockdiag(u):
        z = jnp.zeros_like(u[0])
        return jnp.concatenate([jnp.concatenate([u[0], z], axis=1), jnp.concatenate([z, u[1]], axis=1)], axis=0)

    head = jnp.arange(w) // HEAD_DIM
    ones_bd = (head[:, None] == head[None, :]).astype(BF16)
    full = lambda shape: pl.BlockSpec(shape, lambda i: (0,) * len(shape))
    nb8 = m // 8
    outs = pl.pallas_call(
        functools.partial(_rwkv_prep_kernel, tm=tm, lat_rows=lat_rows, seq=seq, ctx_len=ctx_len),
        out_shape=[jax.ShapeDtypeStruct((m, w), F32)] * 3
        + [jax.ShapeDtypeStruct((2, m, w), F32)] * 3
        + [jax.ShapeDtypeStruct((m, w), F32)] * 2,
        grid=(m // tm,),
        in_specs=[pl.BlockSpec((tm, cols), lambda i: (i, 0)),
                  pl.BlockSpec((8, cols), lambda i: (jnp.maximum(i * (tm // 8) - 1, 0), 0)),
                  pl.BlockSpec((8, cols), lambda i: (jnp.minimum((i + 1) * (tm // 8), nb8 - 1), 0)),
                  full((1, cols)), full((1, cols)), full((1, w)), full((1, w)), full((1, w)),
                  full((1, 2 * w)), full((1, 2 * w)),
                  full((2 * DECAY_LORA, 2 * w)), full((2 * AAA_LORA, 2 * w)), full((GATE_LORA, w)),
                  full((w, w))],
        out_specs=[pl.BlockSpec((tm, w), lambda i: (i, 0))] * 3
        + [pl.BlockSpec((2, tm, w), lambda i: (0, i, 0))] * 3
        + [pl.BlockSpec((tm, w), lambda i: (i, 0))] * 2,
        scratch_shapes=[pltpu.VMEM((tm + 16, cols), F32)],
        compiler_params=_cp(("parallel",)),
        name="rwkv_prep",
    )(p, p, p, row(lp['rwkv_mu_prev']), row(lp['rwkv_mu_next']), row(lp['rwkv_k_k']), row(lp['rwkv_k_a']),
      row(lp['rwkv_r_k']), row(lp['rwkv_w0']), row(lp['rwkv_a0']),
      blockdiag(lp['rwkv_w_up']).astype(BF16), blockdiag(lp['rwkv_a_up']).astype(BF16),
      lp['rwkv_g_up'].astype(BF16), ones_bd)
    return outs


def _rwkv_chunk_prep(off, r_ref, v_ref, kk_ref, ld_ref, ke_ref, bb_ref, incl):
    c = RWKV_CHUNK
    ld = ld_ref[0, pl.ds(off, c), :]
    r = r_ref[pl.ds(off, c), :]
    v = v_ref[pl.ds(off, c), :]
    kk = kk_ref[pl.ds(off, c), :]
    ke = ke_ref[0, pl.ds(off, c), :]
    bb = bb_ref[0, pl.ds(off, c), :]
    cum = jnp.dot(incl, ld, precision=HI, preferred_element_type=F32)
    tot = jnp.sum(ld, axis=0, keepdims=True)
    rt = r * jnp.exp(cum)
    einv = jnp.exp(-cum)
    etail = jnp.exp(tot - cum)
    return dict(at=(-kk * jnp.exp(cum - ld)).astype(BF16), rt=rt, rt_b=rt.astype(BF16),
                bt=(bb * einv).astype(BF16), kt=(ke * einv).astype(BF16),
                bh=(bb * etail).astype(BF16), kh=(ke * etail).astype(BF16),
                v=v, v_b=v.astype(BF16), wtot=jnp.exp(tot))


def _rwkv_chunk_group(offs, in_refs, y_refs, s_sc):
    c = RWKV_CHUNK
    hd = HEAD_DIM
    n_group = len(offs[0])
    ti = lax.broadcasted_iota(jnp.int32, (c, c), 0)
    tj = lax.broadcasted_iota(jnp.int32, (c, c), 1)
    eye = (ti == tj).astype(F32)
    incl = [(tj <= ti).astype(F32), (tj >= ti).astype(F32)]
    strict = [m - eye for m in incl]
    diag_mask = ((ti // RWKV_INV_BASE) == (tj // RWKV_INV_BASE)).astype(F32)
    off_masks = []
    sz = RWKV_INV_BASE
    while sz < c:
        off_masks.append((((ti // (2 * sz)) == (tj // (2 * sz))) & ((ti // sz) != (tj // sz))).astype(F32))
        sz *= 2

    pre = {(d, g): _rwkv_chunk_prep(offs[d][g], *in_refs[d], incl[d])
           for d in range(2) for g in range(n_group)}
    lanes = [(d, g, h) for g in range(n_group) for d in range(2) for h in range(RWKV_HEADS)]
    sl = lambda h: slice(h * hd, (h + 1) * hd)
    get = lambda name: [pre[d, g][name][:, sl(h)] for d, g, h in lanes]
    at, rt, rt_b, bt, kt, bh, kh, v, v_b = (get(n) for n in ('at', 'rt', 'rt_b', 'bt', 'kt', 'bh', 'kh', 'v', 'v_b'))
    nl = range(len(lanes))
    nt = (((1,), (1,)), ((), ()))
    bdot = lambda x, y: jnp.dot(x.astype(BF16), y.astype(BF16), preferred_element_type=F32)

    gm = [lax.dot_general(jnp.concatenate([at[i], rt_b[i]], axis=0), jnp.concatenate([bt[i], kt[i]], axis=0), nt,
                          preferred_element_type=F32) for i in nl]
    zz0 = [bdot(v[i].T, kh[i]) for i in nl]
    aab = [gm[i][:c, :c] * strict[lanes[i][0]] for i in nl]
    aak = [gm[i][:c, c:] * strict[lanes[i][0]] for i in nl]
    arb = [gm[i][c:, :c] * incl[lanes[i][0]] for i in nl]
    ark = [gm[i][c:, c:] * incl[lanes[i][0]] for i in nl]
    av = [bdot(jnp.concatenate([aak[i], ark[i]], axis=0), v_b[i]) for i in nl]
    pw = [aab[i] * diag_mask for i in nl]
    tm_ = [eye + pw[i] for i in nl]
    for _ in range(int(math.log2(RWKV_INV_BASE)) - 1):
        pw = [bdot(pw[i], pw[i]) for i in nl]
        tm_ = [tm_[i] + bdot(tm_[i], pw[i]) for i in nl]
    for off_mask in off_masks:
        tn = [bdot(tm_[i], aab[i] * off_mask) for i in nl]
        tm_ = [tm_[i] + bdot(tn[i], tm_[i]) for i in nl]
    au = [bdot(tm_[i], jnp.concatenate([at[i], av[i][:c].astype(BF16)], axis=1)) for i in nl]
    ry = [bdot(arb[i], au[i]) for i in nl]
    mz = [bdot(au[i].T, bh[i]) for i in nl]
    rbar = [(rt[i] + ry[i][:, :hd]).astype(BF16) for i in nl]
    ybar = [ry[i][:, hd:] + av[i][c:] for i in nl]
    mm = [(eye * pre[lanes[i][0], lanes[i][1]]['wtot'][:, sl(lanes[i][2])] + mz[i][:hd]).astype(BF16) for i in nl]
    zz = [mz[i][hd:] + zz0[i] for i in nl]
    state = {(d, h): s_sc[d, h] for d in range(2) for h in range(RWKV_HEADS)}
    ys = {}
    for g in range(n_group):
        idx = [i for i in nl if lanes[i][1] == g]
        s_b = {i: state[lanes[i][0], lanes[i][2]].astype(BF16) for i in idx}
        for i in idx:
            ys[i] = lax.dot_general(rbar[i], s_b[i], nt, preferred_element_type=F32) + ybar[i]
        for i in idx:
            state[lanes[i][0], lanes[i][2]] = jnp.dot(s_b[i], mm[i], preferred_element_type=F32) + zz[i]
    for (d, h), val in state.items():
        s_sc[d, h] = val
    for d in range(2):
        for g in range(n_group):
            y_refs[d][0, pl.ds(offs[d][g], c), :] = jnp.concatenate(
                [ys[i] for i in nl if lanes[i][0] == d and lanes[i][1] == g], axis=1)


def _rwkv_scan_kernel(rf, vf, kf, ldf, kef, bbf, rb, vb, kb, ldb, keb, bbb, yf, yb, s_sc, *, n_chunks):
    @pl.when(pl.program_id(1) == 0)
    def _():
        s_sc[...] = jnp.zeros(s_sc.shape, F32)

    group = RWKV_GROUP

    def body(t, carry):
        off_f = [pl.multiple_of((t * group + g) * RWKV_CHUNK, RWKV_CHUNK) for g in range(group)]
        off_b = [pl.multiple_of((n_chunks - 1 - t * group - g) * RWKV_CHUNK, RWKV_CHUNK) for g in range(group)]
        _rwkv_chunk_group((off_f, off_b), ((rf, vf, kf, ldf, kef, bbf), (rb, vb, kb, ldb, keb, bbb)),
                          (yf, yb), s_sc)
        return carry

    lax.fori_loop(0, n_chunks // group, body, 0)


def _rwkv_scan(r, v, kk, ld, ke, bb, n_batch, seq, ctx_len):
    m, w = r.shape
    ts = RWKV_STEP_ROWS
    assert ctx_len == ts
    lpb = seq // ts
    ctx0 = n_batch * seq // ts
    nj = 1 + lpb

    def fwd(b, j):
        return jnp.where(j == 0, ctx0 + b, b * lpb + j - 1)

    def bwd(b, j):
        return jnp.where(j == 0, ctx0 + b, b * lpb + lpb - j)

    shared = lambda f: pl.BlockSpec((ts, w), lambda b, j: (f(b, j), 0))
    per_dir = lambda f, d: pl.BlockSpec((1, ts, w), lambda b, j: (d, f(b, j), 0))
    y = pl.pallas_call(
        functools.partial(_rwkv_scan_kernel, n_chunks=ts // RWKV_CHUNK),
        out_shape=[jax.ShapeDtypeStruct((1, m, w), F32)] * 2,
        grid=(n_batch, nj),
        in_specs=[shared(fwd), shared(fwd), shared(fwd), per_dir(fwd, 0), per_dir(fwd, 0), per_dir(fwd, 0),
                  shared(bwd), shared(bwd), shared(bwd), per_dir(bwd, 1), per_dir(bwd, 1), per_dir(bwd, 1)],
        out_specs=[pl.BlockSpec((1, ts, w), lambda b, j: (0, fwd(b, j), 0)),
                   pl.BlockSpec((1, ts, w), lambda b, j: (0, bwd(b, j), 0))],
        scratch_shapes=[pltpu.VMEM((2, RWKV_HEADS, HEAD_DIM, HEAD_DIM), F32)],
        compiler_params=_cp(("parallel", "arbitrary")),
        name="rwkv_scan",
    )(r, v, kk, ld, ke, bb, r, v, kk, ld, ke, bb)
    return y


def _rwkv_finish(y, bonus, g, lnx_g, lnx_b, ones_bd):
    inv = 1.0 / HEAD_DIM
    mu = _group_sum(y, ones_bd) * inv
    yc = y - mu
    var = _group_sum(yc * yc, ones_bd) * inv
    yn = yc * lax.rsqrt(var + RWKV_GN_EPS) * lnx_g + lnx_b
    return (yn + bonus) * g


def _top2(logits):
    lane = lax.broadcasted_iota(jnp.int32, logits.shape, 1)
    logits = jnp.where(lane < N_EXPERTS, logits, -jnp.inf)
    m1 = jnp.max(logits, axis=-1, keepdims=True)
    i1 = jnp.min(jnp.where(logits == m1, lane, LANES), axis=-1, keepdims=True)
    rest = jnp.where(lane == i1, -jnp.inf, logits)
    m2 = jnp.max(rest, axis=-1, keepdims=True)
    i2 = jnp.min(jnp.where(rest == m2, lane, LANES), axis=-1, keepdims=True)
    e = jnp.exp(m2 - m1)
    w1 = 1.0 / (1.0 + e)
    w2 = e / (1.0 + e)
    idx = jnp.where(lane == 0, i1, jnp.where(lane == 1, i2, 0))
    wts = jnp.where(lane == 0, w1, jnp.where(lane == 1, w2, 0.0))
    return idx, wts


def _merge_kernel(*refs, d, with_router):
    (ya_ref, yf_ref, yb_ref, bonus_ref, g_ref, lg_ref, lb_ref, ones_ref, yc_ref, gt_ref, h_ref,
     pa_ref, pb_ref, pc_ref, wo_ref, gpost_ref, gpre_ref, mod_ref) = refs[:18]
    if with_router:
        rwh_ref, rwl_ref, h_out, f_out, idx_out, wt_out = refs[18:]
    else:
        h_out, f_out = refs[18:]
    yb = _rwkv_finish(yf_ref[0] + yb_ref[0], bonus_ref[...], g_ref[...], lg_ref[...], lb_ref[...], ones_ref[...])
    gates = gt_ref[...].astype(F32)
    merged = (_sigmoid(gates[:, 0:d]) * jnp.dot(ya_ref[...], pa_ref[...], preferred_element_type=F32)
              + _sigmoid(gates[:, d:2 * d]) * jnp.dot(yb.astype(BF16), pb_ref[...], preferred_element_type=F32)
              + _sigmoid(gates[:, 2 * d:3 * d]) * jnp.dot(yc_ref[...], pc_ref[...], preferred_element_type=F32))
    out = jnp.dot(merged.astype(BF16), wo_ref[...], preferred_element_type=F32)
    mod = lambda idx: mod_ref[0, :, idx * d:(idx + 1) * d]
    hn = h_ref[...] + mod(2) * (_rms(out, NORM_EPS) * gpost_ref[...])
    h_out[...] = hn
    f = (_rms(hn, NORM_EPS) * gpre_ref[...]) * (1.0 + mod(4)) + mod(3)
    f_out[...] = f.astype(f_out.dtype)
    if with_router:
        rows = f.shape[0]
        hi = f.astype(BF16)
        mid = (f - hi.astype(F32)).astype(BF16)
        part = jnp.dot(jnp.concatenate([hi, mid], axis=0), rwh_ref[...], preferred_element_type=F32)
        logits = part[:rows] + part[rows:] + jnp.dot(hi, rwl_ref[...], preferred_element_type=F32)
        idx, wts = _top2(logits)
        idx_out[...] = idx
        wt_out[...] = wts


def _merge(ya, yf, yb, bonus, g, lnx_g, lnx_b, yc, gates, h, pa, pb, pc, wo, g_post, g_pre, mods, f_dtype,
           lat_bpb, n_batch, router_w=None):
    m, d = h.shape
    w = bonus.shape[1]
    tm = MERGE_TILE
    bpb = lat_bpb * (ROW_TILE // tm)
    seg = functools.partial(_seg_of_block, lat_blocks_per_batch=bpb, n_batch=n_batch)
    rows = lambda width: pl.BlockSpec((tm, width), lambda i: (i, 0))
    rows3 = pl.BlockSpec((1, tm, w), lambda i: (0, i, 0))
    full = lambda a: pl.BlockSpec(a.shape, lambda i: (0, 0))
    head = jnp.arange(w) // HEAD_DIM
    ones_bd = (head[:, None] == head[None, :]).astype(BF16)
    args = [ya, yf, yb, bonus, g, lnx_g.reshape(1, w), lnx_b.reshape(1, w), ones_bd, yc, gates, h,
            pa, pb, pc, wo, g_post, g_pre, mods]
    in_specs = [rows(ya.shape[1]), rows3, rows3, rows(w), rows(w), full(args[5]), full(args[6]), full(ones_bd),
                rows(yc.shape[1]), rows(gates.shape[1]), rows(d), full(pa), full(pb), full(pc), full(wo),
                full(g_post), full(g_pre), pl.BlockSpec((1, 1, mods.shape[2]), lambda i: (seg(i), 0, 0))]
    out_shape = [jax.ShapeDtypeStruct((m, d), F32), jax.ShapeDtypeStruct((m, d), f_dtype)]
    out_specs = [rows(d), rows(d)]
    if router_w is not None:
        w_pad = jnp.zeros((d, LANES), F32).at[:, :N_EXPERTS].set(router_w)
        w_hi = w_pad.astype(BF16)
        w_lo = (w_pad - w_hi.astype(F32)).astype(BF16)
        args += [w_hi, w_lo]
        in_specs += [full(w_hi), full(w_lo)]
        out_shape += [jax.ShapeDtypeStruct((m, LANES), jnp.int32), jax.ShapeDtypeStruct((m, LANES), F32)]
        out_specs += [rows(LANES), rows(LANES)]
    return pl.pallas_call(
        functools.partial(_merge_kernel, d=d, with_router=router_w is not None),
        out_shape=out_shape,
        grid=(m // tm,),
        in_specs=in_specs,
        out_specs=out_specs,
        compiler_params=_cp(("parallel",)),
        name="merge",
    )(*args)


def _swiglu_hidden(x, wg, wu):
    hg = jnp.dot(x, wg, preferred_element_type=F32)
    hu = jnp.dot(x, wu, preferred_element_type=F32)
    return (hg * _sigmoid(hg) * hu).astype(BF16)


def _ffn_kernel(f_ref, wg_ref, wu_ref, wd_ref, h_ref, gpost_ref, mod_ref, o_ref, acc, *, d):
    j = pl.program_id(1)

    @pl.when(j == 0)
    def _():
        acc[...] = jnp.zeros(acc.shape, F32)

    hid = _swiglu_hidden(f_ref[...], wg_ref[...], wu_ref[...])
    acc[...] += jnp.dot(hid, wd_ref[...], preferred_element_type=F32)

    @pl.when(j == pl.num_programs(1) - 1)
    def _():
        gate = mod_ref[0, :, 5 * d:6 * d]
        o_ref[...] = h_ref[...] + gate * (_rms(acc[...], NORM_EPS) * gpost_ref[...])


def _ffn(f, wg, wu, wd, h, g_post, mods, lat_bpb, n_batch):
    m, d = h.shape
    ff = wg.shape[1]
    tm, tf = ROW_TILE, FFN_TF
    seg = functools.partial(_seg_of_block, lat_blocks_per_batch=lat_bpb, n_batch=n_batch)
    return pl.pallas_call(
        functools.partial(_ffn_kernel, d=d),
        out_shape=jax.ShapeDtypeStruct((m, d), F32),
        grid=(m // tm, ff // tf),
        in_specs=[pl.BlockSpec((tm, d), lambda i, j: (i, 0)),
                  pl.BlockSpec((d, tf), lambda i, j: (0, j)),
                  pl.BlockSpec((d, tf), lambda i, j: (0, j)),
                  pl.BlockSpec((tf, d), lambda i, j: (j, 0)),
                  pl.BlockSpec((tm, d), lambda i, j: (i, 0)),
                  pl.BlockSpec((1, d), lambda i, j: (0, 0)),
                  pl.BlockSpec((1, 1, mods.shape[2]), lambda i, j: (seg(i), 0, 0))],
        out_specs=pl.BlockSpec((tm, d), lambda i, j: (i, 0)),
        scratch_shapes=[pltpu.VMEM((tm, d), F32)],
        compiler_params=_cp(("parallel", "arbitrary")),
        name="ffn_dense",
    )(f, wg, wu, wd, h, g_post, mods)


def _moe_gather_copy(f_hbm, xbuf, sem, slot, src_row, dst_row):
    return pltpu.make_async_copy(f_hbm.at[pl.ds(src_row, 1), :], xbuf.at[slot, pl.ds(dst_row, 1), :],
                                 sem.at[slot])


def _moe_ffn_kernel(blk_e_ref, nused_ref, tok_ref, f_hbm, wg_ref, wu_ref, wd_ref, y_ref, xbuf, xb, acc, sem,
                    *, tm, nj):
    i = pl.program_id(0)
    j = pl.program_id(1)
    nused = nused_ref[0]
    active = i < nused
    slot = i % 2
    per_step = tm // nj

    @pl.when((i == 0) & (j == 0))
    def _():
        def issue(r, carry):
            _moe_gather_copy(f_hbm, xbuf, sem, 0, tok_ref[r], r).start()
            return carry

        lax.fori_loop(0, tm, issue, 0)

    @pl.when((i <= nused) & (j == 0))
    def _():
        pltpu.make_async_copy(f_hbm.at[pl.ds(0, tm), :], xbuf.at[slot], sem.at[slot]).wait()

    @pl.when(active & (j == 0))
    def _():
        xb[...] = xbuf[slot].astype(BF16)
        acc[...] = jnp.zeros(acc.shape, F32)

    @pl.when(active)
    def _():
        base = (i + 1) * tm + j * per_step
        for r in range(per_step):
            _moe_gather_copy(f_hbm, xbuf, sem, 1 - slot, tok_ref[base + r], j * per_step + r).start()
        hid = _swiglu_hidden(xb[...], wg_ref[0], wu_ref[0])
        acc[...] += jnp.dot(hid, wd_ref[0], preferred_element_type=F32)

    @pl.when(j == nj - 1)
    def _():
        y_ref[...] = acc[...]


def _moe_ffn(f, blk_e, nused, tok, wg, wu, wd, n_blocks):
    d = f.shape[1]
    ff = wg.shape[2]
    tm, tf = MOE_TILE, MOE_TF

    nj = ff // tf

    def e_of(i, be, nu):
        return be[jnp.minimum(i, nu[0] - 1)]

    def j_of(i, j, nu):
        return jnp.where(i < nu[0], j, nj - 1)

    grid_spec = pltpu.PrefetchScalarGridSpec(
        num_scalar_prefetch=3,
        grid=(n_blocks, nj),
        in_specs=[pl.BlockSpec(memory_space=pl.ANY),
                  pl.BlockSpec((1, d, tf), lambda i, j, be, nu, tk: (e_of(i, be, nu), 0, j_of(i, j, nu))),
                  pl.BlockSpec((1, d, tf), lambda i, j, be, nu, tk: (e_of(i, be, nu), 0, j_of(i, j, nu))),
                  pl.BlockSpec((1, tf, d), lambda i, j, be, nu, tk: (e_of(i, be, nu), j_of(i, j, nu), 0))],
        out_specs=pl.BlockSpec((tm, d), lambda i, j, be, nu, tk: (i, 0)),
        scratch_shapes=[pltpu.VMEM((2, tm, d), F32), pltpu.VMEM((tm, d), BF16), pltpu.VMEM((tm, d), F32),
                        pltpu.SemaphoreType.DMA((2,))],
    )
    return pl.pallas_call(
        functools.partial(_moe_ffn_kernel, tm=tm, nj=nj),
        out_shape=jax.ShapeDtypeStruct((n_blocks * tm, d), F32),
        grid_spec=grid_spec,
        compiler_params=_cp(("arbitrary", "arbitrary")),
        name="moe_ffn",
    )(blk_e, nused, tok, f, wg, wu, wd)


def _moe_combine_kernel(p0_ref, p1_ref, y_hbm, wt_ref, h_ref, gpost_ref, mod_ref, o_ref, b0, b1, sem, *, tm, d):
    i = pl.program_id(0)
    slot = i % 2

    def issue_block(blk, s):
        def issue(r, carry):
            pltpu.make_async_copy(y_hbm.at[pl.ds(p0_ref[blk * tm + r], 1), :], b0.at[s, pl.ds(r, 1), :],
                                  sem.at[0, s]).start()
            pltpu.make_async_copy(y_hbm.at[pl.ds(p1_ref[blk * tm + r], 1), :], b1.at[s, pl.ds(r, 1), :],
                                  sem.at[1, s]).start()
            return carry

        lax.fori_loop(0, tm, issue, 0)

    @pl.when(i == 0)
    def _():
        issue_block(0, 0)

    @pl.when(i + 1 < pl.num_programs(0))
    def _():
        issue_block(i + 1, 1 - slot)

    pltpu.make_async_copy(y_hbm.at[pl.ds(0, tm), :], b0.at[slot], sem.at[0, slot]).wait()
    pltpu.make_async_copy(y_hbm.at[pl.ds(0, tm), :], b1.at[slot], sem.at[1, slot]).wait()
    wt = wt_ref[...]
    y = b0[slot] * wt[:, 0:1] + b1[slot] * wt[:, 1:2]
    gate = mod_ref[0, :, 5 * d:6 * d]
    o_ref[...] = h_ref[...] + gate * (_rms(y, NORM_EPS) * gpost_ref[...])


def _moe_combine(pos0, pos1, y, wt, h, g_post, mods, n_rows, rows_per_batch, n_batch):
    d = h.shape[1]
    tm = COMBINE_TILE
    seg = functools.partial(_seg_of_block, lat_blocks_per_batch=rows_per_batch // tm, n_batch=n_batch)
    grid_spec = pltpu.PrefetchScalarGridSpec(
        num_scalar_prefetch=2,
        grid=(n_rows // tm,),
        in_specs=[pl.BlockSpec(memory_space=pl.ANY),
                  pl.BlockSpec((tm, LANES), lambda i, a, b: (i, 0)),
                  pl.BlockSpec((tm, d), lambda i, a, b: (i, 0)),
                  pl.BlockSpec((1, d), lambda i, a, b: (0, 0)),
                  pl.BlockSpec((1, 1, mods.shape[2]), lambda i, a, b: (seg(i), 0, 0))],
        out_specs=pl.BlockSpec((tm, d), lambda i, a, b: (i, 0)),
        scratch_shapes=[pltpu.VMEM((2, tm, d), F32), pltpu.VMEM((2, tm, d), F32),
                        pltpu.SemaphoreType.DMA((2, 2))],
    )
    return pl.pallas_call(
        functools.partial(_moe_combine_kernel, tm=tm, d=d),
        out_shape=jax.ShapeDtypeStruct((n_rows, d), F32),
        grid_spec=grid_spec,
        compiler_params=_cp(("arbitrary",)),
        name="moe_combine",
    )(pos0, pos1, y, wt, h, g_post, mods)


def _moe_slots(top_i, tile):
    n = top_i.shape[0]
    a = n * 2
    e_flat = top_i.reshape(a)
    onehot = (e_flat[:, None] == jnp.arange(N_EXPERTS, dtype=jnp.int32)[None, :]).astype(jnp.int32)
    sub = LANES
    blocks = onehot.reshape(a // sub, sub, N_EXPERTS).astype(F32)
    tri = (jnp.arange(sub)[:, None] >= jnp.arange(sub)[None, :]).astype(F32)
    within = jnp.einsum('ij,bjk->bik', tri, blocks).astype(jnp.int32)
    totals = within[:, -1, :]
    csum = (within + (jnp.cumsum(totals, axis=0) - totals)[:, None, :]).reshape(a, N_EXPERTS)
    rank = jnp.sum(csum * onehot, axis=1) - 1
    counts = csum[-1]
    padded = (counts + tile - 1) // tile * tile
    pends = jnp.cumsum(padded)
    pstarts = pends - padded
    dest = (jnp.sum(onehot * pstarts[None, :], axis=1) + rank).astype(jnp.int32)
    n_blocks = a // tile + N_EXPERTS
    tok = jnp.zeros((n_blocks * tile,), jnp.int32).at[dest].set(
        jnp.arange(a, dtype=jnp.int32) // 2, unique_indices=True, mode='promise_in_bounds')
    block_start = jnp.arange(n_blocks, dtype=jnp.int32) * tile
    blk_e = jnp.minimum(jnp.sum((block_start[:, None] >= pends[None, :]).astype(jnp.int32), axis=1),
                        N_EXPERTS - 1)
    nused = (pends[-1:] // tile).astype(jnp.int32)
    return tok, blk_e, nused, dest.reshape(n, 2), n_blocks


def _swa_head_order(w, axis):
    group = SWA_HEADS // SWA_KV_HEADS
    order = [g * group + t for t in range(group) for g in range(SWA_KV_HEADS)]
    shape = w.shape
    w = w.reshape(shape[:axis] + (SWA_HEADS, HEAD_DIM) + shape[axis + 1:])
    return jnp.take(w, jnp.array(order), axis=axis).reshape(shape)


def _rope_tables(n_batch, seq, ctx_len):
    t = jnp.arange(seq, dtype=jnp.int32)
    row = (t // GRID_W).astype(F32)
    col = (t % GRID_W).astype(F32)
    axis_dim = HEAD_DIM // 2
    inv_freq = ROPE_THETA ** (-jnp.arange(0, axis_dim, 2, dtype=F32) / axis_dim)
    dd = jnp.arange(LANES) % HEAD_DIM
    pos = jnp.where((dd // axis_dim)[None, :] == 0, row[:, None], col[:, None])
    ang = pos * inv_freq[dd % (axis_dim // 2)][None, :]
    cos = jnp.cos(ang)
    sin = jnp.where(((dd % axis_dim) < axis_dim // 2)[None, :], -jnp.sin(ang), jnp.sin(ang))
    n_ctx = n_batch * ctx_len
    cos = jnp.concatenate([jnp.tile(cos, (n_batch, 1)), jnp.ones((n_ctx, LANES), F32)], axis=0)
    sin = jnp.concatenate([jnp.tile(sin, (n_batch, 1)), jnp.zeros((n_ctx, LANES), F32)], axis=0)
    return cos, sin


def kernel(x, c, ctx, c_ctx, ada_w, ada_b, pre_mix_g, post_mix_g, pre_ffn_g, post_ffn_g, w_in, swa_sink,
           rwkv_mu_prev, rwkv_mu_next, rwkv_w0, rwkv_w_up, rwkv_a0, rwkv_a_up, rwkv_g_up, rwkv_k_k, rwkv_k_a,
           rwkv_r_k, rwkv_lnx_g, rwkv_lnx_b, diff_lambda, diff_subln_g, proj_swa, proj_rwkv, proj_diff, w_out,
           ffn_w_gate, ffn_w_up, ffn_w_down, router_w, moe_w_gate, moe_w_up, moe_w_down):
    n_batch, seq, d = x.shape
    ctx_len = ctx.shape[1]
    depth = w_in.shape[0]
    lat_rows = n_batch * seq
    lat_bpb = seq // ROW_TILE
    assert seq % ROW_TILE == 0 and (n_batch * ctx_len) % ROW_TILE == 0

    h = jnp.concatenate([x.reshape(lat_rows, d), ctx.reshape(n_batch * ctx_len, d)], axis=0)
    m = h.shape[0]
    cond = jnp.zeros((8, d), F32).at[:n_batch].set(c).at[n_batch].set(c_ctx)
    rope = _rope_tables(n_batch, seq, ctx_len)
    row = lambda a: a.reshape(1, -1)

    o_swa = 0
    o_rwkv = o_swa + SWA_Q + 2 * SWA_KV
    o_diff = o_rwkv + RWKV_COLS
    o_gate = o_diff + 2 * DIFF_QK + DIFF_V
    o_end = o_gate + 3 * d

    for layer in range(depth):
        mods = _ada(cond, ada_w[layer], ada_b[layer].reshape(1, -1))[:, None, :]
        wl = w_in[layer].astype(BF16)
        a = _prenorm(h, row(pre_mix_g[layer]), mods, 0, 1, lat_bpb, n_batch)

        w_swa = jnp.concatenate([_swa_head_order(wl[:, o_swa:o_swa + SWA_Q], axis=1),
                                 wl[:, o_swa + SWA_Q:o_rwkv]], axis=1)
        q_swa, kv_swa = _proj(a, w_swa, (SWA_Q, 2 * SWA_KV), (BF16, BF16), rope=rope,
                              rope_cols=SWA_Q + SWA_KV, scale_cols=SWA_Q, q_scale=HEAD_DIM ** -0.5)
        (p_rwkv,) = _proj(a, wl[:, o_rwkv:o_diff], (RWKV_COLS,), (F32,))
        q_diff, k_diff, v_diff = _proj(a, wl[:, o_diff:o_gate], (DIFF_QK, DIFF_QK, DIFF_V), (BF16,) * 3,
                                       rope=rope, rope_cols=2 * DIFF_QK, scale_cols=DIFF_QK,
                                       q_scale=HEAD_DIM ** -0.5 * math.log2(math.e))
        (gates,) = _proj(a, wl[:, o_gate:o_end], (d,), (BF16,), tn=d)

        ya = _swa(q_swa, kv_swa, swa_sink[layer].astype(F32), n_batch, seq, ctx_len)

        lp = {'rwkv_mu_prev': rwkv_mu_prev[layer], 'rwkv_mu_next': rwkv_mu_next[layer],
              'rwkv_w0': rwkv_w0[layer], 'rwkv_w_up': rwkv_w_up[layer], 'rwkv_a0': rwkv_a0[layer],
              'rwkv_a_up': rwkv_a_up[layer], 'rwkv_g_up': rwkv_g_up[layer], 'rwkv_k_k': rwkv_k_k[layer],
              'rwkv_k_a': rwkv_k_a[layer], 'rwkv_r_k': rwkv_r_k[layer]}
        r_, v_, kk_, ld_, ke_, bb_, g_, bonus_ = _rwkv_prep(p_rwkv, lp, n_batch, seq, ctx_len)
        y_f, y_b = _rwkv_scan(r_, v_, kk_, ld_, ke_, bb_, n_batch, seq, ctx_len)

        lam_vec = diff_lambda[layer].astype(F32)
        lam_init = 0.8 - 0.6 * math.exp(-0.3 * layer)
        lam = (jnp.exp(jnp.sum(lam_vec[0] * lam_vec[1])) - jnp.exp(jnp.sum(lam_vec[2] * lam_vec[3]))
               + lam_init).reshape(1)
        yc = _diff(q_diff, k_diff, v_diff, lam, row(diff_subln_g[layer]), lam_init, n_batch, seq, ctx_len)

        moe_layer = layer % 2 == 1
        jj = layer // 2
        merged = _merge(ya, y_f, y_b, bonus_, g_, rwkv_lnx_g[layer], rwkv_lnx_b[layer], yc, gates, h,
                        _swa_head_order(proj_swa[layer], axis=0).astype(BF16), proj_rwkv[layer].astype(BF16),
                        proj_diff[layer].astype(BF16), w_out[layer].astype(BF16), row(post_mix_g[layer]),
                        row(pre_ffn_g[layer]), mods, F32 if moe_layer else BF16, lat_bpb, n_batch,
                        router_w=router_w[jj] if moe_layer else None)
        h, f = merged[:2]
        need_ctx = layer < depth - 1
        if not moe_layer:
            h = _ffn(f, ffn_w_gate[jj].astype(BF16), ffn_w_up[jj].astype(BF16), ffn_w_down[jj].astype(BF16),
                     h, row(post_ffn_g[layer]), mods, lat_bpb, n_batch)
        else:
            n_tok = m if need_ctx else lat_rows
            top_i, top_w = merged[2][:n_tok], merged[3][:n_tok]
            tok, blk_e, nused, dest, n_blocks = _moe_slots(top_i[:, :2], MOE_TILE)
            y = _moe_ffn(f, blk_e, nused, tok, moe_w_gate[jj].astype(BF16), moe_w_up[jj].astype(BF16),
                         moe_w_down[jj].astype(BF16), n_blocks)
            h = _moe_combine(dest[:, 0], dest[:, 1], y, top_w, h, row(post_ffn_g[layer]), mods, n_tok, seq,
                             n_batch)
    return h[:lat_rows].reshape(n_batch, seq, d)
```

```python
import functools
import math

import jax
import jax.numpy as jnp
from jax import lax
from jax.experimental import pallas as pl
from jax.experimental.pallas import tpu as pltpu

F32 = jnp.float32
BF16 = jnp.bfloat16
HI = lax.Precision.HIGHEST

HEAD_DIM = 64
GRID_W = 64
ROPE_THETA = 10000.0
NORM_EPS = 1e-6
NEG_INF = -1e30
SWA_HEADS = 8
SWA_KV_HEADS = 2
SWA_BLOCK = 128
RWKV_HEADS = 8
RWKV_WIDTH = RWKV_HEADS * HEAD_DIM
DECAY_LORA = 64
AAA_LORA = 64
GATE_LORA = 128
RWKV_GN_EPS = 64e-5
DIFF_HEADS = 4
DIFF_V_DIM = 2 * HEAD_DIM
DIFF_SUBLN_EPS = 1e-5
N_EXPERTS = 8
SWA_Q = SWA_HEADS * HEAD_DIM
SWA_KV = SWA_KV_HEADS * HEAD_DIM
assert SWA_KV_HEADS == 2 and SWA_KV == 128
RWKV_COLS = 3 * RWKV_WIDTH + 2 * DECAY_LORA + 2 * AAA_LORA + GATE_LORA
DIFF_QK = DIFF_HEADS * 2 * HEAD_DIM
DIFF_V = DIFF_HEADS * DIFF_V_DIM

LANES = 128
VMEM_LIMIT = 48 * 1024 * 1024
ROW_TILE = 512
RWKV_CHUNK = 64
RWKV_INV_BASE = 8
RWKV_GROUP = 2
RWKV_STEP_ROWS = 256
DIFF_TQ = 512
DIFF_TK = 1024
DIFF_UNROLL = 2
DIFF_ONES_ROWS = 16
MOE_TILE = 512
MOE_TF = 1792
FFN_TF = 1408
COMBINE_TILE = 256
PREP_TILE = 256
MERGE_TILE = 512


def _cp(sem, **kw):
    return pltpu.CompilerParams(dimension_semantics=sem, vmem_limit_bytes=VMEM_LIMIT, **kw)


def _seg_of_block(i, lat_blocks_per_batch, n_batch):
    return jnp.minimum(i // lat_blocks_per_batch, n_batch)


def _rms(x, eps):
    return x * lax.rsqrt(jnp.mean(x * x, axis=-1, keepdims=True) + eps)


def _sigmoid(x):
    return 1.0 / (1.0 + jnp.exp(-x))


def _group_sum(x, ones_bd):
    hi = x.astype(BF16)
    mid = (x - hi.astype(F32)).astype(BF16)
    rows = x.shape[0]
    parts = jnp.dot(jnp.concatenate([hi, mid], axis=0), ones_bd, preferred_element_type=F32)
    return parts[:rows] + parts[rows:]


def _ada_kernel(x_ref, w_ref, b_ref, o_ref):
    x = x_ref[...]
    s = x * _sigmoid(x)
    o_ref[...] = jnp.dot(s, w_ref[...], precision=HI, preferred_element_type=F32) + b_ref[...]


def _ada(cond, w, b):
    rows, d = cond.shape
    n = w.shape[1]
    return pl.pallas_call(
        _ada_kernel,
        out_shape=jax.ShapeDtypeStruct((rows, n), F32),
        grid=(n // d,),
        in_specs=[pl.BlockSpec((rows, d), lambda j: (0, 0)),
                  pl.BlockSpec((d, d), lambda j: (0, j)),
                  pl.BlockSpec((1, d), lambda j: (0, j))],
        out_specs=pl.BlockSpec((rows, d), lambda j: (0, j)),
        compiler_params=_cp(("parallel",)),
        name="ada_mod",
    )(cond, w, b)


def _prenorm_kernel(h_ref, g_ref, mod_ref, o_ref, *, d, shift_idx, scale_idx):
    y = _rms(h_ref[...], NORM_EPS) * g_ref[...]
    shift = mod_ref[0, :, shift_idx * d:(shift_idx + 1) * d]
    scale = mod_ref[0, :, scale_idx * d:(scale_idx + 1) * d]
    o_ref[...] = (y * (1.0 + scale) + shift).astype(o_ref.dtype)


def _prenorm(h, g, mods, shift_idx, scale_idx, lat_bpb, n_batch):
    m, d = h.shape
    tm = ROW_TILE
    seg = functools.partial(_seg_of_block, lat_blocks_per_batch=lat_bpb, n_batch=n_batch)
    return pl.pallas_call(
        functools.partial(_prenorm_kernel, d=d, shift_idx=shift_idx, scale_idx=scale_idx),
        out_shape=jax.ShapeDtypeStruct((m, d), BF16),
        grid=(m // tm,),
        in_specs=[pl.BlockSpec((tm, d), lambda i: (i, 0)),
                  pl.BlockSpec((1, d), lambda i: (0, 0)),
                  pl.BlockSpec((1, 1, mods.shape[2]), lambda i: (seg(i), 0, 0))],
        out_specs=pl.BlockSpec((tm, d), lambda i: (i, 0)),
        compiler_params=_cp(("parallel",)),
        name="prenorm",
    )(h, g, mods)


def _proj_kernel(*refs, splits, rope_cols, scale_cols, q_scale):
    if rope_cols:
        a_ref, w_ref, cos_ref, sin_ref = refs[:4]
        outs = refs[4:]
    else:
        a_ref, w_ref = refs[:2]
        outs = refs[2:]
    y = jnp.dot(a_ref[...], w_ref[...], preferred_element_type=F32)
    tm, tn = y.shape
    if rope_cols:
        cos = cos_ref[...]
        sin = sin_ref[...]
        lane = lax.broadcasted_iota(jnp.int32, (tm, LANES), 1)
        first_half = (lane % 32) < 16
        pieces = []
        for c in range(tn // LANES):
            yc = y[:, c * LANES:(c + 1) * LANES]
            if c * LANES < rope_cols:
                partner = jnp.where(first_half, pltpu.roll(yc, LANES - 16, 1), pltpu.roll(yc, 16, 1))
                yc = yc * cos + partner * sin
            if c * LANES < scale_cols:
                yc = yc * q_scale
            pieces.append(yc)
        y = jnp.concatenate(pieces, axis=1)
    start = 0
    for o_ref, width in zip(outs, splits):
        o_ref[...] = y[:, start:start + width].astype(o_ref.dtype)
        start += width


def _proj(a, w, splits, dtypes, rope=None, rope_cols=0, scale_cols=0, q_scale=1.0, tn=None):
    m, k = a.shape
    n = w.shape[1]
    tm = ROW_TILE
    tn = n if tn is None else tn
    assert sum(splits) == tn and (len(splits) == 1 or tn == n)
    in_specs = [pl.BlockSpec((tm, k), lambda i, j: (i, 0)),
                pl.BlockSpec((k, tn), lambda i, j: (0, j))]
    args = [a, w]
    if rope_cols:
        in_specs += [pl.BlockSpec((tm, LANES), lambda i, j: (i, 0))] * 2
        args += list(rope)
    out_specs = []
    out_shape = []
    if len(splits) == 1:
        out_specs.append(pl.BlockSpec((tm, tn), lambda i, j: (i, j)))
        out_shape.append(jax.ShapeDtypeStruct((m, n), dtypes[0]))
    else:
        for width, dt in zip(splits, dtypes):
            out_specs.append(pl.BlockSpec((tm, width), lambda i, j: (i, 0)))
            out_shape.append(jax.ShapeDtypeStruct((m, width), dt))
    res = pl.pallas_call(
        functools.partial(_proj_kernel, splits=tuple(splits), rope_cols=rope_cols, scale_cols=scale_cols,
                          q_scale=q_scale),
        out_shape=out_shape,
        grid=(m // tm, n // tn),
        in_specs=in_specs,
        out_specs=out_specs,
        compiler_params=_cp(("parallel", "parallel")),
        name="proj",
    )(*args)
    return res


def _swa_kernel(sink_ref, q_ref, kp_ref, km_ref, kn_ref, kx_ref, o_ref, *, nb, n_lat_pairs):
    i = pl.program_id(0)
    is_lat = i < n_lat_pairs
    n0 = (2 * i) % nb
    blk = SWA_BLOCK
    kvx = kx_ref[...]
    windows = [jnp.concatenate([kp_ref[...], km_ref[...], kvx], axis=0),
               jnp.concatenate([km_ref[...], kn_ref[...], kvx], axis=0)]
    nkeys = windows[0].shape[0]
    r = lax.broadcasted_iota(jnp.int32, (blk, nkeys), 0)
    j = lax.broadcasted_iota(jnp.int32, (blk, nkeys), 1)
    band = (j >= r) & (j <= r + 2 * blk) & is_lat
    ctx_keys = j >= 3 * blk
    edges = [j >= jnp.where(n0 > 0, 0, blk), j < jnp.where(n0 + 1 < nb - 1, 3 * blk, 2 * blk)]
    low_half = lax.broadcasted_iota(jnp.int32, (blk, SWA_KV), 1) < HEAD_DIM
    zero = jnp.zeros((blk, SWA_KV), q_ref.dtype)
    group = SWA_HEADS // SWA_KV_HEADS
    parts = []
    for b in range(2):
        kv = windows[b]
        k_t = kv[:, :SWA_KV]
        v_aug = jnp.concatenate([kv[:, SWA_KV:], jnp.ones((nkeys, SWA_KV), kv.dtype)], axis=1)
        bias = jnp.where((band & edges[b]) | ctx_keys, 0.0, NEG_INF)
        q = q_ref[b * blk:(b + 1) * blk, :]
        for t in range(group):
            qt = q[:, t * SWA_KV:(t + 1) * SWA_KV]
            rows = jnp.concatenate([jnp.where(low_half, qt, zero), jnp.where(low_half, zero, qt)], axis=0)
            sk = jnp.concatenate([jnp.full((blk, 1), sink_ref[t], F32),
                                  jnp.full((blk, 1), sink_ref[group + t], F32)], axis=0)
            parts.append((rows, k_t, v_aug, bias, sk))
    scores = [lax.dot_general(rows, k_t, (((1,), (1,)), ((), ())), preferred_element_type=F32)
              for rows, k_t, _, _, _ in parts]
    tiles = []
    for sc, (_, _, v_aug, bias, sk) in zip(scores, parts):
        s = (sc.reshape(2, blk, nkeys) + bias[None]).reshape(2 * blk, nkeys)
        mx = jnp.maximum(jnp.max(s, axis=-1, keepdims=True), sk)
        p = jnp.exp((s - mx).astype(BF16))
        oa = jnp.dot(p, v_aug, preferred_element_type=F32)
        den = oa[:, SWA_KV:] + jnp.exp(sk - mx)
        on = oa[:, :SWA_KV] * (1.0 / den)
        tiles.append(jnp.where(low_half, on[:blk], on[blk:]))
    o_ref[...] = jnp.concatenate([jnp.concatenate(tiles[:group], axis=1), jnp.concatenate(tiles[group:], axis=1)],
                                 axis=0).astype(o_ref.dtype)


def _swa(q, kv, sink, n_batch, seq, ctx_len):
    m = q.shape[0]
    blk = SWA_BLOCK
    nb = seq // blk
    n_lat = n_batch * nb
    cpb = ctx_len // blk
    assert nb % 2 == 0 and cpb % 2 == 0

    def batch_of(i):
        return jnp.where(2 * i < n_lat, (2 * i) // nb, (2 * i - n_lat) // cpb)

    def prev_idx(i, s):
        n0 = (2 * i) % nb
        return (jnp.where(2 * i < n_lat, batch_of(i) * nb + jnp.maximum(n0 - 1, 0), 2 * i), 0)

    def next_idx(i, s):
        n0 = (2 * i) % nb
        return (jnp.where(2 * i < n_lat, batch_of(i) * nb + jnp.minimum(n0 + 2, nb - 1), 2 * i), 0)

    def ctx_idx(i, s):
        return (n_batch * seq // ctx_len + batch_of(i), 0)

    grid_spec = pltpu.PrefetchScalarGridSpec(
        num_scalar_prefetch=1,
        grid=(m // (2 * blk),),
        in_specs=[pl.BlockSpec((2 * blk, SWA_Q), lambda i, s: (i, 0)),
                  pl.BlockSpec((blk, 2 * SWA_KV), prev_idx),
                  pl.BlockSpec((2 * blk, 2 * SWA_KV), lambda i, s: (i, 0)),
                  pl.BlockSpec((blk, 2 * SWA_KV), next_idx),
                  pl.BlockSpec((ctx_len, 2 * SWA_KV), ctx_idx)],
        out_specs=pl.BlockSpec((2 * blk, SWA_Q), lambda i, s: (i, 0)),
    )
    return pl.pallas_call(
        functools.partial(_swa_kernel, nb=nb, n_lat_pairs=n_lat // 2),
        out_shape=jax.ShapeDtypeStruct((m, SWA_Q), BF16),
        grid_spec=grid_spec,
        compiler_params=_cp(("parallel",)),
        name="swa_attn",
    )(sink, q, kv, kv, kv, kv)


def _diff_kernel(*refs, n_lat_chunks, coef):
    if n_lat_chunks:
        lam_ref, gcol_ref, q_ref, kc_ref, vtc_ref, kl_ref, vtl_ref, o_ref, m_sc, acc_sc, st_a, st_b = refs
    else:
        lam_ref, gcol_ref, q_ref, kc_ref, vtc_ref, o_ref, m_sc, acc_sc, st_a, st_b = refs
    q = q_ref[...]
    tq = q.shape[0]
    dv = DIFF_V_DIM
    lane = lax.broadcasted_iota(jnp.int32, q.shape, 1)
    zero = jnp.zeros_like(q)
    qq = jnp.concatenate([jnp.where(lane < HEAD_DIM, q, zero), jnp.where(lane >= HEAD_DIM, q, zero)], axis=0)

    def scores(k):
        return lax.dot_general(k, qq, (((1,), (1,)), ((), ())), preferred_element_type=F32)

    def accumulate(st_ref, vt):
        st = st_ref[0:vt.shape[1], :]
        m_old = m_sc[...]
        m_new = jnp.maximum(m_old, jnp.max(st, axis=0, keepdims=True))
        alpha = jnp.exp2(m_old - m_new)
        pt = jnp.exp2((st - m_new).astype(BF16))
        acc_sc[...] = alpha * acc_sc[...] + jnp.dot(vt, pt, preferred_element_type=F32)
        m_sc[...] = m_new

    tk = st_a.shape[0]
    k_lat = lambda c: kl_ref[pl.ds(pl.multiple_of(c * tk, tk), tk), :]
    m_sc[...] = jnp.full(m_sc.shape, NEG_INF, F32)
    acc_sc[...] = jnp.zeros(acc_sc.shape, F32)
    bufs = (st_a, st_b)
    n = n_lat_chunks
    if n:
        unroll = DIFF_UNROLL
        st_a[...] = scores(k_lat(0))
        n_trips = (n - 1) // unroll

        def body(j, carry):
            for u in range(unroll):
                c = j * unroll + u
                bufs[(u + 1) % 2][...] = scores(k_lat(c + 1))
                accumulate(bufs[u % 2], vtl_ref[0, c])
            return carry

        lax.fori_loop(0, n_trips, body, 0)
        for c in range(n_trips * unroll, n):
            if c + 1 < n:
                bufs[(c + 1) % 2][...] = scores(k_lat(c + 1))
            else:
                bufs[(c + 1) % 2][0:kc_ref.shape[0], :] = scores(kc_ref[...])
            accumulate(bufs[c % 2], vtl_ref[0, c])
    else:
        st_a[0:kc_ref.shape[0], :] = scores(kc_ref[...])
    accumulate(bufs[n % 2], vtc_ref[0, 0])
    acc = acc_sc[...]
    ot = acc[:dv] * (1.0 / acc[dv:dv + 1])
    odt = ot[:, :tq] - lam_ref[0] * ot[:, tq:]
    ms = jnp.mean(odt * odt, axis=0, keepdims=True)
    yt = odt * lax.rsqrt(ms + DIFF_SUBLN_EPS) * (gcol_ref[...] * coef)
    o_ref[...] = yt.T.astype(o_ref.dtype)


def _diff(q, k, v, lam, subln_g, lam_init, n_batch, seq, ctx_len):
    tk = min(DIFF_TK, seq)
    assert seq % tk == 0 and tk % ctx_len == 0
    dv = DIFF_V_DIM
    dva = dv + DIFF_ONES_ROWS
    lat_rows = n_batch * seq

    def transposed_chunks(rows, size):
        t = rows.reshape(rows.shape[0] // size, size, DIFF_HEADS, dv).transpose(2, 0, 3, 1)
        return jnp.concatenate([t, jnp.ones(t.shape[:2] + (DIFF_ONES_ROWS, size), t.dtype)], axis=2)

    vt_lat = transposed_chunks(v[:lat_rows], tk)
    vt_ctx = transposed_chunks(v[lat_rows:], ctx_len)
    gcol = subln_g.reshape(dv, 1)
    lat_chunks = seq // tk
    ctx0 = n_batch * seq // ctx_len

    def call(tq, n_q, q_block0, batch_of, with_lat):
        in_specs = [pl.BlockSpec((dv, 1), lambda h, i, s: (0, 0)),
                    pl.BlockSpec((tq, LANES), lambda h, i, s: (q_block0 + i, h)),
                    pl.BlockSpec((ctx_len, LANES), lambda h, i, s: (ctx0 + batch_of(i), h)),
                    pl.BlockSpec((1, 1, dva, ctx_len), lambda h, i, s: (h, batch_of(i), 0, 0))]
        args = [lam, gcol, q, k, vt_ctx]
        if with_lat:
            in_specs += [pl.BlockSpec((seq, LANES), lambda h, i, s: (batch_of(i), h)),
                         pl.BlockSpec((1, lat_chunks, dva, tk), lambda h, i, s: (h, batch_of(i), 0, 0))]
            args += [k, vt_lat]
        grid_spec = pltpu.PrefetchScalarGridSpec(
            num_scalar_prefetch=1,
            grid=(DIFF_HEADS, n_q),
            in_specs=in_specs,
            out_specs=pl.BlockSpec((tq, LANES), lambda h, i, s: (i, h)),
            scratch_shapes=[pltpu.VMEM((1, 2 * tq), F32), pltpu.VMEM((dva, 2 * tq), F32),
                            pltpu.VMEM((tk, 2 * tq), F32), pltpu.VMEM((tk, 2 * tq), F32)],
        )
        return pl.pallas_call(
            functools.partial(_diff_kernel, n_lat_chunks=lat_chunks if with_lat else 0, coef=1.0 - lam_init),
            out_shape=jax.ShapeDtypeStruct((n_q * tq, DIFF_V), BF16),
            grid_spec=grid_spec,
            compiler_params=_cp(("parallel", "arbitrary")),
            name="diff_attn" if with_lat else "diff_attn_ctx",
        )(*args)

    tq = min(DIFF_TQ, seq)
    y_lat = call(tq, n_batch * seq // tq, 0, lambda i: i // (seq // tq), True)
    y_ctx = call(ctx_len, n_batch, ctx0, lambda i: i, False)
    return jnp.concatenate([y_lat, y_ctx], axis=0)


def _rwkv_prep_kernel(p_ref, hp_ref, hn_ref, mup_ref, mun_ref, kk_w_ref, ka_ref, rk_ref, w0_ref, a0_ref,
                      wup_ref, aup_ref, gup_ref, ones_ref,
                      r_ref, v_ref, kk_ref, ld_ref, ke_ref, bb_ref, g_ref, bonus_ref, sc,
                      *, tm, lat_rows, seq, ctx_len):
    i = pl.program_id(0)
    w = RWKV_WIDTH
    sc[0:8, :] = hp_ref[...]
    sc[8:8 + tm, :] = p_ref[...]
    sc[8 + tm:16 + tm, :] = hn_ref[...]
    p = p_ref[...]
    prev = sc[7:7 + tm, :]
    nxt = sc[9:9 + tm, :]
    row = i * tm + lax.broadcasted_iota(jnp.int32, (tm, 1), 0)
    pos = jnp.where(row < lat_rows, row % seq, (row - lat_rows) % ctx_len)
    seg_len = jnp.where(row < lat_rows, seq, ctx_len)
    prev = jnp.where(pos == 0, 0.0, prev)
    nxt = jnp.where(pos == seg_len - 1, 0.0, nxt)
    ps = p + mup_ref[...] * (prev - p) + mun_ref[...] * (nxt - p)

    r = ps[:, 0:w]
    k = ps[:, w:2 * w]
    v = ps[:, 2 * w:3 * w]
    wd = ps[:, 3 * w:3 * w + 2 * DECAY_LORA]
    ad = ps[:, 3 * w + 2 * DECAY_LORA:3 * w + 2 * DECAY_LORA + 2 * AAA_LORA]
    gd = ps[:, 3 * w + 2 * DECAY_LORA + 2 * AAA_LORA:]

    ones_bd = ones_ref[...]
    g = jnp.dot(_sigmoid(gd).astype(BF16), gup_ref[...], preferred_element_type=F32)
    kk = k * kk_w_ref[...]
    ss = _group_sum(kk * kk, ones_bd)
    kk = kk / jnp.maximum(jnp.sqrt(ss), 1e-12)
    w_raw = w0_ref[...] + jnp.dot(jnp.tanh(wd).astype(BF16), wup_ref[...], preferred_element_type=F32)
    a_raw = a0_ref[...] + jnp.dot(ad.astype(BF16), aup_ref[...], preferred_element_type=F32)
    ld = -math.exp(-0.5) * _sigmoid(w_raw)
    a = _sigmoid(a_raw)
    ka = ka_ref[...]
    ke_sum = jnp.zeros_like(k)
    for d in range(2):
        a_d = a[:, d * w:(d + 1) * w]
        ke = k * (1.0 + (a_d - 1.0) * ka)
        ld_ref[d] = ld[:, d * w:(d + 1) * w]
        ke_ref[d] = ke
        bb_ref[d] = kk * a_d
        ke_sum = ke_sum + ke
    rk = _group_sum(r * ke_sum * rk_ref[...], ones_bd)
    r_ref[...] = r
    v_ref[...] = v
    kk_ref[...] = kk
    g_ref[...] = g
    bonus_ref[...] = rk * v


def _rwkv_prep(p, lp, n_batch, seq, ctx_len):
    m, cols = p.shape
    tm = PREP_TILE
    w = RWKV_WIDTH
    lat_rows = n_batch * seq
    row = lambda a: a.reshape(1, -1).astype(F32)

    def blockdiag(u):
        z = jnp.zeros_like(u[0])
        return jnp.concatenate([jnp.concatenate([u[0], z], axis=1), jnp.concatenate([z, u[1]], axis=1)], axis=0)

    head = jnp.arange(w) // HEAD_DIM
    ones_bd = (head[:, None] == head[None, :]).astype(BF16)
    full = lambda shape: pl.BlockSpec(shape, lambda i: (0,) * len(shape))
    nb8 = m // 8
    outs = pl.pallas_call(
        functools.partial(_rwkv_prep_kernel, tm=tm, lat_rows=lat_rows, seq=seq, ctx_len=ctx_len),
        out_shape=[jax.ShapeDtypeStruct((m, w), F32)] * 3
        + [jax.ShapeDtypeStruct((2, m, w), F32)] * 3
        + [jax.ShapeDtypeStruct((m, w), F32)] * 2,
        grid=(m // tm,),
        in_specs=[pl.BlockSpec((tm, cols), lambda i: (i, 0)),
                  pl.BlockSpec((8, cols), lambda i: (jnp.maximum(i * (tm // 8) - 1, 0), 0)),
                  pl.BlockSpec((8, cols), lambda i: (jnp.minimum((i + 1) * (tm // 8), nb8 - 1), 0)),
                  full((1, cols)), full((1, cols)), full((1, w)), full((1, w)), full((1, w)),
                  full((1, 2 * w)), full((1, 2 * w)),
                  full((2 * DECAY_LORA, 2 * w)), full((2 * AAA_LORA, 2 * w)), full((GATE_LORA, w)),
                  full((w, w))],
        out_specs=[pl.BlockSpec((tm, w), lambda i: (i, 0))] * 3
        + [pl.BlockSpec((2, tm, w), lambda i: (0, i, 0))] * 3
        + [pl.BlockSpec((tm, w), lambda i: (i, 0))] * 2,
        scratch_shapes=[pltpu.VMEM((tm + 16, cols), F32)],
        compiler_params=_cp(("parallel",)),
        name="rwkv_prep",
    )(p, p, p, row(lp['rwkv_mu_prev']), row(lp['rwkv_mu_next']), row(lp['rwkv_k_k']), row(lp['rwkv_k_a']),
      row(lp['rwkv_r_k']), row(lp['rwkv_w0']), row(lp['rwkv_a0']),
      blockdiag(lp['rwkv_w_up']).astype(BF16), blockdiag(lp['rwkv_a_up']).astype(BF16),
      lp['rwkv_g_up'].astype(BF16), ones_bd)
    return outs


def _rwkv_chunk_prep(off, r_ref, v_ref, kk_ref, ld_ref, ke_ref, bb_ref, incl):
    c = RWKV_CHUNK
    ld = ld_ref[0, pl.ds(off, c), :]
    r = r_ref[pl.ds(off, c), :]
    v = v_ref[pl.ds(off, c), :]
    kk = kk_ref[pl.ds(off, c), :]
    ke = ke_ref[0, pl.ds(off, c), :]
    bb = bb_ref[0, pl.ds(off, c), :]
    cum = jnp.dot(incl, ld, precision=HI, preferred_element_type=F32)
    tot = jnp.sum(ld, axis=0, keepdims=True)
    rt = r * jnp.exp(cum)
    einv = jnp.exp(-cum)
    etail = jnp.exp(tot - cum)
    return dict(at=(-kk * jnp.exp(cum - ld)).astype(BF16), rt=rt, rt_b=rt.astype(BF16),
                bt=(bb * einv).astype(BF16), kt=(ke * einv).astype(BF16),
                bh=(bb * etail).astype(BF16), kh=(ke * etail).astype(BF16),
                v=v, v_b=v.astype(BF16), wtot=jnp.exp(tot))


def _rwkv_chunk_group(offs, in_refs, y_refs, s_sc):
    c = RWKV_CHUNK
    hd = HEAD_DIM
    n_group = len(offs[0])
    ti = lax.broadcasted_iota(jnp.int32, (c, c), 0)
    tj = lax.broadcasted_iota(jnp.int32, (c, c), 1)
    eye = (ti == tj).astype(F32)
    incl = [(tj <= ti).astype(F32), (tj >= ti).astype(F32)]
    strict = [m - eye for m in incl]
    diag_mask = ((ti // RWKV_INV_BASE) == (tj // RWKV_INV_BASE)).astype(F32)
    off_masks = []
    sz = RWKV_INV_BASE
    while sz < c:
        off_masks.append((((ti // (2 * sz)) == (tj // (2 * sz))) & ((ti // sz) != (tj // sz))).astype(F32))
        sz *= 2

    pre = {(d, g): _rwkv_chunk_prep(offs[d][g], *in_refs[d], incl[d])
           for d in range(2) for g in range(n_group)}
    lanes = [(d, g, h) for g in range(n_group) for d in range(2) for h in range(RWKV_HEADS)]
    sl = lambda h: slice(h * hd, (h + 1) * hd)
    get = lambda name: [pre[d, g][name][:, sl(h)] for d, g, h in lanes]
    at, rt, rt_b, bt, kt, bh, kh, v, v_b = (get(n) for n in ('at', 'rt', 'rt_b', 'bt', 'kt', 'bh', 'kh', 'v', 'v_b'))
    nl = range(len(lanes))
    nt = (((1,), (1,)), ((), ()))
    bdot = lambda x, y: jnp.dot(x.astype(BF16), y.astype(BF16), preferred_element_type=F32)

    gm = [lax.dot_general(jnp.concatenate([at[i], rt_b[i]], axis=0), jnp.concatenate([bt[i], kt[i]], axis=0), nt,
                          preferred_element_type=F32) for i in nl]
    zz0 = [bdot(v[i].T, kh[i]) for i in nl]
    aab = [gm[i][:c, :c] * strict[lanes[i][0]] for i in nl]
    aak = [gm[i][:c, c:] * strict[lanes[i][0]] for i in nl]
    arb = [gm[i][c:, :c] * incl[lanes[i][0]] for i in nl]
    ark = [gm[i][c:, c:] * incl[lanes[i][0]] for i in nl]
    av = [bdot(jnp.concatenate([aak[i], ark[i]], axis=0), v_b[i]) for i in nl]
    pw = [aab[i] * diag_mask for i in nl]
    tm_ = [eye + pw[i] for i in nl]
    for _ in range(int(math.log2(RWKV_INV_BASE)) - 1):
        pw = [bdot(pw[i], pw[i]) for i in nl]
        tm_ = [tm_[i] + bdot(tm_[i], pw[i]) for i in nl]
    for off_mask in off_masks:
        tn = [bdot(tm_[i], aab[i] * off_mask) for i in nl]
        tm_ = [tm_[i] + bdot(tn[i], tm_[i]) for i in nl]
    au = [bdot(tm_[i], jnp.concatenate([at[i], av[i][:c].astype(BF16)], axis=1)) for i in nl]
    ry = [bdot(arb[i], au[i]) for i in nl]
    mz = [bdot(au[i].T, bh[i]) for i in nl]
    rbar = [(rt[i] + ry[i][:, :hd]).astype(BF16) for i in nl]
    ybar = [ry[i][:, hd:] + av[i][c:] for i in nl]
    mm = [(eye * pre[lanes[i][0], lanes[i][1]]['wtot'][:, sl(lanes[i][2])] + mz[i][:hd]).astype(BF16) for i in nl]
    zz = [mz[i][hd:] + zz0[i] for i in nl]
    state = {(d, h): s_sc[d, h] for d in range(2) for h in range(RWKV_HEADS)}
    ys = {}
    for g in range(n_group):
        idx = [i for i in nl if lanes[i][1] == g]
        s_b = {i: state[lanes[i][0], lanes[i][2]].astype(BF16) for i in idx}
        for i in idx:
            ys[i] = lax.dot_general(rbar[i], s_b[i], nt, preferred_element_type=F32) + ybar[i]
        for i in idx:
            state[lanes[i][0], lanes[i][2]] = jnp.dot(s_b[i], mm[i], preferred_element_type=F32) + zz[i]
    for (d, h), val in state.items():
        s_sc[d, h] = val
    for d in range(2):
        for g in range(n_group):
            y_refs[d][0, pl.ds(offs[d][g], c), :] = jnp.concatenate(
                [ys[i] for i in nl if lanes[i][0] == d and lanes[i][1] == g], axis=1)


def _rwkv_scan_kernel(rf, vf, kf, ldf, kef, bbf, rb, vb, kb, ldb, keb, bbb, yf, yb, s_sc, *, n_chunks):
    @pl.when(pl.program_id(1) == 0)
    def _():
        s_sc[...] = jnp.zeros(s_sc.shape, F32)

    group = RWKV_GROUP

    def body(t, carry):
        off_f = [pl.multiple_of((t * group + g) * RWKV_CHUNK, RWKV_CHUNK) for g in range(group)]
        off_b = [pl.multiple_of((n_chunks - 1 - t * group - g) * RWKV_CHUNK, RWKV_CHUNK) for g in range(group)]
        _rwkv_chunk_group((off_f, off_b), ((rf, vf, kf, ldf, kef, bbf), (rb, vb, kb, ldb, keb, bbb)),
                          (yf, yb), s_sc)
        return carry

    lax.fori_loop(0, n_chunks // group, body, 0)


def _rwkv_scan(r, v, kk, ld, ke, bb, n_batch, seq, ctx_len):
    m, w = r.shape
    ts = RWKV_STEP_ROWS
    assert ctx_len == ts
    lpb = seq // ts
    ctx0 = n_batch * seq // ts
    nj = 1 + lpb

    def fwd(b, j):
        return jnp.where(j == 0, ctx0 + b, b * lpb + j - 1)

    def bwd(b, j):
        return jnp.where(j == 0, ctx0 + b, b * lpb + lpb - j)

    shared = lambda f: pl.BlockSpec((ts, w), lambda b, j: (f(b, j), 0))
    per_dir = lambda f, d: pl.BlockSpec((1, ts, w), lambda b, j: (d, f(b, j), 0))
    y = pl.pallas_call(
        functools.partial(_rwkv_scan_kernel, n_chunks=ts // RWKV_CHUNK),
        out_shape=[jax.ShapeDtypeStruct((1, m, w), F32)] * 2,
        grid=(n_batch, nj),
        in_specs=[shared(fwd), shared(fwd), shared(fwd), per_dir(fwd, 0), per_dir(fwd, 0), per_dir(fwd, 0),
                  shared(bwd), shared(bwd), shared(bwd), per_dir(bwd, 1), per_dir(bwd, 1), per_dir(bwd, 1)],
        out_specs=[pl.BlockSpec((1, ts, w), lambda b, j: (0, fwd(b, j), 0)),
                   pl.BlockSpec((1, ts, w), lambda b, j: (0, bwd(b, j), 0))],
        scratch_shapes=[pltpu.VMEM((2, RWKV_HEADS, HEAD_DIM, HEAD_DIM), F32)],
        compiler_params=_cp(("parallel", "arbitrary")),
        name="rwkv_scan",
    )(r, v, kk, ld, ke, bb, r, v, kk, ld, ke, bb)
    return y


def _rwkv_finish(y, bonus, g, lnx_g, lnx_b, ones_bd):
    inv = 1.0 / HEAD_DIM
    mu = _group_sum(y, ones_bd) * inv
    yc = y - mu
    var = _group_sum(yc * yc, ones_bd) * inv
    yn = yc * lax.rsqrt(var + RWKV_GN_EPS) * lnx_g + lnx_b
    return (yn + bonus) * g


def _top2(logits):
    lane = lax.broadcasted_iota(jnp.int32, logits.shape, 1)
    logits = jnp.where(lane < N_EXPERTS, logits, -jnp.inf)
    m1 = jnp.max(logits, axis=-1, keepdims=True)
    i1 = jnp.min(jnp.where(logits == m1, lane, LANES), axis=-1, keepdims=True)
    rest = jnp.where(lane == i1, -jnp.inf, logits)
    m2 = jnp.max(rest, axis=-1, keepdims=True)
    i2 = jnp.min(jnp.where(rest == m2, lane, LANES), axis=-1, keepdims=True)
    e = jnp.exp(m2 - m1)
    w1 = 1.0 / (1.0 + e)
    w2 = e / (1.0 + e)
    idx = jnp.where(lane == 0, i1, jnp.where(lane == 1, i2, 0))
    wts = jnp.where(lane == 0, w1, jnp.where(lane == 1, w2, 0.0))
    return idx, wts


def _merge_kernel(*refs, d, with_router):
    (ya_ref, yf_ref, yb_ref, bonus_ref, g_ref, lg_ref, lb_ref, ones_ref, yc_ref, gt_ref, h_ref,
     pa_ref, pb_ref, pc_ref, wo_ref, gpost_ref, gpre_ref, mod_ref) = refs[:18]
    if with_router:
        rwh_ref, rwl_ref, h_out, f_out, idx_out, wt_out = refs[18:]
    else:
        h_out, f_out = refs[18:]
    yb = _rwkv_finish(yf_ref[0] + yb_ref[0], bonus_ref[...], g_ref[...], lg_ref[...], lb_ref[...], ones_ref[...])
    gates = gt_ref[...].astype(F32)
    merged = (_sigmoid(gates[:, 0:d]) * jnp.dot(ya_ref[...], pa_ref[...], preferred_element_type=F32)
              + _sigmoid(gates[:, d:2 * d]) * jnp.dot(yb.astype(BF16), pb_ref[...], preferred_element_type=F32)
              + _sigmoid(gates[:, 2 * d:3 * d]) * jnp.dot(yc_ref[...], pc_ref[...], preferred_element_type=F32))
    out = jnp.dot(merged.astype(BF16), wo_ref[...], preferred_element_type=F32)
    mod = lambda idx: mod_ref[0, :, idx * d:(idx + 1) * d]
    hn = h_ref[...] + mod(2) * (_rms(out, NORM_EPS) * gpost_ref[...])
    h_out[...] = hn
    f = (_rms(hn, NORM_EPS) * gpre_ref[...]) * (1.0 + mod(4)) + mod(3)
    f_out[...] = f.astype(f_out.dtype)
    if with_router:
        rows = f.shape[0]
        hi = f.astype(BF16)
        mid = (f - hi.astype(F32)).astype(BF16)
        part = jnp.dot(jnp.concatenate([hi, mid], axis=0), rwh_ref[...], preferred_element_type=F32)
        logits = part[:rows] + part[rows:] + jnp.dot(hi, rwl_ref[...], preferred_element_type=F32)
        idx, wts = _top2(logits)
        idx_out[...] = idx
        wt_out[...] = wts


def _merge(ya, yf, yb, bonus, g, lnx_g, lnx_b, yc, gates, h, pa, pb, pc, wo, g_post, g_pre, mods, f_dtype,
           lat_bpb, n_batch, router_w=None):
    m, d = h.shape
    w = bonus.shape[1]
    tm = MERGE_TILE
    bpb = lat_bpb * (ROW_TILE // tm)
    seg = functools.partial(_seg_of_block, lat_blocks_per_batch=bpb, n_batch=n_batch)
    rows = lambda width: pl.BlockSpec((tm, width), lambda i: (i, 0))
    rows3 = pl.BlockSpec((1, tm, w), lambda i: (0, i, 0))
    full = lambda a: pl.BlockSpec(a.shape, lambda i: (0, 0))
    head = jnp.arange(w) // HEAD_DIM
    ones_bd = (head[:, None] == head[None, :]).astype(BF16)
    args = [ya, yf, yb, bonus, g, lnx_g.reshape(1, w), lnx_b.reshape(1, w), ones_bd, yc, gates, h,
            pa, pb, pc, wo, g_post, g_pre, mods]
    in_specs = [rows(ya.shape[1]), rows3, rows3, rows(w), rows(w), full(args[5]), full(args[6]), full(ones_bd),
                rows(yc.shape[1]), rows(gates.shape[1]), rows(d), full(pa), full(pb), full(pc), full(wo),
                full(g_post), full(g_pre), pl.BlockSpec((1, 1, mods.shape[2]), lambda i: (seg(i), 0, 0))]
    out_shape = [jax.ShapeDtypeStruct((m, d), F32), jax.ShapeDtypeStruct((m, d), f_dtype)]
    out_specs = [rows(d), rows(d)]
    if router_w is not None:
        w_pad = jnp.zeros((d, LANES), F32).at[:, :N_EXPERTS].set(router_w)
        w_hi = w_pad.astype(BF16)
        w_lo = (w_pad - w_hi.astype(F32)).astype(BF16)
        args += [w_hi, w_lo]
        in_specs += [full(w_hi), full(w_lo)]
        out_shape += [jax.ShapeDtypeStruct((m, LANES), jnp.int32), jax.ShapeDtypeStruct((m, LANES), F32)]
        out_specs += [rows(LANES), rows(LANES)]
    return pl.pallas_call(
        functools.partial(_merge_kernel, d=d, with_router=router_w is not None),
        out_shape=out_shape,
        grid=(m // tm,),
        in_specs=in_specs,
        out_specs=out_specs,
        compiler_params=_cp(("parallel",)),
        name="merge",
    )(*args)


def _swiglu_hidden(x, wg, wu):
    hg = jnp.dot(x, wg, preferred_element_type=F32)
    hu = jnp.dot(x, wu, preferred_element_type=F32)
    return (hg * _sigmoid(hg) * hu).astype(BF16)


def _ffn_kernel(f_ref, wg_ref, wu_ref, wd_ref, h_ref, gpost_ref, mod_ref, o_ref, acc, *, d):
    j = pl.program_id(1)

    @pl.when(j == 0)
    def _():
        acc[...] = jnp.zeros(acc.shape, F32)

    hid = _swiglu_hidden(f_ref[...], wg_ref[...], wu_ref[...])
    acc[...] += jnp.dot(hid, wd_ref[...], preferred_element_type=F32)

    @pl.when(j == pl.num_programs(1) - 1)
    def _():
        gate = mod_ref[0, :, 5 * d:6 * d]
        o_ref[...] = h_ref[...] + gate * (_rms(acc[...], NORM_EPS) * gpost_ref[...])


def _ffn(f, wg, wu, wd, h, g_post, mods, lat_bpb, n_batch):
    m, d = h.shape
    ff = wg.shape[1]
    tm, tf = ROW_TILE, FFN_TF
    seg = functools.partial(_seg_of_block, lat_blocks_per_batch=lat_bpb, n_batch=n_batch)
    return pl.pallas_call(
        functools.partial(_ffn_kernel, d=d),
        out_shape=jax.ShapeDtypeStruct((m, d), F32),
        grid=(m // tm, ff // tf),
        in_specs=[pl.BlockSpec((tm, d), lambda i, j: (i, 0)),
                  pl.BlockSpec((d, tf), lambda i, j: (0, j)),
                  pl.BlockSpec((d, tf), lambda i, j: (0, j)),
                  pl.BlockSpec((tf, d), lambda i, j: (j, 0)),
                  pl.BlockSpec((tm, d), lambda i, j: (i, 0)),
                  pl.BlockSpec((1, d), lambda i, j: (0, 0)),
                  pl.BlockSpec((1, 1, mods.shape[2]), lambda i, j: (seg(i), 0, 0))],
        out_specs=pl.BlockSpec((tm, d), lambda i, j: (i, 0)),
        scratch_shapes=[pltpu.VMEM((tm, d), F32)],
        compiler_params=_cp(("parallel", "arbitrary")),
        name="ffn_dense",
    )(f, wg, wu, wd, h, g_post, mods)


def _moe_gather_copy(f_hbm, xbuf, sem, slot, src_row, dst_row):
    return pltpu.make_async_copy(f_hbm.at[pl.ds(src_row, 1), :], xbuf.at[slot, pl.ds(dst_row, 1), :],
                                 sem.at[slot])


def _moe_ffn_kernel(blk_e_ref, nused_ref, tok_ref, f_hbm, wg_ref, wu_ref, wd_ref, y_ref, xbuf, xb, acc, sem,
                    *, tm, nj):
    i = pl.program_id(0)
    j = pl.program_id(1)
    nused = nused_ref[0]
    active = i < nused
    slot = i % 2
    per_step = tm // nj

    @pl.when((i == 0) & (j == 0))
    def _():
        def issue(r, carry):
            _moe_gather_copy(f_hbm, xbuf, sem, 0, tok_ref[r], r).start()
            return carry

        lax.fori_loop(0, tm, issue, 0)

    @pl.when((i <= nused) & (j == 0))
    def _():
        pltpu.make_async_copy(f_hbm.at[pl.ds(0, tm), :], xbuf.at[slot], sem.at[slot]).wait()

    @pl.when(active & (j == 0))
    def _():
        xb[...] = xbuf[slot].astype(BF16)
        acc[...] = jnp.zeros(acc.shape, F32)

    @pl.when(active)
    def _():
        base = (i + 1) * tm + j * per_step
        for r in range(per_step):
            _moe_gather_copy(f_hbm, xbuf, sem, 1 - slot, tok_ref[base + r], j * per_step + r).start()
        hid = _swiglu_hidden(xb[...], wg_ref[0], wu_ref[0])
        acc[...] += jnp.dot(hid, wd_ref[0], preferred_element_type=F32)

    @pl.when(j == nj - 1)
    def _():
        y_ref[...] = acc[...]


def _moe_ffn(f, blk_e, nused, tok, wg, wu, wd, n_blocks):
    d = f.shape[1]
    ff = wg.shape[2]
    tm, tf = MOE_TILE, MOE_TF

    nj = ff // tf

    def e_of(i, be, nu):
        return be[jnp.minimum(i, nu[0] - 1)]

    def j_of(i, j, nu):
        return jnp.where(i < nu[0], j, nj - 1)

    grid_spec = pltpu.PrefetchScalarGridSpec(
        num_scalar_prefetch=3,
        grid=(n_blocks, nj),
        in_specs=[pl.BlockSpec(memory_space=pl.ANY),
                  pl.BlockSpec((1, d, tf), lambda i, j, be, nu, tk: (e_of(i, be, nu), 0, j_of(i, j, nu))),
                  pl.BlockSpec((1, d, tf), lambda i, j, be, nu, tk: (e_of(i, be, nu), 0, j_of(i, j, nu))),
                  pl.BlockSpec((1, tf, d), lambda i, j, be, nu, tk: (e_of(i, be, nu), j_of(i, j, nu), 0))],
        out_specs=pl.BlockSpec((tm, d), lambda i, j, be, nu, tk: (i, 0)),
        scratch_shapes=[pltpu.VMEM((2, tm, d), F32), pltpu.VMEM((tm, d), BF16), pltpu.VMEM((tm, d), F32),
                        pltpu.SemaphoreType.DMA((2,))],
    )
    return pl.pallas_call(
        functools.partial(_moe_ffn_kernel, tm=tm, nj=nj),
        out_shape=jax.ShapeDtypeStruct((n_blocks * tm, d), F32),
        grid_spec=grid_spec,
        compiler_params=_cp(("arbitrary", "arbitrary")),
        name="moe_ffn",
    )(blk_e, nused, tok, f, wg, wu, wd)


def _moe_combine_kernel(p0_ref, p1_ref, y_hbm, wt_ref, h_ref, gpost_ref, mod_ref, o_ref, b0, b1, sem, *, tm, d):
    i = pl.program_id(0)
    slot = i % 2

    def issue_block(blk, s):
        def issue(r, carry):
            pltpu.make_async_copy(y_hbm.at[pl.ds(p0_ref[blk * tm + r], 1), :], b0.at[s, pl.ds(r, 1), :],
                                  sem.at[0, s]).start()
            pltpu.make_async_copy(y_hbm.at[pl.ds(p1_ref[blk * tm + r], 1), :], b1.at[s, pl.ds(r, 1), :],
                                  sem.at[1, s]).start()
            return carry

        lax.fori_loop(0, tm, issue, 0)

    @pl.when(i == 0)
    def _():
        issue_block(0, 0)

    @pl.when(i + 1 < pl.num_programs(0))
    def _():
        issue_block(i + 1, 1 - slot)

    pltpu.make_async_copy(y_hbm.at[pl.ds(0, tm), :], b0.at[slot], sem.at[0, slot]).wait()
    pltpu.make_async_copy(y_hbm.at[pl.ds(0, tm), :], b1.at[slot], sem.at[1, slot]).wait()
    wt = wt_ref[...]
    y = b0[slot] * wt[:, 0:1] + b1[slot] * wt[:, 1:2]
    gate = mod_ref[0, :, 5 * d:6 * d]
    o_ref[...] = h_ref[...] + gate * (_rms(y, NORM_EPS) * gpost_ref[...])


def _moe_combine(pos0, pos1, y, wt, h, g_post, mods, n_rows, rows_per_batch, n_batch):
    d = h.shape[1]
    tm = COMBINE_TILE
    seg = functools.partial(_seg_of_block, lat_blocks_per_batch=rows_per_batch // tm, n_batch=n_batch)
    grid_spec = pltpu.PrefetchScalarGridSpec(
        num_scalar_prefetch=2,
        grid=(n_rows // tm,),
        in_specs=[pl.BlockSpec(memory_space=pl.ANY),
                  pl.BlockSpec((tm, LANES), lambda i, a, b: (i, 0)),
                  pl.BlockSpec((tm, d), lambda i, a, b: (i, 0)),
                  pl.BlockSpec((1, d), lambda i, a, b: (0, 0)),
                  pl.BlockSpec((1, 1, mods.shape[2]), lambda i, a, b: (seg(i), 0, 0))],
        out_specs=pl.BlockSpec((tm, d), lambda i, a, b: (i, 0)),
        scratch_shapes=[pltpu.VMEM((2, tm, d), F32), pltpu.VMEM((2, tm, d), F32),
                        pltpu.SemaphoreType.DMA((2, 2))],
    )
    return pl.pallas_call(
        functools.partial(_moe_combine_kernel, tm=tm, d=d),
        out_shape=jax.ShapeDtypeStruct((n_rows, d), F32),
        grid_spec=grid_spec,
        compiler_params=_cp(("arbitrary",)),
        name="moe_combine",
    )(pos0, pos1, y, wt, h, g_post, mods)


def _moe_slots(top_i, tile):
    n = top_i.shape[0]
    a = n * 2
    e_flat = top_i.reshape(a)
    onehot = (e_flat[:, None] == jnp.arange(N_EXPERTS, dtype=jnp.int32)[None, :]).astype(jnp.int32)
    sub = LANES
    blocks = onehot.reshape(a // sub, sub, N_EXPERTS).astype(F32)
    tri = (jnp.arange(sub)[:, None] >= jnp.arange(sub)[None, :]).astype(F32)
    within = jnp.einsum('ij,bjk->bik', tri, blocks).astype(jnp.int32)
    totals = within[:, -1, :]
    csum = (within + (jnp.cumsum(totals, axis=0) - totals)[:, None, :]).reshape(a, N_EXPERTS)
    rank = jnp.sum(csum * onehot, axis=1) - 1
    counts = csum[-1]
    padded = (counts + tile - 1) // tile * tile
    pends = jnp.cumsum(padded)
    pstarts = pends - padded
    dest = (jnp.sum(onehot * pstarts[None, :], axis=1) + rank).astype(jnp.int32)
    n_blocks = a // tile + N_EXPERTS
    tok = jnp.zeros((n_blocks * tile,), jnp.int32).at[dest].set(
        jnp.arange(a, dtype=jnp.int32) // 2, unique_indices=True, mode='promise_in_bounds')
    block_start = jnp.arange(n_blocks, dtype=jnp.int32) * tile
    blk_e = jnp.minimum(jnp.sum((block_start[:, None] >= pends[None, :]).astype(jnp.int32), axis=1),
                        N_EXPERTS - 1)
    nused = (pends[-1:] // tile).astype(jnp.int32)
    return tok, blk_e, nused, dest.reshape(n, 2), n_blocks


def _swa_head_order(w, axis):
    group = SWA_HEADS // SWA_KV_HEADS
    order = [g * group + t for t in range(group) for g in range(SWA_KV_HEADS)]
    shape = w.shape
    w = w.reshape(shape[:axis] + (SWA_HEADS, HEAD_DIM) + shape[axis + 1:])
    return jnp.take(w, jnp.array(order), axis=axis).reshape(shape)


def _rope_tables(n_batch, seq, ctx_len):
    t = jnp.arange(seq, dtype=jnp.int32)
    row = (t // GRID_W).astype(F32)
    col = (t % GRID_W).astype(F32)
    axis_dim = HEAD_DIM // 2
    inv_freq = ROPE_THETA ** (-jnp.arange(0, axis_dim, 2, dtype=F32) / axis_dim)
    dd = jnp.arange(LANES) % HEAD_DIM
    pos = jnp.where((dd // axis_dim)[None, :] == 0, row[:, None], col[:, None])
    ang = pos * inv_freq[dd % (axis_dim // 2)][None, :]
    cos = jnp.cos(ang)
    sin = jnp.where(((dd % axis_dim) < axis_dim // 2)[None, :], -jnp.sin(ang), jnp.sin(ang))
    n_ctx = n_batch * ctx_len
    cos = jnp.concatenate([jnp.tile(cos, (n_batch, 1)), jnp.ones((n_ctx, LANES), F32)], axis=0)
    sin = jnp.concatenate([jnp.tile(sin, (n_batch, 1)), jnp.zeros((n_ctx, LANES), F32)], axis=0)
    return cos, sin


def kernel(x, c, ctx, c_ctx, ada_w, ada_b, pre_mix_g, post_mix_g, pre_ffn_g, post_ffn_g, w_in, swa_sink,
           rwkv_mu_prev, rwkv_mu_next, rwkv_w0, rwkv_w_up, rwkv_a0, rwkv_a_up, rwkv_g_up, rwkv_k_k, rwkv_k_a,
           rwkv_r_k, rwkv_lnx_g, rwkv_lnx_b, diff_lambda, diff_subln_g, proj_swa, proj_rwkv, proj_diff, w_out,
           ffn_w_gate, ffn_w_up, ffn_w_down, router_w, moe_w_gate, moe_w_up, moe_w_down):
    n_batch, seq, d = x.shape
    ctx_len = ctx.shape[1]
    depth = w_in.shape[0]
    lat_rows = n_batch * seq
    lat_bpb = seq // ROW_TILE
    assert seq % ROW_TILE == 0 and (n_batch * ctx_len) % ROW_TILE == 0

    h = jnp.concatenate([x.reshape(lat_rows, d), ctx.reshape(n_batch * ctx_len, d)], axis=0)
    m = h.shape[0]
    cond = jnp.zeros((8, d), F32).at[:n_batch].set(c).at[n_batch].set(c_ctx)
    rope = _rope_tables(n_batch, seq, ctx_len)
    row = lambda a: a.reshape(1, -1)

    o_swa = 0
    o_rwkv = o_swa + SWA_Q + 2 * SWA_KV
    o_diff = o_rwkv + RWKV_COLS
    o_gate = o_diff + 2 * DIFF_QK + DIFF_V
    o_end = o_gate + 3 * d

    for layer in range(depth):
        mods = _ada(cond, ada_w[layer], ada_b[layer].reshape(1, -1))[:, None, :]
        wl = w_in[layer].astype(BF16)
        a = _prenorm(h, row(pre_mix_g[layer]), mods, 0, 1, lat_bpb, n_batch)

        w_swa = jnp.concatenate([_swa_head_order(wl[:, o_swa:o_swa + SWA_Q], axis=1),
                                 wl[:, o_swa + SWA_Q:o_rwkv]], axis=1)
        q_swa, kv_swa = _proj(a, w_swa, (SWA_Q, 2 * SWA_KV), (BF16, BF16), rope=rope,
                              rope_cols=SWA_Q + SWA_KV, scale_cols=SWA_Q, q_scale=HEAD_DIM ** -0.5)
        (p_rwkv,) = _proj(a, wl[:, o_rwkv:o_diff], (RWKV_COLS,), (F32,))
        q_diff, k_diff, v_diff = _proj(a, wl[:, o_diff:o_gate], (DIFF_QK, DIFF_QK, DIFF_V), (BF16,) * 3,
                                       rope=rope, rope_cols=2 * DIFF_QK, scale_cols=DIFF_QK,
                                       q_scale=HEAD_DIM ** -0.5 * math.log2(math.e))
        (gates,) = _proj(a, wl[:, o_gate:o_end], (d,), (BF16,), tn=d)

        ya = _swa(q_swa, kv_swa, swa_sink[layer].astype(F32), n_batch, seq, ctx_len)

        lp = {'rwkv_mu_prev': rwkv_mu_prev[layer], 'rwkv_mu_next': rwkv_mu_next[layer],
              'rwkv_w0': rwkv_w0[layer], 'rwkv_w_up': rwkv_w_up[layer], 'rwkv_a0': rwkv_a0[layer],
              'rwkv_a_up': rwkv_a_up[layer], 'rwkv_g_up': rwkv_g_up[layer], 'rwkv_k_k': rwkv_k_k[layer],
              'rwkv_k_a': rwkv_k_a[layer], 'rwkv_r_k': rwkv_r_k[layer]}
        r_, v_, kk_, ld_, ke_, bb_, g_, bonus_ = _rwkv_prep(p_rwkv, lp, n_batch, seq, ctx_len)
        y_f, y_b = _rwkv_scan(r_, v_, kk_, ld_, ke_, bb_, n_batch, seq, ctx_len)

        lam_vec = diff_lambda[layer].astype(F32)
        lam_init = 0.8 - 0.6 * math.exp(-0.3 * layer)
        lam = (jnp.exp(jnp.sum(lam_vec[0] * lam_vec[1])) - jnp.exp(jnp.sum(lam_vec[2] * lam_vec[3]))
               + lam_init).reshape(1)
        yc = _diff(q_diff, k_diff, v_diff, lam, row(diff_subln_g[layer]), lam_init, n_batch, seq, ctx_len)

        moe_layer = layer % 2 == 1
        jj = layer // 2
        merged = _merge(ya, y_f, y_b, bonus_, g_, rwkv_lnx_g[layer], rwkv_lnx_b[layer], yc, gates, h,
                        _swa_head_order(proj_swa[layer], axis=0).astype(BF16), proj_rwkv[layer].astype(BF16),
                        proj_diff[layer].astype(BF16), w_out[layer].astype(BF16), row(post_mix_g[layer]),
                        row(pre_ffn_g[layer]), mods, F32 if moe_layer else BF16, lat_bpb, n_batch,
                        router_w=router_w[jj] if moe_layer else None)
        h, f = merged[:2]
        need_ctx = layer < depth - 1
        if not moe_layer:
            h = _ffn(f, ffn_w_gate[jj].astype(BF16), ffn_w_up[jj].astype(BF16), ffn_w_down[jj].astype(BF16),
                     h, row(post_ffn_g[layer]), mods, lat_bpb, n_batch)
        else:
            n_tok = m if need_ctx else lat_rows
            top_i, top_w = merged[2][:n_tok], merged[3][:n_tok]
            tok, blk_e, nused, dest, n_blocks = _moe_slots(top_i[:, :2], MOE_TILE)
            y = _moe_ffn(f, blk_e, nused, tok, moe_w_gate[jj].astype(BF16), moe_w_up[jj].astype(BF16),
                         moe_w_down[jj].astype(BF16), n_blocks)
            h = _moe_combine(dest[:, 0], dest[:, 1], y, top_w, h, row(post_ffn_g[layer]), mods, n_tok, seq,
                             n_batch)
    return h[:lat_rows].reshape(n_batch, seq, d)
```

```python
import functools
import math

import jax
import jax.numpy as jnp
from jax import lax
from jax.experimental import pallas as pl
from jax.experimental.pallas import tpu as pltpu

F32 = jnp.float32
BF16 = jnp.bfloat16
HI = lax.Precision.HIGHEST

HEAD_DIM = 64
GRID_W = 64
ROPE_THETA = 10000.0
NORM_EPS = 1e-6
NEG_INF = -1e30
SWA_HEADS = 8
SWA_KV_HEADS = 2
SWA_BLOCK = 128
RWKV_HEADS = 8
RWKV_WIDTH = RWKV_HEADS * HEAD_DIM
DECAY_LORA = 64
AAA_LORA = 64
GATE_LORA = 128
RWKV_GN_EPS = 64e-5
DIFF_HEADS = 4
DIFF_V_DIM = 2 * HEAD_DIM
DIFF_SUBLN_EPS = 1e-5
N_EXPERTS = 8
SWA_Q = SWA_HEADS * HEAD_DIM
SWA_KV = SWA_KV_HEADS * HEAD_DIM
assert SWA_KV_HEADS == 2 and SWA_KV == 128
RWKV_COLS = 3 * RWKV_WIDTH + 2 * DECAY_LORA + 2 * AAA_LORA + GATE_LORA
DIFF_QK = DIFF_HEADS * 2 * HEAD_DIM
DIFF_V = DIFF_HEADS * DIFF_V_DIM

LANES = 128
VMEM_LIMIT = 48 * 1024 * 1024
ROW_TILE = 512
RWKV_CHUNK = 64
RWKV_INV_BASE = 8
RWKV_GROUP = 2
RWKV_STEP_ROWS = 256
DIFF_TQ = 512
DIFF_TK = 1024
DIFF_UNROLL = 2
DIFF_ONES_ROWS = 16
MOE_TILE = 512
MOE_TF = 1792
FFN_TF = 1408
COMBINE_TILE = 256
PREP_TILE = 256
MERGE_TILE = 512


def _cp(sem, **kw):
    return pltpu.CompilerParams(dimension_semantics=sem, vmem_limit_bytes=VMEM_LIMIT, **kw)


def _seg_of_block(i, lat_blocks_per_batch, n_batch):
    return jnp.minimum(i // lat_blocks_per_batch, n_batch)


def _rms(x, eps):
    return x * lax.rsqrt(jnp.mean(x * x, axis=-1, keepdims=True) + eps)


def _sigmoid(x):
    return 1.0 / (1.0 + jnp.exp(-x))


def _group_sum(x, ones_bd):
    hi = x.astype(BF16)
    mid = (x - hi.astype(F32)).astype(BF16)
    rows = x.shape[0]
    parts = jnp.dot(jnp.concatenate([hi, mid], axis=0), ones_bd, preferred_element_type=F32)
    return parts[:rows] + parts[rows:]


def _ada_kernel(x_ref, w_ref, b_ref, o_ref):
    x = x_ref[...]
    s = x * _sigmoid(x)
    o_ref[...] = jnp.dot(s, w_ref[...], precision=HI, preferred_element_type=F32) + b_ref[...]


def _ada(cond, w, b):
    rows, d = cond.shape
    n = w.shape[1]
    return pl.pallas_call(
        _ada_kernel,
        out_shape=jax.ShapeDtypeStruct((rows, n), F32),
        grid=(n // d,),
        in_specs=[pl.BlockSpec((rows, d), lambda j: (0, 0)),
                  pl.BlockSpec((d, d), lambda j: (0, j)),
                  pl.BlockSpec((1, d), lambda j: (0, j))],
        out_specs=pl.BlockSpec((rows, d), lambda j: (0, j)),
        compiler_params=_cp(("parallel",)),
        name="ada_mod",
    )(cond, w, b)


def _prenorm_kernel(h_ref, g_ref, mod_ref, o_ref, *, d, shift_idx, scale_idx):
    y = _rms(h_ref[...], NORM_EPS) * g_ref[...]
    shift = mod_ref[0, :, shift_idx * d:(shift_idx + 1) * d]
    scale = mod_ref[0, :, scale_idx * d:(scale_idx + 1) * d]
    o_ref[...] = (y * (1.0 + scale) + shift).astype(o_ref.dtype)


def _prenorm(h, g, mods, shift_idx, scale_idx, lat_bpb, n_batch):
    m, d = h.shape
    tm = ROW_TILE
    seg = functools.partial(_seg_of_block, lat_blocks_per_batch=lat_bpb, n_batch=n_batch)
    return pl.pallas_call(
        functools.partial(_prenorm_kernel, d=d, shift_idx=shift_idx, scale_idx=scale_idx),
        out_shape=jax.ShapeDtypeStruct((m, d), BF16),
        grid=(m // tm,),
        in_specs=[pl.BlockSpec((tm, d), lambda i: (i, 0)),
                  pl.BlockSpec((1, d), lambda i: (0, 0)),
                  pl.BlockSpec((1, 1, mods.shape[2]), lambda i: (seg(i), 0, 0))],
        out_specs=pl.BlockSpec((tm, d), lambda i: (i, 0)),
        compiler_params=_cp(("parallel",)),
        name="prenorm",
    )(h, g, mods)


def _proj_kernel(*refs, splits, rope_cols, scale_cols, q_scale):
    if rope_cols:
        a_ref, w_ref, cos_ref, sin_ref = refs[:4]
        outs = refs[4:]
    else:
        a_ref, w_ref = refs[:2]
        outs = refs[2:]
    y = jnp.dot(a_ref[...], w_ref[...], preferred_element_type=F32)
    tm, tn = y.shape
    if rope_cols:
        cos = cos_ref[...]
        sin = sin_ref[...]
        lane = lax.broadcasted_iota(jnp.int32, (tm, LANES), 1)
        first_half = (lane % 32) < 16
        pieces = []
        for c in range(tn // LANES):
            yc = y[:, c * LANES:(c + 1) * LANES]
            if c * LANES < rope_cols:
                partner = jnp.where(first_half, pltpu.roll(yc, LANES - 16, 1), pltpu.roll(yc, 16, 1))
                yc = yc * cos + partner * sin
            if c * LANES < scale_cols:
                yc = yc * q_scale
            pieces.append(yc)
        y = jnp.concatenate(pieces, axis=1)
    start = 0
    for o_ref, width in zip(outs, splits):
        o_ref[...] = y[:, start:start + width].astype(o_ref.dtype)
        start += width


def _proj(a, w, splits, dtypes, rope=None, rope_cols=0, scale_cols=0, q_scale=1.0, tn=None):
    m, k = a.shape
    n = w.shape[1]
    tm = ROW_TILE
    tn = n if tn is None else tn
    assert sum(splits) == tn and (len(splits) == 1 or tn == n)
    in_specs = [pl.BlockSpec((tm, k), lambda i, j: (i, 0)),
                pl.BlockSpec((k, tn), lambda i, j: (0, j))]
    args = [a, w]
    if rope_cols:
        in_specs += [pl.BlockSpec((tm, LANES), lambda i, j: (i, 0))] * 2
        args += list(rope)
    out_specs = []
    out_shape = []
    if len(splits) == 1:
        out_specs.append(pl.BlockSpec((tm, tn), lambda i, j: (i, j)))
        out_shape.append(jax.ShapeDtypeStruct((m, n), dtypes[0]))
    else:
        for width, dt in zip(splits, dtypes):
            out_specs.append(pl.BlockSpec((tm, width), lambda i, j: (i, 0)))
            out_shape.append(jax.ShapeDtypeStruct((m, width), dt))
    res = pl.pallas_call(
        functools.partial(_proj_kernel, splits=tuple(splits), rope_cols=rope_cols, scale_cols=scale_cols,
                          q_scale=q_scale),
        out_shape=out_shape,
        grid=(m // tm, n // tn),
        in_specs=in_specs,
        out_specs=out_specs,
        compiler_params=_cp(("parallel", "parallel")),
        name="proj",
    )(*args)
    return res


def _swa_kernel(sink_ref, q_ref, kp_ref, km_ref, kn_ref, kx_ref, o_ref, *, nb, n_lat_pairs):
    i = pl.program_id(0)
    is_lat = i < n_lat_pairs
    n0 = (2 * i) % nb
    blk = SWA_BLOCK
    kvx = kx_ref[...]
    windows = [jnp.concatenate([kp_ref[...], km_ref[...], kvx], axis=0),
               jnp.concatenate([km_ref[...], kn_ref[...], kvx], axis=0)]
    nkeys = windows[0].shape[0]
    r = lax.broadcasted_iota(jnp.int32, (blk, nkeys), 0)
    j = lax.broadcasted_iota(jnp.int32, (blk, nkeys), 1)
    band = (j >= r) & (j <= r + 2 * blk) & is_lat
    ctx_keys = j >= 3 * blk
    edges = [j >= jnp.where(n0 > 0, 0, blk), j < jnp.where(n0 + 1 < nb - 1, 3 * blk, 2 * blk)]
    low_half = lax.broadcasted_iota(jnp.int32, (blk, SWA_KV), 1) < HEAD_DIM
    zero = jnp.zeros((blk, SWA_KV), q_ref.dtype)
    group = SWA_HEADS // SWA_KV_HEADS
    parts = []
    for b in range(2):
        kv = windows[b]
        k_t = kv[:, :SWA_KV]
        v_aug = jnp.concatenate([kv[:, SWA_KV:], jnp.ones((nkeys, SWA_KV), kv.dtype)], axis=1)
        bias = jnp.where((band & edges[b]) | ctx_keys, 0.0, NEG_INF)
        q = q_ref[b * blk:(b + 1) * blk, :]
        for t in range(group):
            qt = q[:, t * SWA_KV:(t + 1) * SWA_KV]
            rows = jnp.concatenate([jnp.where(low_half, qt, zero), jnp.where(low_half, zero, qt)], axis=0)
            sk = jnp.concatenate([jnp.full((blk, 1), sink_ref[t], F32),
                                  jnp.full((blk, 1), sink_ref[group + t], F32)], axis=0)
            parts.append((rows, k_t, v_aug, bias, sk))
    scores = [lax.dot_general(rows, k_t, (((1,), (1,)), ((), ())), preferred_element_type=F32)
              for rows, k_t, _, _, _ in parts]
    tiles = []
    for sc, (_, _, v_aug, bias, sk) in zip(scores, parts):
        s = (sc.reshape(2, blk, nkeys) + bias[None]).reshape(2 * blk, nkeys)
        mx = jnp.maximum(jnp.max(s, axis=-1, keepdims=True), sk)
        p = jnp.exp((s - mx).astype(BF16))
        oa = jnp.dot(p, v_aug, preferred_element_type=F32)
        den = oa[:, SWA_KV:] + jnp.exp(sk - mx)
        on = oa[:, :SWA_KV] * (1.0 / den)
        tiles.append(jnp.where(low_half, on[:blk], on[blk:]))
    o_ref[...] = jnp.concatenate([jnp.concatenate(tiles[:group], axis=1), jnp.concatenate(tiles[group:], axis=1)],
                                 axis=0).astype(o_ref.dtype)


def _swa(q, kv, sink, n_batch, seq, ctx_len):
    m = q.shape[0]
    blk = SWA_BLOCK
    nb = seq // blk
    n_lat = n_batch * nb
    cpb = ctx_len // blk
    assert nb % 2 == 0 and cpb % 2 == 0

    def batch_of(i):
        return jnp.where(2 * i < n_lat, (2 * i) // nb, (2 * i - n_lat) // cpb)

    def prev_idx(i, s):
        n0 = (2 * i) % nb
        return (jnp.where(2 * i < n_lat, batch_of(i) * nb + jnp.maximum(n0 - 1, 0), 2 * i), 0)

    def next_idx(i, s):
        n0 = (2 * i) % nb
        return (jnp.where(2 * i < n_lat, batch_of(i) * nb + jnp.minimum(n0 + 2, nb - 1), 2 * i), 0)

    def ctx_idx(i, s):
        return (n_batch * seq // ctx_len + batch_of(i), 0)

    grid_spec = pltpu.PrefetchScalarGridSpec(
        num_scalar_prefetch=1,
        grid=(m // (2 * blk),),
        in_specs=[pl.BlockSpec((2 * blk, SWA_Q), lambda i, s: (i, 0)),
                  pl.BlockSpec((blk, 2 * SWA_KV), prev_idx),
                  pl.BlockSpec((2 * blk, 2 * SWA_KV), lambda i, s: (i, 0)),
                  pl.BlockSpec((blk, 2 * SWA_KV), next_idx),
                  pl.BlockSpec((ctx_len, 2 * SWA_KV), ctx_idx)],
        out_specs=pl.BlockSpec((2 * blk, SWA_Q), lambda i, s: (i, 0)),
    )
    return pl.pallas_call(
        functools.partial(_swa_kernel, nb=nb, n_lat_pairs=n_lat // 2),
        out_shape=jax.ShapeDtypeStruct((m, SWA_Q), BF16),
        grid_spec=grid_spec,
        compiler_params=_cp(("parallel",)),
        name="swa_attn",
    )(sink, q, kv, kv, kv, kv)


def _diff_kernel(*refs, n_lat_chunks, coef):
    if n_lat_chunks:
        lam_ref, gcol_ref, q_ref, kc_ref, vtc_ref, kl_ref, vtl_ref, o_ref, m_sc, acc_sc, st_a, st_b = refs
    else:
        lam_ref, gcol_ref, q_ref, kc_ref, vtc_ref, o_ref, m_sc, acc_sc, st_a, st_b = refs
    q = q_ref[...]
    tq = q.shape[0]
    dv = DIFF_V_DIM
    lane = lax.broadcasted_iota(jnp.int32, q.shape, 1)
    zero = jnp.zeros_like(q)
    qq = jnp.concatenate([jnp.where(lane < HEAD_DIM, q, zero), jnp.where(lane >= HEAD_DIM, q, zero)], axis=0)

    def scores(k):
        return lax.dot_general(k, qq, (((1,), (1,)), ((), ())), preferred_element_type=F32)

    def accumulate(st_ref, vt):
        st = st_ref[0:vt.shape[1], :]
        m_old = m_sc[...]
        m_new = jnp.maximum(m_old, jnp.max(st, axis=0, keepdims=True))
        alpha = jnp.exp2(m_old - m_new)
        pt = jnp.exp2((st - m_new).astype(BF16))
        acc_sc[...] = alpha * acc_sc[...] + jnp.dot(vt, pt, preferred_element_type=F32)
        m_sc[...] = m_new

    tk = st_a.shape[0]
    k_lat = lambda c: kl_ref[pl.ds(pl.multiple_of(c * tk, tk), tk), :]
    m_sc[...] = jnp.full(m_sc.shape, NEG_INF, F32)
    acc_sc[...] = jnp.zeros(acc_sc.shape, F32)
    bufs = (st_a, st_b)
    n = n_lat_chunks
    if n:
        unroll = DIFF_UNROLL
        st_a[...] = scores(k_lat(0))
        n_trips = (n - 1) // unroll

        def body(j, carry):
            for u in range(unroll):
                c = j * unroll + u
                bufs[(u + 1) % 2][...] = scores(k_lat(c + 1))
                accumulate(bufs[u % 2], vtl_ref[0, c])
            return carry

        lax.fori_loop(0, n_trips, body, 0)
        for c in range(n_trips * unroll, n):
            if c + 1 < n:
                bufs[(c + 1) % 2][...] = scores(k_lat(c + 1))
            else:
                bufs[(c + 1) % 2][0:kc_ref.shape[0], :] = scores(kc_ref[...])
            accumulate(bufs[c % 2], vtl_ref[0, c])
    else:
        st_a[0:kc_ref.shape[0], :] = scores(kc_ref[...])
    accumulate(bufs[n % 2], vtc_ref[0, 0])
    acc = acc_sc[...]
    ot = acc[:dv] * (1.0 / acc[dv:dv + 1])
    odt = ot[:, :tq] - lam_ref[0] * ot[:, tq:]
    ms = jnp.mean(odt * odt, axis=0, keepdims=True)
    yt = odt * lax.rsqrt(ms + DIFF_SUBLN_EPS) * (gcol_ref[...] * coef)
    o_ref[...] = yt.T.astype(o_ref.dtype)


def _diff(q, k, v, lam, subln_g, lam_init, n_batch, seq, ctx_len):
    tk = min(DIFF_TK, seq)
    assert seq % tk == 0 and tk % ctx_len == 0
    dv = DIFF_V_DIM
    dva = dv + DIFF_ONES_ROWS
    lat_rows = n_batch * seq

    def transposed_chunks(rows, size):
        t = rows.reshape(rows.shape[0] // size, size, DIFF_HEADS, dv).transpose(2, 0, 3, 1)
        return jnp.concatenate([t, jnp.ones(t.shape[:2] + (DIFF_ONES_ROWS, size), t.dtype)], axis=2)

    vt_lat = transposed_chunks(v[:lat_rows], tk)
    vt_ctx = transposed_chunks(v[lat_rows:], ctx_len)
    gcol = subln_g.reshape(dv, 1)
    lat_chunks = seq // tk
    ctx0 = n_batch * seq // ctx_len

    def call(tq, n_q, q_block0, batch_of, with_lat):
        in_specs = [pl.BlockSpec((dv, 1), lambda h, i, s: (0, 0)),
                    pl.BlockSpec((tq, LANES), lambda h, i, s: (q_block0 + i, h)),
                    pl.BlockSpec((ctx_len, LANES), lambda h, i, s: (ctx0 + batch_of(i), h)),
                    pl.BlockSpec((1, 1, dva, ctx_len), lambda h, i, s: (h, batch_of(i), 0, 0))]
        args = [lam, gcol, q, k, vt_ctx]
        if with_lat:
            in_specs += [pl.BlockSpec((seq, LANES), lambda h, i, s: (batch_of(i), h)),
                         pl.BlockSpec((1, lat_chunks, dva, tk), lambda h, i, s: (h, batch_of(i), 0, 0))]
            args += [k, vt_lat]
        grid_spec = pltpu.PrefetchScalarGridSpec(
            num_scalar_prefetch=1,
            grid=(DIFF_HEADS, n_q),
            in_specs=in_specs,
            out_specs=pl.BlockSpec((tq, LANES), lambda h, i, s: (i, h)),
            scratch_shapes=[pltpu.VMEM((1, 2 * tq), F32), pltpu.VMEM((dva, 2 * tq), F32),
                            pltpu.VMEM((tk, 2 * tq), F32), pltpu.VMEM((tk, 2 * tq), F32)],
        )
        return pl.pallas_call(
            functools.partial(_diff_kernel, n_lat_chunks=lat_chunks if with_lat else 0, coef=1.0 - lam_init),
            out_shape=jax.ShapeDtypeStruct((n_q * tq, DIFF_V), BF16),
            grid_spec=grid_spec,
            compiler_params=_cp(("parallel", "arbitrary")),
            name="diff_attn" if with_lat else "diff_attn_ctx",
        )(*args)

    tq = min(DIFF_TQ, seq)
    y_lat = call(tq, n_batch * seq // tq, 0, lambda i: i // (seq // tq), True)
    y_ctx = call(ctx_len, n_batch, ctx0, lambda i: i, False)
    return jnp.concatenate([y_lat, y_ctx], axis=0)


def _rwkv_prep_kernel(p_ref, hp_ref, hn_ref, mup_ref, mun_ref, kk_w_ref, ka_ref, rk_ref, w0_ref, a0_ref,
                      wup_ref, aup_ref, gup_ref, ones_ref,
                      r_ref, v_ref, kk_ref, ld_ref, ke_ref, bb_ref, g_ref, bonus_ref, sc,
                      *, tm, lat_rows, seq, ctx_len):
    i = pl.program_id(0)
    w = RWKV_WIDTH
    first = i * tm
    in_lat = first < lat_rows
    seg_pos = jnp.where(in_lat, first % seq, (first - lat_rows) % ctx_len)
    seg_len = jnp.where(in_lat, seq, ctx_len)
    keep_prev = jnp.where(seg_pos == 0, 0.0, 1.0)
    keep_next = jnp.where(seg_pos + tm == seg_len, 0.0, 1.0)
    sc[0:8, :] = hp_ref[...] * keep_prev
    sc[8:8 + tm, :] = p_ref[...]
    sc[8 + tm:16 + tm, :] = hn_ref[...] * keep_next
    p = p_ref[...]
    prev = sc[7:7 + tm, :]
    nxt = sc[9:9 + tm, :]
    ps = p + mup_ref[...] * (prev - p) + mun_ref[...] * (nxt - p)

    r = ps[:, 0:w]
    k = ps[:, w:2 * w]
    v = ps[:, 2 * w:3 * w]
    wd = ps[:, 3 * w:3 * w + 2 * DECAY_LORA]
    ad = ps[:, 3 * w + 2 * DECAY_LORA:3 * w + 2 * DECAY_LORA + 2 * AAA_LORA]
    gd = ps[:, 3 * w + 2 * DECAY_LORA + 2 * AAA_LORA:]

    ones_bd = ones_ref[...]
    g = jnp.dot(_sigmoid(gd).astype(BF16), gup_ref[...], preferred_element_type=F32)
    kk = k * kk_w_ref[...]
    ss = _group_sum(kk * kk, ones_bd)
    kk = kk * lax.rsqrt(jnp.maximum(ss, 1e-24))
    w_raw = w0_ref[...] + jnp.dot(jnp.tanh(wd).astype(BF16), wup_ref[...], preferred_element_type=F32)
    a_raw = a0_ref[...] + jnp.dot(ad.astype(BF16), aup_ref[...], preferred_element_type=F32)
    ld = -math.exp(-0.5) * _sigmoid(w_raw)
    a = _sigmoid(a_raw)
    ka = ka_ref[...]
    ke_sum = jnp.zeros_like(k)
    for d in range(2):
        a_d = a[:, d * w:(d + 1) * w]
        ke = k * (1.0 + (a_d - 1.0) * ka)
        ld_ref[d] = ld[:, d * w:(d + 1) * w]
        ke_ref[d] = ke
        bb_ref[d] = kk * a_d
        ke_sum = ke_sum + ke
    rk = _group_sum(r * ke_sum * rk_ref[...], ones_bd)
    r_ref[...] = r
    v_ref[...] = v
    kk_ref[...] = kk
    g_ref[...] = g
    bonus_ref[...] = rk * v


def _rwkv_prep(p, lp, n_batch, seq, ctx_len):
    m, cols = p.shape
    tm = PREP_TILE
    assert seq % tm == 0 and ctx_len % tm == 0
    w = RWKV_WIDTH
    lat_rows = n_batch * seq
    row = lambda a: a.reshape(1, -1).astype(F32)

    def blockdiag(u):
        z = jnp.zeros_like(u[0])
        return jnp.concatenate([jnp.concatenate([u[0], z], axis=1), jnp.concatenate([z, u[1]], axis=1)], axis=0)

    head = jnp.arange(w) // HEAD_DIM
    ones_bd = (head[:, None] == head[None, :]).astype(BF16)
    full = lambda shape: pl.BlockSpec(shape, lambda i: (0,) * len(shape))
    nb8 = m // 8
    outs = pl.pallas_call(
        functools.partial(_rwkv_prep_kernel, tm=tm, lat_rows=lat_rows, seq=seq, ctx_len=ctx_len),
        out_shape=[jax.ShapeDtypeStruct((m, w), F32)] * 3
        + [jax.ShapeDtypeStruct((2, m, w), F32)] * 3
        + [jax.ShapeDtypeStruct((m, w), F32)] * 2,
        grid=(m // tm,),
        in_specs=[pl.BlockSpec((tm, cols), lambda i: (i, 0)),
                  pl.BlockSpec((8, cols), lambda i: (jnp.maximum(i * (tm // 8) - 1, 0), 0)),
                  pl.BlockSpec((8, cols), lambda i: (jnp.minimum((i + 1) * (tm // 8), nb8 - 1), 0)),
                  full((1, cols)), full((1, cols)), full((1, w)), full((1, w)), full((1, w)),
                  full((1, 2 * w)), full((1, 2 * w)),
                  full((2 * DECAY_LORA, 2 * w)), full((2 * AAA_LORA, 2 * w)), full((GATE_LORA, w)),
                  full((w, w))],
        out_specs=[pl.BlockSpec((tm, w), lambda i: (i, 0))] * 3
        + [pl.BlockSpec((2, tm, w), lambda i: (0, i, 0))] * 3
        + [pl.BlockSpec((tm, w), lambda i: (i, 0))] * 2,
        scratch_shapes=[pltpu.VMEM((tm + 16, cols), F32)],
        compiler_params=_cp(("parallel",)),
        name="rwkv_prep",
    )(p, p, p, row(lp['rwkv_mu_prev']), row(lp['rwkv_mu_next']), row(lp['rwkv_k_k']), row(lp['rwkv_k_a']),
      row(lp['rwkv_r_k']), row(lp['rwkv_w0']), row(lp['rwkv_a0']),
      blockdiag(lp['rwkv_w_up']).astype(BF16), blockdiag(lp['rwkv_a_up']).astype(BF16),
      lp['rwkv_g_up'].astype(BF16), ones_bd)
    return outs


def _rwkv_chunk_prep(off, r_ref, v_ref, kk_ref, ld_ref, ke_ref, bb_ref, incl):
    c = RWKV_CHUNK
    ld = ld_ref[0, pl.ds(off, c), :]
    r = r_ref[pl.ds(off, c), :]
    v = v_ref[pl.ds(off, c), :]
    kk = kk_ref[pl.ds(off, c), :]
    ke = ke_ref[0, pl.ds(off, c), :]
    bb = bb_ref[0, pl.ds(off, c), :]
    cum = jnp.dot(incl, ld, precision=HI, preferred_element_type=F32)
    tot = jnp.sum(ld, axis=0, keepdims=True)
    rt = r * jnp.exp(cum)
    einv = jnp.exp(-cum)
    etail = jnp.exp(tot - cum)
    return dict(at=(-kk * jnp.exp(cum - ld)).astype(BF16), rt=rt, rt_b=rt.astype(BF16),
                bt=(bb * einv).astype(BF16), kt=(ke * einv).astype(BF16),
                bh=(bb * etail).astype(BF16), kh=(ke * etail).astype(BF16),
                v=v, v_b=v.astype(BF16), wtot=jnp.exp(tot))


def _rwkv_chunk_group(offs, in_refs, y_refs, s_sc):
    c = RWKV_CHUNK
    hd = HEAD_DIM
    n_group = len(offs[0])
    ti = lax.broadcasted_iota(jnp.int32, (c, c), 0)
    tj = lax.broadcasted_iota(jnp.int32, (c, c), 1)
    eye = (ti == tj).astype(F32)
    incl = [(tj <= ti).astype(F32), (tj >= ti).astype(F32)]
    strict = [m - eye for m in incl]
    diag_mask = ((ti // RWKV_INV_BASE) == (tj // RWKV_INV_BASE)).astype(F32)
    off_masks = []
    sz = RWKV_INV_BASE
    while sz < c:
        off_masks.append((((ti // (2 * sz)) == (tj // (2 * sz))) & ((ti // sz) != (tj // sz))).astype(F32))
        sz *= 2

    pre = {(d, g): _rwkv_chunk_prep(offs[d][g], *in_refs[d], incl[d])
           for d in range(2) for g in range(n_group)}
    lanes = [(d, g, h) for g in range(n_group) for d in range(2) for h in range(RWKV_HEADS)]
    sl = lambda h: slice(h * hd, (h + 1) * hd)
    get = lambda name: [pre[d, g][name][:, sl(h)] for d, g, h in lanes]
    at, rt, rt_b, bt, kt, bh, kh, v, v_b = (get(n) for n in ('at', 'rt', 'rt_b', 'bt', 'kt', 'bh', 'kh', 'v', 'v_b'))
    nl = range(len(lanes))
    nt = (((1,), (1,)), ((), ()))
    bdot = lambda x, y: jnp.dot(x.astype(BF16), y.astype(BF16), preferred_element_type=F32)

    gm = [lax.dot_general(jnp.concatenate([at[i], rt_b[i]], axis=0), jnp.concatenate([bt[i], kt[i]], axis=0), nt,
                          preferred_element_type=F32) for i in nl]
    zz0 = [bdot(v[i].T, kh[i]) for i in nl]
    aab = [gm[i][:c, :c] * strict[lanes[i][0]] for i in nl]
    aak = [gm[i][:c, c:] * strict[lanes[i][0]] for i in nl]
    arb = [gm[i][c:, :c] * incl[lanes[i][0]] for i in nl]
    ark = [gm[i][c:, c:] * incl[lanes[i][0]] for i in nl]
    av = [bdot(jnp.concatenate([aak[i], ark[i]], axis=0), v_b[i]) for i in nl]
    pw = [aab[i] * diag_mask for i in nl]
    tm_ = [eye + pw[i] for i in nl]
    for _ in range(int(math.log2(RWKV_INV_BASE)) - 1):
        pw = [bdot(pw[i], pw[i]) for i in nl]
        tm_ = [tm_[i] + bdot(tm_[i], pw[i]) for i in nl]
    for off_mask in off_masks:
        tn = [bdot(tm_[i], aab[i] * off_mask) for i in nl]
        tm_ = [tm_[i] + bdot(tn[i], tm_[i]) for i in nl]
    au = [bdot(tm_[i], jnp.concatenate([at[i], av[i][:c].astype(BF16)], axis=1)) for i in nl]
    ry = [bdot(arb[i], au[i]) for i in nl]
    mz = [bdot(au[i].T, bh[i]) for i in nl]
    rbar = [(rt[i] + ry[i][:, :hd]).astype(BF16) for i in nl]
    ybar = [ry[i][:, hd:] + av[i][c:] for i in nl]
    mm = [(eye * pre[lanes[i][0], lanes[i][1]]['wtot'][:, sl(lanes[i][2])] + mz[i][:hd]).astype(BF16) for i in nl]
    zz = [mz[i][hd:] + zz0[i] for i in nl]
    state = {(d, h): s_sc[d, h] for d in range(2) for h in range(RWKV_HEADS)}
    ys = {}
    for g in range(n_group):
        idx = [i for i in nl if lanes[i][1] == g]
        s_b = {i: state[lanes[i][0], lanes[i][2]].astype(BF16) for i in idx}
        for i in idx:
            ys[i] = lax.dot_general(rbar[i], s_b[i], nt, preferred_element_type=F32) + ybar[i]
        for i in idx:
            state[lanes[i][0], lanes[i][2]] = jnp.dot(s_b[i], mm[i], preferred_element_type=F32) + zz[i]
    for (d, h), val in state.items():
        s_sc[d, h] = val
    for d in range(2):
        for g in range(n_group):
            y_refs[d][0, pl.ds(offs[d][g], c), :] = jnp.concatenate(
                [ys[i] for i in nl if lanes[i][0] == d and lanes[i][1] == g], axis=1)


def _rwkv_scan_kernel(rf, vf, kf, ldf, kef, bbf, rb, vb, kb, ldb, keb, bbb, yf, yb, s_sc, *, n_chunks):
    @pl.when(pl.program_id(1) == 0)
    def _():
        s_sc[...] = jnp.zeros(s_sc.shape, F32)

    group = RWKV_GROUP

    def body(t, carry):
        off_f = [pl.multiple_of((t * group + g) * RWKV_CHUNK, RWKV_CHUNK) for g in range(group)]
        off_b = [pl.multiple_of((n_chunks - 1 - t * group - g) * RWKV_CHUNK, RWKV_CHUNK) for g in range(group)]
        _rwkv_chunk_group((off_f, off_b), ((rf, vf, kf, ldf, kef, bbf), (rb, vb, kb, ldb, keb, bbb)),
                          (yf, yb), s_sc)
        return carry

    lax.fori_loop(0, n_chunks // group, body, 0)


def _rwkv_scan(r, v, kk, ld, ke, bb, n_batch, seq, ctx_len):
    m, w = r.shape
    ts = RWKV_STEP_ROWS
    assert ctx_len == ts
    lpb = seq // ts
    ctx0 = n_batch * seq // ts
    nj = 1 + lpb

    def fwd(b, j):
        return jnp.where(j == 0, ctx0 + b, b * lpb + j - 1)

    def bwd(b, j):
        return jnp.where(j == 0, ctx0 + b, b * lpb + lpb - j)

    shared = lambda f: pl.BlockSpec((ts, w), lambda b, j: (f(b, j), 0))
    per_dir = lambda f, d: pl.BlockSpec((1, ts, w), lambda b, j: (d, f(b, j), 0))
    y = pl.pallas_call(
        functools.partial(_rwkv_scan_kernel, n_chunks=ts // RWKV_CHUNK),
        out_shape=[jax.ShapeDtypeStruct((1, m, w), F32)] * 2,
        grid=(n_batch, nj),
        in_specs=[shared(fwd), shared(fwd), shared(fwd), per_dir(fwd, 0), per_dir(fwd, 0), per_dir(fwd, 0),
                  shared(bwd), shared(bwd), shared(bwd), per_dir(bwd, 1), per_dir(bwd, 1), per_dir(bwd, 1)],
        out_specs=[pl.BlockSpec((1, ts, w), lambda b, j: (0, fwd(b, j), 0)),
                   pl.BlockSpec((1, ts, w), lambda b, j: (0, bwd(b, j), 0))],
        scratch_shapes=[pltpu.VMEM((2, RWKV_HEADS, HEAD_DIM, HEAD_DIM), F32)],
        compiler_params=_cp(("parallel", "arbitrary")),
        name="rwkv_scan",
    )(r, v, kk, ld, ke, bb, r, v, kk, ld, ke, bb)
    return y


def _rwkv_finish(y, bonus, g, lnx_g, lnx_b, ones_bd):
    inv = 1.0 / HEAD_DIM
    mu = _group_sum(y, ones_bd) * inv
    yc = y - mu
    var = _group_sum(yc * yc, ones_bd) * inv
    yn = yc * lax.rsqrt(var + RWKV_GN_EPS) * lnx_g + lnx_b
    return (yn + bonus) * g


def _top2(logits):
    lane = lax.broadcasted_iota(jnp.int32, logits.shape, 1)
    logits = jnp.where(lane < N_EXPERTS, logits, -jnp.inf)
    m1 = jnp.max(logits, axis=-1, keepdims=True)
    i1 = jnp.min(jnp.where(logits == m1, lane, LANES), axis=-1, keepdims=True)
    rest = jnp.where(lane == i1, -jnp.inf, logits)
    m2 = jnp.max(rest, axis=-1, keepdims=True)
    i2 = jnp.min(jnp.where(rest == m2, lane, LANES), axis=-1, keepdims=True)
    e = jnp.exp(m2 - m1)
    w1 = 1.0 / (1.0 + e)
    w2 = e / (1.0 + e)
    idx = jnp.where(lane == 0, i1, jnp.where(lane == 1, i2, 0))
    wts = jnp.where(lane == 0, w1, jnp.where(lane == 1, w2, 0.0))
    return idx, wts


def _merge_kernel(*refs, d, with_router):
    (ya_ref, yf_ref, yb_ref, bonus_ref, g_ref, lg_ref, lb_ref, ones_ref, yc_ref, gt_ref, h_ref,
     pa_ref, pb_ref, pc_ref, wo_ref, gpost_ref, gpre_ref, mod_ref) = refs[:18]
    if with_router:
        rwh_ref, rwl_ref, h_out, f_out, idx_out, wt_out = refs[18:]
    else:
        h_out, f_out = refs[18:]
    yb = _rwkv_finish(yf_ref[0] + yb_ref[0], bonus_ref[...], g_ref[...], lg_ref[...], lb_ref[...], ones_ref[...])
    gates = gt_ref[...].astype(F32)
    merged = (_sigmoid(gates[:, 0:d]) * jnp.dot(ya_ref[...], pa_ref[...], preferred_element_type=F32)
              + _sigmoid(gates[:, d:2 * d]) * jnp.dot(yb.astype(BF16), pb_ref[...], preferred_element_type=F32)
              + _sigmoid(gates[:, 2 * d:3 * d]) * jnp.dot(yc_ref[...], pc_ref[...], preferred_element_type=F32))
    out = jnp.dot(merged.astype(BF16), wo_ref[...], preferred_element_type=F32)
    mod = lambda idx: mod_ref[0, :, idx * d:(idx + 1) * d]
    hn = h_ref[...] + mod(2) * (_rms(out, NORM_EPS) * gpost_ref[...])
    h_out[...] = hn
    f = (_rms(hn, NORM_EPS) * gpre_ref[...]) * (1.0 + mod(4)) + mod(3)
    f_out[...] = f.astype(f_out.dtype)
    if with_router:
        rows = f.shape[0]
        hi = f.astype(BF16)
        mid = (f - hi.astype(F32)).astype(BF16)
        part = jnp.dot(jnp.concatenate([hi, mid], axis=0), rwh_ref[...], preferred_element_type=F32)
        logits = part[:rows] + part[rows:] + jnp.dot(hi, rwl_ref[...], preferred_element_type=F32)
        idx, wts = _top2(logits)
        idx_out[...] = idx
        wt_out[...] = wts


def _merge(ya, yf, yb, bonus, g, lnx_g, lnx_b, yc, gates, h, pa, pb, pc, wo, g_post, g_pre, mods, f_dtype,
           lat_bpb, n_batch, router_w=None):
    m, d = h.shape
    w = bonus.shape[1]
    tm = MERGE_TILE
    bpb = lat_bpb * (ROW_TILE // tm)
    seg = functools.partial(_seg_of_block, lat_blocks_per_batch=bpb, n_batch=n_batch)
    rows = lambda width: pl.BlockSpec((tm, width), lambda i: (i, 0))
    rows3 = pl.BlockSpec((1, tm, w), lambda i: (0, i, 0))
    full = lambda a: pl.BlockSpec(a.shape, lambda i: (0, 0))
    head = jnp.arange(w) // HEAD_DIM
    ones_bd = (head[:, None] == head[None, :]).astype(BF16)
    args = [ya, yf, yb, bonus, g, lnx_g.reshape(1, w), lnx_b.reshape(1, w), ones_bd, yc, gates, h,
            pa, pb, pc, wo, g_post, g_pre, mods]
    in_specs = [rows(ya.shape[1]), rows3, rows3, rows(w), rows(w), full(args[5]), full(args[6]), full(ones_bd),
                rows(yc.shape[1]), rows(gates.shape[1]), rows(d), full(pa), full(pb), full(pc), full(wo),
                full(g_post), full(g_pre), pl.BlockSpec((1, 1, mods.shape[2]), lambda i: (seg(i), 0, 0))]
    out_shape = [jax.ShapeDtypeStruct((m, d), F32), jax.ShapeDtypeStruct((m, d), f_dtype)]
    out_specs = [rows(d), rows(d)]
    if router_w is not None:
        w_pad = jnp.zeros((d, LANES), F32).at[:, :N_EXPERTS].set(router_w)
        w_hi = w_pad.astype(BF16)
        w_lo = (w_pad - w_hi.astype(F32)).astype(BF16)
        args += [w_hi, w_lo]
        in_specs += [full(w_hi), full(w_lo)]
        out_shape += [jax.ShapeDtypeStruct((m, LANES), jnp.int32), jax.ShapeDtypeStruct((m, LANES), F32)]
        out_specs += [rows(LANES), rows(LANES)]
    return pl.pallas_call(
        functools.partial(_merge_kernel, d=d, with_router=router_w is not None),
        out_shape=out_shape,
        grid=(m // tm,),
        in_specs=in_specs,
        out_specs=out_specs,
        compiler_params=_cp(("parallel",)),
        name="merge",
    )(*args)


def _swiglu_hidden(x, wg, wu):
    hg = jnp.dot(x, wg, preferred_element_type=F32)
    hu = jnp.dot(x, wu, preferred_element_type=F32)
    return (hg * _sigmoid(hg) * hu).astype(BF16)


def _ffn_kernel(f_ref, wg_ref, wu_ref, wd_ref, h_ref, gpost_ref, mod_ref, o_ref, acc, *, d):
    j = pl.program_id(1)

    @pl.when(j == 0)
    def _():
        acc[...] = jnp.zeros(acc.shape, F32)

    hid = _swiglu_hidden(f_ref[...], wg_ref[...], wu_ref[...])
    acc[...] += jnp.dot(hid, wd_ref[...], preferred_element_type=F32)

    @pl.when(j == pl.num_programs(1) - 1)
    def _():
        gate = mod_ref[0, :, 5 * d:6 * d]
        o_ref[...] = h_ref[...] + gate * (_rms(acc[...], NORM_EPS) * gpost_ref[...])


def _ffn(f, wg, wu, wd, h, g_post, mods, lat_bpb, n_batch):
    m, d = h.shape
    ff = wg.shape[1]
    tm, tf = ROW_TILE, FFN_TF
    seg = functools.partial(_seg_of_block, lat_blocks_per_batch=lat_bpb, n_batch=n_batch)
    return pl.pallas_call(
        functools.partial(_ffn_kernel, d=d),
        out_shape=jax.ShapeDtypeStruct((m, d), F32),
        grid=(m // tm, ff // tf),
        in_specs=[pl.BlockSpec((tm, d), lambda i, j: (i, 0)),
                  pl.BlockSpec((d, tf), lambda i, j: (0, j)),
                  pl.BlockSpec((d, tf), lambda i, j: (0, j)),
                  pl.BlockSpec((tf, d), lambda i, j: (j, 0)),
                  pl.BlockSpec((tm, d), lambda i, j: (i, 0)),
                  pl.BlockSpec((1, d), lambda i, j: (0, 0)),
                  pl.BlockSpec((1, 1, mods.shape[2]), lambda i, j: (seg(i), 0, 0))],
        out_specs=pl.BlockSpec((tm, d), lambda i, j: (i, 0)),
        scratch_shapes=[pltpu.VMEM((tm, d), F32)],
        compiler_params=_cp(("parallel", "arbitrary")),
        name="ffn_dense",
    )(f, wg, wu, wd, h, g_post, mods)


def _moe_gather_copy(f_hbm, xbuf, sem, slot, src_row, dst_row):
    return pltpu.make_async_copy(f_hbm.at[pl.ds(src_row, 1), :], xbuf.at[slot, pl.ds(dst_row, 1), :],
                                 sem.at[slot])


def _moe_ffn_kernel(blk_e_ref, nused_ref, tok_ref, f_hbm, wg_ref, wu_ref, wd_ref, y_ref, xbuf, xb, acc, sem,
                    *, tm, nj):
    i = pl.program_id(0)
    j = pl.program_id(1)
    nused = nused_ref[0]
    active = i < nused
    slot = i % 2
    per_step = tm // nj

    @pl.when((i == 0) & (j == 0))
    def _():
        def issue(r, carry):
            _moe_gather_copy(f_hbm, xbuf, sem, 0, tok_ref[r], r).start()
            return carry

        lax.fori_loop(0, tm, issue, 0)

    @pl.when((i <= nused) & (j == 0))
    def _():
        pltpu.make_async_copy(f_hbm.at[pl.ds(0, tm), :], xbuf.at[slot], sem.at[slot]).wait()

    @pl.when(active & (j == 0))
    def _():
        xb[...] = xbuf[slot].astype(BF16)
        acc[...] = jnp.zeros(acc.shape, F32)

    @pl.when(active)
    def _():
        base = (i + 1) * tm + j * per_step
        for r in range(per_step):
            _moe_gather_copy(f_hbm, xbuf, sem, 1 - slot, tok_ref[base + r], j * per_step + r).start()
        hid = _swiglu_hidden(xb[...], wg_ref[0], wu_ref[0])
        acc[...] += jnp.dot(hid, wd_ref[0], preferred_element_type=F32)

    @pl.when(j == nj - 1)
    def _():
        y_ref[...] = acc[...]


def _moe_ffn(f, blk_e, nused, tok, wg, wu, wd, n_blocks):
    d = f.shape[1]
    ff = wg.shape[2]
    tm, tf = MOE_TILE, MOE_TF

    nj = ff // tf

    def e_of(i, be, nu):
        return be[jnp.minimum(i, nu[0] - 1)]

    def j_of(i, j, nu):
        return jnp.where(i < nu[0], j, nj - 1)

    grid_spec = pltpu.PrefetchScalarGridSpec(
        num_scalar_prefetch=3,
        grid=(n_blocks, nj),
        in_specs=[pl.BlockSpec(memory_space=pl.ANY),
                  pl.BlockSpec((1, d, tf), lambda i, j, be, nu, tk: (e_of(i, be, nu), 0, j_of(i, j, nu))),
                  pl.BlockSpec((1, d, tf), lambda i, j, be, nu, tk: (e_of(i, be, nu), 0, j_of(i, j, nu))),
                  pl.BlockSpec((1, tf, d), lambda i, j, be, nu, tk: (e_of(i, be, nu), j_of(i, j, nu), 0))],
        out_specs=pl.BlockSpec((tm, d), lambda i, j, be, nu, tk: (i, 0)),
        scratch_shapes=[pltpu.VMEM((2, tm, d), F32), pltpu.VMEM((tm, d), BF16), pltpu.VMEM((tm, d), F32),
                        pltpu.SemaphoreType.DMA((2,))],
    )
    return pl.pallas_call(
        functools.partial(_moe_ffn_kernel, tm=tm, nj=nj),
        out_shape=jax.ShapeDtypeStruct((n_blocks * tm, d), F32),
        grid_spec=grid_spec,
        compiler_params=_cp(("arbitrary", "arbitrary")),
        name="moe_ffn",
    )(blk_e, nused, tok, f, wg, wu, wd)


def _moe_combine_kernel(p0_ref, p1_ref, y_hbm, wt_ref, h_ref, gpost_ref, mod_ref, o_ref, b0, b1, sem, *, tm, d):
    i = pl.program_id(0)
    slot = i % 2

    def issue_block(blk, s):
        def issue(r, carry):
            pltpu.make_async_copy(y_hbm.at[pl.ds(p0_ref[blk * tm + r], 1), :], b0.at[s, pl.ds(r, 1), :],
                                  sem.at[0, s]).start()
            pltpu.make_async_copy(y_hbm.at[pl.ds(p1_ref[blk * tm + r], 1), :], b1.at[s, pl.ds(r, 1), :],
                                  sem.at[1, s]).start()
            return carry

        lax.fori_loop(0, tm, issue, 0)

    @pl.when(i == 0)
    def _():
        issue_block(0, 0)

    @pl.when(i + 1 < pl.num_programs(0))
    def _():
        issue_block(i + 1, 1 - slot)

    pltpu.make_async_copy(y_hbm.at[pl.ds(0, tm), :], b0.at[slot], sem.at[0, slot]).wait()
    pltpu.make_async_copy(y_hbm.at[pl.ds(0, tm), :], b1.at[slot], sem.at[1, slot]).wait()
    wt = wt_ref[...]
    y = b0[slot] * wt[:, 0:1] + b1[slot] * wt[:, 1:2]
    gate = mod_ref[0, :, 5 * d:6 * d]
    o_ref[...] = h_ref[...] + gate * (_rms(y, NORM_EPS) * gpost_ref[...])


def _moe_combine(pos0, pos1, y, wt, h, g_post, mods, n_rows, rows_per_batch, n_batch):
    d = h.shape[1]
    tm = COMBINE_TILE
    seg = functools.partial(_seg_of_block, lat_blocks_per_batch=rows_per_batch // tm, n_batch=n_batch)
    grid_spec = pltpu.PrefetchScalarGridSpec(
        num_scalar_prefetch=2,
        grid=(n_rows // tm,),
        in_specs=[pl.BlockSpec(memory_space=pl.ANY),
                  pl.BlockSpec((tm, LANES), lambda i, a, b: (i, 0)),
                  pl.BlockSpec((tm, d), lambda i, a, b: (i, 0)),
                  pl.BlockSpec((1, d), lambda i, a, b: (0, 0)),
                  pl.BlockSpec((1, 1, mods.shape[2]), lambda i, a, b: (seg(i), 0, 0))],
        out_specs=pl.BlockSpec((tm, d), lambda i, a, b: (i, 0)),
        scratch_shapes=[pltpu.VMEM((2, tm, d), F32), pltpu.VMEM((2, tm, d), F32),
                        pltpu.SemaphoreType.DMA((2, 2))],
    )
    return pl.pallas_call(
        functools.partial(_moe_combine_kernel, tm=tm, d=d),
        out_shape=jax.ShapeDtypeStruct((n_rows, d), F32),
        grid_spec=grid_spec,
        compiler_params=_cp(("arbitrary",)),
        name="moe_combine",
    )(pos0, pos1, y, wt, h, g_post, mods)


def _moe_slots(top_i, tile):
    n = top_i.shape[0]
    a = n * 2
    e_flat = top_i.reshape(a)
    onehot = (e_flat[:, None] == jnp.arange(N_EXPERTS, dtype=jnp.int32)[None, :]).astype(jnp.int32)
    sub = LANES
    blocks = onehot.reshape(a // sub, sub, N_EXPERTS).astype(F32)
    tri = (jnp.arange(sub)[:, None] >= jnp.arange(sub)[None, :]).astype(F32)
    within = jnp.einsum('ij,bjk->bik', tri, blocks).astype(jnp.int32)
    totals = within[:, -1, :]
    csum = (within + (jnp.cumsum(totals, axis=0) - totals)[:, None, :]).reshape(a, N_EXPERTS)
    rank = jnp.sum(csum * onehot, axis=1) - 1
    counts = csum[-1]
    padded = (counts + tile - 1) // tile * tile
    pends = jnp.cumsum(padded)
    pstarts = pends - padded
    dest = (jnp.sum(onehot * pstarts[None, :], axis=1) + rank).astype(jnp.int32)
    n_blocks = a // tile + N_EXPERTS
    tok = jnp.zeros((n_blocks * tile,), jnp.int32).at[dest].set(
        jnp.arange(a, dtype=jnp.int32) // 2, unique_indices=True, mode='promise_in_bounds')
    block_start = jnp.arange(n_blocks, dtype=jnp.int32) * tile
    blk_e = jnp.minimum(jnp.sum((block_start[:, None] >= pends[None, :]).astype(jnp.int32), axis=1),
                        N_EXPERTS - 1)
    nused = (pends[-1:] // tile).astype(jnp.int32)
    return tok, blk_e, nused, dest.reshape(n, 2), n_blocks


def _swa_head_order(w, axis):
    group = SWA_HEADS // SWA_KV_HEADS
    order = [g * group + t for t in range(group) for g in range(SWA_KV_HEADS)]
    shape = w.shape
    w = w.reshape(shape[:axis] + (SWA_HEADS, HEAD_DIM) + shape[axis + 1:])
    return jnp.take(w, jnp.array(order), axis=axis).reshape(shape)


def _rope_tables(n_batch, seq, ctx_len):
    t = jnp.arange(seq, dtype=jnp.int32)
    row = (t // GRID_W).astype(F32)
    col = (t % GRID_W).astype(F32)
    axis_dim = HEAD_DIM // 2
    inv_freq = ROPE_THETA ** (-jnp.arange(0, axis_dim, 2, dtype=F32) / axis_dim)
    dd = jnp.arange(LANES) % HEAD_DIM
    pos = jnp.where((dd // axis_dim)[None, :] == 0, row[:, None], col[:, None])
    ang = pos * inv_freq[dd % (axis_dim // 2)][None, :]
    cos = jnp.cos(ang)
    sin = jnp.where(((dd % axis_dim) < axis_dim // 2)[None, :], -jnp.sin(ang), jnp.sin(ang))
    n_ctx = n_batch * ctx_len
    cos = jnp.concatenate([jnp.tile(cos, (n_batch, 1)), jnp.ones((n_ctx, LANES), F32)], axis=0)
    sin = jnp.concatenate([jnp.tile(sin, (n_batch, 1)), jnp.zeros((n_ctx, LANES), F32)], axis=0)
    return cos, sin


def kernel(x, c, ctx, c_ctx, ada_w, ada_b, pre_mix_g, post_mix_g, pre_ffn_g, post_ffn_g, w_in, swa_sink,
           rwkv_mu_prev, rwkv_mu_next, rwkv_w0, rwkv_w_up, rwkv_a0, rwkv_a_up, rwkv_g_up, rwkv_k_k, rwkv_k_a,
           rwkv_r_k, rwkv_lnx_g, rwkv_lnx_b, diff_lambda, diff_subln_g, proj_swa, proj_rwkv, proj_diff, w_out,
           ffn_w_gate, ffn_w_up, ffn_w_down, router_w, moe_w_gate, moe_w_up, moe_w_down):
    n_batch, seq, d = x.shape
    ctx_len = ctx.shape[1]
    depth = w_in.shape[0]
    lat_rows = n_batch * seq
    lat_bpb = seq // ROW_TILE
    assert seq % ROW_TILE == 0 and (n_batch * ctx_len) % ROW_TILE == 0

    h = jnp.concatenate([x.reshape(lat_rows, d), ctx.reshape(n_batch * ctx_len, d)], axis=0)
    m = h.shape[0]
    cond = jnp.zeros((8, d), F32).at[:n_batch].set(c).at[n_batch].set(c_ctx)
    rope = _rope_tables(n_batch, seq, ctx_len)
    row = lambda a: a.reshape(1, -1)

    o_swa = 0
    o_rwkv = o_swa + SWA_Q + 2 * SWA_KV
    o_diff = o_rwkv + RWKV_COLS
    o_gate = o_diff + 2 * DIFF_QK + DIFF_V
    o_end = o_gate + 3 * d

    for layer in range(depth):
        mods = _ada(cond, ada_w[layer], ada_b[layer].reshape(1, -1))[:, None, :]
        wl = w_in[layer].astype(BF16)
        a = _prenorm(h, row(pre_mix_g[layer]), mods, 0, 1, lat_bpb, n_batch)

        w_swa = jnp.concatenate([_swa_head_order(wl[:, o_swa:o_swa + SWA_Q], axis=1),
                                 wl[:, o_swa + SWA_Q:o_rwkv]], axis=1)
        q_swa, kv_swa = _proj(a, w_swa, (SWA_Q, 2 * SWA_KV), (BF16, BF16), rope=rope,
                              rope_cols=SWA_Q + SWA_KV, scale_cols=SWA_Q, q_scale=HEAD_DIM ** -0.5)
        (p_rwkv,) = _proj(a, wl[:, o_rwkv:o_diff], (RWKV_COLS,), (F32,))
        q_diff, k_diff, v_diff = _proj(a, wl[:, o_diff:o_gate], (DIFF_QK, DIFF_QK, DIFF_V), (BF16,) * 3,
                                       rope=rope, rope_cols=2 * DIFF_QK, scale_cols=DIFF_QK,
                                       q_scale=HEAD_DIM ** -0.5 * math.log2(math.e))
        (gates,) = _proj(a, wl[:, o_gate:o_end], (d,), (BF16,), tn=d)

        ya = _swa(q_swa, kv_swa, swa_sink[layer].astype(F32), n_batch, seq, ctx_len)

        lp = {'rwkv_mu_prev': rwkv_mu_prev[layer], 'rwkv_mu_next': rwkv_mu_next[layer],
              'rwkv_w0': rwkv_w0[layer], 'rwkv_w_up': rwkv_w_up[layer], 'rwkv_a0': rwkv_a0[layer],
              'rwkv_a_up': rwkv_a_up[layer], 'rwkv_g_up': rwkv_g_up[layer], 'rwkv_k_k': rwkv_k_k[layer],
              'rwkv_k_a': rwkv_k_a[layer], 'rwkv_r_k': rwkv_r_k[layer]}
        r_, v_, kk_, ld_, ke_, bb_, g_, bonus_ = _rwkv_prep(p_rwkv, lp, n_batch, seq, ctx_len)
        y_f, y_b = _rwkv_scan(r_, v_, kk_, ld_, ke_, bb_, n_batch, seq, ctx_len)

        lam_vec = diff_lambda[layer].astype(F32)
        lam_init = 0.8 - 0.6 * math.exp(-0.3 * layer)
        lam = (jnp.exp(jnp.sum(lam_vec[0] * lam_vec[1])) - jnp.exp(jnp.sum(lam_vec[2] * lam_vec[3]))
               + lam_init).reshape(1)
        yc = _diff(q_diff, k_diff, v_diff, lam, row(diff_subln_g[layer]), lam_init, n_batch, seq, ctx_len)

        moe_layer = layer % 2 == 1
        jj = layer // 2
        merged = _merge(ya, y_f, y_b, bonus_, g_, rwkv_lnx_g[layer], rwkv_lnx_b[layer], yc, gates, h,
                        _swa_head_order(proj_swa[layer], axis=0).astype(BF16), proj_rwkv[layer].astype(BF16),
                        proj_diff[layer].astype(BF16), w_out[layer].astype(BF16), row(post_mix_g[layer]),
                        row(pre_ffn_g[layer]), mods, F32 if moe_layer else BF16, lat_bpb, n_batch,
                        router_w=router_w[jj] if moe_layer else None)
        h, f = merged[:2]
        need_ctx = layer < depth - 1
        if not moe_layer:
            h = _ffn(f, ffn_w_gate[jj].astype(BF16), ffn_w_up[jj].astype(BF16), ffn_w_down[jj].astype(BF16),
                     h, row(post_ffn_g[layer]), mods, lat_bpb, n_batch)
        else:
            n_tok = m if need_ctx else lat_rows
            top_i, top_w = merged[2][:n_tok], merged[3][:n_tok]
            tok, blk_e, nused, dest, n_blocks = _moe_slots(top_i[:, :2], MOE_TILE)
            y = _moe_ffn(f, blk_e, nused, tok, moe_w_gate[jj].astype(BF16), moe_w_up[jj].astype(BF16),
                         moe_w_down[jj].astype(BF16), n_blocks)
            h = _moe_combine(dest[:, 0], dest[:, 1], y, top_w, h, row(post_ffn_g[layer]), mods, n_tok, seq,
                             n_batch)
    return h[:lat_rows].reshape(n_batch, seq, d)
```
